```python
import jax, jax.numpy as jnp
from jax import lax
import numpy as np

D_MODEL = 1024
BATCH = 4
SEQ = 4096
DEPTH = 2

CTX_LEN = 256
GRID_W = 64
N_MIXERS = 2
EPS = 1e-6
N_MOD = 6
HG_EXPAND = 128
HG_HEADS = D_MODEL // HG_EXPAND
HG_DV = D_MODEL // HG_HEADS
HG_CHUNK = 64
N_HG_PROJ = 5
POOL_WINDOWS = (2, 4, 8, 16)
POOL_GROUPS = len(POOL_WINDOWS)
POOL_GDIM = D_MODEL // POOL_GROUPS
N_EXPERTS = 16
N_EXPERT_GROUPS = 4
EXPERTS_PER_GROUP = N_EXPERTS // N_EXPERT_GROUPS
TOP_K = 2
D_EXPERT = 256

kernel_name = "hybrid_hgrn2_pool_moe_dit"


def rms_norm(x, g):
    xf = x.astype(jnp.float32)
    y = xf * lax.rsqrt(jnp.mean(xf * xf, axis=-1, keepdims=True) + EPS)
    return (y * g.astype(jnp.float32)).astype(x.dtype)


def modulate(x, g, shift, scale):
    return rms_norm(x, g) * (1.0 + scale) + shift


def _heads(t):
    b, l, _ = t.shape
    return t.reshape(b, l, HG_HEADS, -1).transpose(0, 2, 1, 3).astype(jnp.float32)


def _rev(t):
    return jnp.flip(t, axis=2)


def _lower_bound(logits, slot):
    p = jax.nn.softmax(logits.astype(jnp.float32), axis=0)
    return jnp.cumsum(p, axis=0)[slot].reshape(HG_HEADS, 1, HG_EXPAND)


def _gate(z, lb):
    log_f = jnp.logaddexp(jnp.log(lb), jnp.log1p(-lb) + jax.nn.log_sigmoid(z))
    k = (1.0 - lb) * jax.nn.sigmoid(-z)
    return log_f, k


def _gla_scan(q, k, v, log_f, s0):
    b, h, l, _ = q.shape
    dv = v.shape[-1]
    n = l // HG_CHUNK
    rs = lambda t: t.reshape(b, h, n, HG_CHUNK, t.shape[-1])
    q, k, v, log_f = rs(q), rs(k), rs(v), rs(log_f)
    cum = jnp.cumsum(log_f, axis=3)
    ref = cum[:, :, :, HG_CHUNK // 2 - 1:HG_CHUNK // 2, :]
    scores = jnp.einsum('bhnck,bhnsk->bhncs', q * jnp.exp(cum - ref), k * jnp.exp(ref - cum))
    tri = jnp.tril(jnp.ones((HG_CHUNK, HG_CHUNK), dtype=bool))
    scores = jnp.where(tri, scores, 0.0)
    intra = jnp.einsum('bhncs,bhnsv->bhncv', scores, v)
    last = cum[:, :, :, -1:, :]
    chunk_kv = jnp.einsum('bhnck,bhncv->nbhkv', k * jnp.exp(last - cum), v)
    chunk_decay = jnp.moveaxis(jnp.exp(last[:, :, :, 0, :]), 2, 0)

    def step(s, inp):
        dec, kv = inp
        return dec[..., None] * s + kv, s

    s_final, s_prev = lax.scan(step, s0, (chunk_decay, chunk_kv))
    inter = jnp.einsum('bhnck,nbhkv->bhncv', q * jnp.exp(cum), s_prev)
    return (intra + inter).reshape(b, h, l, dv), s_final


def _hgrn_inputs(h, w_in):
    q, v, zf, zb, g = jnp.split(h @ w_in, N_HG_PROJ, axis=-1)
    return jax.nn.silu(_heads(q)), _heads(v), _heads(zf), _heads(zb), g


def _hgrn_readout(o, g, gnorm, w_out):
    b, h, l, dv = o.shape
    o = o * lax.rsqrt(jnp.mean(o * o, axis=-1, keepdims=True) + EPS) * gnorm.astype(jnp.float32)
    o = o.transpose(0, 2, 1, 3).reshape(b, l, h * dv)
    y = (o * jax.nn.silu(g.astype(jnp.float32))).astype(g.dtype)
    return y @ w_out


def hgrn2_mixer(h_lat, h_ctx, w_in, lb_fwd, lb_bwd, gnorm, w_out, slot, need_ctx):
    lbf = _lower_bound(lb_fwd, slot)
    lbb = _lower_bound(lb_bwd, slot)
    qc, vc, zfc, zbc, gc = _hgrn_inputs(h_ctx, w_in)
    ql, vl, zfl, zbl, gl = _hgrn_inputs(h_lat, w_in)
    s0 = jnp.zeros((h_lat.shape[0], HG_HEADS, HG_EXPAND, HG_DV), jnp.float32)
    lf, kf = _gate(zfc, lbf)
    o_cf, s_f = _gla_scan(qc, kf, vc, lf, s0)
    lf, kf = _gate(zfl, lbf)
    o_lf, _ = _gla_scan(ql, kf, vl, lf, s_f)
    lbk, kb = _gate(zbc, lbb)
    o_cb, s_b = _gla_scan(_rev(qc), _rev(kb), _rev(vc), _rev(lbk), s0)
    lbk, kb = _gate(zbl, lbb)
    o_lb, _ = _gla_scan(_rev(ql), _rev(kb), _rev(vl), _rev(lbk), s_b)
    y_lat = _hgrn_readout(o_lf + _rev(o_lb), gl, gnorm, w_out)
    y_ctx = _hgrn_readout(o_cf + _rev(o_cb), gc, gnorm, w_out) if need_ctx else None
    return y_lat, y_ctx


def _box_sum(u, win, axis):
    n = u.shape[axis]
    cs = jnp.cumsum(u, axis=axis)
    pad = [(0, 0)] * u.ndim
    pad[axis] = (1, 0)
    cs = jnp.pad(cs, pad)
    t = jnp.arange(n)
    lo = jnp.clip(t - win // 2, 0, n)
    hi = jnp.clip(t - win // 2 + win, 0, n)
    return jnp.take(cs, hi, axis=axis) - jnp.take(cs, lo, axis=axis), hi - lo


def _grid_mean(u, win):
    b, l, d = u.shape
    rows = l // GRID_W
    g = u.reshape(b, rows, GRID_W, d)
    s, cnt_c = _box_sum(g, win, 2)
    s, cnt_r = _box_sum(s, win, 1)
    cnt = (cnt_r[:, None] * cnt_c[None, :]).astype(u.dtype)
    return (s / cnt[None, :, :, None]).reshape(b, l, d)


def _seq_mean(u, win):
    s, cnt = _box_sum(u, win, 1)
    return s / cnt.astype(u.dtype)[None, :, None]


def _pool_mix(h, mean_fn, w_in, w_grp, p_scale, w_out):
    u = (h @ w_in).astype(jnp.float32)
    parts = []
    for gi, win in enumerate(POOL_WINDOWS):
        ug = u[..., gi * POOL_GDIM:(gi + 1) * POOL_GDIM]
        parts.append((mean_fn(ug, win) - ug) @ w_grp[gi].astype(jnp.float32))
    y = jnp.concatenate(parts, axis=-1) * p_scale.astype(jnp.float32)
    return y.astype(h.dtype) @ w_out


def pool_mixer(h_lat, h_ctx, w_in, w_grp, p_scale, w_out, need_ctx):
    y_lat = _pool_mix(h_lat, _grid_mean, w_in, w_grp, p_scale, w_out)
    y_ctx = _pool_mix(h_ctx, _seq_mean, w_in, w_grp, p_scale, w_out) if need_ctx else None
    return y_lat, y_ctx


def moe_ffn(h, router_w, router_b, w_gate, w_up, w_down):
    n = h.shape[0]
    s = jax.nn.sigmoid((h @ router_w).astype(jnp.float32))
    sel = (s + router_b.astype(jnp.float32)).reshape(n, N_EXPERT_GROUPS, EXPERTS_PER_GROUP)
    top_v, top_i = lax.top_k(sel, TOP_K)
    best = jnp.argmax(jnp.sum(top_v, axis=-1), axis=-1)
    loc = jnp.take_along_axis(top_i, best[:, None, None], axis=1)[:, 0]
    ids = best[:, None] * EXPERTS_PER_GROUP + loc
    w = jnp.take_along_axis(s, ids, axis=1)
    w = w / jnp.sum(w, axis=-1, keepdims=True)
    combine = jnp.sum(jax.nn.one_hot(ids, N_EXPERTS, dtype=jnp.float32) * w[..., None], axis=1)
    y = jnp.zeros(h.shape, jnp.float32)
    for e in range(N_EXPERTS):
        a = jax.nn.silu(h @ w_gate[e]) * (h @ w_up[e])
        y = y + combine[:, e:e + 1] * (a @ w_down[e]).astype(jnp.float32)
    return y.astype(h.dtype)


def setup_inputs(seed: int = 0) -> dict:
    key = jax.random.key(seed)
    ks = jax.random.split(key, 24)
    n_a = (DEPTH + N_MIXERS - 1) // N_MIXERS
    n_b = DEPTH // N_MIXERS
    nrm = lambda k, shape, sc: jax.random.normal(k, shape, jnp.float32) * sc
    d = D_MODEL
    return {
        "x": nrm(ks[0], (BATCH, SEQ, d), 1.0),
        "c": nrm(ks[1], (BATCH, d), 1.0),
        "ctx": nrm(ks[2], (BATCH, CTX_LEN, d), 1.0),
        "c_ctx": nrm(ks[3], (d,), 1.0),
        "w_mod": nrm(ks[4], (DEPTH, d, N_MOD * d), 0.5 * d ** -0.5),
        "b_mod": nrm(ks[5], (DEPTH, N_MOD * d), 0.01),
        "norm1_g": 1.0 + nrm(ks[6], (DEPTH, d), 0.05),
        "norm2_g": 1.0 + nrm(ks[7], (DEPTH, d), 0.05),
        "hg_w_in": nrm(ks[8], (n_a, d, N_HG_PROJ * d), d ** -0.5),
        "hg_lb_fwd": nrm(ks[9], (n_a + 1, d), 0.5),
        "hg_lb_bwd": nrm(ks[10], (n_a + 1, d), 0.5),
        "hg_gnorm": 1.0 + nrm(ks[11], (n_a, HG_DV), 0.05),
        "hg_w_out": nrm(ks[12], (n_a, d, d), d ** -0.5),
        "pool_w_in": nrm(ks[13], (n_b, d, d), d ** -0.5),
        "pool_w_grp": nrm(ks[14], (n_b, POOL_GROUPS, POOL_GDIM, POOL_GDIM), POOL_GDIM ** -0.5),
        "pool_scale": 1.0 + nrm(ks[15], (n_b, d), 0.1),
        "pool_w_out": nrm(ks[16], (n_b, d, d), d ** -0.5),
        "router_w": nrm(ks[17], (d, N_EXPERTS), d ** -0.5),
        "router_b": nrm(ks[18], (N_EXPERTS,), 0.01),
        "moe_w_gate": nrm(ks[19], (DEPTH, N_EXPERTS, d, D_EXPERT), d ** -0.5),
        "moe_w_up": nrm(ks[20], (DEPTH, N_EXPERTS, d, D_EXPERT), d ** -0.5),
        "moe_w_down": nrm(ks[21], (DEPTH, N_EXPERTS, D_EXPERT, d), D_EXPERT ** -0.5),
        "final_g": 1.0 + nrm(ks[22], (d,), 0.05),
    }


def reference(x, c, ctx, c_ctx, w_mod, b_mod, norm1_g, norm2_g, hg_w_in, hg_lb_fwd, hg_lb_bwd, hg_gnorm,
              hg_w_out, pool_w_in, pool_w_grp, pool_scale, pool_w_out, router_w, router_b, moe_w_gate,
              moe_w_up, moe_w_down, final_g):
    x_lat, x_ctx = x, ctx
    d = x.shape[-1]
    for i in range(DEPTH):
        need_ctx = i < DEPTH - 1
        slot = i // N_MIXERS
        mod = jax.nn.silu(c) @ w_mod[i] + b_mod[i]
        sh1, sc1, g1, sh2, sc2, g2 = jnp.split(mod[:, None, :], N_MOD, axis=-1)
        mod_c = jax.nn.silu(c_ctx) @ w_mod[i] + b_mod[i]
        csh1, csc1, cg1, csh2, csc2, cg2 = jnp.split(mod_c, N_MOD, axis=-1)
        h_lat = modulate(x_lat, norm1_g[i], sh1, sc1)
        h_ctx = modulate(x_ctx, norm1_g[i], csh1, csc1)
        if i % N_MIXERS == 0:
            y_lat, y_ctx = hgrn2_mixer(h_lat, h_ctx, hg_w_in[slot], hg_lb_fwd, hg_lb_bwd, hg_gnorm[slot],
                                       hg_w_out[slot], slot, need_ctx)
        else:
            y_lat, y_ctx = pool_mixer(h_lat, h_ctx, pool_w_in[slot], pool_w_grp[slot], pool_scale[slot],
                                      pool_w_out[slot], need_ctx)
        x_lat = x_lat + g1 * y_lat
        h2_lat = modulate(x_lat, norm2_g[i], sh2, sc2).reshape(-1, d)
        n_lat = h2_lat.shape[0]
        if need_ctx:
            x_ctx = x_ctx + cg1 * y_ctx
            h2_ctx = modulate(x_ctx, norm2_g[i], csh2, csc2).reshape(-1, d)
            ff = moe_ffn(jnp.concatenate([h2_lat, h2_ctx], axis=0), router_w, router_b,
                         moe_w_gate[i], moe_w_up[i], moe_w_down[i])
            x_ctx = x_ctx + cg2 * ff[n_lat:].reshape(x_ctx.shape)
            ff_lat = ff[:n_lat]
        else:
            ff_lat = moe_ffn(h2_lat, router_w, router_b, moe_w_gate[i], moe_w_up[i], moe_w_down[i])
        x_lat = x_lat + g2 * ff_lat.reshape(x_lat.shape)
    return rms_norm(x_lat, final_g)
```

```python
import functools

import jax
import jax.numpy as jnp
from jax import lax
from jax.experimental import pallas as pl
from jax.experimental.pallas import tpu as pltpu

F32 = jnp.float32
BF16 = jnp.bfloat16

EPS = 1e-6
N_MOD = 6
HG_HEADS = 8
HG_DK = 128
HG_CHUNK = 64
N_HG_PROJ = 5
POOL_WINDOWS = (2, 4, 8, 16)
GRID_W = 64
GRID_SHIFT = GRID_W.bit_length() - 1
assert 1 << GRID_SHIFT == GRID_W
N_EXPERTS = 16
N_EXPERT_GROUPS = 4
EXPERTS_PER_GROUP = N_EXPERTS // N_EXPERT_GROUPS
D_EXPERT = 256

LANES = 128
MOD_ROWS = 8
VMEM_LIMIT = 56 * 1024 * 1024


def _cparams(sem, vmem=VMEM_LIMIT):
    return pltpu.CompilerParams(dimension_semantics=sem, vmem_limit_bytes=vmem)


def _sigmoid_pair(z):
    e = jnp.exp(-jnp.abs(z))
    r = 1.0 / (1.0 + e)
    er = e * r
    pos = z >= 0
    return jnp.where(pos, r, er), jnp.where(pos, er, r)


def _silu(z):
    return z * _sigmoid_pair(z)[0]


def _mod_kernel(c_ref, w_ref, b_ref, o_ref):
    a = _silu(c_ref[...])
    o_ref[0] = jnp.dot(a, w_ref[0], preferred_element_type=F32,
                       precision=lax.Precision.HIGHEST) + b_ref[0]


def _mod_call(cc, w_mod, b_mod):
    depth, d, n = w_mod.shape
    tn = 1024
    return pl.pallas_call(
        _mod_kernel,
        grid=(depth, n // tn),
        in_specs=[
            pl.BlockSpec((MOD_ROWS, d), lambda i, j: (0, 0)),
            pl.BlockSpec((1, d, tn), lambda i, j: (i, 0, j)),
            pl.BlockSpec((1, 1, tn), lambda i, j: (i, 0, j)),
        ],
        out_specs=pl.BlockSpec((1, MOD_ROWS, tn), lambda i, j: (i, 0, j)),
        out_shape=jax.ShapeDtypeStruct((depth, MOD_ROWS, n), F32),
        compiler_params=_cparams(("parallel", "parallel")),
        name="mod_proj",
    )(cc, w_mod, b_mod.reshape(depth, 1, n))


def _modulate(x, g, shift, scale):
    ms = jnp.mean(x * x, axis=-1, keepdims=True)
    return (x * lax.rsqrt(ms + EPS) * g) * (1.0 + scale) + shift


def _norm_kernel(x_ref, g_ref, mod_ref, o_ref):
    o_ref[0] = _modulate(x_ref[0], g_ref[...], mod_ref[0, 0:1, :], mod_ref[0, 1:2, :]).astype(BF16)


def _norm_call(x, g, mod, tile):
    b, t, d = x.shape
    return pl.pallas_call(
        _norm_kernel,
        grid=(b, t // tile),
        in_specs=[
            pl.BlockSpec((1, tile, d), lambda i, j: (i, j, 0)),
            pl.BlockSpec((1, d), lambda i, j: (0, 0)),
            pl.BlockSpec((1, N_MOD, d), lambda i, j: (i, 0, 0)),
        ],
        out_specs=pl.BlockSpec((1, tile, d), lambda i, j: (i, j, 0)),
        out_shape=jax.ShapeDtypeStruct((b, t, d), BF16),
        compiler_params=_cparams(("parallel", "parallel")),
        name="norm1",
    )(x, g.reshape(1, d), mod)


HG_BLOCK = 256
HG_CPB = HG_BLOCK // HG_CHUNK


def _chunk_prefix(x, row):
    for d in (1, 2, 4, 8, 16, 32):
        x = x + jnp.where(row >= d, pltpu.roll(x, d, axis=0), 0.0)
    return x


def _chunk_suffix(x, row):
    n = x.shape[0]
    for d in (1, 2, 4, 8, 16, 32):
        x = x + jnp.where(row < HG_CHUNK - d, pltpu.roll(x, n - d, axis=0), 0.0)
    return x


def _lower_bound(lb_ref, slot):
    rows = [lb_ref[j, 0] for j in range(lb_ref.shape[0])]
    m = functools.reduce(jnp.maximum, rows)
    es = [jnp.exp(r - m) for r in rows]
    return sum(es[:slot + 1]) / sum(es)


def _hgrn_kernel(hc_ref, hl_ref, w_ref, lbf_ref, lbb_ref, gn_ref, y_ref,
                 oacc, qif, qib, gbuf, kvf, kvb, decf, decb, *, slot, n_ctx_chunks, n_lat_chunks):
    lb_f = _lower_bound(lbf_ref, slot)
    lb_b = _lower_bound(lbb_ref, slot)
    row = lax.broadcasted_iota(jnp.int32, (HG_BLOCK, HG_DK), 0) & (HG_CHUNK - 1)
    ci = lax.broadcasted_iota(jnp.int32, (HG_CPB, HG_CHUNK, HG_CHUNK), 1)
    si = lax.broadcasted_iota(jnp.int32, (HG_CPB, HG_CHUNK, HG_CHUNK), 2)
    w = w_ref[0]

    def c3(t):
        return t.reshape(HG_CPB, HG_CHUNK, HG_DK)

    def direction(qs3, v3b, z, lb, fwd):
        sig, nsig = _sigmoid_pair(z)
        f = lb + (1.0 - lb) * sig
        k3 = c3((1.0 - lb) * nsig)
        lf = jnp.log(f)
        if fwd:
            cum = c3(_chunk_prefix(lf, row))
            ref = cum[:, HG_CHUNK // 2 - 1:HG_CHUNK // 2, :]
            last = cum[:, HG_CHUNK - 1:HG_CHUNK, :]
        else:
            cum = c3(_chunk_suffix(lf, row))
            ref = cum[:, HG_CHUNK // 2:HG_CHUNK // 2 + 1, :]
            last = cum[:, 0:1, :]
        kd = k3 * jnp.exp(ref - cum)
        kl = kd * jnp.exp(last - ref)
        dec = jnp.exp(last)
        kvt = jnp.einsum('ncv,nck->nvk', v3b, kl.astype(BF16), preferred_element_type=F32)
        if qs3 is None:
            return None, None, kvt, dec
        qd = qs3 * jnp.exp(cum - ref)
        qi = qd * jnp.exp(ref)
        sc = jnp.einsum('nck,nsk->ncs', qd.astype(BF16), kd.astype(BF16), preferred_element_type=F32)
        sc = jnp.where((ci >= si) if fwd else (ci <= si), sc, 0.0)
        intra = jnp.einsum('ncs,nsv->ncv', sc.astype(BF16), v3b, preferred_element_type=F32)
        return intra, qi, kvt, dec

    def block(hrows, chunk0, lat_row0):
        p = jnp.dot(hrows, w, preferred_element_type=F32)
        v3b = c3(p[:, HG_DK:2 * HG_DK]).astype(BF16)
        zf = p[:, 2 * HG_DK:3 * HG_DK]
        zb = p[:, 3 * HG_DK:4 * HG_DK]
        if lat_row0 is None:
            qs3 = None
        else:
            qs3 = c3(_silu(p[:, 0:HG_DK]))
        in_f, qi_f, kv_f, dec_f = direction(qs3, v3b, zf, lb_f, True)
        in_b, qi_b, kv_b, dec_b = direction(qs3, v3b, zb, lb_b, False)
        kvf[pl.ds(chunk0, HG_CPB)] = kv_f
        kvb[pl.ds(chunk0, HG_CPB)] = kv_b
        decf[pl.ds(chunk0, HG_CPB)] = dec_f
        decb[pl.ds(chunk0, HG_CPB)] = dec_b
        if lat_row0 is not None:
            rows = pl.ds(lat_row0, HG_BLOCK)
            oacc[rows, :] = (in_f + in_b).reshape(HG_BLOCK, HG_DK)
            qif[rows, :] = qi_f.reshape(HG_BLOCK, HG_DK).astype(BF16)
            qib[rows, :] = qi_b.reshape(HG_BLOCK, HG_DK).astype(BF16)
            gbuf[rows, :] = p[:, 4 * HG_DK:5 * HG_DK].astype(BF16)

    for i in range(n_ctx_chunks // HG_CPB):
        block(hc_ref[0, i * HG_BLOCK:(i + 1) * HG_BLOCK, :], i * HG_CPB, None)

    def lat_block(i, carry):
        r0 = pl.multiple_of(i * HG_BLOCK, HG_BLOCK)
        block(hl_ref[0, pl.ds(r0, HG_BLOCK), :], n_ctx_chunks + i * HG_CPB, r0)
        return carry

    lax.fori_loop(0, n_lat_chunks // HG_CPB, lat_block, 0)

    def advance(s, kv_ref, dec_ref, n):
        return dec_ref[n] * s + kv_ref[n]

    def inter(s, qi_ref, j):
        rows = pl.ds(pl.multiple_of(j * HG_CHUNK, HG_CHUNK), HG_CHUNK)
        return rows, lax.dot_general(qi_ref[rows, :], s.astype(BF16), (((1,), (1,)), ((), ())),
                                     preferred_element_type=F32)

    s = jnp.zeros((HG_DK, HG_DK), F32)
    for n in range(n_ctx_chunks):
        s = advance(s, kvf, decf, n)

    def fwd_step(j, s):
        rows, o = inter(s, qif, j)
        oacc[rows, :] = oacc[rows, :] + o
        return advance(s, kvf, decf, n_ctx_chunks + j)

    lax.fori_loop(0, n_lat_chunks, fwd_step, s)

    s = jnp.zeros((HG_DK, HG_DK), F32)
    for n in reversed(range(n_ctx_chunks)):
        s = advance(s, kvb, decb, n)
    gn = gn_ref[...]

    def bwd_step(t, s):
        j = n_lat_chunks - 1 - t
        rows, o = inter(s, qib, j)
        o = oacc[rows, :] + o
        ms = jnp.mean(o * o, axis=-1, keepdims=True)
        o = o * lax.rsqrt(ms + EPS) * gn
        y_ref[0, rows, :] = (o * _silu(gbuf[rows, :].astype(F32))).astype(BF16)
        return advance(s, kvb, decb, n_ctx_chunks + j)

    lax.fori_loop(0, n_lat_chunks, bwd_step, s)


def _hgrn_call(h_ctx, h_lat, w_heads, lb_fwd, lb_bwd, gnorm, slot):
    b, lc, d = h_ctx.shape
    ll = h_lat.shape[1]
    nrow = lb_fwd.shape[0]
    ncc, nlc = lc // HG_CHUNK, ll // HG_CHUNK
    kern = functools.partial(_hgrn_kernel, slot=slot, n_ctx_chunks=ncc, n_lat_chunks=nlc)
    lb_spec = pl.BlockSpec((nrow, 1, 1, HG_DK), lambda i, h: (0, h, 0, 0))
    return pl.pallas_call(
        kern,
        grid=(b, HG_HEADS),
        in_specs=[
            pl.BlockSpec((1, lc, d), lambda i, h: (i, 0, 0)),
            pl.BlockSpec((1, ll, d), lambda i, h: (i, 0, 0)),
            pl.BlockSpec((1, d, N_HG_PROJ * HG_DK), lambda i, h: (h, 0, 0)),
            lb_spec, lb_spec,
            pl.BlockSpec((1, HG_DK), lambda i, h: (0, 0)),
        ],
        out_specs=pl.BlockSpec((1, ll, HG_DK), lambda i, h: (i, 0, h)),
        out_shape=jax.ShapeDtypeStruct((b, ll, d), BF16),
        scratch_shapes=[
            pltpu.VMEM((ll, HG_DK), F32),
            pltpu.VMEM((ll, HG_DK), BF16),
            pltpu.VMEM((ll, HG_DK), BF16),
            pltpu.VMEM((ll, HG_DK), BF16),
            pltpu.VMEM((ncc + nlc, HG_DK, HG_DK), F32),
            pltpu.VMEM((ncc + nlc, HG_DK, HG_DK), F32),
            pltpu.VMEM((ncc + nlc, 1, HG_DK), F32),
            pltpu.VMEM((ncc + nlc, 1, HG_DK), F32),
        ],
        compiler_params=_cparams(("parallel", "arbitrary")),
        name="hgrn2",
    )(h_ctx, h_lat, w_heads,
      lb_fwd.reshape(nrow, HG_HEADS, 1, HG_DK), lb_bwd.reshape(nrow, HG_HEADS, 1, HG_DK),
      gnorm.reshape(1, HG_DK))


POOL_STRIP = 8
POOL_PAD = 8
POOL_BAND = 256


def _pool_group(h_ref, win_ref, wg_ref, ps_ref, o_ref, upad, *, win, rows):
    gd = win_ref.shape[1]
    half = win // 2
    u = jnp.dot(h_ref[0], win_ref[...], preferred_element_type=F32)
    zeros = jnp.zeros((POOL_PAD, GRID_W, gd), F32)
    upad[0:POOL_PAD] = zeros
    upad[POOL_PAD + rows:POOL_PAD + rows + POOL_PAD] = zeros
    upad[POOL_PAD:POOL_PAD + rows] = u.reshape(rows, GRID_W, gd)

    tok = POOL_STRIP * GRID_W
    bi = lax.broadcasted_iota(jnp.int32, (POOL_BAND, POOL_BAND), 0)
    bj = lax.broadcasted_iota(jnp.int32, (POOL_BAND, POOL_BAND), 1)
    lo_c = (bi & (GRID_W - 1)) - half
    cj = bj & (GRID_W - 1)
    band = jnp.where(bi >> GRID_SHIFT == bj >> GRID_SHIFT, 1.0, 0.0)
    band = jnp.where(cj >= lo_c, band, 0.0)
    band = jnp.where(cj < lo_c + win, band, 0.0).astype(BF16)
    t = lax.broadcasted_iota(jnp.int32, (tok, gd), 0)
    col = t & (GRID_W - 1)
    cnt_c = jnp.minimum(col - half + win, GRID_W) - jnp.maximum(col - half, 0)
    wg = wg_ref[0]
    ps = ps_ref[...]

    def strip(i, carry):
        r0 = i * POOL_STRIP
        slab = upad[pl.ds(r0 + POOL_PAD - half, POOL_STRIP + win - 1)]
        span = 1
        while span < win:
            n = slab.shape[0] - span
            slab = slab[0:n] + slab[span:span + n]
            span *= 2
        rs = slab.reshape(tok, gd)
        hi = rs.astype(BF16)
        lo = (rs - hi.astype(F32)).astype(BF16)
        parts = []
        for k in range(tok // POOL_BAND):
            sl = slice(k * POOL_BAND, (k + 1) * POOL_BAND)
            parts.append(jnp.dot(band, hi[sl], preferred_element_type=F32)
                         + jnp.dot(band, lo[sl], preferred_element_type=F32))
        box = jnp.concatenate(parts, axis=0)
        r = r0 + (t >> GRID_SHIFT)
        cnt_r = jnp.minimum(r - half + win, rows) - jnp.maximum(r - half, 0)
        mean = box / (cnt_r * cnt_c).astype(F32)
        ug = upad[pl.ds(r0 + POOL_PAD, POOL_STRIP)].reshape(tok, gd)
        z = jnp.dot((mean - ug).astype(BF16), wg, preferred_element_type=F32) * ps
        o_ref[0, pl.ds(pl.multiple_of(i * tok, tok), tok), :] = z.astype(BF16)
        return carry

    lax.fori_loop(0, rows // POOL_STRIP, strip, 0)


def _pool_kernel(h_ref, win_ref, wg_ref, ps_ref, o_ref, upad, *, rows):
    g = pl.program_id(1)
    for gi, win in enumerate(POOL_WINDOWS):
        @pl.when(g == gi)
        def _(win=win):
            _pool_group(h_ref, win_ref, wg_ref, ps_ref, o_ref, upad, win=win, rows=rows)


def _pool_call(h, w_in, w_grp, p_scale):
    b, l, d = h.shape
    ng = len(POOL_WINDOWS)
    gd = d // ng
    rows = l // GRID_W
    return pl.pallas_call(
        functools.partial(_pool_kernel, rows=rows),
        grid=(b, ng),
        in_specs=[
            pl.BlockSpec((1, l, d), lambda i, g: (i, 0, 0)),
            pl.BlockSpec((d, gd), lambda i, g: (0, g)),
            pl.BlockSpec((1, gd, gd), lambda i, g: (g, 0, 0)),
            pl.BlockSpec((1, gd), lambda i, g: (0, g)),
        ],
        out_specs=pl.BlockSpec((1, l, gd), lambda i, g: (i, 0, g)),
        out_shape=jax.ShapeDtypeStruct((b, l, d), BF16),
        scratch_shapes=[pltpu.VMEM((rows + 2 * POOL_PAD, GRID_W, gd), F32)],
        compiler_params=_cparams(("parallel", "arbitrary")),
        name="pool_mix",
    )(h, w_in, w_grp, p_scale.reshape(1, d))


def _route(sel, s):
    keep = []
    gsum = []
    for g in range(N_EXPERT_GROUPS):
        a = sel[g * EXPERTS_PER_GROUP:(g + 1) * EXPERTS_PER_GROUP]
        beaten = [jnp.zeros_like(a[0]) for _ in a]
        for i in range(EXPERTS_PER_GROUP):
            for j in range(i + 1, EXPERTS_PER_GROUP):
                ge = jnp.where(a[i] >= a[j], 1.0, 0.0)
                beaten[j] = beaten[j] + ge
                beaten[i] = beaten[i] + (1.0 - ge)
        kg = [jnp.where(bt < 1.5, 1.0, 0.0) for bt in beaten]
        keep.append(kg)
        gsum.append(sum(k * x for k, x in zip(kg, a)))
    out = []
    picked = []
    for g in range(N_EXPERT_GROUPS):
        better = jnp.zeros_like(gsum[0])
        for o in range(N_EXPERT_GROUPS):
            if o < g:
                better = better + jnp.where(gsum[o] >= gsum[g], 1.0, 0.0)
            elif o > g:
                better = better + jnp.where(gsum[o] > gsum[g], 1.0, 0.0)
        best = jnp.where(better < 0.5, 1.0, 0.0)
        for i in range(EXPERTS_PER_GROUP):
            picked.append(best * keep[g][i] * s[g * EXPERTS_PER_GROUP + i])
    den = sum(picked)
    return [p / den for p in picked]


def _post_kernel(x_ref, y_ref, w_ref, mod_ref, g_ref, rwh_ref, rwl_ref, rb_ref,
                 xo_ref, h2_ref, cb_ref):
    yw = jnp.dot(y_ref[...], w_ref[...], preferred_element_type=F32)
    xn = x_ref[...] + mod_ref[0, 2:3, :] * yw
    xo_ref[...] = xn
    h2 = _modulate(xn, g_ref[...], mod_ref[0, 3:4, :], mod_ref[0, 4:5, :])
    hi = h2.astype(BF16)
    h2_ref[...] = hi
    lo = (h2 - hi.astype(F32)).astype(BF16)
    nt = (((1,), (1,)), ((), ()))
    rwh = rwh_ref[...]
    logits = (lax.dot_general(rwh, hi, nt, preferred_element_type=F32)
              + lax.dot_general(rwh, lo, nt, preferred_element_type=F32)
              + lax.dot_general(rwl_ref[...], hi, nt, preferred_element_type=F32))
    s = _sigmoid_pair(logits)[0]
    sel = s + rb_ref[...]
    comb = _route([sel[e:e + 1, :] for e in range(N_EXPERTS)],
                  [s[e:e + 1, :] for e in range(N_EXPERTS)])
    t = logits.shape[1]
    comb_t = jnp.concatenate(comb + [jnp.zeros((LANES - N_EXPERTS, t), F32)], axis=0)
    cb_ref[...] = comb_t.T


def _post_call(x, y, w, mod, g, router_w, router_b, tokens_per_batch, tile=512):
    n, d = x.shape
    per_b = tokens_per_batch // tile
    rw_t = router_w.T
    rwh = rw_t.astype(BF16)
    rwl = (rw_t - rwh.astype(F32)).astype(BF16)
    row = lambda i: (i, 0)
    fixed = lambda i: (0, 0)
    return pl.pallas_call(
        _post_kernel,
        grid=(n // tile,),
        in_specs=[
            pl.BlockSpec((tile, d), row),
            pl.BlockSpec((tile, d), row),
            pl.BlockSpec((d, d), fixed),
            pl.BlockSpec((1, N_MOD, d), lambda i: (i // per_b, 0, 0)),
            pl.BlockSpec((1, d), fixed),
            pl.BlockSpec((N_EXPERTS, d), fixed),
            pl.BlockSpec((N_EXPERTS, d), fixed),
            pl.BlockSpec((N_EXPERTS, 1), fixed),
        ],
        out_specs=[
            pl.BlockSpec((tile, d), row),
            pl.BlockSpec((tile, d), row),
            pl.BlockSpec((tile, LANES), row),
        ],
        out_shape=[
            jax.ShapeDtypeStruct((n, d), F32),
            jax.ShapeDtypeStruct((n, d), BF16),
            jax.ShapeDtypeStruct((n, LANES), F32),
        ],
        compiler_params=_cparams(("parallel",)),
        name="mixer_out_router",
    )(x, y, w, mod, g.reshape(1, d), rwh, rwl, router_b.reshape(N_EXPERTS, 1))


def _moe_kernel(h_ref, cb_ref, wg_ref, wu_ref, wd_ref, x_ref, mod_ref, fg_ref, o_ref, acc,
                *, final_norm):
    e = pl.program_id(1)

    @pl.when(e == 0)
    def _():
        acc[...] = jnp.zeros_like(acc)

    h = h_ref[...]
    a = _silu(jnp.dot(h, wg_ref[0], preferred_element_type=F32)) \
        * jnp.dot(h, wu_ref[0], preferred_element_type=F32)
    cb = cb_ref[...]
    lane = lax.broadcasted_iota(jnp.int32, cb.shape, 1)
    ce = jnp.sum(jnp.where(lane == e, cb, 0.0), axis=1, keepdims=True)
    acc[...] += jnp.dot((a * ce).astype(BF16), wd_ref[0], preferred_element_type=F32)

    @pl.when(e == N_EXPERTS - 1)
    def _():
        xn = x_ref[...] + mod_ref[0, 5:6, :] * acc[...]
        if final_norm:
            ms = jnp.mean(xn * xn, axis=-1, keepdims=True)
            xn = xn * lax.rsqrt(ms + EPS) * fg_ref[...]
        o_ref[...] = xn


def _moe_call(h2, comb, wg, wu, wd, x, mod, final_g, tokens_per_batch, final_norm, tile=1024):
    n, d = x.shape
    per_b = tokens_per_batch // tile
    row = lambda i, e: (i, 0)
    return pl.pallas_call(
        functools.partial(_moe_kernel, final_norm=final_norm),
        grid=(n // tile, N_EXPERTS),
        in_specs=[
            pl.BlockSpec((tile, d), row),
            pl.BlockSpec((tile, LANES), row),
            pl.BlockSpec((1, d, D_EXPERT), lambda i, e: (e, 0, 0)),
            pl.BlockSpec((1, d, D_EXPERT), lambda i, e: (e, 0, 0)),
            pl.BlockSpec((1, D_EXPERT, d), lambda i, e: (e, 0, 0)),
            pl.BlockSpec((tile, d), row),
            pl.BlockSpec((1, N_MOD, d), lambda i, e: (i // per_b, 0, 0)),
            pl.BlockSpec((1, d), lambda i, e: (0, 0)),
        ],
        out_specs=pl.BlockSpec((tile, d), row),
        out_shape=jax.ShapeDtypeStruct((n, d), F32),
        scratch_shapes=[pltpu.VMEM((tile, d), F32)],
        compiler_params=_cparams(("parallel", "arbitrary")),
        name="moe_ffn",
    )(h2, comb, wg, wu, wd, x, mod, final_g.reshape(1, d))


def kernel(x, c, ctx, c_ctx, w_mod, b_mod, norm1_g, norm2_g, hg_w_in, hg_lb_fwd, hg_lb_bwd, hg_gnorm,
           hg_w_out, pool_w_in, pool_w_grp, pool_scale, pool_w_out, router_w, router_b, moe_w_gate,
           moe_w_up, moe_w_down, final_g):
    b, l, d = x.shape
    depth = w_mod.shape[0]
    n_mixers = 2

    cc = jnp.concatenate([c, c_ctx[None, :], jnp.zeros((MOD_ROWS - b - 1, d), F32)], axis=0)
    mods = _mod_call(cc, w_mod, b_mod)

    x_lat = x.reshape(b * l, d)
    for i in range(depth):
        slot = i // n_mixers
        mod_lat = mods[i, :b].reshape(b, N_MOD, d)
        h_lat = _norm_call(x_lat.reshape(b, l, d), norm1_g[i], mod_lat, 512)
        if i % n_mixers == 0:
            mod_ctx = jnp.broadcast_to(mods[i, b].reshape(1, N_MOD, d), (b, N_MOD, d))
            h_ctx = _norm_call(ctx, norm1_g[i], mod_ctx, ctx.shape[1])
            w_heads = (hg_w_in[slot].reshape(d, N_HG_PROJ, HG_HEADS, HG_DK)
                       .transpose(2, 0, 1, 3).reshape(HG_HEADS, d, N_HG_PROJ * HG_DK).astype(BF16))
            y = _hgrn_call(h_ctx, h_lat, w_heads, hg_lb_fwd, hg_lb_bwd, hg_gnorm[slot], slot)
            w_out = hg_w_out[slot]
        else:
            y = _pool_call(h_lat, pool_w_in[slot].astype(BF16), pool_w_grp[slot].astype(BF16),
                           pool_scale[slot])
            w_out = pool_w_out[slot]
        x_lat, h2, comb = _post_call(x_lat, y.reshape(b * l, d), w_out.astype(BF16), mod_lat,
                                     norm2_g[i], router_w, router_b, l)
        x_lat = _moe_call(h2, comb, moe_w_gate[i].astype(BF16), moe_w_up[i].astype(BF16),
                          moe_w_down[i].astype(BF16), x_lat, mod_lat, final_g, l,
                          final_norm=(i == depth - 1))
    return x_lat.reshape(b, l, d)
```

```python
import functools

import jax
import jax.numpy as jnp
from jax import lax
from jax.experimental import pallas as pl
from jax.experimental.pallas import tpu as pltpu

F32 = jnp.float32
BF16 = jnp.bfloat16

EPS = 1e-6
N_MOD = 6
HG_HEADS = 8
HG_DK = 128
HG_CHUNK = 64
N_HG_PROJ = 5
POOL_WINDOWS = (2, 4, 8, 16)
GRID_W = 64
GRID_SHIFT = GRID_W.bit_length() - 1
assert 1 << GRID_SHIFT == GRID_W
N_EXPERTS = 16
N_EXPERT_GROUPS = 4
EXPERTS_PER_GROUP = N_EXPERTS // N_EXPERT_GROUPS
D_EXPERT = 256

LANES = 128
MOD_ROWS = 8
VMEM_LIMIT = 56 * 1024 * 1024


def _cparams(sem, vmem=VMEM_LIMIT):
    return pltpu.CompilerParams(dimension_semantics=sem, vmem_limit_bytes=vmem)


def _sigmoid_pair(z):
    e = jnp.exp(-jnp.abs(z))
    r = 1.0 / (1.0 + e)
    er = e * r
    pos = z >= 0
    return jnp.where(pos, r, er), jnp.where(pos, er, r)


def _silu(z):
    return z * _sigmoid_pair(z)[0]


def _sigmoid_tanh(z):
    return 0.5 * jnp.tanh(0.5 * z) + 0.5


def _silu_tanh(z):
    return z * _sigmoid_tanh(z)


def _mod_kernel(c_ref, w_ref, b_ref, o_ref):
    a = _silu(c_ref[...])
    o_ref[0] = jnp.dot(a, w_ref[0], preferred_element_type=F32,
                       precision=lax.Precision.HIGHEST) + b_ref[0]


def _mod_call(cc, w_mod, b_mod):
    depth, d, n = w_mod.shape
    tn = 1024
    return pl.pallas_call(
        _mod_kernel,
        grid=(depth, n // tn),
        in_specs=[
            pl.BlockSpec((MOD_ROWS, d), lambda i, j: (0, 0)),
            pl.BlockSpec((1, d, tn), lambda i, j: (i, 0, j)),
            pl.BlockSpec((1, 1, tn), lambda i, j: (i, 0, j)),
        ],
        out_specs=pl.BlockSpec((1, MOD_ROWS, tn), lambda i, j: (i, 0, j)),
        out_shape=jax.ShapeDtypeStruct((depth, MOD_ROWS, n), F32),
        compiler_params=_cparams(("parallel", "parallel")),
        name="mod_proj",
    )(cc, w_mod, b_mod.reshape(depth, 1, n))


def _modulate(x, g, shift, scale):
    ms = jnp.mean(x * x, axis=-1, keepdims=True)
    return (x * lax.rsqrt(ms + EPS) * g) * (1.0 + scale) + shift


def _norm_kernel(x_ref, g_ref, mod_ref, *rest):
    h = _modulate(x_ref[0], g_ref[...], mod_ref[0, 0:1, :], mod_ref[0, 1:2, :]).astype(BF16)
    if len(rest) == 1:
        (o_ref,) = rest
    else:
        w_ref, o_ref, p_ref = rest
        p_ref[0] = jnp.dot(h, w_ref[...], preferred_element_type=F32).astype(BF16)
    o_ref[0] = h


def _norm_call(x, g, mod, tile, w_proj=None):
    b, t, d = x.shape
    tok = pl.BlockSpec((1, tile, d), lambda i, j: (i, j, 0))
    in_specs = [tok, pl.BlockSpec((1, d), lambda i, j: (0, 0)),
                pl.BlockSpec((1, N_MOD, d), lambda i, j: (i, 0, 0))]
    args = [x, g.reshape(1, d), mod]
    out_specs, out_shape = tok, jax.ShapeDtypeStruct((b, t, d), BF16)
    if w_proj is not None:
        in_specs.append(pl.BlockSpec((d, d), lambda i, j: (0, 0)))
        args.append(w_proj)
        out_specs, out_shape = [tok, tok], [out_shape, out_shape]
    return pl.pallas_call(
        _norm_kernel,
        grid=(b, t // tile),
        in_specs=in_specs,
        out_specs=out_specs,
        out_shape=out_shape,
        compiler_params=_cparams(("parallel", "parallel")),
        name="norm1",
    )(*args)


HG_BLOCK = 256
HG_CPB = HG_BLOCK // HG_CHUNK


def _chunk_prefix(x, row):
    for d in (1, 2, 4, 8, 16, 32):
        x = x + jnp.where(row >= d, pltpu.roll(x, d, axis=0), 0.0)
    return x


def _chunk_suffix(x, row):
    n = x.shape[0]
    for d in (1, 2, 4, 8, 16, 32):
        x = x + jnp.where(row < HG_CHUNK - d, pltpu.roll(x, n - d, axis=0), 0.0)
    return x


def _lower_bound(lb_ref, slot):
    rows = [lb_ref[j, 0] for j in range(lb_ref.shape[0])]
    m = functools.reduce(jnp.maximum, rows)
    es = [jnp.exp(r - m) for r in rows]
    return sum(es[:slot + 1]) / sum(es)


def _hgrn_kernel(hc_ref, hl_ref, g_ref, w_ref, lbf_ref, lbb_ref, gn_ref, y_ref,
                 oacc, qif, qib, kvf, kvb, decf, decb, spf, spb,
                 *, slot, n_ctx_chunks, n_lat_chunks):
    lb_f = _lower_bound(lbf_ref, slot)
    lb_b = _lower_bound(lbb_ref, slot)
    row = lax.broadcasted_iota(jnp.int32, (HG_BLOCK, HG_DK), 0) & (HG_CHUNK - 1)
    ci = lax.broadcasted_iota(jnp.int32, (HG_CPB, HG_CHUNK, HG_CHUNK), 1)
    si = lax.broadcasted_iota(jnp.int32, (HG_CPB, HG_CHUNK, HG_CHUNK), 2)

    def c3(t):
        return t.reshape(HG_CPB, HG_CHUNK, HG_DK)

    def direction(qs3, v3b, z, lb, fwd):
        sig = _sigmoid_tanh(z)
        f = lb + (1.0 - lb) * sig
        k3 = c3((1.0 - lb) * (1.0 - sig))
        lf = jnp.log(f)
        if fwd:
            cum = c3(_chunk_prefix(lf, row))
            ref = cum[:, HG_CHUNK // 2 - 1:HG_CHUNK // 2, :]
            last = cum[:, HG_CHUNK - 1:HG_CHUNK, :]
        else:
            cum = c3(_chunk_suffix(lf, row))
            ref = cum[:, HG_CHUNK // 2:HG_CHUNK // 2 + 1, :]
            last = cum[:, 0:1, :]
        dec = jnp.exp(last)
        if qs3 is None:
            kl = k3 * jnp.exp(last - cum)
        else:
            e1 = jnp.exp(cum - ref)
            qd = qs3 * e1
            qi = qd * jnp.exp(ref)
            kd = k3 * (1.0 / e1)
            kl = kd * jnp.exp(last - ref)
        kvt = jnp.einsum('ncv,nck->nvk', v3b, kl.astype(BF16), preferred_element_type=F32)
        if qs3 is None:
            return None, None, kvt, dec
        sc = jnp.einsum('nck,nsk->ncs', qd.astype(BF16), kd.astype(BF16), preferred_element_type=F32)
        sc = jnp.where((ci >= si) if fwd else (ci <= si), sc, 0.0)
        intra = jnp.einsum('ncs,nsv->ncv', sc.astype(BF16), v3b, preferred_element_type=F32)
        return intra, qi, kvt, dec

    def block(hrows, chunk0, lat_row0):
        p = jnp.dot(hrows, w_ref[0], preferred_element_type=F32)
        v3b = c3(p[:, HG_DK:2 * HG_DK]).astype(BF16)
        zf = p[:, 2 * HG_DK:3 * HG_DK]
        zb = p[:, 3 * HG_DK:4 * HG_DK]
        if lat_row0 is None:
            qs3 = None
        else:
            qs3 = c3(_silu_tanh(p[:, 0:HG_DK]))
        in_f, qi_f, kv_f, dec_f = direction(qs3, v3b, zf, lb_f, True)
        in_b, qi_b, kv_b, dec_b = direction(qs3, v3b, zb, lb_b, False)
        kvf[pl.ds(chunk0, HG_CPB)] = kv_f
        kvb[pl.ds(chunk0, HG_CPB)] = kv_b
        decf[pl.ds(chunk0, HG_CPB)] = dec_f
        decb[pl.ds(chunk0, HG_CPB)] = dec_b
        if lat_row0 is not None:
            rows = pl.ds(lat_row0, HG_BLOCK)
            oacc[rows, :] = (in_f + in_b).reshape(HG_BLOCK, HG_DK)
            qif[rows, :] = qi_f.reshape(HG_BLOCK, HG_DK).astype(BF16)
            qib[rows, :] = qi_b.reshape(HG_BLOCK, HG_DK).astype(BF16)

    for i in range(n_ctx_chunks // HG_CPB):
        block(hc_ref[0, i * HG_BLOCK:(i + 1) * HG_BLOCK, :], i * HG_CPB, None)

    n_blocks = n_lat_chunks // HG_CPB

    def lat_block(i, carry):
        r0 = pl.multiple_of(i * HG_BLOCK, HG_BLOCK)
        block(hl_ref[0, pl.ds(r0, HG_BLOCK), :], n_ctx_chunks + i * HG_CPB, r0)
        return carry

    lax.fori_loop(0, n_blocks, lat_block, 0, unroll=2)

    def advance(s, kv_ref, dec_ref, n):
        return dec_ref[n] * s + kv_ref[n]

    sf = jnp.zeros((HG_DK, HG_DK), F32)
    for n in range(n_ctx_chunks):
        sf = advance(sf, kvf, decf, n)
    sb = jnp.zeros((HG_DK, HG_DK), F32)
    for n in reversed(range(n_ctx_chunks)):
        sb = advance(sb, kvb, decb, n)

    def scan_step(t, carry):
        sf, sb = carry
        jb = n_lat_chunks - 1 - t
        spf[t] = sf.astype(BF16)
        spb[jb] = sb.astype(BF16)
        return (advance(sf, kvf, decf, n_ctx_chunks + t), advance(sb, kvb, decb, n_ctx_chunks + jb))

    lax.fori_loop(0, n_lat_chunks, scan_step, (sf, sb), unroll=2)

    gn = gn_ref[...]

    def readout(i, carry):
        rows = pl.ds(pl.multiple_of(i * HG_BLOCK, HG_BLOCK), HG_BLOCK)
        chunks = pl.ds(i * HG_CPB, HG_CPB)
        o = (c3(oacc[rows, :])
             + jnp.einsum('nck,nvk->ncv', c3(qif[rows, :]), spf[chunks], preferred_element_type=F32)
             + jnp.einsum('nck,nvk->ncv', c3(qib[rows, :]), spb[chunks], preferred_element_type=F32))
        o = o.reshape(HG_BLOCK, HG_DK)
        ms = jnp.mean(o * o, axis=-1, keepdims=True)
        o = o * lax.rsqrt(ms + EPS) * gn
        y_ref[0, rows, :] = (o * _silu_tanh(g_ref[0, rows, :].astype(F32))).astype(BF16)
        return carry

    lax.fori_loop(0, n_blocks, readout, 0, unroll=2)


def _hgrn_call(h_ctx, h_lat, g_lat, w_heads, lb_fwd, lb_bwd, gnorm, slot):
    b, lc, d = h_ctx.shape
    ll = h_lat.shape[1]
    nrow = lb_fwd.shape[0]
    ncc, nlc = lc // HG_CHUNK, ll // HG_CHUNK
    kern = functools.partial(_hgrn_kernel, slot=slot, n_ctx_chunks=ncc, n_lat_chunks=nlc)
    lb_spec = pl.BlockSpec((nrow, 1, 1, HG_DK), lambda i, h: (0, h, 0, 0))
    head_cols = pl.BlockSpec((1, ll, HG_DK), lambda i, h: (i, 0, h))
    return pl.pallas_call(
        kern,
        grid=(b, HG_HEADS),
        in_specs=[
            pl.BlockSpec((1, lc, d), lambda i, h: (i, 0, 0)),
            pl.BlockSpec((1, ll, d), lambda i, h: (i, 0, 0)),
            head_cols,
            pl.BlockSpec((1, d, w_heads.shape[2]), lambda i, h: (h, 0, 0)),
            lb_spec, lb_spec,
            pl.BlockSpec((1, HG_DK), lambda i, h: (0, 0)),
        ],
        out_specs=head_cols,
        out_shape=jax.ShapeDtypeStruct((b, ll, d), BF16),
        scratch_shapes=[
            pltpu.VMEM((ll, HG_DK), F32),
            pltpu.VMEM((ll, HG_DK), BF16),
            pltpu.VMEM((ll, HG_DK), BF16),
            pltpu.VMEM((ncc + nlc, HG_DK, HG_DK), F32),
            pltpu.VMEM((ncc + nlc, HG_DK, HG_DK), F32),
            pltpu.VMEM((ncc + nlc, 1, HG_DK), F32),
            pltpu.VMEM((ncc + nlc, 1, HG_DK), F32),
            pltpu.VMEM((nlc, HG_DK, HG_DK), BF16),
            pltpu.VMEM((nlc, HG_DK, HG_DK), BF16),
        ],
        compiler_params=_cparams(("parallel", "arbitrary")),
        name="hgrn2",
    )(h_ctx, h_lat, g_lat, w_heads,
      lb_fwd.reshape(nrow, HG_HEADS, 1, HG_DK), lb_bwd.reshape(nrow, HG_HEADS, 1, HG_DK),
      gnorm.reshape(1, HG_DK))


POOL_STRIP = 8
POOL_PAD = 8
POOL_BAND = 256


def _pool_group(h_ref, win_ref, wg_ref, ps_ref, o_ref, upad, *, win, rows):
    gd = win_ref.shape[1]
    half = win // 2
    u = jnp.dot(h_ref[0], win_ref[...], preferred_element_type=F32)
    zeros = jnp.zeros((POOL_PAD, GRID_W, gd), F32)
    upad[0:POOL_PAD] = zeros
    upad[POOL_PAD + rows:POOL_PAD + rows + POOL_PAD] = zeros
    upad[POOL_PAD:POOL_PAD + rows] = u.reshape(rows, GRID_W, gd)

    tok = POOL_STRIP * GRID_W
    bi = lax.broadcasted_iota(jnp.int32, (POOL_BAND, POOL_BAND), 0)
    bj = lax.broadcasted_iota(jnp.int32, (POOL_BAND, POOL_BAND), 1)
    lo_c = (bi & (GRID_W - 1)) - half
    cj = bj & (GRID_W - 1)
    band = jnp.where(bi >> GRID_SHIFT == bj >> GRID_SHIFT, 1.0, 0.0)
    band = jnp.where(cj >= lo_c, band, 0.0)
    band = jnp.where(cj < lo_c + win, band, 0.0).astype(BF16)
    t = lax.broadcasted_iota(jnp.int32, (tok, gd), 0)
    col = t & (GRID_W - 1)
    cnt_c = jnp.minimum(col - half + win, GRID_W) - jnp.maximum(col - half, 0)
    wg = wg_ref[0]
    ps = ps_ref[...]

    def strip(i, carry):
        r0 = i * POOL_STRIP
        slab = upad[pl.ds(r0 + POOL_PAD - half, POOL_STRIP + win - 1)]
        span = 1
        while span < win:
            n = slab.shape[0] - span
            slab = slab[0:n] + slab[span:span + n]
            span *= 2
        rs = slab.reshape(tok, gd)
        hi = rs.astype(BF16)
        lo = (rs - hi.astype(F32)).astype(BF16)
        parts = []
        for k in range(tok // POOL_BAND):
            sl = slice(k * POOL_BAND, (k + 1) * POOL_BAND)
            parts.append(jnp.dot(band, hi[sl], preferred_element_type=F32)
                         + jnp.dot(band, lo[sl], preferred_element_type=F32))
        box = jnp.concatenate(parts, axis=0)
        r = r0 + (t >> GRID_SHIFT)
        cnt_r = jnp.minimum(r - half + win, rows) - jnp.maximum(r - half, 0)
        mean = box / (cnt_r * cnt_c).astype(F32)
        ug = upad[pl.ds(r0 + POOL_PAD, POOL_STRIP)].reshape(tok, gd)
        z = jnp.dot((mean - ug).astype(BF16), wg, preferred_element_type=F32) * ps
        o_ref[0, pl.ds(pl.multiple_of(i * tok, tok), tok), :] = z.astype(BF16)
        return carry

    lax.fori_loop(0, rows // POOL_STRIP, strip, 0)


def _pool_kernel(h_ref, win_ref, wg_ref, ps_ref, o_ref, upad, *, rows):
    g = pl.program_id(1)
    for gi, win in enumerate(POOL_WINDOWS):
        @pl.when(g == gi)
        def _(win=win):
            _pool_group(h_ref, win_ref, wg_ref, ps_ref, o_ref, upad, win=win, rows=rows)


def _pool_call(h, w_in, w_grp, p_scale):
    b, l, d = h.shape
    ng = len(POOL_WINDOWS)
    gd = d // ng
    rows = l // GRID_W
    return pl.pallas_call(
        functools.partial(_pool_kernel, rows=rows),
        grid=(b, ng),
        in_specs=[
            pl.BlockSpec((1, l, d), lambda i, g: (i, 0, 0)),
            pl.BlockSpec((d, gd), lambda i, g: (0, g)),
            pl.BlockSpec((1, gd, gd), lambda i, g: (g, 0, 0)),
            pl.BlockSpec((1, gd), lambda i, g: (0, g)),
        ],
        out_specs=pl.BlockSpec((1, l, gd), lambda i, g: (i, 0, g)),
        out_shape=jax.ShapeDtypeStruct((b, l, d), BF16),
        scratch_shapes=[pltpu.VMEM((rows + 2 * POOL_PAD, GRID_W, gd), F32)],
        compiler_params=_cparams(("parallel", "arbitrary")),
        name="pool_mix",
    )(h, w_in, w_grp, p_scale.reshape(1, d))


def _route(sel, s):
    keep = []
    gsum = []
    for g in range(N_EXPERT_GROUPS):
        a = sel[g * EXPERTS_PER_GROUP:(g + 1) * EXPERTS_PER_GROUP]
        beaten = [jnp.zeros_like(a[0]) for _ in a]
        for i in range(EXPERTS_PER_GROUP):
            for j in range(i + 1, EXPERTS_PER_GROUP):
                ge = jnp.where(a[i] >= a[j], 1.0, 0.0)
                beaten[j] = beaten[j] + ge
                beaten[i] = beaten[i] + (1.0 - ge)
        kg = [jnp.where(bt < 1.5, 1.0, 0.0) for bt in beaten]
        keep.append(kg)
        gsum.append(sum(k * x for k, x in zip(kg, a)))
    out = []
    picked = []
    for g in range(N_EXPERT_GROUPS):
        better = jnp.zeros_like(gsum[0])
        for o in range(N_EXPERT_GROUPS):
            if o < g:
                better = better + jnp.where(gsum[o] >= gsum[g], 1.0, 0.0)
            elif o > g:
                better = better + jnp.where(gsum[o] > gsum[g], 1.0, 0.0)
        best = jnp.where(better < 0.5, 1.0, 0.0)
        for i in range(EXPERTS_PER_GROUP):
            picked.append(best * keep[g][i] * s[g * EXPERTS_PER_GROUP + i])
    den = sum(picked)
    return [p / den for p in picked]


def _post_kernel(x_ref, y_ref, w_ref, mod_ref, g_ref, rwh_ref, rwl_ref, rb_ref,
                 xo_ref, h2_ref, cb_ref):
    yw = jnp.dot(y_ref[...], w_ref[...], preferred_element_type=F32)
    xn = x_ref[...] + mod_ref[0, 2:3, :] * yw
    xo_ref[...] = xn
    h2 = _modulate(xn, g_ref[...], mod_ref[0, 3:4, :], mod_ref[0, 4:5, :])
    hi = h2.astype(BF16)
    h2_ref[...] = hi
    lo = (h2 - hi.astype(F32)).astype(BF16)
    nt = (((1,), (1,)), ((), ()))
    rwh = rwh_ref[...]
    logits = (lax.dot_general(rwh, hi, nt, preferred_element_type=F32)
              + lax.dot_general(rwh, lo, nt, preferred_element_type=F32)
              + lax.dot_general(rwl_ref[...], hi, nt, preferred_element_type=F32))
    s = _sigmoid_pair(logits)[0]
    sel = s + rb_ref[...]
    comb = _route([sel[e:e + 1, :] for e in range(N_EXPERTS)],
                  [s[e:e + 1, :] for e in range(N_EXPERTS)])
    t = logits.shape[1]
    comb_t = jnp.concatenate(comb + [jnp.zeros((LANES - N_EXPERTS, t), F32)], axis=0)
    cb_ref[...] = comb_t.T


def _post_call(x, y, w, mod, g, router_w, router_b, tokens_per_batch, tile=512):
    n, d = x.shape
    per_b = tokens_per_batch // tile
    rw_t = router_w.T
    rwh = rw_t.astype(BF16)
    rwl = (rw_t - rwh.astype(F32)).astype(BF16)
    row = lambda i: (i, 0)
    fixed = lambda i: (0, 0)
    return pl.pallas_call(
        _post_kernel,
        grid=(n // tile,),
        in_specs=[
            pl.BlockSpec((tile, d), row),
            pl.BlockSpec((tile, d), row),
            pl.BlockSpec((d, d), fixed),
            pl.BlockSpec((1, N_MOD, d), lambda i: (i // per_b, 0, 0)),
            pl.BlockSpec((1, d), fixed),
            pl.BlockSpec((N_EXPERTS, d), fixed),
            pl.BlockSpec((N_EXPERTS, d), fixed),
            pl.BlockSpec((N_EXPERTS, 1), fixed),
        ],
        out_specs=[
            pl.BlockSpec((tile, d), row),
            pl.BlockSpec((tile, d), row),
            pl.BlockSpec((tile, LANES), row),
        ],
        out_shape=[
            jax.ShapeDtypeStruct((n, d), F32),
            jax.ShapeDtypeStruct((n, d), BF16),
            jax.ShapeDtypeStruct((n, LANES), F32),
        ],
        compiler_params=_cparams(("parallel",)),
        name="mixer_out_router",
    )(x, y, w, mod, g.reshape(1, d), rwh, rwl, router_b.reshape(N_EXPERTS, 1))


def _moe_kernel(h_ref, cb_ref, wg_ref, wu_ref, wd_ref, x_ref, mod_ref, fg_ref, o_ref, acc,
                *, final_norm):
    e = pl.program_id(1)

    @pl.when(e == 0)
    def _():
        acc[...] = jnp.zeros_like(acc)

    h = h_ref[...]
    a = _silu_tanh(jnp.dot(h, wg_ref[0], preferred_element_type=F32)) \
        * jnp.dot(h, wu_ref[0], preferred_element_type=F32)
    cb = cb_ref[...]
    lane = lax.broadcasted_iota(jnp.int32, cb.shape, 1)
    ce = jnp.sum(jnp.where(lane == e, cb, 0.0), axis=1, keepdims=True)
    acc[...] += jnp.dot((a * ce).astype(BF16), wd_ref[0], preferred_element_type=F32)

    @pl.when(e == N_EXPERTS - 1)
    def _():
        xn = x_ref[...] + mod_ref[0, 5:6, :] * acc[...]
        if final_norm:
            ms = jnp.mean(xn * xn, axis=-1, keepdims=True)
            xn = xn * lax.rsqrt(ms + EPS) * fg_ref[...]
        o_ref[...] = xn


def _moe_call(h2, comb, wg, wu, wd, x, mod, final_g, tokens_per_batch, final_norm, tile=1024):
    n, d = x.shape
    per_b = tokens_per_batch // tile
    row = lambda i, e: (i, 0)
    return pl.pallas_call(
        functools.partial(_moe_kernel, final_norm=final_norm),
        grid=(n // tile, N_EXPERTS),
        in_specs=[
            pl.BlockSpec((tile, d), row),
            pl.BlockSpec((tile, LANES), row),
            pl.BlockSpec((1, d, D_EXPERT), lambda i, e: (e, 0, 0)),
            pl.BlockSpec((1, d, D_EXPERT), lambda i, e: (e, 0, 0)),
            pl.BlockSpec((1, D_EXPERT, d), lambda i, e: (e, 0, 0)),
            pl.BlockSpec((tile, d), row),
            pl.BlockSpec((1, N_MOD, d), lambda i, e: (i // per_b, 0, 0)),
            pl.BlockSpec((1, d), lambda i, e: (0, 0)),
        ],
        out_specs=pl.BlockSpec((tile, d), row),
        out_shape=jax.ShapeDtypeStruct((n, d), F32),
        scratch_shapes=[pltpu.VMEM((tile, d), F32)],
        compiler_params=_cparams(("parallel", "arbitrary")),
        name="moe_ffn",
    )(h2, comb, wg, wu, wd, x, mod, final_g.reshape(1, d))


def kernel(x, c, ctx, c_ctx, w_mod, b_mod, norm1_g, norm2_g, hg_w_in, hg_lb_fwd, hg_lb_bwd, hg_gnorm,
           hg_w_out, pool_w_in, pool_w_grp, pool_scale, pool_w_out, router_w, router_b, moe_w_gate,
           moe_w_up, moe_w_down, final_g):
    b, l, d = x.shape
    depth = w_mod.shape[0]
    n_mixers = 2

    cc = jnp.concatenate([c, c_ctx[None, :], jnp.zeros((MOD_ROWS - b - 1, d), F32)], axis=0)
    mods = _mod_call(cc, w_mod, b_mod)

    x_lat = x.reshape(b * l, d)
    for i in range(depth):
        slot = i // n_mixers
        mod_lat = mods[i, :b].reshape(b, N_MOD, d)
        if i % n_mixers == 0:
            n_rec = N_HG_PROJ - 1
            w_rec = (hg_w_in[slot][:, :n_rec * d].reshape(d, n_rec, HG_HEADS, HG_DK)
                     .transpose(2, 0, 1, 3).reshape(HG_HEADS, d, n_rec * HG_DK).astype(BF16))
            w_gate = hg_w_in[slot][:, n_rec * d:].astype(BF16)
            h_lat, g_lat = _norm_call(x_lat.reshape(b, l, d), norm1_g[i], mod_lat, 512, w_gate)
            mod_ctx = jnp.broadcast_to(mods[i, b].reshape(1, N_MOD, d), (b, N_MOD, d))
            h_ctx = _norm_call(ctx, norm1_g[i], mod_ctx, ctx.shape[1])
            y = _hgrn_call(h_ctx, h_lat, g_lat, w_rec, hg_lb_fwd, hg_lb_bwd, hg_gnorm[slot], slot)
            w_out = hg_w_out[slot]
        else:
            h_lat = _norm_call(x_lat.reshape(b, l, d), norm1_g[i], mod_lat, 512)
            y = _pool_call(h_lat, pool_w_in[slot].astype(BF16), pool_w_grp[slot].astype(BF16),
                           pool_scale[slot])
            w_out = pool_w_out[slot]
        x_lat, h2, comb = _post_call(x_lat, y.reshape(b * l, d), w_out.astype(BF16), mod_lat,
                                     norm2_g[i], router_w, router_b, l)
        x_lat = _moe_call(h2, comb, moe_w_gate[i].astype(BF16), moe_w_up[i].astype(BF16),
                          moe_w_down[i].astype(BF16), x_lat, mod_lat, final_g, l,
                          final_norm=(i == depth - 1))
    return x_lat.reshape(b, l, d)
```

```python
import functools

import jax
import jax.numpy as jnp
from jax import lax
from jax.experimental import pallas as pl
from jax.experimental.pallas import tpu as pltpu

F32 = jnp.float32
BF16 = jnp.bfloat16

EPS = 1e-6
N_MOD = 6
HG_HEADS = 8
HG_DK = 128
HG_CHUNK = 64
N_HG_PROJ = 5
POOL_WINDOWS = (2, 4, 8, 16)
GRID_W = 64
GRID_SHIFT = GRID_W.bit_length() - 1
assert 1 << GRID_SHIFT == GRID_W
N_EXPERTS = 16
N_EXPERT_GROUPS = 4
EXPERTS_PER_GROUP = N_EXPERTS // N_EXPERT_GROUPS
D_EXPERT = 256

LANES = 128
MOD_ROWS = 8
VMEM_LIMIT = 56 * 1024 * 1024


def _cparams(sem, vmem=VMEM_LIMIT):
    return pltpu.CompilerParams(dimension_semantics=sem, vmem_limit_bytes=vmem)


def _sigmoid_pair(z):
    e = jnp.exp(-jnp.abs(z))
    r = 1.0 / (1.0 + e)
    er = e * r
    pos = z >= 0
    return jnp.where(pos, r, er), jnp.where(pos, er, r)


def _silu(z):
    return z * _sigmoid_pair(z)[0]


def _sigmoid_tanh(z):
    return 0.5 * jnp.tanh(0.5 * z) + 0.5


def _silu_tanh(z):
    return z * _sigmoid_tanh(z)


def _mod_kernel(c_ref, w_ref, b_ref, o_ref):
    a = _silu(c_ref[...])
    o_ref[0] = jnp.dot(a, w_ref[0], preferred_element_type=F32,
                       precision=lax.Precision.HIGHEST) + b_ref[0]


def _mod_call(cc, w_mod, b_mod):
    depth, d, n = w_mod.shape
    tn = 1024
    return pl.pallas_call(
        _mod_kernel,
        grid=(depth, n // tn),
        in_specs=[
            pl.BlockSpec((MOD_ROWS, d), lambda i, j: (0, 0)),
            pl.BlockSpec((1, d, tn), lambda i, j: (i, 0, j)),
            pl.BlockSpec((1, 1, tn), lambda i, j: (i, 0, j)),
        ],
        out_specs=pl.BlockSpec((1, MOD_ROWS, tn), lambda i, j: (i, 0, j)),
        out_shape=jax.ShapeDtypeStruct((depth, MOD_ROWS, n), F32),
        compiler_params=_cparams(("parallel", "parallel")),
        name="mod_proj",
    )(cc, w_mod, b_mod.reshape(depth, 1, n))


def _modulate(x, g, shift, scale):
    ms = jnp.mean(x * x, axis=-1, keepdims=True)
    return (x * lax.rsqrt(ms + EPS) * g) * (1.0 + scale) + shift


def _norm_kernel(x_ref, g_ref, mod_ref, *rest):
    h = _modulate(x_ref[0], g_ref[...], mod_ref[0, 0:1, :], mod_ref[0, 1:2, :]).astype(BF16)
    if len(rest) == 1:
        (o_ref,) = rest
    else:
        w_ref, o_ref, p_ref = rest
        p_ref[0] = jnp.dot(h, w_ref[...], preferred_element_type=F32).astype(BF16)
    o_ref[0] = h


def _norm_call(x, g, mod, tile, w_proj=None):
    b, t, d = x.shape
    tok = pl.BlockSpec((1, tile, d), lambda i, j: (i, j, 0))
    in_specs = [tok, pl.BlockSpec((1, d), lambda i, j: (0, 0)),
                pl.BlockSpec((1, N_MOD, d), lambda i, j: (i, 0, 0))]
    args = [x, g.reshape(1, d), mod]
    out_specs, out_shape = tok, jax.ShapeDtypeStruct((b, t, d), BF16)
    if w_proj is not None:
        in_specs.append(pl.BlockSpec((d, d), lambda i, j: (0, 0)))
        args.append(w_proj)
        out_specs, out_shape = [tok, tok], [out_shape, out_shape]
    return pl.pallas_call(
        _norm_kernel,
        grid=(b, t // tile),
        in_specs=in_specs,
        out_specs=out_specs,
        out_shape=out_shape,
        compiler_params=_cparams(("parallel", "parallel")),
        name="norm1",
    )(*args)


HG_BLOCK = 256
HG_CPB = HG_BLOCK // HG_CHUNK


def _chunk_prefix(x, row):
    for d in (1, 2, 4, 8, 16, 32):
        x = x + jnp.where(row >= d, pltpu.roll(x, d, axis=0), 0.0)
    return x


def _chunk_suffix(x, row):
    n = x.shape[0]
    for d in (1, 2, 4, 8, 16, 32):
        x = x + jnp.where(row < HG_CHUNK - d, pltpu.roll(x, n - d, axis=0), 0.0)
    return x


def _lower_bound(lb_ref, slot):
    rows = [lb_ref[j, 0] for j in range(lb_ref.shape[0])]
    m = functools.reduce(jnp.maximum, rows)
    es = [jnp.exp(r - m) for r in rows]
    return sum(es[:slot + 1]) / sum(es)


def _hgrn_kernel(hc_ref, hl_ref, g_ref, w_ref, lbf_ref, lbb_ref, gn_ref, y_ref,
                 oacc, qif, qib, kvf, kvb, decf, decb, spf, spb,
                 *, slot, n_ctx_chunks, n_lat_chunks):
    lb_f = _lower_bound(lbf_ref, slot)
    lb_b = _lower_bound(lbb_ref, slot)
    row = lax.broadcasted_iota(jnp.int32, (HG_BLOCK, HG_DK), 0) & (HG_CHUNK - 1)
    ci = lax.broadcasted_iota(jnp.int32, (HG_CPB, HG_CHUNK, HG_CHUNK), 1)
    si = lax.broadcasted_iota(jnp.int32, (HG_CPB, HG_CHUNK, HG_CHUNK), 2)

    def c3(t):
        return t.reshape(HG_CPB, HG_CHUNK, HG_DK)

    def direction(qs3, v3b, z, lb, fwd):
        sig = _sigmoid_tanh(z)
        f = lb + (1.0 - lb) * sig
        k3 = c3((1.0 - lb) * (1.0 - sig))
        lf = jnp.log(f)
        if fwd:
            cum = c3(_chunk_prefix(lf, row))
            ref = cum[:, HG_CHUNK // 2 - 1:HG_CHUNK // 2, :]
            last = cum[:, HG_CHUNK - 1:HG_CHUNK, :]
        else:
            cum = c3(_chunk_suffix(lf, row))
            ref = cum[:, HG_CHUNK // 2:HG_CHUNK // 2 + 1, :]
            last = cum[:, 0:1, :]
        dec = jnp.exp(last)
        if qs3 is None:
            kl = k3 * jnp.exp(last - cum)
        else:
            e1 = jnp.exp(cum - ref)
            qd = qs3 * e1
            qi = qd * jnp.exp(ref)
            kd = k3 * (1.0 / e1)
            kl = kd * jnp.exp(last - ref)
        kvt = jnp.einsum('ncv,nck->nvk', v3b, kl.astype(BF16), preferred_element_type=F32)
        if qs3 is None:
            return None, None, kvt, dec
        sc = jnp.einsum('nck,nsk->ncs', qd.astype(BF16), kd.astype(BF16), preferred_element_type=F32)
        sc = jnp.where((ci >= si) if fwd else (ci <= si), sc, 0.0)
        intra = jnp.einsum('ncs,nsv->ncv', sc.astype(BF16), v3b, preferred_element_type=F32)
        return intra, qi, kvt, dec

    def block(hrows, chunk0, lat_row0):
        p = jnp.dot(hrows, w_ref[0], preferred_element_type=F32)
        v3b = c3(p[:, HG_DK:2 * HG_DK]).astype(BF16)
        zf = p[:, 2 * HG_DK:3 * HG_DK]
        zb = p[:, 3 * HG_DK:4 * HG_DK]
        if lat_row0 is None:
            qs3 = None
        else:
            qs3 = c3(_silu_tanh(p[:, 0:HG_DK]))
        in_f, qi_f, kv_f, dec_f = direction(qs3, v3b, zf, lb_f, True)
        in_b, qi_b, kv_b, dec_b = direction(qs3, v3b, zb, lb_b, False)
        kvf[pl.ds(chunk0, HG_CPB)] = kv_f
        kvb[pl.ds(chunk0, HG_CPB)] = kv_b
        decf[pl.ds(chunk0, HG_CPB)] = dec_f
        decb[pl.ds(chunk0, HG_CPB)] = dec_b
        if lat_row0 is not None:
            rows = pl.ds(lat_row0, HG_BLOCK)
            oacc[rows, :] = (in_f + in_b).reshape(HG_BLOCK, HG_DK)
            qif[rows, :] = qi_f.reshape(HG_BLOCK, HG_DK).astype(BF16)
            qib[rows, :] = qi_b.reshape(HG_BLOCK, HG_DK).astype(BF16)

    for i in range(n_ctx_chunks // HG_CPB):
        block(hc_ref[0, i * HG_BLOCK:(i + 1) * HG_BLOCK, :], i * HG_CPB, None)

    n_blocks = n_lat_chunks // HG_CPB

    def lat_block(i, carry):
        r0 = pl.multiple_of(i * HG_BLOCK, HG_BLOCK)
        block(hl_ref[0, pl.ds(r0, HG_BLOCK), :], n_ctx_chunks + i * HG_CPB, r0)
        return carry

    lax.fori_loop(0, n_blocks, lat_block, 0, unroll=2)

    def advance(s, kv_ref, dec_ref, n):
        return dec_ref[n] * s + kv_ref[n]

    sf = jnp.zeros((HG_DK, HG_DK), F32)
    for n in range(n_ctx_chunks):
        sf = advance(sf, kvf, decf, n)
    sb = jnp.zeros((HG_DK, HG_DK), F32)
    for n in reversed(range(n_ctx_chunks)):
        sb = advance(sb, kvb, decb, n)

    def scan_step(t, carry):
        sf, sb = carry
        jb = n_lat_chunks - 1 - t
        spf[t] = sf.astype(BF16)
        spb[jb] = sb.astype(BF16)
        return (advance(sf, kvf, decf, n_ctx_chunks + t), advance(sb, kvb, decb, n_ctx_chunks + jb))

    lax.fori_loop(0, n_lat_chunks, scan_step, (sf, sb), unroll=2)

    gn = gn_ref[...]

    def readout(i, carry):
        rows = pl.ds(pl.multiple_of(i * HG_BLOCK, HG_BLOCK), HG_BLOCK)
        chunks = pl.ds(i * HG_CPB, HG_CPB)
        o = (c3(oacc[rows, :])
             + jnp.einsum('nck,nvk->ncv', c3(qif[rows, :]), spf[chunks], preferred_element_type=F32)
             + jnp.einsum('nck,nvk->ncv', c3(qib[rows, :]), spb[chunks], preferred_element_type=F32))
        o = o.reshape(HG_BLOCK, HG_DK)
        ms = jnp.mean(o * o, axis=-1, keepdims=True)
        o = o * lax.rsqrt(ms + EPS) * gn
        y_ref[0, rows, :] = (o * _silu_tanh(g_ref[0, rows, :].astype(F32))).astype(BF16)
        return carry

    lax.fori_loop(0, n_blocks, readout, 0, unroll=2)


def _hgrn_call(h_ctx, h_lat, g_lat, w_heads, lb_fwd, lb_bwd, gnorm, slot):
    b, lc, d = h_ctx.shape
    ll = h_lat.shape[1]
    nrow = lb_fwd.shape[0]
    ncc, nlc = lc // HG_CHUNK, ll // HG_CHUNK
    kern = functools.partial(_hgrn_kernel, slot=slot, n_ctx_chunks=ncc, n_lat_chunks=nlc)
    lb_spec = pl.BlockSpec((nrow, 1, 1, HG_DK), lambda i, h: (0, h, 0, 0))
    head_cols = pl.BlockSpec((1, ll, HG_DK), lambda i, h: (i, 0, h))
    return pl.pallas_call(
        kern,
        grid=(b, HG_HEADS),
        in_specs=[
            pl.BlockSpec((1, lc, d), lambda i, h: (i, 0, 0)),
            pl.BlockSpec((1, ll, d), lambda i, h: (i, 0, 0)),
            head_cols,
            pl.BlockSpec((1, d, w_heads.shape[2]), lambda i, h: (h, 0, 0)),
            lb_spec, lb_spec,
            pl.BlockSpec((1, HG_DK), lambda i, h: (0, 0)),
        ],
        out_specs=head_cols,
        out_shape=jax.ShapeDtypeStruct((b, ll, d), BF16),
        scratch_shapes=[
            pltpu.VMEM((ll, HG_DK), F32),
            pltpu.VMEM((ll, HG_DK), BF16),
            pltpu.VMEM((ll, HG_DK), BF16),
            pltpu.VMEM((ncc + nlc, HG_DK, HG_DK), F32),
            pltpu.VMEM((ncc + nlc, HG_DK, HG_DK), F32),
            pltpu.VMEM((ncc + nlc, 1, HG_DK), F32),
            pltpu.VMEM((ncc + nlc, 1, HG_DK), F32),
            pltpu.VMEM((nlc, HG_DK, HG_DK), BF16),
            pltpu.VMEM((nlc, HG_DK, HG_DK), BF16),
        ],
        compiler_params=_cparams(("parallel", "arbitrary")),
        name="hgrn2",
    )(h_ctx, h_lat, g_lat, w_heads,
      lb_fwd.reshape(nrow, HG_HEADS, 1, HG_DK), lb_bwd.reshape(nrow, HG_HEADS, 1, HG_DK),
      gnorm.reshape(1, HG_DK))


POOL_STRIP = 8
POOL_PAD = 8
POOL_BAND = 256


def _pool_group(h_ref, win_ref, wg_ref, ps_ref, o_ref, upad, *, win, rows):
    gd = win_ref.shape[1]
    half = win // 2
    u = jnp.dot(h_ref[0], win_ref[...], preferred_element_type=F32)
    zeros = jnp.zeros((POOL_PAD, GRID_W, gd), F32)
    upad[0:POOL_PAD] = zeros
    upad[POOL_PAD + rows:POOL_PAD + rows + POOL_PAD] = zeros
    upad[POOL_PAD:POOL_PAD + rows] = u.reshape(rows, GRID_W, gd)

    tok = POOL_STRIP * GRID_W
    bi = lax.broadcasted_iota(jnp.int32, (POOL_BAND, POOL_BAND), 0)
    bj = lax.broadcasted_iota(jnp.int32, (POOL_BAND, POOL_BAND), 1)
    lo_c = (bi & (GRID_W - 1)) - half
    cj = bj & (GRID_W - 1)
    band = jnp.where(bi >> GRID_SHIFT == bj >> GRID_SHIFT, 1.0, 0.0)
    band = jnp.where(cj >= lo_c, band, 0.0)
    band = jnp.where(cj < lo_c + win, band, 0.0).astype(BF16)
    t = lax.broadcasted_iota(jnp.int32, (tok, gd), 0)
    col = t & (GRID_W - 1)
    cnt_c = jnp.minimum(col - half + win, GRID_W) - jnp.maximum(col - half, 0)
    wg = wg_ref[0]
    ps = ps_ref[...]

    def strip(i, carry):
        r0 = i * POOL_STRIP
        slab = upad[pl.ds(r0 + POOL_PAD - half, POOL_STRIP + win - 1)]
        span = 1
        while span < win:
            n = slab.shape[0] - span
            slab = slab[0:n] + slab[span:span + n]
            span *= 2
        rs = slab.reshape(tok, gd)
        hi = rs.astype(BF16)
        lo = (rs - hi.astype(F32)).astype(BF16)
        parts = []
        for k in range(tok // POOL_BAND):
            sl = slice(k * POOL_BAND, (k + 1) * POOL_BAND)
            parts.append(jnp.dot(band, hi[sl], preferred_element_type=F32)
                         + jnp.dot(band, lo[sl], preferred_element_type=F32))
        box = jnp.concatenate(parts, axis=0)
        r = r0 + (t >> GRID_SHIFT)
        cnt_r = jnp.minimum(r - half + win, rows) - jnp.maximum(r - half, 0)
        mean = box / (cnt_r * cnt_c).astype(F32)
        ug = upad[pl.ds(r0 + POOL_PAD, POOL_STRIP)].reshape(tok, gd)
        z = jnp.dot((mean - ug).astype(BF16), wg, preferred_element_type=F32) * ps
        o_ref[0, pl.ds(pl.multiple_of(i * tok, tok), tok), :] = z.astype(BF16)
        return carry

    lax.fori_loop(0, rows // POOL_STRIP, strip, 0)


def _pool_kernel(h_ref, win_ref, wg_ref, ps_ref, o_ref, upad, *, rows):
    g = pl.program_id(1)
    for gi, win in enumerate(POOL_WINDOWS):
        @pl.when(g == gi)
        def _(win=win):
            _pool_group(h_ref, win_ref, wg_ref, ps_ref, o_ref, upad, win=win, rows=rows)


def _pool_call(h, w_in, w_grp, p_scale):
    b, l, d = h.shape
    ng = len(POOL_WINDOWS)
    gd = d // ng
    rows = l // GRID_W
    return pl.pallas_call(
        functools.partial(_pool_kernel, rows=rows),
        grid=(b, ng),
        in_specs=[
            pl.BlockSpec((1, l, d), lambda i, g: (i, 0, 0)),
            pl.BlockSpec((d, gd), lambda i, g: (0, g)),
            pl.BlockSpec((1, gd, gd), lambda i, g: (g, 0, 0)),
            pl.BlockSpec((1, gd), lambda i, g: (0, g)),
        ],
        out_specs=pl.BlockSpec((1, l, gd), lambda i, g: (i, 0, g)),
        out_shape=jax.ShapeDtypeStruct((b, l, d), BF16),
        scratch_shapes=[pltpu.VMEM((rows + 2 * POOL_PAD, GRID_W, gd), F32)],
        compiler_params=_cparams(("parallel", "arbitrary")),
        name="pool_mix",
    )(h, w_in, w_grp, p_scale.reshape(1, d))


MOE_TILE = 1024
MOE_WINDOW = 320
MOE_EXTRA = 64
ROW_ALIGN = 16
DEST_LANE = N_EXPERTS
AUX_ROWS = 8


def _route(sel, s):
    keep = []
    gsum = []
    for g in range(N_EXPERT_GROUPS):
        a = sel[g * EXPERTS_PER_GROUP:(g + 1) * EXPERTS_PER_GROUP]
        beaten = [jnp.zeros_like(a[0]) for _ in a]
        for i in range(EXPERTS_PER_GROUP):
            for j in range(i + 1, EXPERTS_PER_GROUP):
                ge = jnp.where(a[i] >= a[j], 1.0, 0.0)
                beaten[j] = beaten[j] + ge
                beaten[i] = beaten[i] + (1.0 - ge)
        kg = [jnp.where(bt < 1.5, 1.0, 0.0) for bt in beaten]
        keep.append(kg)
        gsum.append(sum(k * x for k, x in zip(kg, a)))
    picked = []
    bests = []
    for g in range(N_EXPERT_GROUPS):
        better = jnp.zeros_like(gsum[0])
        for o in range(N_EXPERT_GROUPS):
            if o < g:
                better = better + jnp.where(gsum[o] >= gsum[g], 1.0, 0.0)
            elif o > g:
                better = better + jnp.where(gsum[o] > gsum[g], 1.0, 0.0)
        best = jnp.where(better < 0.5, 1.0, 0.0)
        bests.append(best)
        for i in range(EXPERTS_PER_GROUP):
            picked.append(best * keep[g][i] * s[g * EXPERTS_PER_GROUP + i])
    den = sum(picked)
    return [p / den for p in picked], bests


def _post_kernel(x_ref, y_ref, w_ref, mod_ref, g_ref, rwh_ref, rwl_ref, rb_ref,
                 xo_ref, h2_ref, rt_ref, aux_ref):
    yw = jnp.dot(y_ref[...], w_ref[...], preferred_element_type=F32)
    xn = x_ref[...] + mod_ref[0, 2:3, :] * yw
    xo_ref[...] = xn
    h2 = _modulate(xn, g_ref[...], mod_ref[0, 3:4, :], mod_ref[0, 4:5, :])
    hi = h2.astype(BF16)
    h2_ref[...] = hi
    lo = (h2 - hi.astype(F32)).astype(BF16)
    nt = (((1,), (1,)), ((), ()))
    rwh = rwh_ref[...]
    logits = (lax.dot_general(rwh, hi, nt, preferred_element_type=F32)
              + lax.dot_general(rwh, lo, nt, preferred_element_type=F32)
              + lax.dot_general(rwl_ref[...], hi, nt, preferred_element_type=F32))
    s = _sigmoid_pair(logits)[0]
    sel = s + rb_ref[...]
    comb, bests = _route([sel[e:e + 1, :] for e in range(N_EXPERTS)],
                         [s[e:e + 1, :] for e in range(N_EXPERTS)])
    t = logits.shape[1]

    ind = jnp.concatenate(bests + [jnp.zeros((AUX_ROWS - N_EXPERT_GROUPS, t), F32)], axis=0)
    upper = jnp.where(lax.broadcasted_iota(jnp.int32, (t, t), 0)
                      <= lax.broadcasted_iota(jnp.int32, (t, t), 1), 1.0, 0.0).astype(BF16)
    cum = jnp.dot(ind.astype(BF16), upper, preferred_element_type=F32)
    counts = [cum[g:g + 1, t - 1:t] for g in range(N_EXPERT_GROUPS)]
    starts = [jnp.zeros((1, 1), F32)]
    for g in range(1, N_EXPERT_GROUPS):
        starts.append(starts[-1] + counts[g - 1])
    dest = sum(bests[g] * (starts[g] + cum[g:g + 1, :] - 1.0) for g in range(N_EXPERT_GROUPS))

    lane = lax.broadcasted_iota(jnp.int32, (1, t), 1)
    seg = sum(jnp.where(lane == k, v, 0.0) for k, v in enumerate(starts + counts))
    aux_ref[0] = jnp.concatenate([dest, seg, jnp.zeros((AUX_ROWS - 2, t), F32)], axis=0)
    table = jnp.concatenate(comb + [dest, jnp.zeros((LANES - N_EXPERTS - 1, t), F32)], axis=0)
    rt_ref[...] = table.T


def _post_call(x, y, w, mod, g, router_w, router_b, tokens_per_batch, tile=MOE_TILE):
    n, d = x.shape
    per_b = tokens_per_batch // tile
    rw_t = router_w.T
    rwh = rw_t.astype(BF16)
    rwl = (rw_t - rwh.astype(F32)).astype(BF16)
    row = lambda i: (i, 0)
    fixed = lambda i: (0, 0)
    return pl.pallas_call(
        _post_kernel,
        grid=(n // tile,),
        in_specs=[
            pl.BlockSpec((tile, d), row),
            pl.BlockSpec((tile, d), row),
            pl.BlockSpec((d, d), fixed),
            pl.BlockSpec((1, N_MOD, d), lambda i: (i // per_b, 0, 0)),
            pl.BlockSpec((1, d), fixed),
            pl.BlockSpec((N_EXPERTS, d), fixed),
            pl.BlockSpec((N_EXPERTS, d), fixed),
            pl.BlockSpec((N_EXPERTS, 1), fixed),
        ],
        out_specs=[
            pl.BlockSpec((tile, d), row),
            pl.BlockSpec((tile, d), row),
            pl.BlockSpec((tile, LANES), row),
            pl.BlockSpec((1, AUX_ROWS, tile), lambda i: (i, 0, 0)),
        ],
        out_shape=[
            jax.ShapeDtypeStruct((n, d), F32),
            jax.ShapeDtypeStruct((n, d), BF16),
            jax.ShapeDtypeStruct((n, LANES), F32),
            jax.ShapeDtypeStruct((n // tile, AUX_ROWS, tile), F32),
        ],
        compiler_params=_cparams(("parallel",)),
        name="mixer_out_router",
    )(x, y, w, mod, g.reshape(1, d), rwh, rwl, router_b.reshape(N_EXPERTS, 1))


def _moe_kernel(seg_ref, h_ref, rt_ref, aux_ref, wg_ref, wu_ref, wd_ref, x_ref, mod_ref, fg_ref,
                o_ref, xs, cws, ys, *, final_norm):
    i = pl.program_id(0)
    g = pl.program_id(1)
    t = h_ref.shape[0]

    @pl.when(g == 0)
    def _():
        dest_row = aux_ref[0, 0:1, :]
        perm = jnp.where(lax.broadcasted_iota(jnp.int32, (t, t), 0).astype(F32) == dest_row,
                         1.0, 0.0).astype(BF16)
        xs[...] = jnp.dot(perm, h_ref[...], preferred_element_type=F32).astype(BF16)
        rt = rt_ref[...]
        p1 = rt.astype(BF16)
        r1 = rt - p1.astype(F32)
        p2 = r1.astype(BF16)
        p3 = (r1 - p2.astype(F32)).astype(BF16)
        cws[...] = (jnp.dot(perm, p1, preferred_element_type=F32)
                    + jnp.dot(perm, p2, preferred_element_type=F32)
                    + jnp.dot(perm, p3, preferred_element_type=F32))
        ys[...] = jnp.zeros_like(ys)

    start = seg_ref[i, g]
    end = start + seg_ref[i, N_EXPERT_GROUPS + g]
    wd = wd_ref[...].reshape(EXPERTS_PER_GROUP * D_EXPERT, wd_ref.shape[2])

    def window(w0, rows, lo, hi):
        sl = pl.ds(pl.multiple_of(w0, ROW_ALIGN), rows)
        xw = xs[sl, :]
        cw = cws[sl, :]
        r = w0 + lax.broadcasted_iota(jnp.int32, cw.shape, 0)
        lane = lax.broadcasted_iota(jnp.int32, cw.shape, 1)
        cw = jnp.where(r >= lo, cw, 0.0)
        cw = jnp.where(r < hi, cw, 0.0)
        acts = []
        for e in range(EXPERTS_PER_GROUP):
            ce = jnp.sum(jnp.where(lane == g * EXPERTS_PER_GROUP + e, cw, 0.0), axis=1, keepdims=True)
            a = _silu_tanh(jnp.dot(xw, wg_ref[e], preferred_element_type=F32)) \
                * jnp.dot(xw, wu_ref[e], preferred_element_type=F32)
            acts.append((a * ce).astype(BF16))
        ys[sl, :] += jnp.dot(jnp.concatenate(acts, axis=1), wd, preferred_element_type=F32)

    w0 = jnp.minimum(start & -ROW_ALIGN, t - MOE_WINDOW)
    covered = w0 + MOE_WINDOW
    window(w0, MOE_WINDOW, start, jnp.minimum(end, covered))

    def extra(k, carry):
        lo = covered + k * MOE_EXTRA
        window(jnp.minimum(lo, t - MOE_EXTRA), MOE_EXTRA, lo, jnp.minimum(end, lo + MOE_EXTRA))
        return carry

    n_extra = jnp.maximum(end - covered + MOE_EXTRA - 1, 0) >> (MOE_EXTRA.bit_length() - 1)
    lax.fori_loop(0, n_extra, extra, 0)

    @pl.when(g == N_EXPERT_GROUPS - 1)
    def _():
        dest_col = rt_ref[:, DEST_LANE:DEST_LANE + 1]
        unperm = jnp.where(lax.broadcasted_iota(jnp.int32, (t, t), 1).astype(F32) == dest_col,
                           1.0, 0.0).astype(BF16)
        ff = jnp.dot(unperm, ys[...].astype(BF16), preferred_element_type=F32)
        xn = x_ref[...] + mod_ref[0, 5:6, :] * ff
        if final_norm:
            ms = jnp.mean(xn * xn, axis=-1, keepdims=True)
            xn = xn * lax.rsqrt(ms + EPS) * fg_ref[...]
        o_ref[...] = xn


def _moe_call(h2, table, aux, wg, wu, wd, x, mod, final_g, tokens_per_batch, final_norm):
    n, d = x.shape
    tile = MOE_TILE
    per_b = tokens_per_batch // tile
    seg = aux[:, 1, :2 * N_EXPERT_GROUPS].astype(jnp.int32)
    row = lambda i, g, seg: (i, 0)
    grp = lambda i, g, seg: (g, 0, 0)
    grid_spec = pltpu.PrefetchScalarGridSpec(
        num_scalar_prefetch=1,
        grid=(n // tile, N_EXPERT_GROUPS),
        in_specs=[
            pl.BlockSpec((tile, d), row),
            pl.BlockSpec((tile, LANES), row),
            pl.BlockSpec((1, AUX_ROWS, tile), lambda i, g, seg: (i, 0, 0)),
            pl.BlockSpec((EXPERTS_PER_GROUP, d, D_EXPERT), grp),
            pl.BlockSpec((EXPERTS_PER_GROUP, d, D_EXPERT), grp),
            pl.BlockSpec((EXPERTS_PER_GROUP, D_EXPERT, d), grp),
            pl.BlockSpec((tile, d), row),
            pl.BlockSpec((1, N_MOD, d), lambda i, g, seg: (i // per_b, 0, 0)),
            pl.BlockSpec((1, d), lambda i, g, seg: (0, 0)),
        ],
        out_specs=pl.BlockSpec((tile, d), row),
        scratch_shapes=[
            pltpu.VMEM((tile, d), BF16),
            pltpu.VMEM((tile, LANES), F32),
            pltpu.VMEM((tile, d), F32),
        ],
    )
    return pl.pallas_call(
        functools.partial(_moe_kernel, final_norm=final_norm),
        grid_spec=grid_spec,
        out_shape=jax.ShapeDtypeStruct((n, d), F32),
        compiler_params=_cparams(("parallel", "arbitrary")),
        name="moe_ffn",
    )(seg, h2, table, aux, wg, wu, wd, x, mod, final_g.reshape(1, d))


def kernel(x, c, ctx, c_ctx, w_mod, b_mod, norm1_g, norm2_g, hg_w_in, hg_lb_fwd, hg_lb_bwd, hg_gnorm,
           hg_w_out, pool_w_in, pool_w_grp, pool_scale, pool_w_out, router_w, router_b, moe_w_gate,
           moe_w_up, moe_w_down, final_g):
    b, l, d = x.shape
    depth = w_mod.shape[0]
    n_mixers = 2

    cc = jnp.concatenate([c, c_ctx[None, :], jnp.zeros((MOD_ROWS - b - 1, d), F32)], axis=0)
    mods = _mod_call(cc, w_mod, b_mod)

    x_lat = x.reshape(b * l, d)
    for i in range(depth):
        slot = i // n_mixers
        mod_lat = mods[i, :b].reshape(b, N_MOD, d)
        if i % n_mixers == 0:
            n_rec = N_HG_PROJ - 1
            w_rec = (hg_w_in[slot][:, :n_rec * d].reshape(d, n_rec, HG_HEADS, HG_DK)
                     .transpose(2, 0, 1, 3).reshape(HG_HEADS, d, n_rec * HG_DK).astype(BF16))
            w_gate = hg_w_in[slot][:, n_rec * d:].astype(BF16)
            h_lat, g_lat = _norm_call(x_lat.reshape(b, l, d), norm1_g[i], mod_lat, 512, w_gate)
            mod_ctx = jnp.broadcast_to(mods[i, b].reshape(1, N_MOD, d), (b, N_MOD, d))
            h_ctx = _norm_call(ctx, norm1_g[i], mod_ctx, ctx.shape[1])
            y = _hgrn_call(h_ctx, h_lat, g_lat, w_rec, hg_lb_fwd, hg_lb_bwd, hg_gnorm[slot], slot)
            w_out = hg_w_out[slot]
        else:
            h_lat = _norm_call(x_lat.reshape(b, l, d), norm1_g[i], mod_lat, 512)
            y = _pool_call(h_lat, pool_w_in[slot].astype(BF16), pool_w_grp[slot].astype(BF16),
                           pool_scale[slot])
            w_out = pool_w_out[slot]
        x_lat, h2, table, aux = _post_call(x_lat, y.reshape(b * l, d), w_out.astype(BF16), mod_lat,
                                           norm2_g[i], router_w, router_b, l)
        x_lat = _moe_call(h2, table, aux, moe_w_gate[i].astype(BF16), moe_w_up[i].astype(BF16),
                          moe_w_down[i].astype(BF16), x_lat, mod_lat, final_g, l,
                          final_norm=(i == depth - 1))
    return x_lat.reshape(b, l, d)
```

```python
import functools

import jax
import jax.numpy as jnp
from jax import lax
from jax.experimental import pallas as pl
from jax.experimental.pallas import tpu as pltpu

F32 = jnp.float32
BF16 = jnp.bfloat16

EPS = 1e-6
N_MOD = 6
HG_HEADS = 8
HG_DK = 128
HG_CHUNK = 64
N_HG_PROJ = 5
POOL_WINDOWS = (2, 4, 8, 16)
GRID_W = 64
GRID_SHIFT = GRID_W.bit_length() - 1
assert 1 << GRID_SHIFT == GRID_W
N_EXPERTS = 16
N_EXPERT_GROUPS = 4
EXPERTS_PER_GROUP = N_EXPERTS // N_EXPERT_GROUPS
D_EXPERT = 256

LANES = 128
MOD_ROWS = 8
VMEM_LIMIT = 56 * 1024 * 1024


def _cparams(sem, vmem=VMEM_LIMIT):
    return pltpu.CompilerParams(dimension_semantics=sem, vmem_limit_bytes=vmem)


def _sigmoid_pair(z):
    e = jnp.exp(-jnp.abs(z))
    r = 1.0 / (1.0 + e)
    er = e * r
    pos = z >= 0
    return jnp.where(pos, r, er), jnp.where(pos, er, r)


def _silu(z):
    return z * _sigmoid_pair(z)[0]


def _sigmoid_tanh(z):
    return 0.5 * jnp.tanh(0.5 * z) + 0.5


def _silu_tanh(z):
    return z * _sigmoid_tanh(z)


def _mod_kernel(c_ref, w_ref, b_ref, o_ref):
    a = _silu(c_ref[...])
    o_ref[0] = jnp.dot(a, w_ref[0], preferred_element_type=F32,
                       precision=lax.Precision.HIGHEST) + b_ref[0]


def _mod_call(cc, w_mod, b_mod):
    depth, d, n = w_mod.shape
    tn = 1024
    return pl.pallas_call(
        _mod_kernel,
        grid=(depth, n // tn),
        in_specs=[
            pl.BlockSpec((MOD_ROWS, d), lambda i, j: (0, 0)),
            pl.BlockSpec((1, d, tn), lambda i, j: (i, 0, j)),
            pl.BlockSpec((1, 1, tn), lambda i, j: (i, 0, j)),
        ],
        out_specs=pl.BlockSpec((1, MOD_ROWS, tn), lambda i, j: (i, 0, j)),
        out_shape=jax.ShapeDtypeStruct((depth, MOD_ROWS, n), F32),
        compiler_params=_cparams(("parallel", "parallel")),
        name="mod_proj",
    )(cc, w_mod, b_mod.reshape(depth, 1, n))


def _modulate(x, g, shift, scale):
    ms = jnp.mean(x * x, axis=-1, keepdims=True)
    return (x * lax.rsqrt(ms + EPS) * g) * (1.0 + scale) + shift


def _norm_kernel(x_ref, g_ref, mod_ref, *rest):
    h = _modulate(x_ref[0], g_ref[...], mod_ref[0, 0:1, :], mod_ref[0, 1:2, :]).astype(BF16)
    if len(rest) == 1:
        (o_ref,) = rest
    else:
        w_ref, o_ref, p_ref = rest
        p_ref[0] = jnp.dot(h, w_ref[...], preferred_element_type=F32).astype(BF16)
    o_ref[0] = h


def _norm_call(x, g, mod, tile, w_proj=None):
    b, t, d = x.shape
    tok = pl.BlockSpec((1, tile, d), lambda i, j: (i, j, 0))
    in_specs = [tok, pl.BlockSpec((1, d), lambda i, j: (0, 0)),
                pl.BlockSpec((1, N_MOD, d), lambda i, j: (i, 0, 0))]
    args = [x, g.reshape(1, d), mod]
    out_specs, out_shape = tok, jax.ShapeDtypeStruct((b, t, d), BF16)
    if w_proj is not None:
        in_specs.append(pl.BlockSpec((d, d), lambda i, j: (0, 0)))
        args.append(w_proj)
        out_specs, out_shape = [tok, tok], [out_shape, out_shape]
    return pl.pallas_call(
        _norm_kernel,
        grid=(b, t // tile),
        in_specs=in_specs,
        out_specs=out_specs,
        out_shape=out_shape,
        compiler_params=_cparams(("parallel", "parallel")),
        name="norm1",
    )(*args)


HG_BLOCK = 256
HG_CPB = HG_BLOCK // HG_CHUNK


def _chunk_prefix(x, row):
    for d in (1, 2, 4, 8, 16, 32):
        x = x + jnp.where(row >= d, pltpu.roll(x, d, axis=0), 0.0)
    return x


def _chunk_suffix(x, row):
    n = x.shape[0]
    for d in (1, 2, 4, 8, 16, 32):
        x = x + jnp.where(row < HG_CHUNK - d, pltpu.roll(x, n - d, axis=0), 0.0)
    return x


def _lower_bound(lb_ref, slot):
    rows = [lb_ref[j, 0] for j in range(lb_ref.shape[0])]
    m = functools.reduce(jnp.maximum, rows)
    es = [jnp.exp(r - m) for r in rows]
    return sum(es[:slot + 1]) / sum(es)


def _hgrn_kernel(hc_ref, hl_ref, g_ref, w_ref, lbf_ref, lbb_ref, gn_ref, y_ref,
                 pbuf0, pbuf1, pbuf2, pbuf3, oacc, qif, qib, kvf, kvb, decf, decb, spf, spb,
                 *, slot, n_ctx_chunks, n_lat_chunks):
    lb_f = _lower_bound(lbf_ref, slot)
    lb_b = _lower_bound(lbb_ref, slot)
    row = lax.broadcasted_iota(jnp.int32, (HG_BLOCK, HG_DK), 0) & (HG_CHUNK - 1)
    ci = lax.broadcasted_iota(jnp.int32, (HG_CPB, HG_CHUNK, HG_CHUNK), 1)
    si = lax.broadcasted_iota(jnp.int32, (HG_CPB, HG_CHUNK, HG_CHUNK), 2)

    def c3(t):
        return t.reshape(HG_CPB, HG_CHUNK, HG_DK)

    def direction(qs3, v3b, z, lb, fwd):
        sig = _sigmoid_tanh(z)
        f = lb + (1.0 - lb) * sig
        k3 = c3((1.0 - lb) * (1.0 - sig))
        lf = jnp.log(f)
        if fwd:
            cum = c3(_chunk_prefix(lf, row))
            ref = cum[:, HG_CHUNK // 2 - 1:HG_CHUNK // 2, :]
            last = cum[:, HG_CHUNK - 1:HG_CHUNK, :]
        else:
            cum = c3(_chunk_suffix(lf, row))
            ref = cum[:, HG_CHUNK // 2:HG_CHUNK // 2 + 1, :]
            last = cum[:, 0:1, :]
        dec = jnp.exp(last)
        if qs3 is None:
            kl = k3 * jnp.exp(last - cum)
        else:
            e1 = jnp.exp(cum - ref)
            qd = qs3 * e1
            qi = qd * jnp.exp(ref)
            kd = k3 * (1.0 / e1)
            kl = kd * jnp.exp(last - ref)
        kvt = jnp.einsum('ncv,nck->nvk', v3b, kl.astype(BF16), preferred_element_type=F32)
        if qs3 is None:
            return None, None, kvt, dec
        sc = jnp.einsum('nck,nsk->ncs', qd.astype(BF16), kd.astype(BF16), preferred_element_type=F32)
        sc = jnp.where((ci >= si) if fwd else (ci <= si), sc, 0.0)
        intra = jnp.einsum('ncs,nsv->ncv', sc.astype(BF16), v3b, preferred_element_type=F32)
        return intra, qi, kvt, dec

    def project(hrows):
        return jnp.dot(hrows, w_ref[0], preferred_element_type=F32)

    def block(p, chunk0, lat_row0):
        v3b = c3(p[:, HG_DK:2 * HG_DK]).astype(BF16)
        zf = p[:, 2 * HG_DK:3 * HG_DK]
        zb = p[:, 3 * HG_DK:4 * HG_DK]
        if lat_row0 is None:
            qs3 = None
        else:
            qs3 = c3(_silu_tanh(p[:, 0:HG_DK]))
        in_f, qi_f, kv_f, dec_f = direction(qs3, v3b, zf, lb_f, True)
        in_b, qi_b, kv_b, dec_b = direction(qs3, v3b, zb, lb_b, False)
        kvf[pl.ds(chunk0, HG_CPB)] = kv_f
        kvb[pl.ds(chunk0, HG_CPB)] = kv_b
        decf[pl.ds(chunk0, HG_CPB)] = dec_f
        decb[pl.ds(chunk0, HG_CPB)] = dec_b
        if lat_row0 is not None:
            rows = pl.ds(lat_row0, HG_BLOCK)
            oacc[rows, :] = (in_f + in_b).reshape(HG_BLOCK, HG_DK)
            qif[rows, :] = qi_f.reshape(HG_BLOCK, HG_DK).astype(BF16)
            qib[rows, :] = qi_b.reshape(HG_BLOCK, HG_DK).astype(BF16)

    for i in range(n_ctx_chunks // HG_CPB):
        block(project(hc_ref[0, i * HG_BLOCK:(i + 1) * HG_BLOCK, :]), i * HG_CPB, None)

    n_blocks = n_lat_chunks // HG_CPB

    def lat_rows(i):
        return hl_ref[0, pl.ds(pl.multiple_of(i * HG_BLOCK, HG_BLOCK), HG_BLOCK), :]

    def lat_terms(p_ref, i):
        block(p_ref[...], n_ctx_chunks + i * HG_CPB, pl.multiple_of(i * HG_BLOCK, HG_BLOCK))

    pbuf0[...] = project(lat_rows(0))
    pbuf1[...] = project(lat_rows(1))

    def lat_quad(i0, last):
        pbuf2[...] = project(lat_rows(i0 + 2))
        lat_terms(pbuf0, i0)
        pbuf3[...] = project(lat_rows(i0 + 3))
        lat_terms(pbuf1, i0 + 1)
        if not last:
            pbuf0[...] = project(lat_rows(i0 + 4))
        lat_terms(pbuf2, i0 + 2)
        if not last:
            pbuf1[...] = project(lat_rows(i0 + 5))
        lat_terms(pbuf3, i0 + 3)

    def lat_body(t, carry):
        lat_quad(4 * t, False)
        return carry

    lax.fori_loop(0, n_blocks // 4 - 1, lat_body, 0)
    lat_quad(n_blocks - 4, True)

    def advance(s, kv_ref, dec_ref, n):
        return dec_ref[n] * s + kv_ref[n]

    sf = jnp.zeros((HG_DK, HG_DK), F32)
    for n in range(n_ctx_chunks):
        sf = advance(sf, kvf, decf, n)
    sb = jnp.zeros((HG_DK, HG_DK), F32)
    for n in reversed(range(n_ctx_chunks)):
        sb = advance(sb, kvb, decb, n)

    def scan_step(t, carry):
        sf, sb = carry
        jb = n_lat_chunks - 1 - t
        spf[t] = sf.astype(BF16)
        spb[jb] = sb.astype(BF16)
        return (advance(sf, kvf, decf, n_ctx_chunks + t), advance(sb, kvb, decb, n_ctx_chunks + jb))

    lax.fori_loop(0, n_lat_chunks, scan_step, (sf, sb), unroll=2)

    gn = gn_ref[...]

    def block_rows(i):
        return pl.ds(pl.multiple_of(i * HG_BLOCK, HG_BLOCK), HG_BLOCK)

    def inter(i):
        rows = block_rows(i)
        chunks = pl.ds(i * HG_CPB, HG_CPB)
        return (jnp.einsum('nck,nvk->ncv', c3(qif[rows, :]), spf[chunks], preferred_element_type=F32)
                + jnp.einsum('nck,nvk->ncv', c3(qib[rows, :]), spb[chunks], preferred_element_type=F32)
                ).reshape(HG_BLOCK, HG_DK)

    def readout(o_ref, i):
        rows = block_rows(i)
        o = oacc[rows, :] + o_ref[...]
        ms = jnp.mean(o * o, axis=-1, keepdims=True)
        o = o * lax.rsqrt(ms + EPS) * gn
        y_ref[0, rows, :] = (o * _silu_tanh(g_ref[0, rows, :].astype(F32))).astype(BF16)

    obuf0, obuf1 = pbuf0.at[:, 0:HG_DK], pbuf1.at[:, 0:HG_DK]
    obuf0[...] = inter(0)

    def readout_pair(t, carry):
        obuf1[...] = inter(2 * t + 1)
        readout(obuf0, 2 * t)
        obuf0[...] = inter(2 * t + 2)
        readout(obuf1, 2 * t + 1)
        return carry

    lax.fori_loop(0, n_blocks // 2 - 1, readout_pair, 0)
    obuf1[...] = inter(n_blocks - 1)
    readout(obuf0, n_blocks - 2)
    readout(obuf1, n_blocks - 1)


def _hgrn_call(h_ctx, h_lat, g_lat, w_heads, lb_fwd, lb_bwd, gnorm, slot):
    b, lc, d = h_ctx.shape
    ll = h_lat.shape[1]
    nrow = lb_fwd.shape[0]
    ncc, nlc = lc // HG_CHUNK, ll // HG_CHUNK
    kern = functools.partial(_hgrn_kernel, slot=slot, n_ctx_chunks=ncc, n_lat_chunks=nlc)
    lb_spec = pl.BlockSpec((nrow, 1, 1, HG_DK), lambda i, h: (0, h, 0, 0))
    head_cols = pl.BlockSpec((1, ll, HG_DK), lambda i, h: (i, 0, h))
    return pl.pallas_call(
        kern,
        grid=(b, HG_HEADS),
        in_specs=[
            pl.BlockSpec((1, lc, d), lambda i, h: (i, 0, 0)),
            pl.BlockSpec((1, ll, d), lambda i, h: (i, 0, 0)),
            head_cols,
            pl.BlockSpec((1, d, w_heads.shape[2]), lambda i, h: (h, 0, 0)),
            lb_spec, lb_spec,
            pl.BlockSpec((1, HG_DK), lambda i, h: (0, 0)),
        ],
        out_specs=head_cols,
        out_shape=jax.ShapeDtypeStruct((b, ll, d), BF16),
        scratch_shapes=[
            pltpu.VMEM((HG_BLOCK, w_heads.shape[2]), F32),
            pltpu.VMEM((HG_BLOCK, w_heads.shape[2]), F32),
            pltpu.VMEM((HG_BLOCK, w_heads.shape[2]), F32),
            pltpu.VMEM((HG_BLOCK, w_heads.shape[2]), F32),
            pltpu.VMEM((ll, HG_DK), F32),
            pltpu.VMEM((ll, HG_DK), BF16),
            pltpu.VMEM((ll, HG_DK), BF16),
            pltpu.VMEM((ncc + nlc, HG_DK, HG_DK), F32),
            pltpu.VMEM((ncc + nlc, HG_DK, HG_DK), F32),
            pltpu.VMEM((ncc + nlc, 1, HG_DK), F32),
            pltpu.VMEM((ncc + nlc, 1, HG_DK), F32),
            pltpu.VMEM((nlc, HG_DK, HG_DK), BF16),
            pltpu.VMEM((nlc, HG_DK, HG_DK), BF16),
        ],
        compiler_params=_cparams(("parallel", "arbitrary")),
        name="hgrn2",
    )(h_ctx, h_lat, g_lat, w_heads,
      lb_fwd.reshape(nrow, HG_HEADS, 1, HG_DK), lb_bwd.reshape(nrow, HG_HEADS, 1, HG_DK),
      gnorm.reshape(1, HG_DK))


POOL_STRIP = 8
POOL_PAD = 8
POOL_BAND = 256


def _pool_group(h_ref, win_ref, wg_ref, ps_ref, o_ref, upad, *, win, rows):
    gd = win_ref.shape[1]
    half = win // 2
    u = jnp.dot(h_ref[0], win_ref[...], preferred_element_type=F32)
    zeros = jnp.zeros((POOL_PAD, GRID_W, gd), F32)
    upad[0:POOL_PAD] = zeros
    upad[POOL_PAD + rows:POOL_PAD + rows + POOL_PAD] = zeros
    upad[POOL_PAD:POOL_PAD + rows] = u.reshape(rows, GRID_W, gd)

    tok = POOL_STRIP * GRID_W
    bi = lax.broadcasted_iota(jnp.int32, (POOL_BAND, POOL_BAND), 0)
    bj = lax.broadcasted_iota(jnp.int32, (POOL_BAND, POOL_BAND), 1)
    lo_c = (bi & (GRID_W - 1)) - half
    cj = bj & (GRID_W - 1)
    band = jnp.where(bi >> GRID_SHIFT == bj >> GRID_SHIFT, 1.0, 0.0)
    band = jnp.where(cj >= lo_c, band, 0.0)
    band = jnp.where(cj < lo_c + win, band, 0.0).astype(BF16)
    t = lax.broadcasted_iota(jnp.int32, (tok, gd), 0)
    col = t & (GRID_W - 1)
    cnt_c = jnp.minimum(col - half + win, GRID_W) - jnp.maximum(col - half, 0)
    wg = wg_ref[0]
    ps = ps_ref[...]

    def strip(i, carry):
        r0 = i * POOL_STRIP
        slab = upad[pl.ds(r0 + POOL_PAD - half, POOL_STRIP + win - 1)]
        span = 1
        while span < win:
            n = slab.shape[0] - span
            slab = slab[0:n] + slab[span:span + n]
            span *= 2
        rs = slab.reshape(tok, gd)
        hi = rs.astype(BF16)
        lo = (rs - hi.astype(F32)).astype(BF16)
        parts = []
        for k in range(tok // POOL_BAND):
            sl = slice(k * POOL_BAND, (k + 1) * POOL_BAND)
            parts.append(jnp.dot(band, hi[sl], preferred_element_type=F32)
                         + jnp.dot(band, lo[sl], preferred_element_type=F32))
        box = jnp.concatenate(parts, axis=0)
        r = r0 + (t >> GRID_SHIFT)
        cnt_r = jnp.minimum(r - half + win, rows) - jnp.maximum(r - half, 0)
        mean = box / (cnt_r * cnt_c).astype(F32)
        ug = upad[pl.ds(r0 + POOL_PAD, POOL_STRIP)].reshape(tok, gd)
        z = jnp.dot((mean - ug).astype(BF16), wg, preferred_element_type=F32) * ps
        o_ref[0, pl.ds(pl.multiple_of(i * tok, tok), tok), :] = z.astype(BF16)
        return carry

    lax.fori_loop(0, rows // POOL_STRIP, strip, 0)


def _pool_kernel(h_ref, win_ref, wg_ref, ps_ref, o_ref, upad, *, rows):
    g = pl.program_id(1)
    for gi, win in enumerate(POOL_WINDOWS):
        @pl.when(g == gi)
        def _(win=win):
            _pool_group(h_ref, win_ref, wg_ref, ps_ref, o_ref, upad, win=win, rows=rows)


def _pool_call(h, w_in, w_grp, p_scale):
    b, l, d = h.shape
    ng = len(POOL_WINDOWS)
    gd = d // ng
    rows = l // GRID_W
    return pl.pallas_call(
        functools.partial(_pool_kernel, rows=rows),
        grid=(b, ng),
        in_specs=[
            pl.BlockSpec((1, l, d), lambda i, g: (i, 0, 0)),
            pl.BlockSpec((d, gd), lambda i, g: (0, g)),
            pl.BlockSpec((1, gd, gd), lambda i, g: (g, 0, 0)),
            pl.BlockSpec((1, gd), lambda i, g: (0, g)),
        ],
        out_specs=pl.BlockSpec((1, l, gd), lambda i, g: (i, 0, g)),
        out_shape=jax.ShapeDtypeStruct((b, l, d), BF16),
        scratch_shapes=[pltpu.VMEM((rows + 2 * POOL_PAD, GRID_W, gd), F32)],
        compiler_params=_cparams(("parallel", "arbitrary")),
        name="pool_mix",
    )(h, w_in, w_grp, p_scale.reshape(1, d))


MOE_TILE = 1024
MOE_WINDOW = 320
MOE_EXTRA = 64
ROW_ALIGN = 16
DEST_LANE = N_EXPERTS
AUX_ROWS = 8


def _route(sel, s):
    keep = []
    gsum = []
    for g in range(N_EXPERT_GROUPS):
        a = sel[g * EXPERTS_PER_GROUP:(g + 1) * EXPERTS_PER_GROUP]
        beaten = [jnp.zeros_like(a[0]) for _ in a]
        for i in range(EXPERTS_PER_GROUP):
            for j in range(i + 1, EXPERTS_PER_GROUP):
                ge = jnp.where(a[i] >= a[j], 1.0, 0.0)
                beaten[j] = beaten[j] + ge
                beaten[i] = beaten[i] + (1.0 - ge)
        kg = [jnp.where(bt < 1.5, 1.0, 0.0) for bt in beaten]
        keep.append(kg)
        gsum.append(sum(k * x for k, x in zip(kg, a)))
    picked = []
    bests = []
    for g in range(N_EXPERT_GROUPS):
        better = jnp.zeros_like(gsum[0])
        for o in range(N_EXPERT_GROUPS):
            if o < g:
                better = better + jnp.where(gsum[o] >= gsum[g], 1.0, 0.0)
            elif o > g:
                better = better + jnp.where(gsum[o] > gsum[g], 1.0, 0.0)
        best = jnp.where(better < 0.5, 1.0, 0.0)
        bests.append(best)
        for i in range(EXPERTS_PER_GROUP):
            picked.append(best * keep[g][i] * s[g * EXPERTS_PER_GROUP + i])
    den = sum(picked)
    return [p / den for p in picked], bests


def _post_kernel(x_ref, y_ref, w_ref, mod_ref, g_ref, rwh_ref, rwl_ref, rb_ref,
                 xo_ref, h2_ref, rt_ref, aux_ref):
    yw = jnp.dot(y_ref[...], w_ref[...], preferred_element_type=F32)
    xn = x_ref[...] + mod_ref[0, 2:3, :] * yw
    xo_ref[...] = xn
    h2 = _modulate(xn, g_ref[...], mod_ref[0, 3:4, :], mod_ref[0, 4:5, :])
    hi = h2.astype(BF16)
    h2_ref[...] = hi
    lo = (h2 - hi.astype(F32)).astype(BF16)
    nt = (((1,), (1,)), ((), ()))
    rwh = rwh_ref[...]
    logits = (lax.dot_general(rwh, hi, nt, preferred_element_type=F32)
              + lax.dot_general(rwh, lo, nt, preferred_element_type=F32)
              + lax.dot_general(rwl_ref[...], hi, nt, preferred_element_type=F32))
    s = _sigmoid_pair(logits)[0]
    sel = s + rb_ref[...]
    comb, bests = _route([sel[e:e + 1, :] for e in range(N_EXPERTS)],
                         [s[e:e + 1, :] for e in range(N_EXPERTS)])
    t = logits.shape[1]

    ind = jnp.concatenate(bests + [jnp.zeros((AUX_ROWS - N_EXPERT_GROUPS, t), F32)], axis=0)
    upper = jnp.where(lax.broadcasted_iota(jnp.int32, (t, t), 0)
                      <= lax.broadcasted_iota(jnp.int32, (t, t), 1), 1.0, 0.0).astype(BF16)
    cum = jnp.dot(ind.astype(BF16), upper, preferred_element_type=F32)
    counts = [cum[g:g + 1, t - 1:t] for g in range(N_EXPERT_GROUPS)]
    starts = [jnp.zeros((1, 1), F32)]
    for g in range(1, N_EXPERT_GROUPS):
        starts.append(starts[-1] + counts[g - 1])
    dest = sum(bests[g] * (starts[g] + cum[g:g + 1, :] - 1.0) for g in range(N_EXPERT_GROUPS))

    lane = lax.broadcasted_iota(jnp.int32, (1, t), 1)
    seg = sum(jnp.where(lane == k, v, 0.0) for k, v in enumerate(starts + counts))
    aux_ref[0] = jnp.concatenate([dest, seg, jnp.zeros((AUX_ROWS - 2, t), F32)], axis=0)
    table = jnp.concatenate(comb + [dest, jnp.zeros((LANES - N_EXPERTS - 1, t), F32)], axis=0)
    rt_ref[...] = table.T


def _post_call(x, y, w, mod, g, router_w, router_b, tokens_per_batch, tile=MOE_TILE):
    n, d = x.shape
    per_b = tokens_per_batch // tile
    rw_t = router_w.T
    rwh = rw_t.astype(BF16)
    rwl = (rw_t - rwh.astype(F32)).astype(BF16)
    row = lambda i: (i, 0)
    fixed = lambda i: (0, 0)
    return pl.pallas_call(
        _post_kernel,
        grid=(n // tile,),
        in_specs=[
            pl.BlockSpec((tile, d), row),
            pl.BlockSpec((tile, d), row),
            pl.BlockSpec((d, d), fixed),
            pl.BlockSpec((1, N_MOD, d), lambda i: (i // per_b, 0, 0)),
            pl.BlockSpec((1, d), fixed),
            pl.BlockSpec((N_EXPERTS, d), fixed),
            pl.BlockSpec((N_EXPERTS, d), fixed),
            pl.BlockSpec((N_EXPERTS, 1), fixed),
        ],
        out_specs=[
            pl.BlockSpec((tile, d), row),
            pl.BlockSpec((tile, d), row),
            pl.BlockSpec((tile, LANES), row),
            pl.BlockSpec((1, AUX_ROWS, tile), lambda i: (i, 0, 0)),
        ],
        out_shape=[
            jax.ShapeDtypeStruct((n, d), F32),
            jax.ShapeDtypeStruct((n, d), BF16),
            jax.ShapeDtypeStruct((n, LANES), F32),
            jax.ShapeDtypeStruct((n // tile, AUX_ROWS, tile), F32),
        ],
        compiler_params=_cparams(("parallel",)),
        name="mixer_out_router",
    )(x, y, w, mod, g.reshape(1, d), rwh, rwl, router_b.reshape(N_EXPERTS, 1))


def _moe_kernel(seg_ref, h_ref, rt_ref, aux_ref, wg_ref, wu_ref, wd_ref, x_ref, mod_ref, fg_ref,
                o_ref, xs, cws, ys, *, final_norm):
    i = pl.program_id(0)
    g = pl.program_id(1)
    t = h_ref.shape[0]

    @pl.when(g == 0)
    def _():
        dest_row = aux_ref[0, 0:1, :]
        perm = jnp.where(lax.broadcasted_iota(jnp.int32, (t, t), 0).astype(F32) == dest_row,
                         1.0, 0.0).astype(BF16)
        xs[...] = jnp.dot(perm, h_ref[...], preferred_element_type=F32).astype(BF16)
        rt = rt_ref[...]
        p1 = rt.astype(BF16)
        r1 = rt - p1.astype(F32)
        p2 = r1.astype(BF16)
        p3 = (r1 - p2.astype(F32)).astype(BF16)
        cws[...] = (jnp.dot(perm, p1, preferred_element_type=F32)
                    + jnp.dot(perm, p2, preferred_element_type=F32)
                    + jnp.dot(perm, p3, preferred_element_type=F32))
        ys[...] = jnp.zeros_like(ys)

    start = seg_ref[i, g]
    end = start + seg_ref[i, N_EXPERT_GROUPS + g]
    wd = wd_ref[...].reshape(EXPERTS_PER_GROUP * D_EXPERT, wd_ref.shape[2])

    def window(w0, rows, lo, hi):
        sl = pl.ds(pl.multiple_of(w0, ROW_ALIGN), rows)
        xw = xs[sl, :]
        cw = cws[sl, :]
        r = w0 + lax.broadcasted_iota(jnp.int32, cw.shape, 0)
        lane = lax.broadcasted_iota(jnp.int32, cw.shape, 1)
        cw = jnp.where(r >= lo, cw, 0.0)
        cw = jnp.where(r < hi, cw, 0.0)
        acts = []
        for e in range(EXPERTS_PER_GROUP):
            ce = jnp.sum(jnp.where(lane == g * EXPERTS_PER_GROUP + e, cw, 0.0), axis=1, keepdims=True)
            a = _silu_tanh(jnp.dot(xw, wg_ref[e], preferred_element_type=F32)) \
                * jnp.dot(xw, wu_ref[e], preferred_element_type=F32)
            acts.append((a * ce).astype(BF16))
        ys[sl, :] += jnp.dot(jnp.concatenate(acts, axis=1), wd, preferred_element_type=F32)

    w0 = jnp.minimum(start & -ROW_ALIGN, t - MOE_WINDOW)
    covered = w0 + MOE_WINDOW
    window(w0, MOE_WINDOW, start, jnp.minimum(end, covered))

    def extra(k, carry):
        lo = covered + k * MOE_EXTRA
        window(jnp.minimum(lo, t - MOE_EXTRA), MOE_EXTRA, lo, jnp.minimum(end, lo + MOE_EXTRA))
        return carry

    n_extra = jnp.maximum(end - covered + MOE_EXTRA - 1, 0) >> (MOE_EXTRA.bit_length() - 1)
    lax.fori_loop(0, n_extra, extra, 0)

    @pl.when(g == N_EXPERT_GROUPS - 1)
    def _():
        dest_col = rt_ref[:, DEST_LANE:DEST_LANE + 1]
        unperm = jnp.where(lax.broadcasted_iota(jnp.int32, (t, t), 1).astype(F32) == dest_col,
                           1.0, 0.0).astype(BF16)
        ff = jnp.dot(unperm, ys[...].astype(BF16), preferred_element_type=F32)
        xn = x_ref[...] + mod_ref[0, 5:6, :] * ff
        if final_norm:
            ms = jnp.mean(xn * xn, axis=-1, keepdims=True)
            xn = xn * lax.rsqrt(ms + EPS) * fg_ref[...]
        o_ref[...] = xn


def _moe_call(h2, table, aux, wg, wu, wd, x, mod, final_g, tokens_per_batch, final_norm):
    n, d = x.shape
    tile = MOE_TILE
    per_b = tokens_per_batch // tile
    seg = aux[:, 1, :2 * N_EXPERT_GROUPS].astype(jnp.int32)
    row = lambda i, g, seg: (i, 0)
    grp = lambda i, g, seg: (g, 0, 0)
    grid_spec = pltpu.PrefetchScalarGridSpec(
        num_scalar_prefetch=1,
        grid=(n // tile, N_EXPERT_GROUPS),
        in_specs=[
            pl.BlockSpec((tile, d), row),
            pl.BlockSpec((tile, LANES), row),
            pl.BlockSpec((1, AUX_ROWS, tile), lambda i, g, seg: (i, 0, 0)),
            pl.BlockSpec((EXPERTS_PER_GROUP, d, D_EXPERT), grp),
            pl.BlockSpec((EXPERTS_PER_GROUP, d, D_EXPERT), grp),
            pl.BlockSpec((EXPERTS_PER_GROUP, D_EXPERT, d), grp),
            pl.BlockSpec((tile, d), row),
            pl.BlockSpec((1, N_MOD, d), lambda i, g, seg: (i // per_b, 0, 0)),
            pl.BlockSpec((1, d), lambda i, g, seg: (0, 0)),
        ],
        out_specs=pl.BlockSpec((tile, d), row),
        scratch_shapes=[
            pltpu.VMEM((tile, d), BF16),
            pltpu.VMEM((tile, LANES), F32),
            pltpu.VMEM((tile, d), F32),
        ],
    )
    return pl.pallas_call(
        functools.partial(_moe_kernel, final_norm=final_norm),
        grid_spec=grid_spec,
        out_shape=jax.ShapeDtypeStruct((n, d), F32),
        compiler_params=_cparams(("parallel", "arbitrary")),
        name="moe_ffn",
    )(seg, h2, table, aux, wg, wu, wd, x, mod, final_g.reshape(1, d))


def kernel(x, c, ctx, c_ctx, w_mod, b_mod, norm1_g, norm2_g, hg_w_in, hg_lb_fwd, hg_lb_bwd, hg_gnorm,
           hg_w_out, pool_w_in, pool_w_grp, pool_scale, pool_w_out, router_w, router_b, moe_w_gate,
           moe_w_up, moe_w_down, final_g):
    b, l, d = x.shape
    depth = w_mod.shape[0]
    n_mixers = 2

    cc = jnp.concatenate([c, c_ctx[None, :], jnp.zeros((MOD_ROWS - b - 1, d), F32)], axis=0)
    mods = _mod_call(cc, w_mod, b_mod)

    x_lat = x.reshape(b * l, d)
    for i in range(depth):
        slot = i // n_mixers
        mod_lat = mods[i, :b].reshape(b, N_MOD, d)
        if i % n_mixers == 0:
            n_rec = N_HG_PROJ - 1
            w_rec = (hg_w_in[slot][:, :n_rec * d].reshape(d, n_rec, HG_HEADS, HG_DK)
                     .transpose(2, 0, 1, 3).reshape(HG_HEADS, d, n_rec * HG_DK).astype(BF16))
            w_gate = hg_w_in[slot][:, n_rec * d:].astype(BF16)
            h_lat, g_lat = _norm_call(x_lat.reshape(b, l, d), norm1_g[i], mod_lat, 512, w_gate)
            mod_ctx = jnp.broadcast_to(mods[i, b].reshape(1, N_MOD, d), (b, N_MOD, d))
            h_ctx = _norm_call(ctx, norm1_g[i], mod_ctx, ctx.shape[1])
            y = _hgrn_call(h_ctx, h_lat, g_lat, w_rec, hg_lb_fwd, hg_lb_bwd, hg_gnorm[slot], slot)
            w_out = hg_w_out[slot]
        else:
            h_lat = _norm_call(x_lat.reshape(b, l, d), norm1_g[i], mod_lat, 512)
            y = _pool_call(h_lat, pool_w_in[slot].astype(BF16), pool_w_grp[slot].astype(BF16),
                           pool_scale[slot])
            w_out = pool_w_out[slot]
        x_lat, h2, table, aux = _post_call(x_lat, y.reshape(b * l, d), w_out.astype(BF16), mod_lat,
                                           norm2_g[i], router_w, router_b, l)
        x_lat = _moe_call(h2, table, aux, moe_w_gate[i].astype(BF16), moe_w_up[i].astype(BF16),
                          moe_w_down[i].astype(BF16), x_lat, mod_lat, final_g, l,
                          final_norm=(i == depth - 1))
    return x_lat.reshape(b, l, d)
```

```python
import functools

import jax
import jax.numpy as jnp
from jax import lax
from jax.experimental import pallas as pl
from jax.experimental.pallas import tpu as pltpu

F32 = jnp.float32
BF16 = jnp.bfloat16

EPS = 1e-6
N_MOD = 6
HG_HEADS = 8
HG_DK = 128
HG_CHUNK = 64
N_HG_PROJ = 5
POOL_WINDOWS = (2, 4, 8, 16)
GRID_W = 64
GRID_SHIFT = GRID_W.bit_length() - 1
assert 1 << GRID_SHIFT == GRID_W
N_EXPERTS = 16
N_EXPERT_GROUPS = 4
EXPERTS_PER_GROUP = N_EXPERTS // N_EXPERT_GROUPS
D_EXPERT = 256

LANES = 128
MOD_ROWS = 8
VMEM_LIMIT = 56 * 1024 * 1024


def _cparams(sem, vmem=VMEM_LIMIT):
    return pltpu.CompilerParams(dimension_semantics=sem, vmem_limit_bytes=vmem)


def _sigmoid_pair(z):
    e = jnp.exp(-jnp.abs(z))
    r = 1.0 / (1.0 + e)
    er = e * r
    pos = z >= 0
    return jnp.where(pos, r, er), jnp.where(pos, er, r)


def _silu(z):
    return z * _sigmoid_pair(z)[0]


def _sigmoid_tanh(z):
    return 0.5 * jnp.tanh(0.5 * z) + 0.5


def _silu_tanh(z):
    return z * _sigmoid_tanh(z)


def _mod_kernel(c_ref, w_ref, b_ref, o_ref):
    a = _silu(c_ref[...])
    o_ref[0] = jnp.dot(a, w_ref[0], preferred_element_type=F32,
                       precision=lax.Precision.HIGHEST) + b_ref[0]


def _mod_call(cc, w_mod, b_mod):
    depth, d, n = w_mod.shape
    tn = 1024
    return pl.pallas_call(
        _mod_kernel,
        grid=(depth, n // tn),
        in_specs=[
            pl.BlockSpec((MOD_ROWS, d), lambda i, j: (0, 0)),
            pl.BlockSpec((1, d, tn), lambda i, j: (i, 0, j)),
            pl.BlockSpec((1, 1, tn), lambda i, j: (i, 0, j)),
        ],
        out_specs=pl.BlockSpec((1, MOD_ROWS, tn), lambda i, j: (i, 0, j)),
        out_shape=jax.ShapeDtypeStruct((depth, MOD_ROWS, n), F32),
        compiler_params=_cparams(("parallel", "parallel")),
        name="mod_proj",
    )(cc, w_mod, b_mod.reshape(depth, 1, n))


def _modulate(x, g, shift, scale):
    ms = jnp.mean(x * x, axis=-1, keepdims=True)
    return (x * lax.rsqrt(ms + EPS) * g) * (1.0 + scale) + shift


def _norm_kernel(x_ref, g_ref, mod_ref, *rest):
    h = _modulate(x_ref[0], g_ref[...], mod_ref[0, 0:1, :], mod_ref[0, 1:2, :]).astype(BF16)
    if len(rest) == 1:
        (o_ref,) = rest
    else:
        w_ref, o_ref, p_ref, wbuf = rest

        @pl.when((pl.program_id(0) == 0) & (pl.program_id(1) == 0))
        def _():
            wbuf[...] = w_ref[...].astype(BF16)

        p_ref[0] = jnp.dot(h, wbuf[...], preferred_element_type=F32).astype(BF16)
    o_ref[0] = h


def _norm_call(x, g, mod, tile, w_proj=None, w_col=0):
    b, t, d = x.shape
    tok = pl.BlockSpec((1, tile, d), lambda i, j: (i, j, 0))
    in_specs = [tok, pl.BlockSpec((1, d), lambda i, j: (0, 0)),
                pl.BlockSpec((1, N_MOD, d), lambda i, j: (i, 0, 0))]
    args = [x, g.reshape(1, d), mod]
    out_specs, out_shape = tok, jax.ShapeDtypeStruct((b, t, d), BF16)
    scratch = []
    if w_proj is not None:
        in_specs.append(pl.BlockSpec((d, d), lambda i, j: (0, w_col)))
        args.append(w_proj)
        out_specs, out_shape = [tok, tok], [out_shape, out_shape]
        scratch = [pltpu.VMEM((d, d), BF16)]
    return pl.pallas_call(
        _norm_kernel,
        grid=(b, t // tile),
        in_specs=in_specs,
        out_specs=out_specs,
        out_shape=out_shape,
        scratch_shapes=scratch,
        compiler_params=_cparams(("arbitrary", "arbitrary")),
        name="norm1",
    )(*args)


HG_BLOCK = 256
HG_CPB = HG_BLOCK // HG_CHUNK


def _chunk_prefix(x, row):
    for d in (1, 2, 4, 8, 16, 32):
        x = x + jnp.where(row >= d, pltpu.roll(x, d, axis=0), 0.0)
    return x


def _chunk_suffix(x, row):
    n = x.shape[0]
    for d in (1, 2, 4, 8, 16, 32):
        x = x + jnp.where(row < HG_CHUNK - d, pltpu.roll(x, n - d, axis=0), 0.0)
    return x


def _lower_bound(lb_ref, slot):
    rows = [lb_ref[j, 0] for j in range(lb_ref.shape[0])]
    m = functools.reduce(jnp.maximum, rows)
    es = [jnp.exp(r - m) for r in rows]
    return sum(es[:slot + 1]) / sum(es)


def _hgrn_kernel(hc_ref, hl_ref, g_ref, wq_ref, wv_ref, wf_ref, wb_ref, lbf_ref, lbb_ref, gn_ref, y_ref,
                 wbuf, pbuf0, pbuf1, pbuf2, pbuf3, oacc, qif, qib, kvf, kvb, decf, decb, spf, spb,
                 *, slot, n_ctx_chunks, n_lat_chunks):
    lb_f = _lower_bound(lbf_ref, slot)
    lb_b = _lower_bound(lbb_ref, slot)
    row = lax.broadcasted_iota(jnp.int32, (HG_BLOCK, HG_DK), 0) & (HG_CHUNK - 1)
    ci = lax.broadcasted_iota(jnp.int32, (HG_CPB, HG_CHUNK, HG_CHUNK), 1)
    si = lax.broadcasted_iota(jnp.int32, (HG_CPB, HG_CHUNK, HG_CHUNK), 2)

    def c3(t):
        return t.reshape(HG_CPB, HG_CHUNK, HG_DK)

    def direction(qs3, v3b, z, lb, fwd):
        sig = _sigmoid_tanh(z)
        f = lb + (1.0 - lb) * sig
        k3 = c3((1.0 - lb) * (1.0 - sig))
        lf = jnp.log(f)
        if fwd:
            cum = c3(_chunk_prefix(lf, row))
            ref = cum[:, HG_CHUNK // 2 - 1:HG_CHUNK // 2, :]
            last = cum[:, HG_CHUNK - 1:HG_CHUNK, :]
        else:
            cum = c3(_chunk_suffix(lf, row))
            ref = cum[:, HG_CHUNK // 2:HG_CHUNK // 2 + 1, :]
            last = cum[:, 0:1, :]
        dec = jnp.exp(last)
        if qs3 is None:
            kl = k3 * jnp.exp(last - cum)
        else:
            e1 = jnp.exp(cum - ref)
            qd = qs3 * e1
            qi = qd * jnp.exp(ref)
            kd = k3 * (1.0 / e1)
            kl = kd * jnp.exp(last - ref)
        kvt = jnp.einsum('ncv,nck->nvk', v3b, kl.astype(BF16), preferred_element_type=F32)
        if qs3 is None:
            return None, None, kvt, dec
        sc = jnp.einsum('nck,nsk->ncs', qd.astype(BF16), kd.astype(BF16), preferred_element_type=F32)
        sc = jnp.where((ci >= si) if fwd else (ci <= si), sc, 0.0)
        intra = jnp.einsum('ncs,nsv->ncv', sc.astype(BF16), v3b, preferred_element_type=F32)
        return intra, qi, kvt, dec

    for p, wp_ref in enumerate((wq_ref, wv_ref, wf_ref, wb_ref)):
        wbuf[:, p * HG_DK:(p + 1) * HG_DK] = wp_ref[...].astype(BF16)

    def project(hrows):
        return jnp.dot(hrows, wbuf[...], preferred_element_type=F32)

    def block(p, chunk0, lat_row0):
        v3b = c3(p[:, HG_DK:2 * HG_DK]).astype(BF16)
        zf = p[:, 2 * HG_DK:3 * HG_DK]
        zb = p[:, 3 * HG_DK:4 * HG_DK]
        if lat_row0 is None:
            qs3 = None
        else:
            qs3 = c3(_silu_tanh(p[:, 0:HG_DK]))
        in_f, qi_f, kv_f, dec_f = direction(qs3, v3b, zf, lb_f, True)
        in_b, qi_b, kv_b, dec_b = direction(qs3, v3b, zb, lb_b, False)
        kvf[pl.ds(chunk0, HG_CPB)] = kv_f
        kvb[pl.ds(chunk0, HG_CPB)] = kv_b
        decf[pl.ds(chunk0, HG_CPB)] = dec_f
        decb[pl.ds(chunk0, HG_CPB)] = dec_b
        if lat_row0 is not None:
            rows = pl.ds(lat_row0, HG_BLOCK)
            oacc[rows, :] = (in_f + in_b).reshape(HG_BLOCK, HG_DK)
            qif[rows, :] = qi_f.reshape(HG_BLOCK, HG_DK).astype(BF16)
            qib[rows, :] = qi_b.reshape(HG_BLOCK, HG_DK).astype(BF16)

    for i in range(n_ctx_chunks // HG_CPB):
        block(project(hc_ref[0, i * HG_BLOCK:(i + 1) * HG_BLOCK, :]), i * HG_CPB, None)

    n_blocks = n_lat_chunks // HG_CPB

    def lat_rows(i):
        return hl_ref[0, pl.ds(pl.multiple_of(i * HG_BLOCK, HG_BLOCK), HG_BLOCK), :]

    def lat_terms(p_ref, i):
        block(p_ref[...], n_ctx_chunks + i * HG_CPB, pl.multiple_of(i * HG_BLOCK, HG_BLOCK))

    pbuf0[...] = project(lat_rows(0))
    pbuf1[...] = project(lat_rows(1))

    def lat_quad(i0, last):
        pbuf2[...] = project(lat_rows(i0 + 2))
        lat_terms(pbuf0, i0)
        pbuf3[...] = project(lat_rows(i0 + 3))
        lat_terms(pbuf1, i0 + 1)
        if not last:
            pbuf0[...] = project(lat_rows(i0 + 4))
        lat_terms(pbuf2, i0 + 2)
        if not last:
            pbuf1[...] = project(lat_rows(i0 + 5))
        lat_terms(pbuf3, i0 + 3)

    def lat_body(t, carry):
        lat_quad(4 * t, False)
        return carry

    lax.fori_loop(0, n_blocks // 4 - 1, lat_body, 0)
    lat_quad(n_blocks - 4, True)

    def advance(s, kv_ref, dec_ref, n):
        return dec_ref[n] * s + kv_ref[n]

    sf = jnp.zeros((HG_DK, HG_DK), F32)
    for n in range(n_ctx_chunks):
        sf = advance(sf, kvf, decf, n)
    sb = jnp.zeros((HG_DK, HG_DK), F32)
    for n in reversed(range(n_ctx_chunks)):
        sb = advance(sb, kvb, decb, n)

    def scan_step(t, carry):
        sf, sb = carry
        jb = n_lat_chunks - 1 - t
        spf[t] = sf.astype(BF16)
        spb[jb] = sb.astype(BF16)
        return (advance(sf, kvf, decf, n_ctx_chunks + t), advance(sb, kvb, decb, n_ctx_chunks + jb))

    lax.fori_loop(0, n_lat_chunks, scan_step, (sf, sb), unroll=2)

    gn = gn_ref[...]

    def block_rows(i):
        return pl.ds(pl.multiple_of(i * HG_BLOCK, HG_BLOCK), HG_BLOCK)

    def inter(i):
        rows = block_rows(i)
        chunks = pl.ds(i * HG_CPB, HG_CPB)
        return (jnp.einsum('nck,nvk->ncv', c3(qif[rows, :]), spf[chunks], preferred_element_type=F32)
                + jnp.einsum('nck,nvk->ncv', c3(qib[rows, :]), spb[chunks], preferred_element_type=F32)
                ).reshape(HG_BLOCK, HG_DK)

    def readout(o_ref, i):
        rows = block_rows(i)
        o = oacc[rows, :] + o_ref[...]
        ms = jnp.mean(o * o, axis=-1, keepdims=True)
        o = o * lax.rsqrt(ms + EPS) * gn
        y_ref[0, rows, :] = (o * _silu_tanh(g_ref[0, rows, :].astype(F32))).astype(BF16)

    obuf0, obuf1 = pbuf0.at[:, 0:HG_DK], pbuf1.at[:, 0:HG_DK]
    obuf0[...] = inter(0)

    def readout_pair(t, carry):
        obuf1[...] = inter(2 * t + 1)
        readout(obuf0, 2 * t)
        obuf0[...] = inter(2 * t + 2)
        readout(obuf1, 2 * t + 1)
        return carry

    lax.fori_loop(0, n_blocks // 2 - 1, readout_pair, 0)
    obuf1[...] = inter(n_blocks - 1)
    readout(obuf0, n_blocks - 2)
    readout(obuf1, n_blocks - 1)


def _hgrn_call(h_ctx, h_lat, g_lat, w_in, lb_fwd, lb_bwd, gnorm, slot):
    b, lc, d = h_ctx.shape
    ll = h_lat.shape[1]
    nrow = lb_fwd.shape[0]
    ncc, nlc = lc // HG_CHUNK, ll // HG_CHUNK
    kern = functools.partial(_hgrn_kernel, slot=slot, n_ctx_chunks=ncc, n_lat_chunks=nlc)
    lb_spec = pl.BlockSpec((nrow, 1, 1, HG_DK), lambda i, h: (0, h, 0, 0))
    head_cols = pl.BlockSpec((1, ll, HG_DK), lambda i, h: (i, 0, h))
    n_rec = N_HG_PROJ - 1
    w_cols = [pl.BlockSpec((d, HG_DK), functools.partial(lambda i, h, p: (0, p * HG_HEADS + h), p=p))
              for p in range(n_rec)]
    return pl.pallas_call(
        kern,
        grid=(b, HG_HEADS),
        in_specs=[
            pl.BlockSpec((1, lc, d), lambda i, h: (i, 0, 0)),
            pl.BlockSpec((1, ll, d), lambda i, h: (i, 0, 0)),
            head_cols,
            *w_cols,
            lb_spec, lb_spec,
            pl.BlockSpec((1, HG_DK), lambda i, h: (0, 0)),
        ],
        out_specs=head_cols,
        out_shape=jax.ShapeDtypeStruct((b, ll, d), BF16),
        scratch_shapes=[
            pltpu.VMEM((d, n_rec * HG_DK), BF16),
            pltpu.VMEM((HG_BLOCK, n_rec * HG_DK), F32),
            pltpu.VMEM((HG_BLOCK, n_rec * HG_DK), F32),
            pltpu.VMEM((HG_BLOCK, n_rec * HG_DK), F32),
            pltpu.VMEM((HG_BLOCK, n_rec * HG_DK), F32),
            pltpu.VMEM((ll, HG_DK), F32),
            pltpu.VMEM((ll, HG_DK), BF16),
            pltpu.VMEM((ll, HG_DK), BF16),
            pltpu.VMEM((ncc + nlc, HG_DK, HG_DK), F32),
            pltpu.VMEM((ncc + nlc, HG_DK, HG_DK), F32),
            pltpu.VMEM((ncc + nlc, 1, HG_DK), F32),
            pltpu.VMEM((ncc + nlc, 1, HG_DK), F32),
            pltpu.VMEM((nlc, HG_DK, HG_DK), BF16),
            pltpu.VMEM((nlc, HG_DK, HG_DK), BF16),
        ],
        compiler_params=_cparams(("parallel", "arbitrary")),
        name="hgrn2",
    )(h_ctx, h_lat, g_lat, *([w_in] * n_rec),
      lb_fwd.reshape(nrow, HG_HEADS, 1, HG_DK), lb_bwd.reshape(nrow, HG_HEADS, 1, HG_DK),
      gnorm.reshape(1, HG_DK))


POOL_STRIP = 8
POOL_PAD = 8
POOL_BAND = 256


def _pool_group(h_ref, win_ref, wg_ref, ps_ref, o_ref, upad, *, win, rows):
    gd = win_ref.shape[1]
    half = win // 2
    u = jnp.dot(h_ref[0], win_ref[...].astype(BF16), preferred_element_type=F32)
    zeros = jnp.zeros((POOL_PAD, GRID_W, gd), F32)
    upad[0:POOL_PAD] = zeros
    upad[POOL_PAD + rows:POOL_PAD + rows + POOL_PAD] = zeros
    upad[POOL_PAD:POOL_PAD + rows] = u.reshape(rows, GRID_W, gd)

    tok = POOL_STRIP * GRID_W
    bi = lax.broadcasted_iota(jnp.int32, (POOL_BAND, POOL_BAND), 0)
    bj = lax.broadcasted_iota(jnp.int32, (POOL_BAND, POOL_BAND), 1)
    lo_c = (bi & (GRID_W - 1)) - half
    cj = bj & (GRID_W - 1)
    band = jnp.where(bi >> GRID_SHIFT == bj >> GRID_SHIFT, 1.0, 0.0)
    band = jnp.where(cj >= lo_c, band, 0.0)
    band = jnp.where(cj < lo_c + win, band, 0.0).astype(BF16)
    t = lax.broadcasted_iota(jnp.int32, (tok, gd), 0)
    col = t & (GRID_W - 1)
    cnt_c = jnp.minimum(col - half + win, GRID_W) - jnp.maximum(col - half, 0)
    wg = wg_ref[0].astype(BF16)
    ps = ps_ref[...]

    def strip(i, carry):
        r0 = i * POOL_STRIP
        slab = upad[pl.ds(r0 + POOL_PAD - half, POOL_STRIP + win - 1)]
        span = 1
        while span < win:
            n = slab.shape[0] - span
            slab = slab[0:n] + slab[span:span + n]
            span *= 2
        rs = slab.reshape(tok, gd)
        hi = rs.astype(BF16)
        lo = (rs - hi.astype(F32)).astype(BF16)
        parts = []
        for k in range(tok // POOL_BAND):
            sl = slice(k * POOL_BAND, (k + 1) * POOL_BAND)
            parts.append(jnp.dot(band, hi[sl], preferred_element_type=F32)
                         + jnp.dot(band, lo[sl], preferred_element_type=F32))
        box = jnp.concatenate(parts, axis=0)
        r = r0 + (t >> GRID_SHIFT)
        cnt_r = jnp.minimum(r - half + win, rows) - jnp.maximum(r - half, 0)
        mean = box / (cnt_r * cnt_c).astype(F32)
        ug = upad[pl.ds(r0 + POOL_PAD, POOL_STRIP)].reshape(tok, gd)
        z = jnp.dot((mean - ug).astype(BF16), wg, preferred_element_type=F32) * ps
        o_ref[0, pl.ds(pl.multiple_of(i * tok, tok), tok), :] = z.astype(BF16)
        return carry

    lax.fori_loop(0, rows // POOL_STRIP, strip, 0)


def _pool_kernel(h_ref, win_ref, wg_ref, ps_ref, o_ref, upad, *, rows):
    g = pl.program_id(1)
    for gi, win in enumerate(POOL_WINDOWS):
        @pl.when(g == gi)
        def _(win=win):
            _pool_group(h_ref, win_ref, wg_ref, ps_ref, o_ref, upad, win=win, rows=rows)


def _pool_call(h, w_in, w_grp, p_scale):
    b, l, d = h.shape
    ng = len(POOL_WINDOWS)
    gd = d // ng
    rows = l // GRID_W
    return pl.pallas_call(
        functools.partial(_pool_kernel, rows=rows),
        grid=(b, ng),
        in_specs=[
            pl.BlockSpec((1, l, d), lambda i, g: (i, 0, 0)),
            pl.BlockSpec((d, gd), lambda i, g: (0, g)),
            pl.BlockSpec((1, gd, gd), lambda i, g: (g, 0, 0)),
            pl.BlockSpec((1, gd), lambda i, g: (0, g)),
        ],
        out_specs=pl.BlockSpec((1, l, gd), lambda i, g: (i, 0, g)),
        out_shape=jax.ShapeDtypeStruct((b, l, d), BF16),
        scratch_shapes=[pltpu.VMEM((rows + 2 * POOL_PAD, GRID_W, gd), F32)],
        compiler_params=_cparams(("parallel", "arbitrary")),
        name="pool_mix",
    )(h, w_in, w_grp, p_scale.reshape(1, d))


MOE_TILE = 1024
MOE_WINDOW = 320
MOE_EXTRA = 64
ROW_ALIGN = 16
DEST_LANE = N_EXPERTS
PIECE_STRIDE = 32
N_PIECES = 3
AUX_ROWS = 8


def _route(sel, s):
    keep = []
    gsum = []
    for g in range(N_EXPERT_GROUPS):
        a = sel[g * EXPERTS_PER_GROUP:(g + 1) * EXPERTS_PER_GROUP]
        beaten = [jnp.zeros_like(a[0]) for _ in a]
        for i in range(EXPERTS_PER_GROUP):
            for j in range(i + 1, EXPERTS_PER_GROUP):
                ge = jnp.where(a[i] >= a[j], 1.0, 0.0)
                beaten[j] = beaten[j] + ge
                beaten[i] = beaten[i] + (1.0 - ge)
        kg = [jnp.where(bt < 1.5, 1.0, 0.0) for bt in beaten]
        keep.append(kg)
        gsum.append(sum(k * x for k, x in zip(kg, a)))
    picked = []
    bests = []
    for g in range(N_EXPERT_GROUPS):
        better = jnp.zeros_like(gsum[0])
        for o in range(N_EXPERT_GROUPS):
            if o < g:
                better = better + jnp.where(gsum[o] >= gsum[g], 1.0, 0.0)
            elif o > g:
                better = better + jnp.where(gsum[o] > gsum[g], 1.0, 0.0)
        best = jnp.where(better < 0.5, 1.0, 0.0)
        bests.append(best)
        for i in range(EXPERTS_PER_GROUP):
            picked.append(best * keep[g][i] * s[g * EXPERTS_PER_GROUP + i])
    den = sum(picked)
    return [p / den for p in picked], bests


def _post_kernel(x_ref, y_ref, w_ref, mod_ref, g_ref, rwh_ref, rwl_ref, rb_ref,
                 xo_ref, h2_ref, rt_ref, aux_ref, wbuf):
    @pl.when(pl.program_id(0) == 0)
    def _():
        wbuf[...] = w_ref[...].astype(BF16)

    yw = jnp.dot(y_ref[...], wbuf[...], preferred_element_type=F32)
    xn = x_ref[...] + mod_ref[0, 2:3, :] * yw
    xo_ref[...] = xn
    h2 = _modulate(xn, g_ref[...], mod_ref[0, 3:4, :], mod_ref[0, 4:5, :])
    hi = h2.astype(BF16)
    h2_ref[...] = hi
    lo = (h2 - hi.astype(F32)).astype(BF16)
    nt = (((1,), (1,)), ((), ()))
    rwh = rwh_ref[...]
    logits = (lax.dot_general(rwh, hi, nt, preferred_element_type=F32)
              + lax.dot_general(rwh, lo, nt, preferred_element_type=F32)
              + lax.dot_general(rwl_ref[...], hi, nt, preferred_element_type=F32))
    s = _sigmoid_pair(logits)[0]
    sel = s + rb_ref[...]
    comb, bests = _route([sel[e:e + 1, :] for e in range(N_EXPERTS)],
                         [s[e:e + 1, :] for e in range(N_EXPERTS)])
    t = logits.shape[1]

    ind = jnp.concatenate(bests + [jnp.zeros((AUX_ROWS - N_EXPERT_GROUPS, t), F32)], axis=0)
    upper = jnp.where(lax.broadcasted_iota(jnp.int32, (t, t), 0)
                      <= lax.broadcasted_iota(jnp.int32, (t, t), 1), 1.0, 0.0).astype(BF16)
    cum = jnp.dot(ind.astype(BF16), upper, preferred_element_type=F32)
    counts = [cum[g:g + 1, t - 1:t] for g in range(N_EXPERT_GROUPS)]
    starts = [jnp.zeros((1, 1), F32)]
    for g in range(1, N_EXPERT_GROUPS):
        starts.append(starts[-1] + counts[g - 1])
    dest = sum(bests[g] * (starts[g] + cum[g:g + 1, :] - 1.0) for g in range(N_EXPERT_GROUPS))

    lane = lax.broadcasted_iota(jnp.int32, (1, t), 1)
    seg = sum(jnp.where(lane == k, v, 0.0) for k, v in enumerate(starts + counts))
    aux_ref[0] = jnp.concatenate([dest, seg, jnp.zeros((AUX_ROWS - 2, t), F32)], axis=0)
    p1 = [w.astype(BF16).astype(F32) for w in comb]
    r1 = [w - p for w, p in zip(comb, p1)]
    p2 = [r.astype(BF16).astype(F32) for r in r1]
    p3 = [r - p for r, p in zip(r1, p2)]
    pad = [jnp.zeros((PIECE_STRIDE - N_EXPERTS, t), F32)]
    table = jnp.concatenate(p1 + [dest] + [jnp.zeros((PIECE_STRIDE - N_EXPERTS - 1, t), F32)] + p2 + pad + p3 + pad
                            + [jnp.zeros((LANES - N_PIECES * PIECE_STRIDE, t), F32)], axis=0)
    rt_ref[...] = table.T


def _post_call(x, y, w, mod, g, router_w, router_b, tokens_per_batch, tile=MOE_TILE):
    n, d = x.shape
    per_b = tokens_per_batch // tile
    rw_t = router_w.T
    rwh = rw_t.astype(BF16)
    rwl = (rw_t - rwh.astype(F32)).astype(BF16)
    row = lambda i: (i, 0)
    fixed = lambda i: (0, 0)
    return pl.pallas_call(
        _post_kernel,
        grid=(n // tile,),
        in_specs=[
            pl.BlockSpec((tile, d), row),
            pl.BlockSpec((tile, d), row),
            pl.BlockSpec((d, d), fixed),
            pl.BlockSpec((1, N_MOD, d), lambda i: (i // per_b, 0, 0)),
            pl.BlockSpec((1, d), fixed),
            pl.BlockSpec((N_EXPERTS, d), fixed),
            pl.BlockSpec((N_EXPERTS, d), fixed),
            pl.BlockSpec((N_EXPERTS, 1), fixed),
        ],
        out_specs=[
            pl.BlockSpec((tile, d), row),
            pl.BlockSpec((tile, d), row),
            pl.BlockSpec((tile, LANES), row),
            pl.BlockSpec((1, AUX_ROWS, tile), lambda i: (i, 0, 0)),
        ],
        out_shape=[
            jax.ShapeDtypeStruct((n, d), F32),
            jax.ShapeDtypeStruct((n, d), BF16),
            jax.ShapeDtypeStruct((n, LANES), F32),
            jax.ShapeDtypeStruct((n // tile, AUX_ROWS, tile), F32),
        ],
        scratch_shapes=[pltpu.VMEM((d, d), BF16)],
        compiler_params=_cparams(("arbitrary",)),
        name="mixer_out_router",
    )(x, y, w, mod, g.reshape(1, d), rwh, rwl, router_b.reshape(N_EXPERTS, 1))


def _moe_kernel(seg_ref, h_ref, rt_ref, aux_ref, wg_ref, wu_ref, wd_ref, x_ref, mod_ref, fg_ref,
                o_ref, xs, cws, ys, *, final_norm):
    i = pl.program_id(0)
    g = pl.program_id(1)
    t = h_ref.shape[0]

    @pl.when(g == 0)
    def _():
        dest_row = aux_ref[0, 0:1, :]
        perm = jnp.where(lax.broadcasted_iota(jnp.int32, (t, t), 0).astype(F32) == dest_row,
                         1.0, 0.0).astype(BF16)
        xs[...] = jnp.dot(perm, h_ref[...], preferred_element_type=F32).astype(BF16)
        cws[...] = jnp.dot(perm, rt_ref[...].astype(BF16), preferred_element_type=F32)
        ys[...] = jnp.zeros_like(ys)

    start = seg_ref[i, g]
    end = start + seg_ref[i, N_EXPERT_GROUPS + g]
    wd = wd_ref[...].reshape(EXPERTS_PER_GROUP * D_EXPERT, wd_ref.shape[2])

    def window(w0, rows, lo, hi):
        sl = pl.ds(pl.multiple_of(w0, ROW_ALIGN), rows)
        xw = xs[sl, :]
        cw = cws[sl, :]
        r = w0 + lax.broadcasted_iota(jnp.int32, cw.shape, 0)
        lane = lax.broadcasted_iota(jnp.int32, cw.shape, 1)
        piece_lane = jnp.where(lane < N_PIECES * PIECE_STRIDE, lane & (PIECE_STRIDE - 1), -1)
        cw = jnp.where(r >= lo, cw, 0.0)
        cw = jnp.where(r < hi, cw, 0.0)
        acts = []
        for e in range(EXPERTS_PER_GROUP):
            ce = jnp.sum(jnp.where(piece_lane == g * EXPERTS_PER_GROUP + e, cw, 0.0), axis=1, keepdims=True)
            a = _silu_tanh(jnp.dot(xw, wg_ref[e], preferred_element_type=F32)) \
                * jnp.dot(xw, wu_ref[e], preferred_element_type=F32)
            acts.append((a * ce).astype(BF16))
        ys[sl, :] += jnp.dot(jnp.concatenate(acts, axis=1), wd, preferred_element_type=F32)

    w0 = jnp.minimum(start & -ROW_ALIGN, t - MOE_WINDOW)
    covered = w0 + MOE_WINDOW
    window(w0, MOE_WINDOW, start, jnp.minimum(end, covered))

    def extra(k, carry):
        lo = covered + k * MOE_EXTRA
        window(jnp.minimum(lo, t - MOE_EXTRA), MOE_EXTRA, lo, jnp.minimum(end, lo + MOE_EXTRA))
        return carry

    n_extra = jnp.maximum(end - covered + MOE_EXTRA - 1, 0) >> (MOE_EXTRA.bit_length() - 1)
    lax.fori_loop(0, n_extra, extra, 0)

    @pl.when(g == N_EXPERT_GROUPS - 1)
    def _():
        dest_col = rt_ref[:, DEST_LANE:DEST_LANE + 1]
        unperm = jnp.where(lax.broadcasted_iota(jnp.int32, (t, t), 1).astype(F32) == dest_col,
                           1.0, 0.0).astype(BF16)
        ff = jnp.dot(unperm, ys[...].astype(BF16), preferred_element_type=F32)
        xn = x_ref[...] + mod_ref[0, 5:6, :] * ff
        if final_norm:
            ms = jnp.mean(xn * xn, axis=-1, keepdims=True)
            xn = xn * lax.rsqrt(ms + EPS) * fg_ref[...]
        o_ref[...] = xn


def _moe_call(h2, table, aux, wg, wu, wd, x, mod, final_g, tokens_per_batch, final_norm):
    n, d = x.shape
    tile = MOE_TILE
    per_b = tokens_per_batch // tile
    seg = aux[:, 1, :2 * N_EXPERT_GROUPS].astype(jnp.int32)
    row = lambda i, g, seg: (i, 0)
    grp = lambda i, g, seg: (g, 0, 0)
    grid_spec = pltpu.PrefetchScalarGridSpec(
        num_scalar_prefetch=1,
        grid=(n // tile, N_EXPERT_GROUPS),
        in_specs=[
            pl.BlockSpec((tile, d), row),
            pl.BlockSpec((tile, LANES), row),
            pl.BlockSpec((1, AUX_ROWS, tile), lambda i, g, seg: (i, 0, 0)),
            pl.BlockSpec((EXPERTS_PER_GROUP, d, D_EXPERT), grp),
            pl.BlockSpec((EXPERTS_PER_GROUP, d, D_EXPERT), grp),
            pl.BlockSpec((EXPERTS_PER_GROUP, D_EXPERT, d), grp),
            pl.BlockSpec((tile, d), row),
            pl.BlockSpec((1, N_MOD, d), lambda i, g, seg: (i // per_b, 0, 0)),
            pl.BlockSpec((1, d), lambda i, g, seg: (0, 0)),
        ],
        out_specs=pl.BlockSpec((tile, d), row),
        scratch_shapes=[
            pltpu.VMEM((tile, d), BF16),
            pltpu.VMEM((tile, LANES), F32),
            pltpu.VMEM((tile, d), F32),
        ],
    )
    return pl.pallas_call(
        functools.partial(_moe_kernel, final_norm=final_norm),
        grid_spec=grid_spec,
        out_shape=jax.ShapeDtypeStruct((n, d), F32),
        compiler_params=_cparams(("parallel", "arbitrary")),
        name="moe_ffn",
    )(seg, h2, table, aux, wg, wu, wd, x, mod, final_g.reshape(1, d))


def kernel(x, c, ctx, c_ctx, w_mod, b_mod, norm1_g, norm2_g, hg_w_in, hg_lb_fwd, hg_lb_bwd, hg_gnorm,
           hg_w_out, pool_w_in, pool_w_grp, pool_scale, pool_w_out, router_w, router_b, moe_w_gate,
           moe_w_up, moe_w_down, final_g):
    b, l, d = x.shape
    depth = w_mod.shape[0]
    n_mixers = 2

    cc = jnp.concatenate([c, c_ctx[None, :], jnp.zeros((MOD_ROWS - b - 1, d), F32)], axis=0)
    mods = _mod_call(cc, w_mod, b_mod)

    x_lat = x.reshape(b * l, d)
    for i in range(depth):
        slot = i // n_mixers
        mod_lat = mods[i, :b].reshape(b, N_MOD, d)
        if i % n_mixers == 0:
            h_lat, g_lat = _norm_call(x_lat.reshape(b, l, d), norm1_g[i], mod_lat, 512,
                                      hg_w_in[slot], N_HG_PROJ - 1)
            mod_ctx = jnp.broadcast_to(mods[i, b].reshape(1, N_MOD, d), (b, N_MOD, d))
            h_ctx = _norm_call(ctx, norm1_g[i], mod_ctx, ctx.shape[1])
            y = _hgrn_call(h_ctx, h_lat, g_lat, hg_w_in[slot], hg_lb_fwd, hg_lb_bwd, hg_gnorm[slot], slot)
            w_out = hg_w_out[slot]
        else:
            h_lat = _norm_call(x_lat.reshape(b, l, d), norm1_g[i], mod_lat, 512)
            y = _pool_call(h_lat, pool_w_in[slot], pool_w_grp[slot], pool_scale[slot])
            w_out = pool_w_out[slot]
        x_lat, h2, table, aux = _post_call(x_lat, y.reshape(b * l, d), w_out, mod_lat,
                                           norm2_g[i], router_w, router_b, l)
        x_lat = _moe_call(h2, table, aux, moe_w_gate[i].astype(BF16), moe_w_up[i].astype(BF16),
                          moe_w_down[i].astype(BF16), x_lat, mod_lat, final_g, l,
                          final_norm=(i == depth - 1))
    return x_lat.reshape(b, l, d)
```

```python
import functools

import jax
import jax.numpy as jnp
from jax import lax
from jax.experimental import pallas as pl
from jax.experimental.pallas import tpu as pltpu

F32 = jnp.float32
BF16 = jnp.bfloat16

EPS = 1e-6
N_MOD = 6
HG_HEADS = 8
HG_DK = 128
HG_CHUNK = 64
N_HG_PROJ = 5
POOL_WINDOWS = (2, 4, 8, 16)
GRID_W = 64
GRID_SHIFT = GRID_W.bit_length() - 1
assert 1 << GRID_SHIFT == GRID_W
N_EXPERTS = 16
N_EXPERT_GROUPS = 4
EXPERTS_PER_GROUP = N_EXPERTS // N_EXPERT_GROUPS
D_EXPERT = 256

LANES = 128
MOD_ROWS = 8
VMEM_LIMIT = 56 * 1024 * 1024


def _cparams(sem, vmem=VMEM_LIMIT):
    return pltpu.CompilerParams(dimension_semantics=sem, vmem_limit_bytes=vmem)


def _sigmoid_pair(z):
    e = jnp.exp(-jnp.abs(z))
    r = 1.0 / (1.0 + e)
    er = e * r
    pos = z >= 0
    return jnp.where(pos, r, er), jnp.where(pos, er, r)


def _silu(z):
    return z * _sigmoid_pair(z)[0]


def _sigmoid_tanh(z):
    return 0.5 * jnp.tanh(0.5 * z) + 0.5


def _silu_tanh(z):
    return z * _sigmoid_tanh(z)


def _mod_kernel(c_ref, w_ref, b_ref, o_ref):
    a = _silu(c_ref[...])
    o_ref[0] = jnp.dot(a, w_ref[0], preferred_element_type=F32,
                       precision=lax.Precision.HIGHEST) + b_ref[0]


def _mod_call(cc, w_mod, b_mod):
    depth, d, n = w_mod.shape
    tn = 1024
    return pl.pallas_call(
        _mod_kernel,
        grid=(depth, n // tn),
        in_specs=[
            pl.BlockSpec((MOD_ROWS, d), lambda i, j: (0, 0)),
            pl.BlockSpec((1, d, tn), lambda i, j: (i, 0, j)),
            pl.BlockSpec((1, 1, tn), lambda i, j: (i, 0, j)),
        ],
        out_specs=pl.BlockSpec((1, MOD_ROWS, tn), lambda i, j: (i, 0, j)),
        out_shape=jax.ShapeDtypeStruct((depth, MOD_ROWS, n), F32),
        compiler_params=_cparams(("parallel", "parallel")),
        name="mod_proj",
    )(cc, w_mod, b_mod.reshape(depth, 1, n))


def _modulate(x, g, shift, scale):
    ms = jnp.mean(x * x, axis=-1, keepdims=True)
    return (x * lax.rsqrt(ms + EPS) * g) * (1.0 + scale) + shift


def _norm_kernel(x_ref, g_ref, mod_ref, *rest):
    h = _modulate(x_ref[0], g_ref[...], mod_ref[0, 0:1, :], mod_ref[0, 1:2, :]).astype(BF16)
    if len(rest) == 1:
        (o_ref,) = rest
    else:
        w_ref, o_ref, p_ref, wbuf = rest

        @pl.when((pl.program_id(0) == 0) & (pl.program_id(1) == 0))
        def _():
            wbuf[...] = w_ref[...].astype(BF16)

        p_ref[0] = jnp.dot(h, wbuf[...], preferred_element_type=F32).astype(BF16)
    o_ref[0] = h


def _norm_call(x, g, mod, tile, w_proj=None, w_col=0):
    b, t, d = x.shape
    tok = pl.BlockSpec((1, tile, d), lambda i, j: (i, j, 0))
    in_specs = [tok, pl.BlockSpec((1, d), lambda i, j: (0, 0)),
                pl.BlockSpec((1, N_MOD, d), lambda i, j: (i, 0, 0))]
    args = [x, g.reshape(1, d), mod]
    out_specs, out_shape = tok, jax.ShapeDtypeStruct((b, t, d), BF16)
    scratch = []
    if w_proj is not None:
        in_specs.append(pl.BlockSpec((d, d), lambda i, j: (0, w_col)))
        args.append(w_proj)
        out_specs, out_shape = [tok, tok], [out_shape, out_shape]
        scratch = [pltpu.VMEM((d, d), BF16)]
    return pl.pallas_call(
        _norm_kernel,
        grid=(b, t // tile),
        in_specs=in_specs,
        out_specs=out_specs,
        out_shape=out_shape,
        scratch_shapes=scratch,
        compiler_params=_cparams(("arbitrary", "arbitrary")),
        name="norm1",
    )(*args)


HG_BLOCK = 256
HG_CPB = HG_BLOCK // HG_CHUNK


def _chunk_prefix(x, row):
    for d in (1, 2, 4, 8, 16, 32):
        x = x + jnp.where(row >= d, pltpu.roll(x, d, axis=0), 0.0)
    return x


def _chunk_suffix(x, row):
    n = x.shape[0]
    for d in (1, 2, 4, 8, 16, 32):
        x = x + jnp.where(row < HG_CHUNK - d, pltpu.roll(x, n - d, axis=0), 0.0)
    return x


def _lower_bound(lb_ref, slot):
    rows = [lb_ref[j, 0] for j in range(lb_ref.shape[0])]
    m = functools.reduce(jnp.maximum, rows)
    es = [jnp.exp(r - m) for r in rows]
    return sum(es[:slot + 1]) / sum(es)


def _hgrn_kernel(hc_ref, hl_ref, g_ref, wq_ref, wv_ref, wf_ref, wb_ref, lbf_ref, lbb_ref, gn_ref, y_ref,
                 wbuf, pbuf0, pbuf1, pbuf2, pbuf3, oacc, qif, qib, kvf, kvb, decf, decb, spf, spb,
                 *, slot, n_ctx_chunks, n_lat_chunks):
    lb_f = _lower_bound(lbf_ref, slot)
    lb_b = _lower_bound(lbb_ref, slot)
    row = lax.broadcasted_iota(jnp.int32, (HG_BLOCK, HG_DK), 0) & (HG_CHUNK - 1)
    ci = lax.broadcasted_iota(jnp.int32, (HG_CPB, HG_CHUNK, HG_CHUNK), 1)
    si = lax.broadcasted_iota(jnp.int32, (HG_CPB, HG_CHUNK, HG_CHUNK), 2)

    def c3(t):
        return t.reshape(HG_CPB, HG_CHUNK, HG_DK)

    def direction(qs3, v3b, z, lb, fwd):
        sig = _sigmoid_tanh(z)
        f = lb + (1.0 - lb) * sig
        k3 = c3((1.0 - lb) * (1.0 - sig))
        lf = jnp.log(f)
        if fwd:
            cum = c3(_chunk_prefix(lf, row))
            ref = cum[:, HG_CHUNK // 2 - 1:HG_CHUNK // 2, :]
            last = cum[:, HG_CHUNK - 1:HG_CHUNK, :]
        else:
            cum = c3(_chunk_suffix(lf, row))
            ref = cum[:, HG_CHUNK // 2:HG_CHUNK // 2 + 1, :]
            last = cum[:, 0:1, :]
        dec = jnp.exp(last)
        if qs3 is None:
            kl = k3 * jnp.exp(last - cum)
        else:
            e1 = jnp.exp(cum - ref)
            qd = qs3 * e1
            qi = qd * jnp.exp(ref)
            kd = k3 * (1.0 / e1)
            kl = kd * jnp.exp(last - ref)
        kvt = jnp.einsum('ncv,nck->nvk', v3b, kl.astype(BF16), preferred_element_type=F32)
        if qs3 is None:
            return None, None, kvt, dec
        sc = jnp.einsum('nck,nsk->ncs', qd.astype(BF16), kd.astype(BF16), preferred_element_type=F32)
        sc = jnp.where((ci >= si) if fwd else (ci <= si), sc, 0.0)
        intra = jnp.einsum('ncs,nsv->ncv', sc.astype(BF16), v3b, preferred_element_type=F32)
        return intra, qi, kvt, dec

    for p, wp_ref in enumerate((wq_ref, wv_ref, wf_ref, wb_ref)):
        wbuf[:, p * HG_DK:(p + 1) * HG_DK] = wp_ref[...].astype(BF16)

    def project(hrows):
        return jnp.dot(hrows, wbuf[...], preferred_element_type=F32)

    def block(p, chunk0, lat_row0):
        v3b = c3(p[:, HG_DK:2 * HG_DK]).astype(BF16)
        zf = p[:, 2 * HG_DK:3 * HG_DK]
        zb = p[:, 3 * HG_DK:4 * HG_DK]
        if lat_row0 is None:
            qs3 = None
        else:
            qs3 = c3(_silu_tanh(p[:, 0:HG_DK]))
        in_f, qi_f, kv_f, dec_f = direction(qs3, v3b, zf, lb_f, True)
        in_b, qi_b, kv_b, dec_b = direction(qs3, v3b, zb, lb_b, False)
        kvf[pl.ds(chunk0, HG_CPB)] = kv_f
        kvb[pl.ds(chunk0, HG_CPB)] = kv_b
        decf[pl.ds(chunk0, HG_CPB)] = dec_f
        decb[pl.ds(chunk0, HG_CPB)] = dec_b
        if lat_row0 is not None:
            rows = pl.ds(lat_row0, HG_BLOCK)
            oacc[rows, :] = (in_f + in_b).reshape(HG_BLOCK, HG_DK)
            qif[rows, :] = qi_f.reshape(HG_BLOCK, HG_DK).astype(BF16)
            qib[rows, :] = qi_b.reshape(HG_BLOCK, HG_DK).astype(BF16)

    for i in range(n_ctx_chunks // HG_CPB):
        block(project(hc_ref[0, i * HG_BLOCK:(i + 1) * HG_BLOCK, :]), i * HG_CPB, None)

    n_blocks = n_lat_chunks // HG_CPB

    def lat_rows(i):
        return hl_ref[0, pl.ds(pl.multiple_of(i * HG_BLOCK, HG_BLOCK), HG_BLOCK), :]

    def lat_terms(p_ref, i):
        block(p_ref[...], n_ctx_chunks + i * HG_CPB, pl.multiple_of(i * HG_BLOCK, HG_BLOCK))

    pbuf0[...] = project(lat_rows(0))
    pbuf1[...] = project(lat_rows(1))

    def lat_quad(i0, last):
        pbuf2[...] = project(lat_rows(i0 + 2))
        lat_terms(pbuf0, i0)
        pbuf3[...] = project(lat_rows(i0 + 3))
        lat_terms(pbuf1, i0 + 1)
        if not last:
            pbuf0[...] = project(lat_rows(i0 + 4))
        lat_terms(pbuf2, i0 + 2)
        if not last:
            pbuf1[...] = project(lat_rows(i0 + 5))
        lat_terms(pbuf3, i0 + 3)

    def lat_body(t, carry):
        lat_quad(4 * t, False)
        return carry

    lax.fori_loop(0, n_blocks // 4 - 1, lat_body, 0)
    lat_quad(n_blocks - 4, True)

    def advance(s, kv_ref, dec_ref, n):
        return dec_ref[n] * s + kv_ref[n]

    sf = jnp.zeros((HG_DK, HG_DK), F32)
    for n in range(n_ctx_chunks):
        sf = advance(sf, kvf, decf, n)
    sb = jnp.zeros((HG_DK, HG_DK), F32)
    for n in reversed(range(n_ctx_chunks)):
        sb = advance(sb, kvb, decb, n)

    def scan_step(t, carry):
        sf, sb = carry
        jb = n_lat_chunks - 1 - t
        spf[t] = sf.astype(BF16)
        spb[jb] = sb.astype(BF16)
        return (advance(sf, kvf, decf, n_ctx_chunks + t), advance(sb, kvb, decb, n_ctx_chunks + jb))

    lax.fori_loop(0, n_lat_chunks, scan_step, (sf, sb), unroll=2)

    gn = gn_ref[...]

    def block_rows(i):
        return pl.ds(pl.multiple_of(i * HG_BLOCK, HG_BLOCK), HG_BLOCK)

    def inter(i):
        rows = block_rows(i)
        chunks = pl.ds(i * HG_CPB, HG_CPB)
        return (jnp.einsum('nck,nvk->ncv', c3(qif[rows, :]), spf[chunks], preferred_element_type=F32)
                + jnp.einsum('nck,nvk->ncv', c3(qib[rows, :]), spb[chunks], preferred_element_type=F32)
                ).reshape(HG_BLOCK, HG_DK)

    def readout(o_ref, i):
        rows = block_rows(i)
        o = oacc[rows, :] + o_ref[...]
        ms = jnp.mean(o * o, axis=-1, keepdims=True)
        o = o * lax.rsqrt(ms + EPS) * gn
        y_ref[0, rows, :] = (o * _silu_tanh(g_ref[0, rows, :].astype(F32))).astype(BF16)

    obuf0, obuf1 = pbuf0.at[:, 0:HG_DK], pbuf1.at[:, 0:HG_DK]
    obuf0[...] = inter(0)

    def readout_pair(t, carry):
        obuf1[...] = inter(2 * t + 1)
        readout(obuf0, 2 * t)
        obuf0[...] = inter(2 * t + 2)
        readout(obuf1, 2 * t + 1)
        return carry

    lax.fori_loop(0, n_blocks // 2 - 1, readout_pair, 0)
    obuf1[...] = inter(n_blocks - 1)
    readout(obuf0, n_blocks - 2)
    readout(obuf1, n_blocks - 1)


def _hgrn_call(h_ctx, h_lat, g_lat, w_in, lb_fwd, lb_bwd, gnorm, slot):
    b, lc, d = h_ctx.shape
    ll = h_lat.shape[1]
    nrow = lb_fwd.shape[0]
    ncc, nlc = lc // HG_CHUNK, ll // HG_CHUNK
    kern = functools.partial(_hgrn_kernel, slot=slot, n_ctx_chunks=ncc, n_lat_chunks=nlc)
    lb_spec = pl.BlockSpec((nrow, 1, 1, HG_DK), lambda i, h: (0, h, 0, 0))
    head_cols = pl.BlockSpec((1, ll, HG_DK), lambda i, h: (i, 0, h))
    n_rec = N_HG_PROJ - 1
    w_cols = [pl.BlockSpec((d, HG_DK), functools.partial(lambda i, h, p: (0, p * HG_HEADS + h), p=p))
              for p in range(n_rec)]
    return pl.pallas_call(
        kern,
        grid=(b, HG_HEADS),
        in_specs=[
            pl.BlockSpec((1, lc, d), lambda i, h: (i, 0, 0)),
            pl.BlockSpec((1, ll, d), lambda i, h: (i, 0, 0)),
            head_cols,
            *w_cols,
            lb_spec, lb_spec,
            pl.BlockSpec((1, HG_DK), lambda i, h: (0, 0)),
        ],
        out_specs=head_cols,
        out_shape=jax.ShapeDtypeStruct((b, ll, d), BF16),
        scratch_shapes=[
            pltpu.VMEM((d, n_rec * HG_DK), BF16),
            pltpu.VMEM((HG_BLOCK, n_rec * HG_DK), F32),
            pltpu.VMEM((HG_BLOCK, n_rec * HG_DK), F32),
            pltpu.VMEM((HG_BLOCK, n_rec * HG_DK), F32),
            pltpu.VMEM((HG_BLOCK, n_rec * HG_DK), F32),
            pltpu.VMEM((ll, HG_DK), F32),
            pltpu.VMEM((ll, HG_DK), BF16),
            pltpu.VMEM((ll, HG_DK), BF16),
            pltpu.VMEM((ncc + nlc, HG_DK, HG_DK), F32),
            pltpu.VMEM((ncc + nlc, HG_DK, HG_DK), F32),
            pltpu.VMEM((ncc + nlc, 1, HG_DK), F32),
            pltpu.VMEM((ncc + nlc, 1, HG_DK), F32),
            pltpu.VMEM((nlc, HG_DK, HG_DK), BF16),
            pltpu.VMEM((nlc, HG_DK, HG_DK), BF16),
        ],
        compiler_params=_cparams(("parallel", "arbitrary")),
        name="hgrn2",
    )(h_ctx, h_lat, g_lat, *([w_in] * n_rec),
      lb_fwd.reshape(nrow, HG_HEADS, 1, HG_DK), lb_bwd.reshape(nrow, HG_HEADS, 1, HG_DK),
      gnorm.reshape(1, HG_DK))


POOL_STRIP = 8
POOL_PAD = 8
POOL_BAND = 256


def _pool_group(h_ref, win_ref, wg_ref, ps_ref, o_ref, upad, *, win, rows):
    gd = win_ref.shape[1]
    half = win // 2
    u = jnp.dot(h_ref[0], win_ref[...].astype(BF16), preferred_element_type=F32)
    zeros = jnp.zeros((POOL_PAD, GRID_W, gd), F32)
    upad[0:POOL_PAD] = zeros
    upad[POOL_PAD + rows:POOL_PAD + rows + POOL_PAD] = zeros
    upad[POOL_PAD:POOL_PAD + rows] = u.reshape(rows, GRID_W, gd)

    tok = POOL_STRIP * GRID_W
    bi = lax.broadcasted_iota(jnp.int32, (POOL_BAND, POOL_BAND), 0)
    bj = lax.broadcasted_iota(jnp.int32, (POOL_BAND, POOL_BAND), 1)
    lo_c = (bi & (GRID_W - 1)) - half
    cj = bj & (GRID_W - 1)
    band = jnp.where(bi >> GRID_SHIFT == bj >> GRID_SHIFT, 1.0, 0.0)
    band = jnp.where(cj >= lo_c, band, 0.0)
    band = jnp.where(cj < lo_c + win, band, 0.0).astype(BF16)
    t = lax.broadcasted_iota(jnp.int32, (tok, gd), 0)
    col = t & (GRID_W - 1)
    cnt_c = jnp.minimum(col - half + win, GRID_W) - jnp.maximum(col - half, 0)
    wg = wg_ref[0].astype(BF16)
    ps = ps_ref[...]

    def strip(i, carry):
        r0 = i * POOL_STRIP
        slab = upad[pl.ds(r0 + POOL_PAD - half, POOL_STRIP + win - 1)]
        span = 1
        while span < win:
            n = slab.shape[0] - span
            slab = slab[0:n] + slab[span:span + n]
            span *= 2
        rs = slab.reshape(tok, gd)
        hi = rs.astype(BF16)
        lo = (rs - hi.astype(F32)).astype(BF16)
        parts = []
        for k in range(tok // POOL_BAND):
            sl = slice(k * POOL_BAND, (k + 1) * POOL_BAND)
            parts.append(jnp.dot(band, hi[sl], preferred_element_type=F32)
                         + jnp.dot(band, lo[sl], preferred_element_type=F32))
        box = jnp.concatenate(parts, axis=0)
        r = r0 + (t >> GRID_SHIFT)
        cnt_r = jnp.minimum(r - half + win, rows) - jnp.maximum(r - half, 0)
        mean = box / (cnt_r * cnt_c).astype(F32)
        ug = upad[pl.ds(r0 + POOL_PAD, POOL_STRIP)].reshape(tok, gd)
        z = jnp.dot((mean - ug).astype(BF16), wg, preferred_element_type=F32) * ps
        o_ref[0, pl.ds(pl.multiple_of(i * tok, tok), tok), :] = z.astype(BF16)
        return carry

    lax.fori_loop(0, rows // POOL_STRIP, strip, 0)


def _pool_kernel(h_ref, win_ref, wg_ref, ps_ref, o_ref, upad, *, rows):
    g = pl.program_id(1)
    for gi, win in enumerate(POOL_WINDOWS):
        @pl.when(g == gi)
        def _(win=win):
            _pool_group(h_ref, win_ref, wg_ref, ps_ref, o_ref, upad, win=win, rows=rows)


def _pool_call(h, w_in, w_grp, p_scale):
    b, l, d = h.shape
    ng = len(POOL_WINDOWS)
    gd = d // ng
    rows = l // GRID_W
    return pl.pallas_call(
        functools.partial(_pool_kernel, rows=rows),
        grid=(b, ng),
        in_specs=[
            pl.BlockSpec((1, l, d), lambda i, g: (i, 0, 0)),
            pl.BlockSpec((d, gd), lambda i, g: (0, g)),
            pl.BlockSpec((1, gd, gd), lambda i, g: (g, 0, 0)),
            pl.BlockSpec((1, gd), lambda i, g: (0, g)),
        ],
        out_specs=pl.BlockSpec((1, l, gd), lambda i, g: (i, 0, g)),
        out_shape=jax.ShapeDtypeStruct((b, l, d), BF16),
        scratch_shapes=[pltpu.VMEM((rows + 2 * POOL_PAD, GRID_W, gd), F32)],
        compiler_params=_cparams(("parallel", "arbitrary")),
        name="pool_mix",
    )(h, w_in, w_grp, p_scale.reshape(1, d))


MOE_TILE = 1024
MOE_WINDOW = 304
MOE_EXTRA = 64
ROW_ALIGN = 16
DEST_LANE = N_EXPERTS
PIECE_STRIDE = 32
N_PIECES = 3
AUX_ROWS = 8


def _route(sel, s):
    keep = []
    gsum = []
    for g in range(N_EXPERT_GROUPS):
        a = sel[g * EXPERTS_PER_GROUP:(g + 1) * EXPERTS_PER_GROUP]
        beaten = [jnp.zeros_like(a[0]) for _ in a]
        for i in range(EXPERTS_PER_GROUP):
            for j in range(i + 1, EXPERTS_PER_GROUP):
                ge = jnp.where(a[i] >= a[j], 1.0, 0.0)
                beaten[j] = beaten[j] + ge
                beaten[i] = beaten[i] + (1.0 - ge)
        kg = [jnp.where(bt < 1.5, 1.0, 0.0) for bt in beaten]
        keep.append(kg)
        gsum.append(sum(k * x for k, x in zip(kg, a)))
    picked = []
    bests = []
    for g in range(N_EXPERT_GROUPS):
        better = jnp.zeros_like(gsum[0])
        for o in range(N_EXPERT_GROUPS):
            if o < g:
                better = better + jnp.where(gsum[o] >= gsum[g], 1.0, 0.0)
            elif o > g:
                better = better + jnp.where(gsum[o] > gsum[g], 1.0, 0.0)
        best = jnp.where(better < 0.5, 1.0, 0.0)
        bests.append(best)
        for i in range(EXPERTS_PER_GROUP):
            picked.append(best * keep[g][i] * s[g * EXPERTS_PER_GROUP + i])
    den = sum(picked)
    return [p / den for p in picked], bests


def _post_kernel(x_ref, y_ref, w_ref, mod_ref, g_ref, rw_ref, rb_ref,
                 xo_ref, h2_ref, rt_ref, aux_ref, wbuf):
    @pl.when(pl.program_id(0) == 0)
    def _():
        wbuf[...] = w_ref[...].astype(BF16)

    yw = jnp.dot(y_ref[...], wbuf[...], preferred_element_type=F32)
    xn = x_ref[...] + mod_ref[0, 2:3, :] * yw
    xo_ref[...] = xn
    h2 = _modulate(xn, g_ref[...], mod_ref[0, 3:4, :], mod_ref[0, 4:5, :])
    hi = h2.astype(BF16)
    h2_ref[...] = hi
    lo = (h2 - hi.astype(F32)).astype(BF16)
    rw = rw_ref[...]
    half = hi.shape[0] // 2
    prod = jnp.concatenate(
        [jnp.dot(hi[r:r + half], rw, preferred_element_type=F32)
         + jnp.dot(lo[r:r + half], rw, preferred_element_type=F32) for r in (0, half)], axis=0).T
    logits = prod[0:N_EXPERTS, :] + prod[N_EXPERTS:2 * N_EXPERTS, :]
    s = _sigmoid_pair(logits)[0]
    sel = s + rb_ref[...]
    comb, bests = _route([sel[e:e + 1, :] for e in range(N_EXPERTS)],
                         [s[e:e + 1, :] for e in range(N_EXPERTS)])
    t = logits.shape[1]

    ind = jnp.concatenate(bests + [jnp.zeros((AUX_ROWS - N_EXPERT_GROUPS, t), F32)], axis=0)
    n_seg = t // LANES
    stacked = jnp.concatenate([ind[:, j * LANES:(j + 1) * LANES] for j in range(n_seg)], axis=0)
    upper = jnp.where(lax.broadcasted_iota(jnp.int32, (LANES, LANES), 0)
                      <= lax.broadcasted_iota(jnp.int32, (LANES, LANES), 1), 1.0, 0.0).astype(BF16)
    local = jnp.dot(stacked.astype(BF16), upper, preferred_element_type=F32)
    off = jnp.zeros((AUX_ROWS, 1), F32)
    pieces = []
    for j in range(n_seg):
        seg_cum = local[j * AUX_ROWS:(j + 1) * AUX_ROWS, :]
        pieces.append(seg_cum + off)
        off = off + seg_cum[:, LANES - 1:LANES]
    cum = jnp.concatenate(pieces, axis=1)
    counts = [off[g:g + 1, :] for g in range(N_EXPERT_GROUPS)]
    starts = [jnp.zeros((1, 1), F32)]
    for g in range(1, N_EXPERT_GROUPS):
        starts.append(starts[-1] + counts[g - 1])
    dest = sum(bests[g] * (starts[g] + cum[g:g + 1, :] - 1.0) for g in range(N_EXPERT_GROUPS))

    lane = lax.broadcasted_iota(jnp.int32, (1, t), 1)
    seg = sum(jnp.where(lane == k, v, 0.0) for k, v in enumerate(starts + counts))
    aux_ref[0] = jnp.concatenate([dest, seg, jnp.zeros((AUX_ROWS - 2, t), F32)], axis=0)
    p1 = [w.astype(BF16).astype(F32) for w in comb]
    r1 = [w - p for w, p in zip(comb, p1)]
    p2 = [r.astype(BF16).astype(F32) for r in r1]
    p3 = [r - p for r, p in zip(r1, p2)]
    pad = [jnp.zeros((PIECE_STRIDE - N_EXPERTS, t), F32)]
    table = jnp.concatenate(p1 + [dest] + [jnp.zeros((PIECE_STRIDE - N_EXPERTS - 1, t), F32)] + p2 + pad + p3 + pad
                            + [jnp.zeros((LANES - N_PIECES * PIECE_STRIDE, t), F32)], axis=0)
    rt_ref[...] = table.T


def _post_call(x, y, w, mod, g, router_w, router_b, tokens_per_batch, tile=MOE_TILE):
    n, d = x.shape
    per_b = tokens_per_batch // tile
    rwh = router_w.astype(BF16)
    rwl = (router_w - rwh.astype(F32)).astype(BF16)
    rw = jnp.concatenate([rwh, rwl, jnp.zeros((d, LANES - 2 * N_EXPERTS), BF16)], axis=1)
    row = lambda i: (i, 0)
    fixed = lambda i: (0, 0)
    return pl.pallas_call(
        _post_kernel,
        grid=(n // tile,),
        in_specs=[
            pl.BlockSpec((tile, d), row),
            pl.BlockSpec((tile, d), row),
            pl.BlockSpec((d, d), fixed),
            pl.BlockSpec((1, N_MOD, d), lambda i: (i // per_b, 0, 0)),
            pl.BlockSpec((1, d), fixed),
            pl.BlockSpec((d, LANES), fixed),
            pl.BlockSpec((N_EXPERTS, 1), fixed),
        ],
        out_specs=[
            pl.BlockSpec((tile, d), row),
            pl.BlockSpec((tile, d), row),
            pl.BlockSpec((tile, LANES), row),
            pl.BlockSpec((1, AUX_ROWS, tile), lambda i: (i, 0, 0)),
        ],
        out_shape=[
            jax.ShapeDtypeStruct((n, d), F32),
            jax.ShapeDtypeStruct((n, d), BF16),
            jax.ShapeDtypeStruct((n, LANES), F32),
            jax.ShapeDtypeStruct((n // tile, AUX_ROWS, tile), F32),
        ],
        scratch_shapes=[pltpu.VMEM((d, d), BF16)],
        compiler_params=_cparams(("arbitrary",)),
        name="mixer_out_router",
    )(x, y, w, mod, g.reshape(1, d), rw, router_b.reshape(N_EXPERTS, 1))


def _moe_kernel(seg_ref, h_ref, rt_ref, aux_ref, wg_ref, wu_ref, wd_ref, x_ref, mod_ref, ng_ref, nmod_ref,
                o_ref, *rest, final_norm):
    if final_norm:
        xs, cws, ys = rest
    else:
        hn_ref, xs, cws, ys = rest
    i = pl.program_id(0)
    g = pl.program_id(1)
    t = h_ref.shape[0]

    @pl.when(g == 0)
    def _():
        dest_row = aux_ref[0, 0:1, :]
        perm = jnp.where(lax.broadcasted_iota(jnp.int32, (t, t), 0).astype(F32) == dest_row,
                         1.0, 0.0).astype(BF16)
        xs[...] = jnp.dot(perm, h_ref[...], preferred_element_type=F32).astype(BF16)
        cws[...] = jnp.dot(perm, rt_ref[...].astype(BF16), preferred_element_type=F32)
        ys[...] = jnp.zeros_like(ys)

    start = seg_ref[i, g]
    end = start + seg_ref[i, N_EXPERT_GROUPS + g]
    wd = wd_ref[...].reshape(EXPERTS_PER_GROUP * D_EXPERT, wd_ref.shape[2])

    def window(w0, rows, lo, hi):
        sl = pl.ds(pl.multiple_of(w0, ROW_ALIGN), rows)
        xw = xs[sl, :]
        cw = cws[sl, :]
        r = w0 + lax.broadcasted_iota(jnp.int32, cw.shape, 0)
        lane = lax.broadcasted_iota(jnp.int32, cw.shape, 1)
        piece_lane = jnp.where(lane < N_PIECES * PIECE_STRIDE, lane & (PIECE_STRIDE - 1), -1)
        cw = jnp.where(r >= lo, cw, 0.0)
        cw = jnp.where(r < hi, cw, 0.0)
        acts = []
        for e in range(EXPERTS_PER_GROUP):
            ce = jnp.sum(jnp.where(piece_lane == g * EXPERTS_PER_GROUP + e, cw, 0.0), axis=1, keepdims=True)
            a = _silu_tanh(jnp.dot(xw, wg_ref[e], preferred_element_type=F32)) \
                * jnp.dot(xw, wu_ref[e], preferred_element_type=F32)
            acts.append((a * ce).astype(BF16))
        ys[sl, :] += jnp.dot(jnp.concatenate(acts, axis=1), wd, preferred_element_type=F32)

    w0 = jnp.minimum(start & -ROW_ALIGN, t - MOE_WINDOW)
    covered = w0 + MOE_WINDOW
    window(w0, MOE_WINDOW, start, jnp.minimum(end, covered))

    def extra(k, carry):
        lo = covered + k * MOE_EXTRA
        window(jnp.minimum(lo, t - MOE_EXTRA), MOE_EXTRA, lo, jnp.minimum(end, lo + MOE_EXTRA))
        return carry

    n_extra = jnp.maximum(end - covered + MOE_EXTRA - 1, 0) >> (MOE_EXTRA.bit_length() - 1)
    lax.fori_loop(0, n_extra, extra, 0)

    @pl.when(g == N_EXPERT_GROUPS - 1)
    def _():
        dest_col = rt_ref[:, DEST_LANE:DEST_LANE + 1]
        unperm = jnp.where(lax.broadcasted_iota(jnp.int32, (t, t), 1).astype(F32) == dest_col,
                           1.0, 0.0).astype(BF16)
        ff = jnp.dot(unperm, ys[...].astype(BF16), preferred_element_type=F32)
        xn = x_ref[...] + mod_ref[0, 5:6, :] * ff
        if final_norm:
            ms = jnp.mean(xn * xn, axis=-1, keepdims=True)
            xn = xn * lax.rsqrt(ms + EPS) * ng_ref[...]
        else:
            hn_ref[...] = _modulate(xn, ng_ref[...], nmod_ref[0, 0:1, :], nmod_ref[0, 1:2, :]).astype(BF16)
        o_ref[...] = xn


def _moe_call(h2, table, aux, wg, wu, wd, x, mod, norm_g, next_mod, tokens_per_batch, final_norm):
    n, d = x.shape
    tile = MOE_TILE
    per_b = tokens_per_batch // tile
    seg = aux[:, 1, :2 * N_EXPERT_GROUPS].astype(jnp.int32)
    row = lambda i, g, seg: (i, 0)
    grp = lambda i, g, seg: (g, 0, 0)
    tok = pl.BlockSpec((tile, d), row)
    x_out = jax.ShapeDtypeStruct((n, d), F32)
    grid_spec = pltpu.PrefetchScalarGridSpec(
        num_scalar_prefetch=1,
        grid=(n // tile, N_EXPERT_GROUPS),
        in_specs=[
            pl.BlockSpec((tile, d), row),
            pl.BlockSpec((tile, LANES), row),
            pl.BlockSpec((1, AUX_ROWS, tile), lambda i, g, seg: (i, 0, 0)),
            pl.BlockSpec((EXPERTS_PER_GROUP, d, D_EXPERT), grp),
            pl.BlockSpec((EXPERTS_PER_GROUP, d, D_EXPERT), grp),
            pl.BlockSpec((EXPERTS_PER_GROUP, D_EXPERT, d), grp),
            pl.BlockSpec((tile, d), row),
            pl.BlockSpec((1, N_MOD, d), lambda i, g, seg: (i // per_b, 0, 0)),
            pl.BlockSpec((1, d), lambda i, g, seg: (0, 0)),
            pl.BlockSpec((1, N_MOD, d), lambda i, g, seg: (i // per_b, 0, 0)),
        ],
        out_specs=tok if final_norm else [tok, tok],
        scratch_shapes=[
            pltpu.VMEM((tile, d), BF16),
            pltpu.VMEM((tile, LANES), F32),
            pltpu.VMEM((tile, d), F32),
        ],
    )
    return pl.pallas_call(
        functools.partial(_moe_kernel, final_norm=final_norm),
        grid_spec=grid_spec,
        out_shape=x_out if final_norm else [x_out, jax.ShapeDtypeStruct((n, d), BF16)],
        compiler_params=_cparams(("parallel", "arbitrary")),
        name="moe_ffn",
    )(seg, h2, table, aux, wg, wu, wd, x, mod, norm_g.reshape(1, d), next_mod)


def kernel(x, c, ctx, c_ctx, w_mod, b_mod, norm1_g, norm2_g, hg_w_in, hg_lb_fwd, hg_lb_bwd, hg_gnorm,
           hg_w_out, pool_w_in, pool_w_grp, pool_scale, pool_w_out, router_w, router_b, moe_w_gate,
           moe_w_up, moe_w_down, final_g):
    b, l, d = x.shape
    depth = w_mod.shape[0]
    n_mixers = 2

    cc = jnp.concatenate([c, c_ctx[None, :], jnp.zeros((MOD_ROWS - b - 1, d), F32)], axis=0)
    mods = _mod_call(cc, w_mod, b_mod)

    x_lat = x.reshape(b * l, d)
    h_lat = None
    for i in range(depth):
        slot = i // n_mixers
        mod_lat = mods[i, :b].reshape(b, N_MOD, d)
        if i % n_mixers == 0:
            h_lat, g_lat = _norm_call(x_lat.reshape(b, l, d), norm1_g[i], mod_lat, 512,
                                      hg_w_in[slot], N_HG_PROJ - 1)
            mod_ctx = jnp.broadcast_to(mods[i, b].reshape(1, N_MOD, d), (b, N_MOD, d))
            h_ctx = _norm_call(ctx, norm1_g[i], mod_ctx, ctx.shape[1])
            y = _hgrn_call(h_ctx, h_lat, g_lat, hg_w_in[slot], hg_lb_fwd, hg_lb_bwd, hg_gnorm[slot], slot)
            w_out = hg_w_out[slot]
        else:
            if h_lat is None:
                h_lat = _norm_call(x_lat.reshape(b, l, d), norm1_g[i], mod_lat, 512)
            y = _pool_call(h_lat, pool_w_in[slot], pool_w_grp[slot], pool_scale[slot])
            w_out = pool_w_out[slot]
        x_lat, h2, table, aux = _post_call(x_lat, y.reshape(b * l, d), w_out, mod_lat,
                                           norm2_g[i], router_w, router_b, l)
        experts = (moe_w_gate[i].astype(BF16), moe_w_up[i].astype(BF16), moe_w_down[i].astype(BF16))
        if i == depth - 1:
            x_lat = _moe_call(h2, table, aux, *experts, x_lat, mod_lat, final_g, mod_lat, l, final_norm=True)
        else:
            mod_next = mods[i + 1, :b].reshape(b, N_MOD, d)
            x_lat, h_next = _moe_call(h2, table, aux, *experts, x_lat, mod_lat, norm1_g[i + 1], mod_next, l,
                                      final_norm=False)
            h_lat = h_next.reshape(b, l, d) if (i + 1) % n_mixers != 0 else None
    return x_lat.reshape(b, l, d)
```

```python
import functools

import jax
import jax.numpy as jnp
from jax import lax
from jax.experimental import pallas as pl
from jax.experimental.pallas import tpu as pltpu

F32 = jnp.float32
BF16 = jnp.bfloat16

EPS = 1e-6
N_MOD = 6
HG_HEADS = 8
HG_DK = 128
HG_CHUNK = 64
N_HG_PROJ = 5
POOL_WINDOWS = (2, 4, 8, 16)
GRID_W = 64
GRID_SHIFT = GRID_W.bit_length() - 1
assert 1 << GRID_SHIFT == GRID_W
N_EXPERTS = 16
N_EXPERT_GROUPS = 4
EXPERTS_PER_GROUP = N_EXPERTS // N_EXPERT_GROUPS
D_EXPERT = 256

LANES = 128
MOD_ROWS = 8
VMEM_LIMIT = 56 * 1024 * 1024


def _cparams(sem, vmem=VMEM_LIMIT):
    return pltpu.CompilerParams(dimension_semantics=sem, vmem_limit_bytes=vmem)


def _sigmoid_pair(z):
    e = jnp.exp(-jnp.abs(z))
    r = 1.0 / (1.0 + e)
    er = e * r
    pos = z >= 0
    return jnp.where(pos, r, er), jnp.where(pos, er, r)


def _silu(z):
    return z * _sigmoid_pair(z)[0]


def _sigmoid_tanh(z):
    return 0.5 * jnp.tanh(0.5 * z) + 0.5


def _silu_tanh(z):
    return z * _sigmoid_tanh(z)


def _mod_kernel(c_ref, w_ref, b_ref, o_ref):
    a = _silu(c_ref[...])
    o_ref[0] = jnp.dot(a, w_ref[0], preferred_element_type=F32,
                       precision=lax.Precision.HIGHEST) + b_ref[0]


def _mod_call(cc, w_mod, b_mod):
    depth, d, n = w_mod.shape
    tn = 1024
    return pl.pallas_call(
        _mod_kernel,
        grid=(depth, n // tn),
        in_specs=[
            pl.BlockSpec((MOD_ROWS, d), lambda i, j: (0, 0)),
            pl.BlockSpec((1, d, tn), lambda i, j: (i, 0, j)),
            pl.BlockSpec((1, 1, tn), lambda i, j: (i, 0, j)),
        ],
        out_specs=pl.BlockSpec((1, MOD_ROWS, tn), lambda i, j: (i, 0, j)),
        out_shape=jax.ShapeDtypeStruct((depth, MOD_ROWS, n), F32),
        compiler_params=_cparams(("parallel", "parallel")),
        name="mod_proj",
    )(cc, w_mod, b_mod.reshape(depth, 1, n))


def _modulate(x, g, shift, scale):
    ms = jnp.mean(x * x, axis=-1, keepdims=True)
    return (x * lax.rsqrt(ms + EPS) * g) * (1.0 + scale) + shift


def _norm_kernel(x_ref, g_ref, mod_ref, *rest):
    h = _modulate(x_ref[0], g_ref[...], mod_ref[0, 0:1, :], mod_ref[0, 1:2, :]).astype(BF16)
    if len(rest) == 1:
        (o_ref,) = rest
    else:
        w_ref, o_ref, p_ref, wbuf = rest

        @pl.when((pl.program_id(0) == 0) & (pl.program_id(1) == 0))
        def _():
            wbuf[...] = w_ref[...].astype(BF16)

        p_ref[0] = jnp.dot(h, wbuf[...], preferred_element_type=F32).astype(BF16)
    o_ref[0] = h


def _norm_call(x, g, mod, tile, w_proj=None, w_slot=0, w_col=0):
    b, t, d = x.shape
    tok = pl.BlockSpec((1, tile, d), lambda i, j: (i, j, 0))
    in_specs = [tok, pl.BlockSpec((1, d), lambda i, j: (0, 0)),
                pl.BlockSpec((1, N_MOD, d), lambda i, j: (i, 0, 0))]
    args = [x, g.reshape(1, d), mod]
    out_specs, out_shape = tok, jax.ShapeDtypeStruct((b, t, d), BF16)
    scratch = []
    if w_proj is not None:
        in_specs.append(pl.BlockSpec((None, d, d), lambda i, j: (w_slot, 0, w_col)))
        args.append(w_proj)
        out_specs, out_shape = [tok, tok], [out_shape, out_shape]
        scratch = [pltpu.VMEM((d, d), BF16)]
    return pl.pallas_call(
        _norm_kernel,
        grid=(b, t // tile),
        in_specs=in_specs,
        out_specs=out_specs,
        out_shape=out_shape,
        scratch_shapes=scratch,
        compiler_params=_cparams(("arbitrary", "arbitrary")),
        name="norm1",
    )(*args)


HG_BLOCK = 256
HG_CPB = HG_BLOCK // HG_CHUNK


def _chunk_prefix(x, row):
    for d in (1, 2, 4, 8, 16, 32):
        x = x + jnp.where(row >= d, pltpu.roll(x, d, axis=0), 0.0)
    return x


def _chunk_suffix(x, row):
    n = x.shape[0]
    for d in (1, 2, 4, 8, 16, 32):
        x = x + jnp.where(row < HG_CHUNK - d, pltpu.roll(x, n - d, axis=0), 0.0)
    return x


def _lower_bound(lb_ref, slot):
    rows = [lb_ref[j, 0] for j in range(lb_ref.shape[0])]
    m = functools.reduce(jnp.maximum, rows)
    es = [jnp.exp(r - m) for r in rows]
    return sum(es[:slot + 1]) / sum(es)


def _hgrn_kernel(hc_ref, hl_ref, g_ref, wq_ref, wv_ref, wf_ref, wb_ref, lbf_ref, lbb_ref, gn_ref, y_ref,
                 wbuf, pbuf0, pbuf1, pbuf2, pbuf3, oacc, qif, qib, kvf, kvb, decf, decb, spf, spb,
                 *, slot, n_ctx_chunks, n_lat_chunks):
    lb_f = _lower_bound(lbf_ref, slot)
    lb_b = _lower_bound(lbb_ref, slot)
    row = lax.broadcasted_iota(jnp.int32, (HG_BLOCK, HG_DK), 0) & (HG_CHUNK - 1)
    ci = lax.broadcasted_iota(jnp.int32, (HG_CPB, HG_CHUNK, HG_CHUNK), 1)
    si = lax.broadcasted_iota(jnp.int32, (HG_CPB, HG_CHUNK, HG_CHUNK), 2)

    def c3(t):
        return t.reshape(HG_CPB, HG_CHUNK, HG_DK)

    def direction(qs3, v3b, z, lb, fwd):
        sig = _sigmoid_tanh(z)
        f = lb + (1.0 - lb) * sig
        k3 = c3((1.0 - lb) * (1.0 - sig))
        lf = jnp.log(f)
        if fwd:
            cum = c3(_chunk_prefix(lf, row))
            ref = cum[:, HG_CHUNK // 2 - 1:HG_CHUNK // 2, :]
            last = cum[:, HG_CHUNK - 1:HG_CHUNK, :]
        else:
            cum = c3(_chunk_suffix(lf, row))
            ref = cum[:, HG_CHUNK // 2:HG_CHUNK // 2 + 1, :]
            last = cum[:, 0:1, :]
        dec = jnp.exp(last)
        if qs3 is None:
            kl = k3 * jnp.exp(last - cum)
        else:
            e1 = jnp.exp(cum - ref)
            qd = qs3 * e1
            qi = qd * jnp.exp(ref)
            kd = k3 * (1.0 / e1)
            kl = kd * jnp.exp(last - ref)
        kvt = jnp.einsum('ncv,nck->nvk', v3b, kl.astype(BF16), preferred_element_type=F32)
        if qs3 is None:
            return None, None, kvt, dec
        sc = jnp.einsum('nck,nsk->ncs', qd.astype(BF16), kd.astype(BF16), preferred_element_type=F32)
        sc = jnp.where((ci >= si) if fwd else (ci <= si), sc, 0.0)
        intra = jnp.einsum('ncs,nsv->ncv', sc.astype(BF16), v3b, preferred_element_type=F32)
        return intra, qi, kvt, dec

    for p, wp_ref in enumerate((wq_ref, wv_ref, wf_ref, wb_ref)):
        wbuf[:, p * HG_DK:(p + 1) * HG_DK] = wp_ref[...].astype(BF16)

    def project(hrows):
        return jnp.dot(hrows, wbuf[...], preferred_element_type=F32)

    def block(p, chunk0, lat_row0):
        v3b = c3(p[:, HG_DK:2 * HG_DK]).astype(BF16)
        zf = p[:, 2 * HG_DK:3 * HG_DK]
        zb = p[:, 3 * HG_DK:4 * HG_DK]
        if lat_row0 is None:
            qs3 = None
        else:
            qs3 = c3(_silu_tanh(p[:, 0:HG_DK]))
        in_f, qi_f, kv_f, dec_f = direction(qs3, v3b, zf, lb_f, True)
        in_b, qi_b, kv_b, dec_b = direction(qs3, v3b, zb, lb_b, False)
        kvf[pl.ds(chunk0, HG_CPB)] = kv_f
        kvb[pl.ds(chunk0, HG_CPB)] = kv_b
        decf[pl.ds(chunk0, HG_CPB)] = dec_f
        decb[pl.ds(chunk0, HG_CPB)] = dec_b
        if lat_row0 is not None:
            rows = pl.ds(lat_row0, HG_BLOCK)
            oacc[rows, :] = (in_f + in_b).reshape(HG_BLOCK, HG_DK)
            qif[rows, :] = qi_f.reshape(HG_BLOCK, HG_DK).astype(BF16)
            qib[rows, :] = qi_b.reshape(HG_BLOCK, HG_DK).astype(BF16)

    for i in range(n_ctx_chunks // HG_CPB):
        block(project(hc_ref[0, i * HG_BLOCK:(i + 1) * HG_BLOCK, :]), i * HG_CPB, None)

    n_blocks = n_lat_chunks // HG_CPB

    def lat_rows(i):
        return hl_ref[0, pl.ds(pl.multiple_of(i * HG_BLOCK, HG_BLOCK), HG_BLOCK), :]

    def lat_terms(p_ref, i):
        block(p_ref[...], n_ctx_chunks + i * HG_CPB, pl.multiple_of(i * HG_BLOCK, HG_BLOCK))

    pbuf0[...] = project(lat_rows(0))
    pbuf1[...] = project(lat_rows(1))

    def lat_quad(i0, last):
        pbuf2[...] = project(lat_rows(i0 + 2))
        lat_terms(pbuf0, i0)
        pbuf3[...] = project(lat_rows(i0 + 3))
        lat_terms(pbuf1, i0 + 1)
        if not last:
            pbuf0[...] = project(lat_rows(i0 + 4))
        lat_terms(pbuf2, i0 + 2)
        if not last:
            pbuf1[...] = project(lat_rows(i0 + 5))
        lat_terms(pbuf3, i0 + 3)

    def lat_body(t, carry):
        lat_quad(4 * t, False)
        return carry

    lax.fori_loop(0, n_blocks // 4 - 1, lat_body, 0)
    lat_quad(n_blocks - 4, True)

    def advance(s, kv_ref, dec_ref, n):
        return dec_ref[n] * s + kv_ref[n]

    sf = jnp.zeros((HG_DK, HG_DK), F32)
    for n in range(n_ctx_chunks):
        sf = advance(sf, kvf, decf, n)
    sb = jnp.zeros((HG_DK, HG_DK), F32)
    for n in reversed(range(n_ctx_chunks)):
        sb = advance(sb, kvb, decb, n)

    def scan_step(t, carry):
        sf, sb = carry
        jb = n_lat_chunks - 1 - t
        spf[t] = sf.astype(BF16)
        spb[jb] = sb.astype(BF16)
        return (advance(sf, kvf, decf, n_ctx_chunks + t), advance(sb, kvb, decb, n_ctx_chunks + jb))

    lax.fori_loop(0, n_lat_chunks, scan_step, (sf, sb), unroll=2)

    gn = gn_ref[...]

    def block_rows(i):
        return pl.ds(pl.multiple_of(i * HG_BLOCK, HG_BLOCK), HG_BLOCK)

    def inter(i):
        rows = block_rows(i)
        chunks = pl.ds(i * HG_CPB, HG_CPB)
        return (jnp.einsum('nck,nvk->ncv', c3(qif[rows, :]), spf[chunks], preferred_element_type=F32)
                + jnp.einsum('nck,nvk->ncv', c3(qib[rows, :]), spb[chunks], preferred_element_type=F32)
                ).reshape(HG_BLOCK, HG_DK)

    def readout(o_ref, i):
        rows = block_rows(i)
        o = oacc[rows, :] + o_ref[...]
        ms = jnp.mean(o * o, axis=-1, keepdims=True)
        o = o * lax.rsqrt(ms + EPS) * gn
        y_ref[0, rows, :] = (o * _silu_tanh(g_ref[0, rows, :].astype(F32))).astype(BF16)

    obuf0, obuf1 = pbuf0.at[:, 0:HG_DK], pbuf1.at[:, 0:HG_DK]
    obuf0[...] = inter(0)

    def readout_pair(t, carry):
        obuf1[...] = inter(2 * t + 1)
        readout(obuf0, 2 * t)
        obuf0[...] = inter(2 * t + 2)
        readout(obuf1, 2 * t + 1)
        return carry

    lax.fori_loop(0, n_blocks // 2 - 1, readout_pair, 0)
    obuf1[...] = inter(n_blocks - 1)
    readout(obuf0, n_blocks - 2)
    readout(obuf1, n_blocks - 1)


def _hgrn_call(h_ctx, h_lat, g_lat, w_in, lb_fwd, lb_bwd, gnorm, slot):
    b, lc, d = h_ctx.shape
    ll = h_lat.shape[1]
    nrow = lb_fwd.shape[0]
    ncc, nlc = lc // HG_CHUNK, ll // HG_CHUNK
    kern = functools.partial(_hgrn_kernel, slot=slot, n_ctx_chunks=ncc, n_lat_chunks=nlc)
    lb_spec = pl.BlockSpec((nrow, 1, 1, HG_DK), lambda i, h: (0, h, 0, 0))
    head_cols = pl.BlockSpec((1, ll, HG_DK), lambda i, h: (i, 0, h))
    n_rec = N_HG_PROJ - 1
    w_cols = [pl.BlockSpec((None, d, HG_DK), functools.partial(lambda i, h, p: (slot, 0, p * HG_HEADS + h), p=p))
              for p in range(n_rec)]
    return pl.pallas_call(
        kern,
        grid=(b, HG_HEADS),
        in_specs=[
            pl.BlockSpec((1, lc, d), lambda i, h: (i, 0, 0)),
            pl.BlockSpec((1, ll, d), lambda i, h: (i, 0, 0)),
            head_cols,
            *w_cols,
            lb_spec, lb_spec,
            pl.BlockSpec((1, HG_DK), lambda i, h: (0, 0)),
        ],
        out_specs=head_cols,
        out_shape=jax.ShapeDtypeStruct((b, ll, d), BF16),
        scratch_shapes=[
            pltpu.VMEM((d, n_rec * HG_DK), BF16),
            pltpu.VMEM((HG_BLOCK, n_rec * HG_DK), F32),
            pltpu.VMEM((HG_BLOCK, n_rec * HG_DK), F32),
            pltpu.VMEM((HG_BLOCK, n_rec * HG_DK), F32),
            pltpu.VMEM((HG_BLOCK, n_rec * HG_DK), F32),
            pltpu.VMEM((ll, HG_DK), F32),
            pltpu.VMEM((ll, HG_DK), BF16),
            pltpu.VMEM((ll, HG_DK), BF16),
            pltpu.VMEM((ncc + nlc, HG_DK, HG_DK), F32),
            pltpu.VMEM((ncc + nlc, HG_DK, HG_DK), F32),
            pltpu.VMEM((ncc + nlc, 1, HG_DK), F32),
            pltpu.VMEM((ncc + nlc, 1, HG_DK), F32),
            pltpu.VMEM((nlc, HG_DK, HG_DK), BF16),
            pltpu.VMEM((nlc, HG_DK, HG_DK), BF16),
        ],
        compiler_params=_cparams(("parallel", "arbitrary")),
        name="hgrn2",
    )(h_ctx, h_lat, g_lat, *([w_in] * n_rec),
      lb_fwd.reshape(nrow, HG_HEADS, 1, HG_DK), lb_bwd.reshape(nrow, HG_HEADS, 1, HG_DK),
      gnorm.reshape(1, HG_DK))


POOL_STRIP = 8
POOL_PAD = 8
POOL_BAND = 256


def _pool_group(h_ref, win_ref, wg_ref, ps_ref, o_ref, upad, *, win, rows):
    gd = win_ref.shape[1]
    half = win // 2
    u = jnp.dot(h_ref[0], win_ref[...].astype(BF16), preferred_element_type=F32)
    zeros = jnp.zeros((POOL_PAD, GRID_W, gd), F32)
    upad[0:POOL_PAD] = zeros
    upad[POOL_PAD + rows:POOL_PAD + rows + POOL_PAD] = zeros
    upad[POOL_PAD:POOL_PAD + rows] = u.reshape(rows, GRID_W, gd)

    tok = POOL_STRIP * GRID_W
    bi = lax.broadcasted_iota(jnp.int32, (POOL_BAND, POOL_BAND), 0)
    bj = lax.broadcasted_iota(jnp.int32, (POOL_BAND, POOL_BAND), 1)
    lo_c = (bi & (GRID_W - 1)) - half
    cj = bj & (GRID_W - 1)
    band = jnp.where(bi >> GRID_SHIFT == bj >> GRID_SHIFT, 1.0, 0.0)
    band = jnp.where(cj >= lo_c, band, 0.0)
    band = jnp.where(cj < lo_c + win, band, 0.0).astype(BF16)
    t = lax.broadcasted_iota(jnp.int32, (tok, LANES), 0)
    col = t & (GRID_W - 1)
    cnt_c = jnp.minimum(col - half + win, GRID_W) - jnp.maximum(col - half, 0)
    wg = wg_ref[0].astype(BF16)
    ps = ps_ref[...]

    def strip(i, carry):
        r0 = i * POOL_STRIP
        slab = upad[pl.ds(r0 + POOL_PAD - half, POOL_STRIP + win - 1)]
        span = 1
        while span < win:
            n = slab.shape[0] - span
            slab = slab[0:n] + slab[span:span + n]
            span *= 2
        rs = slab.reshape(tok, gd)
        hi = rs.astype(BF16)
        lo = (rs - hi.astype(F32)).astype(BF16)
        parts = []
        for k in range(tok // POOL_BAND):
            sl = slice(k * POOL_BAND, (k + 1) * POOL_BAND)
            parts.append(jnp.dot(band, hi[sl], preferred_element_type=F32)
                         + jnp.dot(band, lo[sl], preferred_element_type=F32))
        box = jnp.concatenate(parts, axis=0)
        r = r0 + (t >> GRID_SHIFT)
        cnt_r = jnp.minimum(r - half + win, rows) - jnp.maximum(r - half, 0)
        inv = 1.0 / (cnt_r * cnt_c).astype(F32)
        mean = box * jnp.concatenate([inv] * (gd // LANES), axis=1)
        ug = upad[pl.ds(r0 + POOL_PAD, POOL_STRIP)].reshape(tok, gd)
        z = jnp.dot((mean - ug).astype(BF16), wg, preferred_element_type=F32) * ps
        o_ref[0, pl.ds(pl.multiple_of(i * tok, tok), tok), :] = z.astype(BF16)
        return carry

    lax.fori_loop(0, rows // POOL_STRIP, strip, 0, unroll=2)


def _pool_kernel(h_ref, win_ref, wg_ref, ps_ref, o_ref, upad, *, rows):
    g = pl.program_id(1)
    for gi, win in enumerate(POOL_WINDOWS):
        @pl.when(g == gi)
        def _(win=win):
            _pool_group(h_ref, win_ref, wg_ref, ps_ref, o_ref, upad, win=win, rows=rows)


def _pool_call(h, w_in, w_grp, p_scale, slot):
    b, l, d = h.shape
    ng = len(POOL_WINDOWS)
    gd = d // ng
    rows = l // GRID_W
    return pl.pallas_call(
        functools.partial(_pool_kernel, rows=rows),
        grid=(b, ng),
        in_specs=[
            pl.BlockSpec((1, l, d), lambda i, g: (i, 0, 0)),
            pl.BlockSpec((None, d, gd), lambda i, g: (slot, 0, g)),
            pl.BlockSpec((None, 1, gd, gd), lambda i, g: (slot, g, 0, 0)),
            pl.BlockSpec((None, 1, gd), lambda i, g: (slot, 0, g)),
        ],
        out_specs=pl.BlockSpec((1, l, gd), lambda i, g: (i, 0, g)),
        out_shape=jax.ShapeDtypeStruct((b, l, d), BF16),
        scratch_shapes=[pltpu.VMEM((rows + 2 * POOL_PAD, GRID_W, gd), F32)],
        compiler_params=_cparams(("parallel", "arbitrary")),
        name="pool_mix",
    )(h, w_in, w_grp, p_scale.reshape(p_scale.shape[0], 1, d))


MOE_TILE = 1024
MOE_WINDOW = 304
MOE_EXTRA = 64
ROW_ALIGN = 16
DEST_LANE = N_EXPERTS
PIECE_STRIDE = 32
N_PIECES = 3
AUX_ROWS = 8


def _route(sel, s):
    keep = []
    gsum = []
    for g in range(N_EXPERT_GROUPS):
        a = sel[g * EXPERTS_PER_GROUP:(g + 1) * EXPERTS_PER_GROUP]
        beaten = [jnp.zeros_like(a[0]) for _ in a]
        for i in range(EXPERTS_PER_GROUP):
            for j in range(i + 1, EXPERTS_PER_GROUP):
                ge = jnp.where(a[i] >= a[j], 1.0, 0.0)
                beaten[j] = beaten[j] + ge
                beaten[i] = beaten[i] + (1.0 - ge)
        kg = [jnp.where(bt < 1.5, 1.0, 0.0) for bt in beaten]
        keep.append(kg)
        gsum.append(sum(k * x for k, x in zip(kg, a)))
    picked = []
    bests = []
    for g in range(N_EXPERT_GROUPS):
        better = jnp.zeros_like(gsum[0])
        for o in range(N_EXPERT_GROUPS):
            if o < g:
                better = better + jnp.where(gsum[o] >= gsum[g], 1.0, 0.0)
            elif o > g:
                better = better + jnp.where(gsum[o] > gsum[g], 1.0, 0.0)
        best = jnp.where(better < 0.5, 1.0, 0.0)
        bests.append(best)
        for i in range(EXPERTS_PER_GROUP):
            picked.append(best * keep[g][i] * s[g * EXPERTS_PER_GROUP + i])
    den = sum(picked)
    return [p / den for p in picked], bests


def _post_kernel(x_ref, y_ref, w_ref, mod_ref, g_ref, rw_ref, rb_ref,
                 xo_ref, h2_ref, rt_ref, aux_ref, wbuf):
    @pl.when(pl.program_id(0) == 0)
    def _():
        wbuf[...] = w_ref[...].astype(BF16)

    yw = jnp.dot(y_ref[...], wbuf[...], preferred_element_type=F32)
    xn = x_ref[...] + mod_ref[0, 2:3, :] * yw
    xo_ref[...] = xn
    h2 = _modulate(xn, g_ref[...], mod_ref[0, 3:4, :], mod_ref[0, 4:5, :])
    hi = h2.astype(BF16)
    h2_ref[...] = hi
    lo = (h2 - hi.astype(F32)).astype(BF16)
    rw = rw_ref[...]
    half = hi.shape[0] // 2
    prod = jnp.concatenate(
        [jnp.dot(hi[r:r + half], rw, preferred_element_type=F32)
         + jnp.dot(lo[r:r + half], rw, preferred_element_type=F32) for r in (0, half)], axis=0).T
    logits = prod[0:N_EXPERTS, :] + prod[N_EXPERTS:2 * N_EXPERTS, :]
    s = _sigmoid_pair(logits)[0]
    sel = s + rb_ref[...]
    comb, bests = _route([sel[e:e + 1, :] for e in range(N_EXPERTS)],
                         [s[e:e + 1, :] for e in range(N_EXPERTS)])
    t = logits.shape[1]

    ind = jnp.concatenate(bests + [jnp.zeros((AUX_ROWS - N_EXPERT_GROUPS, t), F32)], axis=0)
    n_seg = t // LANES
    stacked = jnp.concatenate([ind[:, j * LANES:(j + 1) * LANES] for j in range(n_seg)], axis=0)
    upper = jnp.where(lax.broadcasted_iota(jnp.int32, (LANES, LANES), 0)
                      <= lax.broadcasted_iota(jnp.int32, (LANES, LANES), 1), 1.0, 0.0).astype(BF16)
    local = jnp.dot(stacked.astype(BF16), upper, preferred_element_type=F32)
    off = jnp.zeros((AUX_ROWS, 1), F32)
    pieces = []
    for j in range(n_seg):
        seg_cum = local[j * AUX_ROWS:(j + 1) * AUX_ROWS, :]
        pieces.append(seg_cum + off)
        off = off + seg_cum[:, LANES - 1:LANES]
    cum = jnp.concatenate(pieces, axis=1)
    counts = [off[g:g + 1, :] for g in range(N_EXPERT_GROUPS)]
    starts = [jnp.zeros((1, 1), F32)]
    for g in range(1, N_EXPERT_GROUPS):
        starts.append(starts[-1] + counts[g - 1])
    dest = sum(bests[g] * (starts[g] + cum[g:g + 1, :] - 1.0) for g in range(N_EXPERT_GROUPS))

    lane = lax.broadcasted_iota(jnp.int32, (1, t), 1)
    seg = sum(jnp.where(lane == k, v, 0.0) for k, v in enumerate(starts + counts))
    aux_ref[0] = jnp.concatenate([dest, seg, jnp.zeros((AUX_ROWS - 2, t), F32)], axis=0)
    p1 = [w.astype(BF16).astype(F32) for w in comb]
    r1 = [w - p for w, p in zip(comb, p1)]
    p2 = [r.astype(BF16).astype(F32) for r in r1]
    p3 = [r - p for r, p in zip(r1, p2)]
    pad = [jnp.zeros((PIECE_STRIDE - N_EXPERTS, t), F32)]
    table = jnp.concatenate(p1 + [dest] + [jnp.zeros((PIECE_STRIDE - N_EXPERTS - 1, t), F32)] + p2 + pad + p3 + pad
                            + [jnp.zeros((LANES - N_PIECES * PIECE_STRIDE, t), F32)], axis=0)
    rt_ref[...] = table.T


def _post_call(x, y, w, w_slot, mod, g, router_w, router_b, tokens_per_batch, tile=MOE_TILE):
    n, d = x.shape
    per_b = tokens_per_batch // tile
    rwh = router_w.astype(BF16)
    rwl = (router_w - rwh.astype(F32)).astype(BF16)
    rw = jnp.concatenate([rwh, rwl, jnp.zeros((d, LANES - 2 * N_EXPERTS), BF16)], axis=1)
    row = lambda i: (i, 0)
    fixed = lambda i: (0, 0)
    return pl.pallas_call(
        _post_kernel,
        grid=(n // tile,),
        in_specs=[
            pl.BlockSpec((tile, d), row),
            pl.BlockSpec((tile, d), row),
            pl.BlockSpec((None, d, d), lambda i: (w_slot, 0, 0)),
            pl.BlockSpec((1, N_MOD, d), lambda i: (i // per_b, 0, 0)),
            pl.BlockSpec((1, d), fixed),
            pl.BlockSpec((d, LANES), fixed),
            pl.BlockSpec((N_EXPERTS, 1), fixed),
        ],
        out_specs=[
            pl.BlockSpec((tile, d), row),
            pl.BlockSpec((tile, d), row),
            pl.BlockSpec((tile, LANES), row),
            pl.BlockSpec((1, AUX_ROWS, tile), lambda i: (i, 0, 0)),
        ],
        out_shape=[
            jax.ShapeDtypeStruct((n, d), F32),
            jax.ShapeDtypeStruct((n, d), BF16),
            jax.ShapeDtypeStruct((n, LANES), F32),
            jax.ShapeDtypeStruct((n // tile, AUX_ROWS, tile), F32),
        ],
        scratch_shapes=[pltpu.VMEM((d, d), BF16)],
        compiler_params=_cparams(("arbitrary",)),
        name="mixer_out_router",
    )(x, y, w, mod, g.reshape(1, d), rw, router_b.reshape(N_EXPERTS, 1))


def _moe_kernel(seg_ref, h_ref, rt_ref, aux_ref, wg_ref, wu_ref, wd_ref, x_ref, mod_ref, ng_ref, nmod_ref,
                o_ref, *rest, final_norm):
    if final_norm:
        xs, cws, ys = rest
    else:
        hn_ref, xs, cws, ys = rest
    i = pl.program_id(0)
    g = pl.program_id(1)
    t = h_ref.shape[0]

    @pl.when(g == 0)
    def _():
        dest_row = aux_ref[0, 0:1, :]
        perm = jnp.where(lax.broadcasted_iota(jnp.int32, (t, t), 0).astype(F32) == dest_row,
                         1.0, 0.0).astype(BF16)
        xs[...] = jnp.dot(perm, h_ref[...], preferred_element_type=F32).astype(BF16)
        cws[...] = jnp.dot(perm, rt_ref[...].astype(BF16), preferred_element_type=F32)
        ys[...] = jnp.zeros_like(ys)

    start = seg_ref[i, g]
    end = start + seg_ref[i, N_EXPERT_GROUPS + g]
    wd = wd_ref[...].reshape(EXPERTS_PER_GROUP * D_EXPERT, wd_ref.shape[2])

    def window(w0, rows, lo, hi):
        sl = pl.ds(pl.multiple_of(w0, ROW_ALIGN), rows)
        xw = xs[sl, :]
        cw = cws[sl, :]
        r = w0 + lax.broadcasted_iota(jnp.int32, cw.shape, 0)
        lane = lax.broadcasted_iota(jnp.int32, cw.shape, 1)
        piece_lane = jnp.where(lane < N_PIECES * PIECE_STRIDE, lane & (PIECE_STRIDE - 1), -1)
        cw = jnp.where(r >= lo, cw, 0.0)
        cw = jnp.where(r < hi, cw, 0.0)
        acts = []
        for e in range(EXPERTS_PER_GROUP):
            ce = jnp.sum(jnp.where(piece_lane == g * EXPERTS_PER_GROUP + e, cw, 0.0), axis=1, keepdims=True)
            a = _silu_tanh(jnp.dot(xw, wg_ref[e], preferred_element_type=F32)) \
                * jnp.dot(xw, wu_ref[e], preferred_element_type=F32)
            acts.append((a * ce).astype(BF16))
        ys[sl, :] += jnp.dot(jnp.concatenate(acts, axis=1), wd, preferred_element_type=F32)

    w0 = jnp.minimum(start & -ROW_ALIGN, t - MOE_WINDOW)
    covered = w0 + MOE_WINDOW
    window(w0, MOE_WINDOW, start, jnp.minimum(end, covered))

    def extra(k, carry):
        lo = covered + k * MOE_EXTRA
        window(jnp.minimum(lo, t - MOE_EXTRA), MOE_EXTRA, lo, jnp.minimum(end, lo + MOE_EXTRA))
        return carry

    n_extra = jnp.maximum(end - covered + MOE_EXTRA - 1, 0) >> (MOE_EXTRA.bit_length() - 1)
    lax.fori_loop(0, n_extra, extra, 0)

    @pl.when(g == N_EXPERT_GROUPS - 1)
    def _():
        dest_col = rt_ref[:, DEST_LANE:DEST_LANE + 1]
        unperm = jnp.where(lax.broadcasted_iota(jnp.int32, (t, t), 1).astype(F32) == dest_col,
                           1.0, 0.0).astype(BF16)
        ff = jnp.dot(unperm, ys[...].astype(BF16), preferred_element_type=F32)
        xn = x_ref[...] + mod_ref[0, 5:6, :] * ff
        if final_norm:
            ms = jnp.mean(xn * xn, axis=-1, keepdims=True)
            xn = xn * lax.rsqrt(ms + EPS) * ng_ref[...]
        else:
            hn_ref[...] = _modulate(xn, ng_ref[...], nmod_ref[0, 0:1, :], nmod_ref[0, 1:2, :]).astype(BF16)
        o_ref[...] = xn


def _moe_call(h2, table, aux, wg, wu, wd, x, mod, norm_g, next_mod, tokens_per_batch, final_norm):
    n, d = x.shape
    tile = MOE_TILE
    per_b = tokens_per_batch // tile
    seg = aux[:, 1, :2 * N_EXPERT_GROUPS].astype(jnp.int32)
    row = lambda i, g, seg: (i, 0)
    grp = lambda i, g, seg: (g, 0, 0)
    tok = pl.BlockSpec((tile, d), row)
    x_out = jax.ShapeDtypeStruct((n, d), F32)
    grid_spec = pltpu.PrefetchScalarGridSpec(
        num_scalar_prefetch=1,
        grid=(n // tile, N_EXPERT_GROUPS),
        in_specs=[
            pl.BlockSpec((tile, d), row),
            pl.BlockSpec((tile, LANES), row),
            pl.BlockSpec((1, AUX_ROWS, tile), lambda i, g, seg: (i, 0, 0)),
            pl.BlockSpec((EXPERTS_PER_GROUP, d, D_EXPERT), grp),
            pl.BlockSpec((EXPERTS_PER_GROUP, d, D_EXPERT), grp),
            pl.BlockSpec((EXPERTS_PER_GROUP, D_EXPERT, d), grp),
            pl.BlockSpec((tile, d), row),
            pl.BlockSpec((1, N_MOD, d), lambda i, g, seg: (i // per_b, 0, 0)),
            pl.BlockSpec((1, d), lambda i, g, seg: (0, 0)),
            pl.BlockSpec((1, N_MOD, d), lambda i, g, seg: (i // per_b, 0, 0)),
        ],
        out_specs=tok if final_norm else [tok, tok],
        scratch_shapes=[
            pltpu.VMEM((tile, d), BF16),
            pltpu.VMEM((tile, LANES), F32),
            pltpu.VMEM((tile, d), F32),
        ],
    )
    return pl.pallas_call(
        functools.partial(_moe_kernel, final_norm=final_norm),
        grid_spec=grid_spec,
        out_shape=x_out if final_norm else [x_out, jax.ShapeDtypeStruct((n, d), BF16)],
        compiler_params=_cparams(("parallel", "arbitrary")),
        name="moe_ffn",
    )(seg, h2, table, aux, wg, wu, wd, x, mod, norm_g.reshape(1, d), next_mod)


def kernel(x, c, ctx, c_ctx, w_mod, b_mod, norm1_g, norm2_g, hg_w_in, hg_lb_fwd, hg_lb_bwd, hg_gnorm,
           hg_w_out, pool_w_in, pool_w_grp, pool_scale, pool_w_out, router_w, router_b, moe_w_gate,
           moe_w_up, moe_w_down, final_g):
    b, l, d = x.shape
    depth = w_mod.shape[0]
    n_mixers = 2

    cc = jnp.concatenate([c, c_ctx[None, :], jnp.zeros((MOD_ROWS - b - 1, d), F32)], axis=0)
    mods = _mod_call(cc, w_mod, b_mod)

    x_lat = x.reshape(b * l, d)
    h_lat = None
    for i in range(depth):
        slot = i // n_mixers
        mod_lat = mods[i, :b].reshape(b, N_MOD, d)
        if i % n_mixers == 0:
            h_lat, g_lat = _norm_call(x_lat.reshape(b, l, d), norm1_g[i], mod_lat, 512,
                                      hg_w_in, slot, N_HG_PROJ - 1)
            mod_ctx = jnp.broadcast_to(mods[i, b].reshape(1, N_MOD, d), (b, N_MOD, d))
            h_ctx = _norm_call(ctx, norm1_g[i], mod_ctx, ctx.shape[1])
            y = _hgrn_call(h_ctx, h_lat, g_lat, hg_w_in, hg_lb_fwd, hg_lb_bwd, hg_gnorm[slot], slot)
            w_out = hg_w_out
        else:
            if h_lat is None:
                h_lat = _norm_call(x_lat.reshape(b, l, d), norm1_g[i], mod_lat, 512)
            y = _pool_call(h_lat, pool_w_in, pool_w_grp, pool_scale, slot)
            w_out = pool_w_out
        x_lat, h2, table, aux = _post_call(x_lat, y.reshape(b * l, d), w_out, slot, mod_lat,
                                           norm2_g[i], router_w, router_b, l)
        experts = (moe_w_gate[i].astype(BF16), moe_w_up[i].astype(BF16), moe_w_down[i].astype(BF16))
        if i == depth - 1:
            x_lat = _moe_call(h2, table, aux, *experts, x_lat, mod_lat, final_g, mod_lat, l, final_norm=True)
        else:
            mod_next = mods[i + 1, :b].reshape(b, N_MOD, d)
            x_lat, h_next = _moe_call(h2, table, aux, *experts, x_lat, mod_lat, norm1_g[i + 1], mod_next, l,
                                      final_norm=False)
            h_lat = h_next.reshape(b, l, d) if (i + 1) % n_mixers != 0 else None
    return x_lat.reshape(b, l, d)
```

```python
import functools

import jax
import jax.numpy as jnp
from jax import lax
from jax.experimental import pallas as pl
from jax.experimental.pallas import tpu as pltpu

F32 = jnp.float32
BF16 = jnp.bfloat16

EPS = 1e-6
N_MOD = 6
HG_HEADS = 8
HG_DK = 128
HG_CHUNK = 64
N_HG_PROJ = 5
POOL_WINDOWS = (2, 4, 8, 16)
GRID_W = 64
GRID_SHIFT = GRID_W.bit_length() - 1
assert 1 << GRID_SHIFT == GRID_W
N_EXPERTS = 16
N_EXPERT_GROUPS = 4
EXPERTS_PER_GROUP = N_EXPERTS // N_EXPERT_GROUPS
D_EXPERT = 256

LANES = 128
MOD_ROWS = 8
VMEM_LIMIT = 56 * 1024 * 1024


def _cparams(sem, vmem=VMEM_LIMIT):
    return pltpu.CompilerParams(dimension_semantics=sem, vmem_limit_bytes=vmem)


def _sigmoid_pair(z):
    e = jnp.exp(-jnp.abs(z))
    r = 1.0 / (1.0 + e)
    er = e * r
    pos = z >= 0
    return jnp.where(pos, r, er), jnp.where(pos, er, r)


def _silu(z):
    return z * _sigmoid_pair(z)[0]


def _sigmoid_tanh(z):
    return 0.5 * jnp.tanh(0.5 * z) + 0.5


def _silu_tanh(z):
    return z * _sigmoid_tanh(z)


def _mod_kernel(c_ref, w_ref, b_ref, o_ref):
    a = _silu(c_ref[...])
    o_ref[0] = jnp.dot(a, w_ref[0], preferred_element_type=F32,
                       precision=lax.Precision.HIGHEST) + b_ref[0]


def _mod_call(cc, w_mod, b_mod):
    depth, d, n = w_mod.shape
    tn = 1024
    return pl.pallas_call(
        _mod_kernel,
        grid=(depth, n // tn),
        in_specs=[
            pl.BlockSpec((MOD_ROWS, d), lambda i, j: (0, 0)),
            pl.BlockSpec((1, d, tn), lambda i, j: (i, 0, j)),
            pl.BlockSpec((1, 1, tn), lambda i, j: (i, 0, j)),
        ],
        out_specs=pl.BlockSpec((1, MOD_ROWS, tn), lambda i, j: (i, 0, j)),
        out_shape=jax.ShapeDtypeStruct((depth, MOD_ROWS, n), F32),
        compiler_params=_cparams(("parallel", "parallel")),
        name="mod_proj",
    )(cc, w_mod, b_mod.reshape(depth, 1, n))


def _modulate(x, g, shift, scale):
    ms = jnp.mean(x * x, axis=-1, keepdims=True)
    return (x * lax.rsqrt(ms + EPS) * g) * (1.0 + scale) + shift


def _norm_kernel(x_ref, g_ref, mod_ref, *rest):
    h = _modulate(x_ref[0], g_ref[...], mod_ref[0, 0:1, :], mod_ref[0, 1:2, :]).astype(BF16)
    if len(rest) == 1:
        (o_ref,) = rest
    else:
        w_ref, o_ref, p_ref, wbuf = rest

        @pl.when((pl.program_id(0) == 0) & (pl.program_id(1) == 0))
        def _():
            wbuf[...] = w_ref[...].astype(BF16)

        p_ref[0] = jnp.dot(h, wbuf[...], preferred_element_type=F32).astype(BF16)
    o_ref[0] = h


def _norm_call(x, g, mod, tile, w_proj=None, w_slot=0, w_col=0):
    b, t, d = x.shape
    tok = pl.BlockSpec((1, tile, d), lambda i, j: (i, j, 0))
    in_specs = [tok, pl.BlockSpec((1, d), lambda i, j: (0, 0)),
                pl.BlockSpec((1, N_MOD, d), lambda i, j: (i, 0, 0))]
    args = [x, g.reshape(1, d), mod]
    out_specs, out_shape = tok, jax.ShapeDtypeStruct((b, t, d), BF16)
    scratch = []
    if w_proj is not None:
        in_specs.append(pl.BlockSpec((None, d, d), lambda i, j: (w_slot, 0, w_col)))
        args.append(w_proj)
        out_specs, out_shape = [tok, tok], [out_shape, out_shape]
        scratch = [pltpu.VMEM((d, d), BF16)]
    return pl.pallas_call(
        _norm_kernel,
        grid=(b, t // tile),
        in_specs=in_specs,
        out_specs=out_specs,
        out_shape=out_shape,
        scratch_shapes=scratch,
        compiler_params=_cparams(("arbitrary", "arbitrary")),
        name="norm1",
    )(*args)


HG_BLOCK = 256
HG_CPB = HG_BLOCK // HG_CHUNK


def _chunk_prefix(x, row):
    for d in (1, 2, 4, 8, 16, 32):
        x = x + jnp.where(row >= d, pltpu.roll(x, d, axis=0), 0.0)
    return x


def _chunk_suffix(x, row):
    n = x.shape[0]
    for d in (1, 2, 4, 8, 16, 32):
        x = x + jnp.where(row < HG_CHUNK - d, pltpu.roll(x, n - d, axis=0), 0.0)
    return x


def _lower_bound(lb_ref, slot):
    rows = [lb_ref[j, 0] for j in range(lb_ref.shape[0])]
    m = functools.reduce(jnp.maximum, rows)
    es = [jnp.exp(r - m) for r in rows]
    return sum(es[:slot + 1]) / sum(es)


def _hgrn_kernel(hc_ref, hl_ref, g_ref, wq_ref, wv_ref, wf_ref, wb_ref, lbf_ref, lbb_ref, gn_ref, y_ref,
                 wbuf, pbuf0, pbuf1, pbuf2, pbuf3, oacc, qif, qib, kvf, kvb, decf, decb, spf, spb,
                 *, slot, n_ctx_chunks, n_lat_chunks):
    lb_f = _lower_bound(lbf_ref, slot)
    lb_b = _lower_bound(lbb_ref, slot)
    row = lax.broadcasted_iota(jnp.int32, (HG_BLOCK, HG_DK), 0) & (HG_CHUNK - 1)
    ci = lax.broadcasted_iota(jnp.int32, (HG_CPB, HG_CHUNK, HG_CHUNK), 1)
    si = lax.broadcasted_iota(jnp.int32, (HG_CPB, HG_CHUNK, HG_CHUNK), 2)

    def c3(t):
        return t.reshape(HG_CPB, HG_CHUNK, HG_DK)

    def direction(qs3, v3b, z, lb, fwd):
        sig = _sigmoid_tanh(z)
        f = lb + (1.0 - lb) * sig
        k3 = c3((1.0 - lb) * (1.0 - sig))
        lf = jnp.log(f)
        if fwd:
            cum = c3(_chunk_prefix(lf, row))
            ref = cum[:, HG_CHUNK // 2 - 1:HG_CHUNK // 2, :]
            last = cum[:, HG_CHUNK - 1:HG_CHUNK, :]
        else:
            cum = c3(_chunk_suffix(lf, row))
            ref = cum[:, HG_CHUNK // 2:HG_CHUNK // 2 + 1, :]
            last = cum[:, 0:1, :]
        dec = jnp.exp(last)
        if qs3 is None:
            kl = k3 * jnp.exp(last - cum)
        else:
            e1 = jnp.exp(cum - ref)
            qd = qs3 * e1
            qi = qd * jnp.exp(ref)
            kd = k3 * (1.0 / e1)
            kl = kd * jnp.exp(last - ref)
        kvt = jnp.einsum('ncv,nck->nvk', v3b, kl.astype(BF16), preferred_element_type=F32)
        if qs3 is None:
            return None, None, kvt, dec
        sc = jnp.einsum('nck,nsk->ncs', qd.astype(BF16), kd.astype(BF16), preferred_element_type=F32)
        sc = jnp.where((ci >= si) if fwd else (ci <= si), sc, 0.0)
        intra = jnp.einsum('ncs,nsv->ncv', sc.astype(BF16), v3b, preferred_element_type=F32)
        return intra, qi, kvt, dec

    for p, wp_ref in enumerate((wq_ref, wv_ref, wf_ref, wb_ref)):
        wbuf[:, p * HG_DK:(p + 1) * HG_DK] = wp_ref[...].astype(BF16)

    def project(hrows):
        return jnp.dot(hrows, wbuf[...], preferred_element_type=F32)

    def block(p, chunk0, lat_row0):
        v3b = c3(p[:, HG_DK:2 * HG_DK]).astype(BF16)
        zf = p[:, 2 * HG_DK:3 * HG_DK]
        zb = p[:, 3 * HG_DK:4 * HG_DK]
        if lat_row0 is None:
            qs3 = None
        else:
            qs3 = c3(_silu_tanh(p[:, 0:HG_DK]))
        in_f, qi_f, kv_f, dec_f = direction(qs3, v3b, zf, lb_f, True)
        in_b, qi_b, kv_b, dec_b = direction(qs3, v3b, zb, lb_b, False)
        kvf[pl.ds(chunk0, HG_CPB)] = kv_f
        kvb[pl.ds(chunk0, HG_CPB)] = kv_b
        decf[pl.ds(chunk0, HG_CPB)] = dec_f
        decb[pl.ds(chunk0, HG_CPB)] = dec_b
        if lat_row0 is not None:
            rows = pl.ds(lat_row0, HG_BLOCK)
            oacc[rows, :] = (in_f + in_b).reshape(HG_BLOCK, HG_DK)
            qif[rows, :] = qi_f.reshape(HG_BLOCK, HG_DK).astype(BF16)
            qib[rows, :] = qi_b.reshape(HG_BLOCK, HG_DK).astype(BF16)

    n_blocks = n_lat_chunks // HG_CPB

    def lat_rows(i):
        return hl_ref[0, pl.ds(pl.multiple_of(i * HG_BLOCK, HG_BLOCK), HG_BLOCK), :]

    def lat_terms(p_ref, i):
        block(p_ref[...], n_ctx_chunks + i * HG_CPB, pl.multiple_of(i * HG_BLOCK, HG_BLOCK))

    ctx_p = [project(hc_ref[0, i * HG_BLOCK:(i + 1) * HG_BLOCK, :]) for i in range(n_ctx_chunks // HG_CPB)]
    pbuf0[...] = project(lat_rows(0))
    for i, p in enumerate(ctx_p):
        block(p, i * HG_CPB, None)
    pbuf1[...] = project(lat_rows(1))

    def lat_quad(i0, last):
        pbuf2[...] = project(lat_rows(i0 + 2))
        lat_terms(pbuf0, i0)
        pbuf3[...] = project(lat_rows(i0 + 3))
        lat_terms(pbuf1, i0 + 1)
        if not last:
            pbuf0[...] = project(lat_rows(i0 + 4))
        lat_terms(pbuf2, i0 + 2)
        if not last:
            pbuf1[...] = project(lat_rows(i0 + 5))
        lat_terms(pbuf3, i0 + 3)

    def lat_body(t, carry):
        lat_quad(4 * t, False)
        return carry

    lax.fori_loop(0, n_blocks // 4 - 1, lat_body, 0)
    lat_quad(n_blocks - 4, True)

    def advance(s, kv_ref, dec_ref, n):
        return dec_ref[n] * s + kv_ref[n]

    sf = jnp.zeros((HG_DK, HG_DK), F32)
    for n in range(n_ctx_chunks):
        sf = advance(sf, kvf, decf, n)
    sb = jnp.zeros((HG_DK, HG_DK), F32)
    for n in reversed(range(n_ctx_chunks)):
        sb = advance(sb, kvb, decb, n)

    def scan_step(t, carry):
        sf, sb = carry
        jb = n_lat_chunks - 1 - t
        spf[t] = sf.astype(BF16)
        spb[jb] = sb.astype(BF16)
        return (advance(sf, kvf, decf, n_ctx_chunks + t), advance(sb, kvb, decb, n_ctx_chunks + jb))

    lax.fori_loop(0, n_lat_chunks, scan_step, (sf, sb), unroll=2)

    gn = gn_ref[...]

    def block_rows(i):
        return pl.ds(pl.multiple_of(i * HG_BLOCK, HG_BLOCK), HG_BLOCK)

    def inter(i):
        rows = block_rows(i)
        chunks = pl.ds(i * HG_CPB, HG_CPB)
        return (jnp.einsum('nck,nvk->ncv', c3(qif[rows, :]), spf[chunks], preferred_element_type=F32)
                + jnp.einsum('nck,nvk->ncv', c3(qib[rows, :]), spb[chunks], preferred_element_type=F32)
                ).reshape(HG_BLOCK, HG_DK)

    def readout(o_ref, i):
        rows = block_rows(i)
        o = oacc[rows, :] + o_ref[...]
        ms = jnp.mean(o * o, axis=-1, keepdims=True)
        o = o * lax.rsqrt(ms + EPS) * gn
        y_ref[0, rows, :] = (o * _silu_tanh(g_ref[0, rows, :].astype(F32))).astype(BF16)

    obuf0, obuf1 = pbuf0.at[:, 0:HG_DK], pbuf1.at[:, 0:HG_DK]
    obuf0[...] = inter(0)

    def readout_pair(t, carry):
        obuf1[...] = inter(2 * t + 1)
        readout(obuf0, 2 * t)
        obuf0[...] = inter(2 * t + 2)
        readout(obuf1, 2 * t + 1)
        return carry

    lax.fori_loop(0, n_blocks // 2 - 1, readout_pair, 0)
    obuf1[...] = inter(n_blocks - 1)
    readout(obuf0, n_blocks - 2)
    readout(obuf1, n_blocks - 1)


def _hgrn_call(h_ctx, h_lat, g_lat, w_in, lb_fwd, lb_bwd, gnorm, slot):
    b, lc, d = h_ctx.shape
    ll = h_lat.shape[1]
    nrow = lb_fwd.shape[0]
    ncc, nlc = lc // HG_CHUNK, ll // HG_CHUNK
    kern = functools.partial(_hgrn_kernel, slot=slot, n_ctx_chunks=ncc, n_lat_chunks=nlc)
    lb_spec = pl.BlockSpec((nrow, 1, 1, HG_DK), lambda i, h: (0, h, 0, 0))
    head_cols = pl.BlockSpec((1, ll, HG_DK), lambda i, h: (i, 0, h))
    n_rec = N_HG_PROJ - 1
    w_cols = [pl.BlockSpec((None, d, HG_DK), functools.partial(lambda i, h, p: (slot, 0, p * HG_HEADS + h), p=p))
              for p in range(n_rec)]
    return pl.pallas_call(
        kern,
        grid=(b, HG_HEADS),
        in_specs=[
            pl.BlockSpec((1, lc, d), lambda i, h: (i, 0, 0)),
            pl.BlockSpec((1, ll, d), lambda i, h: (i, 0, 0)),
            head_cols,
            *w_cols,
            lb_spec, lb_spec,
            pl.BlockSpec((1, HG_DK), lambda i, h: (0, 0)),
        ],
        out_specs=head_cols,
        out_shape=jax.ShapeDtypeStruct((b, ll, d), BF16),
        scratch_shapes=[
            pltpu.VMEM((d, n_rec * HG_DK), BF16),
            pltpu.VMEM((HG_BLOCK, n_rec * HG_DK), F32),
            pltpu.VMEM((HG_BLOCK, n_rec * HG_DK), F32),
            pltpu.VMEM((HG_BLOCK, n_rec * HG_DK), F32),
            pltpu.VMEM((HG_BLOCK, n_rec * HG_DK), F32),
            pltpu.VMEM((ll, HG_DK), F32),
            pltpu.VMEM((ll, HG_DK), BF16),
            pltpu.VMEM((ll, HG_DK), BF16),
            pltpu.VMEM((ncc + nlc, HG_DK, HG_DK), F32),
            pltpu.VMEM((ncc + nlc, HG_DK, HG_DK), F32),
            pltpu.VMEM((ncc + nlc, 1, HG_DK), F32),
            pltpu.VMEM((ncc + nlc, 1, HG_DK), F32),
            pltpu.VMEM((nlc, HG_DK, HG_DK), BF16),
            pltpu.VMEM((nlc, HG_DK, HG_DK), BF16),
        ],
        compiler_params=_cparams(("parallel", "arbitrary")),
        name="hgrn2",
    )(h_ctx, h_lat, g_lat, *([w_in] * n_rec),
      lb_fwd.reshape(nrow, HG_HEADS, 1, HG_DK), lb_bwd.reshape(nrow, HG_HEADS, 1, HG_DK),
      gnorm.reshape(1, HG_DK))


POOL_STRIP = 8
POOL_PAD = 8
POOL_BAND = 256


def _pool_group(h_ref, win_ref, wg_ref, ps_ref, o_ref, upad, *, win, rows):
    gd = win_ref.shape[1]
    half = win // 2
    u = jnp.dot(h_ref[0], win_ref[...].astype(BF16), preferred_element_type=F32)
    zeros = jnp.zeros((POOL_PAD, GRID_W, gd), F32)
    upad[0:POOL_PAD] = zeros
    upad[POOL_PAD + rows:POOL_PAD + rows + POOL_PAD] = zeros
    upad[POOL_PAD:POOL_PAD + rows] = u.reshape(rows, GRID_W, gd)

    tok = POOL_STRIP * GRID_W
    bi = lax.broadcasted_iota(jnp.int32, (POOL_BAND, POOL_BAND), 0)
    bj = lax.broadcasted_iota(jnp.int32, (POOL_BAND, POOL_BAND), 1)
    lo_c = (bi & (GRID_W - 1)) - half
    cj = bj & (GRID_W - 1)
    band = jnp.where(bi >> GRID_SHIFT == bj >> GRID_SHIFT, 1.0, 0.0)
    band = jnp.where(cj >= lo_c, band, 0.0)
    band = jnp.where(cj < lo_c + win, band, 0.0).astype(BF16)
    t = lax.broadcasted_iota(jnp.int32, (tok, LANES), 0)
    col = t & (GRID_W - 1)
    cnt_c = jnp.minimum(col - half + win, GRID_W) - jnp.maximum(col - half, 0)
    wg = wg_ref[0].astype(BF16)
    ps = ps_ref[...]

    def strip(i, carry):
        r0 = i * POOL_STRIP
        slab = upad[pl.ds(r0 + POOL_PAD - half, POOL_STRIP + win - 1)]
        span = 1
        while span < win:
            n = slab.shape[0] - span
            slab = slab[0:n] + slab[span:span + n]
            span *= 2
        rs = slab.reshape(tok, gd)
        hi = rs.astype(BF16)
        lo = (rs - hi.astype(F32)).astype(BF16)
        parts = []
        for k in range(tok // POOL_BAND):
            sl = slice(k * POOL_BAND, (k + 1) * POOL_BAND)
            parts.append(jnp.dot(band, hi[sl], preferred_element_type=F32)
                         + jnp.dot(band, lo[sl], preferred_element_type=F32))
        box = jnp.concatenate(parts, axis=0)
        r = r0 + (t >> GRID_SHIFT)
        cnt_r = jnp.minimum(r - half + win, rows) - jnp.maximum(r - half, 0)
        inv = 1.0 / (cnt_r * cnt_c).astype(F32)
        mean = box * jnp.concatenate([inv] * (gd // LANES), axis=1)
        ug = upad[pl.ds(r0 + POOL_PAD, POOL_STRIP)].reshape(tok, gd)
        z = jnp.dot((mean - ug).astype(BF16), wg, preferred_element_type=F32) * ps
        o_ref[0, pl.ds(pl.multiple_of(i * tok, tok), tok), :] = z.astype(BF16)
        return carry

    lax.fori_loop(0, rows // POOL_STRIP, strip, 0, unroll=2)


def _pool_kernel(h_ref, win_ref, wg_ref, ps_ref, o_ref, upad, *, rows):
    g = pl.program_id(1)
    for gi, win in enumerate(POOL_WINDOWS):
        @pl.when(g == gi)
        def _(win=win):
            _pool_group(h_ref, win_ref, wg_ref, ps_ref, o_ref, upad, win=win, rows=rows)


def _pool_call(h, w_in, w_grp, p_scale, slot):
    b, l, d = h.shape
    ng = len(POOL_WINDOWS)
    gd = d // ng
    rows = l // GRID_W
    return pl.pallas_call(
        functools.partial(_pool_kernel, rows=rows),
        grid=(b, ng),
        in_specs=[
            pl.BlockSpec((1, l, d), lambda i, g: (i, 0, 0)),
            pl.BlockSpec((None, d, gd), lambda i, g: (slot, 0, g)),
            pl.BlockSpec((None, 1, gd, gd), lambda i, g: (slot, g, 0, 0)),
            pl.BlockSpec((None, 1, gd), lambda i, g: (slot, 0, g)),
        ],
        out_specs=pl.BlockSpec((1, l, gd), lambda i, g: (i, 0, g)),
        out_shape=jax.ShapeDtypeStruct((b, l, d), BF16),
        scratch_shapes=[pltpu.VMEM((rows + 2 * POOL_PAD, GRID_W, gd), F32)],
        compiler_params=_cparams(("parallel", "arbitrary")),
        name="pool_mix",
    )(h, w_in, w_grp, p_scale.reshape(p_scale.shape[0], 1, d))


MOE_TILE = 1024
MOE_WINDOW = 304
MOE_EXTRA = 64
ROW_ALIGN = 16
DEST_LANE = N_EXPERTS
PIECE_STRIDE = 32
N_PIECES = 3
AUX_ROWS = 8


def _route(sel, s):
    keep = []
    gsum = []
    for g in range(N_EXPERT_GROUPS):
        a = sel[g * EXPERTS_PER_GROUP:(g + 1) * EXPERTS_PER_GROUP]
        beaten = [jnp.zeros_like(a[0]) for _ in a]
        for i in range(EXPERTS_PER_GROUP):
            for j in range(i + 1, EXPERTS_PER_GROUP):
                ge = jnp.where(a[i] >= a[j], 1.0, 0.0)
                beaten[j] = beaten[j] + ge
                beaten[i] = beaten[i] + (1.0 - ge)
        kg = [jnp.where(bt < 1.5, 1.0, 0.0) for bt in beaten]
        keep.append(kg)
        gsum.append(sum(k * x for k, x in zip(kg, a)))
    picked = []
    bests = []
    for g in range(N_EXPERT_GROUPS):
        better = jnp.zeros_like(gsum[0])
        for o in range(N_EXPERT_GROUPS):
            if o < g:
                better = better + jnp.where(gsum[o] >= gsum[g], 1.0, 0.0)
            elif o > g:
                better = better + jnp.where(gsum[o] > gsum[g], 1.0, 0.0)
        best = jnp.where(better < 0.5, 1.0, 0.0)
        bests.append(best)
        for i in range(EXPERTS_PER_GROUP):
            picked.append(best * keep[g][i] * s[g * EXPERTS_PER_GROUP + i])
    den = sum(picked)
    return [p / den for p in picked], bests


def _post_kernel(x_ref, y_ref, w_ref, mod_ref, g_ref, rw_ref, rb_ref,
                 xo_ref, h2_ref, rt_ref, aux_ref, wbuf):
    @pl.when(pl.program_id(0) == 0)
    def _():
        wbuf[...] = w_ref[...].astype(BF16)

    yw = jnp.dot(y_ref[...], wbuf[...], preferred_element_type=F32)
    xn = x_ref[...] + mod_ref[0, 2:3, :] * yw
    xo_ref[...] = xn
    h2 = _modulate(xn, g_ref[...], mod_ref[0, 3:4, :], mod_ref[0, 4:5, :])
    hi = h2.astype(BF16)
    h2_ref[...] = hi
    lo = (h2 - hi.astype(F32)).astype(BF16)
    rw = rw_ref[...]
    half = hi.shape[0] // 2
    prod = jnp.concatenate(
        [jnp.dot(hi[r:r + half], rw, preferred_element_type=F32)
         + jnp.dot(lo[r:r + half], rw, preferred_element_type=F32) for r in (0, half)], axis=0).T
    logits = prod[0:N_EXPERTS, :] + prod[N_EXPERTS:2 * N_EXPERTS, :]
    s = _sigmoid_pair(logits)[0]
    sel = s + rb_ref[...]
    comb, bests = _route([sel[e:e + 1, :] for e in range(N_EXPERTS)],
                         [s[e:e + 1, :] for e in range(N_EXPERTS)])
    t = logits.shape[1]

    ind = jnp.concatenate(bests + [jnp.zeros((AUX_ROWS - N_EXPERT_GROUPS, t), F32)], axis=0)
    n_seg = t // LANES
    stacked = jnp.concatenate([ind[:, j * LANES:(j + 1) * LANES] for j in range(n_seg)], axis=0)
    upper = jnp.where(lax.broadcasted_iota(jnp.int32, (LANES, LANES), 0)
                      <= lax.broadcasted_iota(jnp.int32, (LANES, LANES), 1), 1.0, 0.0).astype(BF16)
    local = jnp.dot(stacked.astype(BF16), upper, preferred_element_type=F32)
    off = jnp.zeros((AUX_ROWS, 1), F32)
    pieces = []
    for j in range(n_seg):
        seg_cum = local[j * AUX_ROWS:(j + 1) * AUX_ROWS, :]
        pieces.append(seg_cum + off)
        off = off + seg_cum[:, LANES - 1:LANES]
    cum = jnp.concatenate(pieces, axis=1)
    counts = [off[g:g + 1, :] for g in range(N_EXPERT_GROUPS)]
    starts = [jnp.zeros((1, 1), F32)]
    for g in range(1, N_EXPERT_GROUPS):
        starts.append(starts[-1] + counts[g - 1])
    dest = sum(bests[g] * (starts[g] + cum[g:g + 1, :] - 1.0) for g in range(N_EXPERT_GROUPS))

    lane = lax.broadcasted_iota(jnp.int32, (1, t), 1)
    seg = sum(jnp.where(lane == k, v, 0.0) for k, v in enumerate(starts + counts))
    aux_ref[0] = jnp.concatenate([dest, seg, jnp.zeros((AUX_ROWS - 2, t), F32)], axis=0)
    p1 = [w.astype(BF16).astype(F32) for w in comb]
    r1 = [w - p for w, p in zip(comb, p1)]
    p2 = [r.astype(BF16).astype(F32) for r in r1]
    p3 = [r - p for r, p in zip(r1, p2)]
    pad = [jnp.zeros((PIECE_STRIDE - N_EXPERTS, t), F32)]
    table = jnp.concatenate(p1 + [dest] + [jnp.zeros((PIECE_STRIDE - N_EXPERTS - 1, t), F32)] + p2 + pad + p3 + pad
                            + [jnp.zeros((LANES - N_PIECES * PIECE_STRIDE, t), F32)], axis=0)
    rt_ref[...] = table.T


def _post_call(x, y, w, w_slot, mod, g, router_w, router_b, tokens_per_batch, tile=MOE_TILE):
    n, d = x.shape
    per_b = tokens_per_batch // tile
    rwh = router_w.astype(BF16)
    rwl = (router_w - rwh.astype(F32)).astype(BF16)
    rw = jnp.concatenate([rwh, rwl, jnp.zeros((d, LANES - 2 * N_EXPERTS), BF16)], axis=1)
    row = lambda i: (i, 0)
    fixed = lambda i: (0, 0)
    return pl.pallas_call(
        _post_kernel,
        grid=(n // tile,),
        in_specs=[
            pl.BlockSpec((tile, d), row),
            pl.BlockSpec((tile, d), row),
            pl.BlockSpec((None, d, d), lambda i: (w_slot, 0, 0)),
            pl.BlockSpec((1, N_MOD, d), lambda i: (i // per_b, 0, 0)),
            pl.BlockSpec((1, d), fixed),
            pl.BlockSpec((d, LANES), fixed),
            pl.BlockSpec((N_EXPERTS, 1), fixed),
        ],
        out_specs=[
            pl.BlockSpec((tile, d), row),
            pl.BlockSpec((tile, d), row),
            pl.BlockSpec((tile, LANES), row),
            pl.BlockSpec((1, AUX_ROWS, tile), lambda i: (i, 0, 0)),
        ],
        out_shape=[
            jax.ShapeDtypeStruct((n, d), F32),
            jax.ShapeDtypeStruct((n, d), BF16),
            jax.ShapeDtypeStruct((n, LANES), F32),
            jax.ShapeDtypeStruct((n // tile, AUX_ROWS, tile), F32),
        ],
        scratch_shapes=[pltpu.VMEM((d, d), BF16)],
        compiler_params=_cparams(("arbitrary",)),
        name="mixer_out_router",
    )(x, y, w, mod, g.reshape(1, d), rw, router_b.reshape(N_EXPERTS, 1))


def _moe_kernel(seg_ref, h_ref, rt_ref, aux_ref, wg_ref, wu_ref, wd_ref, x_ref, mod_ref, ng_ref, nmod_ref,
                o_ref, *rest, final_norm):
    if final_norm:
        xs, cws, ys = rest
    else:
        hn_ref, xs, cws, ys = rest
    i = pl.program_id(0)
    g = pl.program_id(1)
    t = h_ref.shape[0]

    @pl.when(g == 0)
    def _():
        dest_row = aux_ref[0, 0:1, :]
        perm = jnp.where(lax.broadcasted_iota(jnp.int32, (t, t), 0).astype(F32) == dest_row,
                         1.0, 0.0).astype(BF16)
        xs[...] = jnp.dot(perm, h_ref[...], preferred_element_type=F32).astype(BF16)
        cws[...] = jnp.dot(perm, rt_ref[...].astype(BF16), preferred_element_type=F32)
        ys[...] = jnp.zeros_like(ys)

    start = seg_ref[i, g]
    end = start + seg_ref[i, N_EXPERT_GROUPS + g]
    wd = wd_ref[...].reshape(EXPERTS_PER_GROUP * D_EXPERT, wd_ref.shape[2])

    def window(w0, rows, lo, hi):
        sl = pl.ds(pl.multiple_of(w0, ROW_ALIGN), rows)
        xw = xs[sl, :]
        cw = cws[sl, :]
        r = w0 + lax.broadcasted_iota(jnp.int32, cw.shape, 0)
        lane = lax.broadcasted_iota(jnp.int32, cw.shape, 1)
        piece_lane = jnp.where(lane < N_PIECES * PIECE_STRIDE, lane & (PIECE_STRIDE - 1), -1)
        cw = jnp.where(r >= lo, cw, 0.0)
        cw = jnp.where(r < hi, cw, 0.0)
        acts = []
        for e in range(EXPERTS_PER_GROUP):
            ce = jnp.sum(jnp.where(piece_lane == g * EXPERTS_PER_GROUP + e, cw, 0.0), axis=1, keepdims=True)
            a = _silu_tanh(jnp.dot(xw, wg_ref[e], preferred_element_type=F32)) \
                * jnp.dot(xw, wu_ref[e], preferred_element_type=F32)
            acts.append((a * ce).astype(BF16))
        ys[sl, :] += jnp.dot(jnp.concatenate(acts, axis=1), wd, preferred_element_type=F32)

    w0 = jnp.minimum(start & -ROW_ALIGN, t - MOE_WINDOW)
    covered = w0 + MOE_WINDOW
    window(w0, MOE_WINDOW, start, jnp.minimum(end, covered))

    def extra(k, carry):
        lo = covered + k * MOE_EXTRA
        window(jnp.minimum(lo, t - MOE_EXTRA), MOE_EXTRA, lo, jnp.minimum(end, lo + MOE_EXTRA))
        return carry

    n_extra = jnp.maximum(end - covered + MOE_EXTRA - 1, 0) >> (MOE_EXTRA.bit_length() - 1)
    lax.fori_loop(0, n_extra, extra, 0)

    @pl.when(g == N_EXPERT_GROUPS - 1)
    def _():
        dest_col = rt_ref[:, DEST_LANE:DEST_LANE + 1]
        unperm = jnp.where(lax.broadcasted_iota(jnp.int32, (t, t), 1).astype(F32) == dest_col,
                           1.0, 0.0).astype(BF16)
        ff = jnp.dot(unperm, ys[...].astype(BF16), preferred_element_type=F32)
        xn = x_ref[...] + mod_ref[0, 5:6, :] * ff
        if final_norm:
            ms = jnp.mean(xn * xn, axis=-1, keepdims=True)
            xn = xn * lax.rsqrt(ms + EPS) * ng_ref[...]
        else:
            hn_ref[...] = _modulate(xn, ng_ref[...], nmod_ref[0, 0:1, :], nmod_ref[0, 1:2, :]).astype(BF16)
        o_ref[...] = xn


def _moe_call(h2, table, aux, wg, wu, wd, layer, x, mod, norm_g, next_mod, tokens_per_batch, final_norm):
    n, d = x.shape
    tile = MOE_TILE
    per_b = tokens_per_batch // tile
    seg = aux[:, 1, :2 * N_EXPERT_GROUPS].astype(jnp.int32)
    row = lambda i, g, seg: (i, 0)
    grp = lambda i, g, seg: (layer, g, 0, 0)
    tok = pl.BlockSpec((tile, d), row)
    x_out = jax.ShapeDtypeStruct((n, d), F32)
    grid_spec = pltpu.PrefetchScalarGridSpec(
        num_scalar_prefetch=1,
        grid=(n // tile, N_EXPERT_GROUPS),
        in_specs=[
            pl.BlockSpec((tile, d), row),
            pl.BlockSpec((tile, LANES), row),
            pl.BlockSpec((1, AUX_ROWS, tile), lambda i, g, seg: (i, 0, 0)),
            pl.BlockSpec((None, EXPERTS_PER_GROUP, d, D_EXPERT), grp),
            pl.BlockSpec((None, EXPERTS_PER_GROUP, d, D_EXPERT), grp),
            pl.BlockSpec((None, EXPERTS_PER_GROUP, D_EXPERT, d), grp),
            pl.BlockSpec((tile, d), row),
            pl.BlockSpec((1, N_MOD, d), lambda i, g, seg: (i // per_b, 0, 0)),
            pl.BlockSpec((1, d), lambda i, g, seg: (0, 0)),
            pl.BlockSpec((1, N_MOD, d), lambda i, g, seg: (i // per_b, 0, 0)),
        ],
        out_specs=tok if final_norm else [tok, tok],
        scratch_shapes=[
            pltpu.VMEM((tile, d), BF16),
            pltpu.VMEM((tile, LANES), F32),
            pltpu.VMEM((tile, d), F32),
        ],
    )
    return pl.pallas_call(
        functools.partial(_moe_kernel, final_norm=final_norm),
        grid_spec=grid_spec,
        out_shape=x_out if final_norm else [x_out, jax.ShapeDtypeStruct((n, d), BF16)],
        compiler_params=_cparams(("parallel", "arbitrary")),
        name="moe_ffn",
    )(seg, h2, table, aux, wg, wu, wd, x, mod, norm_g.reshape(1, d), next_mod)


def kernel(x, c, ctx, c_ctx, w_mod, b_mod, norm1_g, norm2_g, hg_w_in, hg_lb_fwd, hg_lb_bwd, hg_gnorm,
           hg_w_out, pool_w_in, pool_w_grp, pool_scale, pool_w_out, router_w, router_b, moe_w_gate,
           moe_w_up, moe_w_down, final_g):
    b, l, d = x.shape
    depth = w_mod.shape[0]
    n_mixers = 2

    cc = jnp.concatenate([c, c_ctx[None, :], jnp.zeros((MOD_ROWS - b - 1, d), F32)], axis=0)
    mods = _mod_call(cc, w_mod, b_mod)

    experts = (moe_w_gate.astype(BF16), moe_w_up.astype(BF16), moe_w_down.astype(BF16))
    x_lat = x.reshape(b * l, d)
    h_lat = None
    for i in range(depth):
        slot = i // n_mixers
        mod_lat = mods[i, :b].reshape(b, N_MOD, d)
        if i % n_mixers == 0:
            h_lat, g_lat = _norm_call(x_lat.reshape(b, l, d), norm1_g[i], mod_lat, 512,
                                      hg_w_in, slot, N_HG_PROJ - 1)
            mod_ctx = jnp.broadcast_to(mods[i, b].reshape(1, N_MOD, d), (b, N_MOD, d))
            h_ctx = _norm_call(ctx, norm1_g[i], mod_ctx, ctx.shape[1])
            y = _hgrn_call(h_ctx, h_lat, g_lat, hg_w_in, hg_lb_fwd, hg_lb_bwd, hg_gnorm[slot], slot)
            w_out = hg_w_out
        else:
            if h_lat is None:
                h_lat = _norm_call(x_lat.reshape(b, l, d), norm1_g[i], mod_lat, 512)
            y = _pool_call(h_lat, pool_w_in, pool_w_grp, pool_scale, slot)
            w_out = pool_w_out
        x_lat, h2, table, aux = _post_call(x_lat, y.reshape(b * l, d), w_out, slot, mod_lat,
                                           norm2_g[i], router_w, router_b, l)
        if i == depth - 1:
            x_lat = _moe_call(h2, table, aux, *experts, i, x_lat, mod_lat, final_g, mod_lat, l, final_norm=True)
        else:
            mod_next = mods[i + 1, :b].reshape(b, N_MOD, d)
            x_lat, h_next = _moe_call(h2, table, aux, *experts, i, x_lat, mod_lat, norm1_g[i + 1], mod_next, l,
                                      final_norm=False)
            h_lat = h_next.reshape(b, l, d) if (i + 1) % n_mixers != 0 else None
    return x_lat.reshape(b, l, d)
```

```python
import functools

import jax
import jax.numpy as jnp
from jax import lax
from jax.experimental import pallas as pl
from jax.experimental.pallas import tpu as pltpu

F32 = jnp.float32
BF16 = jnp.bfloat16

EPS = 1e-6
N_MOD = 6
HG_HEADS = 8
HG_DK = 128
HG_CHUNK = 64
N_HG_PROJ = 5
POOL_WINDOWS = (2, 4, 8, 16)
GRID_W = 64
GRID_SHIFT = GRID_W.bit_length() - 1
assert 1 << GRID_SHIFT == GRID_W
N_EXPERTS = 16
N_EXPERT_GROUPS = 4
EXPERTS_PER_GROUP = N_EXPERTS // N_EXPERT_GROUPS
D_EXPERT = 256

LANES = 128
MOD_ROWS = 8
VMEM_LIMIT = 56 * 1024 * 1024


def _cparams(sem, vmem=VMEM_LIMIT):
    return pltpu.CompilerParams(dimension_semantics=sem, vmem_limit_bytes=vmem)


def _sigmoid_pair(z):
    e = jnp.exp(-jnp.abs(z))
    r = 1.0 / (1.0 + e)
    er = e * r
    pos = z >= 0
    return jnp.where(pos, r, er), jnp.where(pos, er, r)


def _silu(z):
    return z * _sigmoid_pair(z)[0]


def _sigmoid_tanh(z):
    return 0.5 * jnp.tanh(0.5 * z) + 0.5


def _silu_tanh(z):
    return z * _sigmoid_tanh(z)


def _split_bf16(x):
    hi = x.astype(BF16)
    return hi, (x - hi.astype(F32)).astype(BF16)


def _mod_kernel(c_ref, w_ref, b_ref, o_ref):
    a = jnp.concatenate(_split_bf16(_silu(c_ref[...])), axis=0)
    w_hi, w_lo = _split_bf16(w_ref[0])
    acc = jnp.dot(a, w_hi, preferred_element_type=F32) + jnp.dot(a, w_lo, preferred_element_type=F32)
    o_ref[0] = acc[0:MOD_ROWS] + acc[MOD_ROWS:2 * MOD_ROWS] + b_ref[0]


def _mod_call(cc, w_mod, b_mod):
    depth, d, n = w_mod.shape
    tn = 1024
    return pl.pallas_call(
        _mod_kernel,
        grid=(depth, n // tn),
        in_specs=[
            pl.BlockSpec((MOD_ROWS, d), lambda i, j: (0, 0)),
            pl.BlockSpec((1, d, tn), lambda i, j: (i, 0, j)),
            pl.BlockSpec((1, 1, tn), lambda i, j: (i, 0, j)),
        ],
        out_specs=pl.BlockSpec((1, MOD_ROWS, tn), lambda i, j: (i, 0, j)),
        out_shape=jax.ShapeDtypeStruct((depth, MOD_ROWS, n), F32),
        compiler_params=_cparams(("parallel", "parallel")),
        name="mod_proj",
    )(cc, w_mod, b_mod.reshape(depth, 1, n))


def _modulate(x, g, shift, scale):
    ms = jnp.mean(x * x, axis=-1, keepdims=True)
    return (x * lax.rsqrt(ms + EPS) * g) * (1.0 + scale) + shift


def _norm_kernel(x_ref, g_ref, mod_ref, *rest):
    h = _modulate(x_ref[0], g_ref[...], mod_ref[0, 0:1, :], mod_ref[0, 1:2, :]).astype(BF16)
    if len(rest) == 1:
        (o_ref,) = rest
    else:
        w_ref, o_ref, p_ref, wbuf = rest

        @pl.when((pl.program_id(0) == 0) & (pl.program_id(1) == 0))
        def _():
            wbuf[...] = w_ref[...].astype(BF16)

        p_ref[0] = jnp.dot(h, wbuf[...], preferred_element_type=F32).astype(BF16)
    o_ref[0] = h


def _norm_call(x, g, mod, tile, w_proj=None, w_slot=0, w_col=0):
    b, t, d = x.shape
    tok = pl.BlockSpec((1, tile, d), lambda i, j: (i, j, 0))
    in_specs = [tok, pl.BlockSpec((1, d), lambda i, j: (0, 0)),
                pl.BlockSpec((1, N_MOD, d), lambda i, j: (i, 0, 0))]
    args = [x, g.reshape(1, d), mod]
    out_specs, out_shape = tok, jax.ShapeDtypeStruct((b, t, d), BF16)
    scratch = []
    if w_proj is not None:
        in_specs.append(pl.BlockSpec((None, d, d), lambda i, j: (w_slot, 0, w_col)))
        args.append(w_proj)
        out_specs, out_shape = [tok, tok], [out_shape, out_shape]
        scratch = [pltpu.VMEM((d, d), BF16)]
    return pl.pallas_call(
        _norm_kernel,
        grid=(b, t // tile),
        in_specs=in_specs,
        out_specs=out_specs,
        out_shape=out_shape,
        scratch_shapes=scratch,
        compiler_params=_cparams(("arbitrary", "arbitrary")),
        name="norm1",
    )(*args)


HG_BLOCK = 256
HG_CPB = HG_BLOCK // HG_CHUNK


def _chunk_prefix(x, row):
    for d in (1, 2, 4, 8, 16, 32):
        x = x + jnp.where(row >= d, pltpu.roll(x, d, axis=0), 0.0)
    return x


def _chunk_suffix(x, row):
    n = x.shape[0]
    for d in (1, 2, 4, 8, 16, 32):
        x = x + jnp.where(row < HG_CHUNK - d, pltpu.roll(x, n - d, axis=0), 0.0)
    return x


def _lower_bound(lb_ref, slot):
    rows = [lb_ref[j, 0] for j in range(lb_ref.shape[0])]
    m = functools.reduce(jnp.maximum, rows)
    es = [jnp.exp(r - m) for r in rows]
    return sum(es[:slot + 1]) / sum(es)


def _hgrn_kernel(hc_ref, hl_ref, g_ref, wq_ref, wv_ref, wf_ref, wb_ref, lbf_ref, lbb_ref, gn_ref, y_ref,
                 wbuf, pbuf0, pbuf1, pbuf2, pbuf3, oacc, qif, qib, kvf, kvb, decf, decb, spf, spb,
                 *, slot, n_ctx_chunks, n_lat_chunks):
    lb_f = _lower_bound(lbf_ref, slot)
    lb_b = _lower_bound(lbb_ref, slot)
    row = lax.broadcasted_iota(jnp.int32, (HG_BLOCK, HG_DK), 0) & (HG_CHUNK - 1)
    ci = lax.broadcasted_iota(jnp.int32, (HG_CPB, HG_CHUNK, HG_CHUNK), 1)
    si = lax.broadcasted_iota(jnp.int32, (HG_CPB, HG_CHUNK, HG_CHUNK), 2)

    def c3(t):
        return t.reshape(HG_CPB, HG_CHUNK, HG_DK)

    def direction(qs3, v3b, z, lb, fwd):
        sig = _sigmoid_tanh(z)
        f = lb + (1.0 - lb) * sig
        k3 = c3((1.0 - lb) * (1.0 - sig))
        lf = jnp.log(f)
        if fwd:
            cum = c3(_chunk_prefix(lf, row))
            ref = cum[:, HG_CHUNK // 2 - 1:HG_CHUNK // 2, :]
            last = cum[:, HG_CHUNK - 1:HG_CHUNK, :]
        else:
            cum = c3(_chunk_suffix(lf, row))
            ref = cum[:, HG_CHUNK // 2:HG_CHUNK // 2 + 1, :]
            last = cum[:, 0:1, :]
        dec = jnp.exp(last)
        if qs3 is None:
            kl = k3 * jnp.exp(last - cum)
        else:
            e1 = jnp.exp(cum - ref)
            qd = qs3 * e1
            qi = qd * jnp.exp(ref)
            kd = k3 * (1.0 / e1)
            kl = kd * jnp.exp(last - ref)
        kvt = jnp.einsum('ncv,nck->nvk', v3b, kl.astype(BF16), preferred_element_type=F32)
        if qs3 is None:
            return None, None, kvt, dec
        sc = jnp.einsum('nck,nsk->ncs', qd.astype(BF16), kd.astype(BF16), preferred_element_type=F32)
        sc = jnp.where((ci >= si) if fwd else (ci <= si), sc, 0.0)
        intra = jnp.einsum('ncs,nsv->ncv', sc.astype(BF16), v3b, preferred_element_type=F32)
        return intra, qi, kvt, dec

    for p, wp_ref in enumerate((wq_ref, wv_ref, wf_ref, wb_ref)):
        wbuf[:, p * HG_DK:(p + 1) * HG_DK] = wp_ref[...].astype(BF16)

    def project(hrows):
        return jnp.dot(hrows, wbuf[...], preferred_element_type=F32)

    def block(p, chunk0, lat_row0):
        v3b = c3(p[:, HG_DK:2 * HG_DK]).astype(BF16)
        zf = p[:, 2 * HG_DK:3 * HG_DK]
        zb = p[:, 3 * HG_DK:4 * HG_DK]
        if lat_row0 is None:
            qs3 = None
        else:
            qs3 = c3(_silu_tanh(p[:, 0:HG_DK]))
        in_f, qi_f, kv_f, dec_f = direction(qs3, v3b, zf, lb_f, True)
        in_b, qi_b, kv_b, dec_b = direction(qs3, v3b, zb, lb_b, False)
        kvf[pl.ds(chunk0, HG_CPB)] = kv_f
        kvb[pl.ds(chunk0, HG_CPB)] = kv_b
        decf[pl.ds(chunk0, HG_CPB)] = dec_f
        decb[pl.ds(chunk0, HG_CPB)] = dec_b
        if lat_row0 is not None:
            rows = pl.ds(lat_row0, HG_BLOCK)
            oacc[rows, :] = (in_f + in_b).reshape(HG_BLOCK, HG_DK)
            qif[rows, :] = qi_f.reshape(HG_BLOCK, HG_DK).astype(BF16)
            qib[rows, :] = qi_b.reshape(HG_BLOCK, HG_DK).astype(BF16)

    n_blocks = n_lat_chunks // HG_CPB

    def lat_rows(i):
        return hl_ref[0, pl.ds(pl.multiple_of(i * HG_BLOCK, HG_BLOCK), HG_BLOCK), :]

    def lat_terms(p_ref, i):
        block(p_ref[...], n_ctx_chunks + i * HG_CPB, pl.multiple_of(i * HG_BLOCK, HG_BLOCK))

    ctx_p = [project(hc_ref[0, i * HG_BLOCK:(i + 1) * HG_BLOCK, :]) for i in range(n_ctx_chunks // HG_CPB)]
    pbuf0[...] = project(lat_rows(0))
    for i, p in enumerate(ctx_p):
        block(p, i * HG_CPB, None)
    pbuf1[...] = project(lat_rows(1))

    def lat_quad(i0, last):
        pbuf2[...] = project(lat_rows(i0 + 2))
        lat_terms(pbuf0, i0)
        pbuf3[...] = project(lat_rows(i0 + 3))
        lat_terms(pbuf1, i0 + 1)
        if not last:
            pbuf0[...] = project(lat_rows(i0 + 4))
        lat_terms(pbuf2, i0 + 2)
        if not last:
            pbuf1[...] = project(lat_rows(i0 + 5))
        lat_terms(pbuf3, i0 + 3)

    def lat_body(t, carry):
        lat_quad(4 * t, False)
        return carry

    lax.fori_loop(0, n_blocks // 4 - 1, lat_body, 0)
    lat_quad(n_blocks - 4, True)

    def advance(s, kv_ref, dec_ref, n):
        return dec_ref[n] * s + kv_ref[n]

    sf = jnp.zeros((HG_DK, HG_DK), F32)
    for n in range(n_ctx_chunks):
        sf = advance(sf, kvf, decf, n)
    sb = jnp.zeros((HG_DK, HG_DK), F32)
    for n in reversed(range(n_ctx_chunks)):
        sb = advance(sb, kvb, decb, n)

    def scan_step(t, carry):
        sf, sb = carry
        jb = n_lat_chunks - 1 - t
        spf[t] = sf.astype(BF16)
        spb[jb] = sb.astype(BF16)
        return (advance(sf, kvf, decf, n_ctx_chunks + t), advance(sb, kvb, decb, n_ctx_chunks + jb))

    lax.fori_loop(0, n_lat_chunks, scan_step, (sf, sb), unroll=2)

    gn = gn_ref[...]

    def block_rows(i):
        return pl.ds(pl.multiple_of(i * HG_BLOCK, HG_BLOCK), HG_BLOCK)

    def inter(i):
        rows = block_rows(i)
        chunks = pl.ds(i * HG_CPB, HG_CPB)
        return (jnp.einsum('nck,nvk->ncv', c3(qif[rows, :]), spf[chunks], preferred_element_type=F32)
                + jnp.einsum('nck,nvk->ncv', c3(qib[rows, :]), spb[chunks], preferred_element_type=F32)
                ).reshape(HG_BLOCK, HG_DK)

    def readout(o_ref, i):
        rows = block_rows(i)
        o = oacc[rows, :] + o_ref[...]
        ms = jnp.mean(o * o, axis=-1, keepdims=True)
        o = o * lax.rsqrt(ms + EPS) * gn
        y_ref[0, rows, :] = (o * _silu_tanh(g_ref[0, rows, :].astype(F32))).astype(BF16)

    obuf0, obuf1 = pbuf0.at[:, 0:HG_DK], pbuf1.at[:, 0:HG_DK]
    obuf0[...] = inter(0)

    def readout_pair(t, carry):
        obuf1[...] = inter(2 * t + 1)
        readout(obuf0, 2 * t)
        obuf0[...] = inter(2 * t + 2)
        readout(obuf1, 2 * t + 1)
        return carry

    lax.fori_loop(0, n_blocks // 2 - 1, readout_pair, 0)
    obuf1[...] = inter(n_blocks - 1)
    readout(obuf0, n_blocks - 2)
    readout(obuf1, n_blocks - 1)


def _hgrn_call(h_ctx, h_lat, g_lat, w_in, lb_fwd, lb_bwd, gnorm, slot):
    b, lc, d = h_ctx.shape
    ll = h_lat.shape[1]
    nrow = lb_fwd.shape[0]
    ncc, nlc = lc // HG_CHUNK, ll // HG_CHUNK
    kern = functools.partial(_hgrn_kernel, slot=slot, n_ctx_chunks=ncc, n_lat_chunks=nlc)
    lb_spec = pl.BlockSpec((nrow, 1, 1, HG_DK), lambda i, h: (0, h, 0, 0))
    head_cols = pl.BlockSpec((1, ll, HG_DK), lambda i, h: (i, 0, h))
    n_rec = N_HG_PROJ - 1
    w_cols = [pl.BlockSpec((None, d, HG_DK), functools.partial(lambda i, h, p: (slot, 0, p * HG_HEADS + h), p=p))
              for p in range(n_rec)]
    return pl.pallas_call(
        kern,
        grid=(b, HG_HEADS),
        in_specs=[
            pl.BlockSpec((1, lc, d), lambda i, h: (i, 0, 0)),
            pl.BlockSpec((1, ll, d), lambda i, h: (i, 0, 0)),
            head_cols,
            *w_cols,
            lb_spec, lb_spec,
            pl.BlockSpec((1, HG_DK), lambda i, h: (0, 0)),
        ],
        out_specs=head_cols,
        out_shape=jax.ShapeDtypeStruct((b, ll, d), BF16),
        scratch_shapes=[
            pltpu.VMEM((d, n_rec * HG_DK), BF16),
            pltpu.VMEM((HG_BLOCK, n_rec * HG_DK), F32),
            pltpu.VMEM((HG_BLOCK, n_rec * HG_DK), F32),
            pltpu.VMEM((HG_BLOCK, n_rec * HG_DK), F32),
            pltpu.VMEM((HG_BLOCK, n_rec * HG_DK), F32),
            pltpu.VMEM((ll, HG_DK), F32),
            pltpu.VMEM((ll, HG_DK), BF16),
            pltpu.VMEM((ll, HG_DK), BF16),
            pltpu.VMEM((ncc + nlc, HG_DK, HG_DK), F32),
            pltpu.VMEM((ncc + nlc, HG_DK, HG_DK), F32),
            pltpu.VMEM((ncc + nlc, 1, HG_DK), F32),
            pltpu.VMEM((ncc + nlc, 1, HG_DK), F32),
            pltpu.VMEM((nlc, HG_DK, HG_DK), BF16),
            pltpu.VMEM((nlc, HG_DK, HG_DK), BF16),
        ],
        compiler_params=_cparams(("parallel", "arbitrary")),
        name="hgrn2",
    )(h_ctx, h_lat, g_lat, *([w_in] * n_rec),
      lb_fwd.reshape(nrow, HG_HEADS, 1, HG_DK), lb_bwd.reshape(nrow, HG_HEADS, 1, HG_DK),
      gnorm.reshape(1, HG_DK))


POOL_STRIP = 8
POOL_PAD = 8
POOL_BAND = 256


def _pool_group(h_ref, win_ref, wg_ref, ps_ref, o_ref, upad, *, win, rows):
    gd = win_ref.shape[1]
    half = win // 2
    u = jnp.dot(h_ref[0], win_ref[...].astype(BF16), preferred_element_type=F32)
    zeros = jnp.zeros((POOL_PAD, GRID_W, gd), F32)
    upad[0:POOL_PAD] = zeros
    upad[POOL_PAD + rows:POOL_PAD + rows + POOL_PAD] = zeros
    upad[POOL_PAD:POOL_PAD + rows] = u.reshape(rows, GRID_W, gd)

    tok = POOL_STRIP * GRID_W
    bi = lax.broadcasted_iota(jnp.int32, (POOL_BAND, POOL_BAND), 0)
    bj = lax.broadcasted_iota(jnp.int32, (POOL_BAND, POOL_BAND), 1)
    lo_c = (bi & (GRID_W - 1)) - half
    cj = bj & (GRID_W - 1)
    band = jnp.where(bi >> GRID_SHIFT == bj >> GRID_SHIFT, 1.0, 0.0)
    band = jnp.where(cj >= lo_c, band, 0.0)
    band = jnp.where(cj < lo_c + win, band, 0.0).astype(BF16)
    t = lax.broadcasted_iota(jnp.int32, (tok, LANES), 0)
    col = t & (GRID_W - 1)
    cnt_c = jnp.minimum(col - half + win, GRID_W) - jnp.maximum(col - half, 0)
    wg = wg_ref[0].astype(BF16)
    ps = ps_ref[...]

    def strip(i, carry):
        r0 = i * POOL_STRIP
        slab = upad[pl.ds(r0 + POOL_PAD - half, POOL_STRIP + win - 1)]
        span = 1
        while span < win:
            n = slab.shape[0] - span
            slab = slab[0:n] + slab[span:span + n]
            span *= 2
        rs = slab.reshape(tok, gd)
        hi = rs.astype(BF16)
        lo = (rs - hi.astype(F32)).astype(BF16)
        parts = []
        for k in range(tok // POOL_BAND):
            sl = slice(k * POOL_BAND, (k + 1) * POOL_BAND)
            parts.append(jnp.dot(band, hi[sl], preferred_element_type=F32)
                         + jnp.dot(band, lo[sl], preferred_element_type=F32))
        box = jnp.concatenate(parts, axis=0)
        r = r0 + (t >> GRID_SHIFT)
        cnt_r = jnp.minimum(r - half + win, rows) - jnp.maximum(r - half, 0)
        inv = 1.0 / (cnt_r * cnt_c).astype(F32)
        mean = box * jnp.concatenate([inv] * (gd // LANES), axis=1)
        ug = upad[pl.ds(r0 + POOL_PAD, POOL_STRIP)].reshape(tok, gd)
        z = jnp.dot((mean - ug).astype(BF16), wg, preferred_element_type=F32) * ps
        o_ref[0, pl.ds(pl.multiple_of(i * tok, tok), tok), :] = z.astype(BF16)
        return carry

    lax.fori_loop(0, rows // POOL_STRIP, strip, 0, unroll=2)


def _pool_kernel(h_ref, win_ref, wg_ref, ps_ref, o_ref, upad, *, rows):
    g = pl.program_id(1)
    for gi, win in enumerate(POOL_WINDOWS):
        @pl.when(g == gi)
        def _(win=win):
            _pool_group(h_ref, win_ref, wg_ref, ps_ref, o_ref, upad, win=win, rows=rows)


def _pool_call(h, w_in, w_grp, p_scale, slot):
    b, l, d = h.shape
    ng = len(POOL_WINDOWS)
    gd = d // ng
    rows = l // GRID_W
    return pl.pallas_call(
        functools.partial(_pool_kernel, rows=rows),
        grid=(b, ng),
        in_specs=[
            pl.BlockSpec((1, l, d), lambda i, g: (i, 0, 0)),
            pl.BlockSpec((None, d, gd), lambda i, g: (slot, 0, g)),
            pl.BlockSpec((None, 1, gd, gd), lambda i, g: (slot, g, 0, 0)),
            pl.BlockSpec((None, 1, gd), lambda i, g: (slot, 0, g)),
        ],
        out_specs=pl.BlockSpec((1, l, gd), lambda i, g: (i, 0, g)),
        out_shape=jax.ShapeDtypeStruct((b, l, d), BF16),
        scratch_shapes=[pltpu.VMEM((rows + 2 * POOL_PAD, GRID_W, gd), F32)],
        compiler_params=_cparams(("parallel", "arbitrary")),
        name="pool_mix",
    )(h, w_in, w_grp, p_scale.reshape(p_scale.shape[0], 1, d))


MOE_TILE = 1024
MOE_WINDOW = 304
MOE_EXTRA = 64
ROW_ALIGN = 16
DEST_LANE = N_EXPERTS
PIECE_STRIDE = 32
N_PIECES = 3
AUX_ROWS = 8


def _route(sel, s):
    keep = []
    gsum = []
    for g in range(N_EXPERT_GROUPS):
        a = sel[g * EXPERTS_PER_GROUP:(g + 1) * EXPERTS_PER_GROUP]
        beaten = [jnp.zeros_like(a[0]) for _ in a]
        for i in range(EXPERTS_PER_GROUP):
            for j in range(i + 1, EXPERTS_PER_GROUP):
                ge = jnp.where(a[i] >= a[j], 1.0, 0.0)
                beaten[j] = beaten[j] + ge
                beaten[i] = beaten[i] + (1.0 - ge)
        kg = [jnp.where(bt < 1.5, 1.0, 0.0) for bt in beaten]
        keep.append(kg)
        gsum.append(sum(k * x for k, x in zip(kg, a)))
    picked = []
    bests = []
    for g in range(N_EXPERT_GROUPS):
        better = jnp.zeros_like(gsum[0])
        for o in range(N_EXPERT_GROUPS):
            if o < g:
                better = better + jnp.where(gsum[o] >= gsum[g], 1.0, 0.0)
            elif o > g:
                better = better + jnp.where(gsum[o] > gsum[g], 1.0, 0.0)
        best = jnp.where(better < 0.5, 1.0, 0.0)
        bests.append(best)
        for i in range(EXPERTS_PER_GROUP):
            picked.append(best * keep[g][i] * s[g * EXPERTS_PER_GROUP + i])
    den = sum(picked)
    return [p / den for p in picked], bests


POST_PIECES = 4


def _post_kernel(x_ref, y_ref, w_ref, mod_ref, g_ref, rw_ref, rb_ref,
                 xo_ref, h2_ref, rt_ref, aux_ref, wbuf, yw0, yw1, *, n_tiles):
    s = pl.program_id(0)
    t, d = x_ref.shape
    cols = d // POST_PIECES
    rows = t // POST_PIECES

    def project(dst, k):
        c = slice(k * cols, (k + 1) * cols)
        dst[:, c] = jnp.dot(y_ref[...], wbuf[:, c], preferred_element_type=F32)

    def rows_part(src, k):
        r = slice(k * rows, (k + 1) * rows)
        xn = x_ref[r, :] + mod_ref[0, 2:3, :] * src[r, :]
        xo_ref[r, :] = xn
        h2 = _modulate(xn, g_ref[...], mod_ref[0, 3:4, :], mod_ref[0, 4:5, :])
        hi, lo = _split_bf16(h2)
        h2_ref[r, :] = hi
        rw = rw_ref[...]
        return jnp.dot(hi, rw, preferred_element_type=F32) + jnp.dot(lo, rw, preferred_element_type=F32)

    route = functools.partial(_post_route, rb_ref, rt_ref, aux_ref)

    @pl.when(s == 0)
    def _():
        wbuf[...] = w_ref[...].astype(BF16)
        for k in range(POST_PIECES):
            project(yw0, k)

    for parity, (cur, prev) in enumerate(((yw0, yw1), (yw1, yw0))):
        @pl.when((s > 0) & (s < n_tiles) & (s % 2 == parity))
        def _(cur=cur, prev=prev):
            prods = []
            for k in range(POST_PIECES):
                project(cur, k)
                prods.append(rows_part(prev, k))
            route(prods)

    @pl.when(s == n_tiles)
    def _():
        last = (yw0, yw1)[(n_tiles - 1) % 2]
        route([rows_part(last, k) for k in range(POST_PIECES)])


def _post_route(rb_ref, rt_ref, aux_ref, prods):
    prod = jnp.concatenate(prods, axis=0).T
    logits = prod[0:N_EXPERTS, :] + prod[N_EXPERTS:2 * N_EXPERTS, :]
    s = _sigmoid_pair(logits)[0]
    sel = s + rb_ref[...]
    comb, bests = _route([sel[e:e + 1, :] for e in range(N_EXPERTS)],
                         [s[e:e + 1, :] for e in range(N_EXPERTS)])
    t = logits.shape[1]

    ind = jnp.concatenate(bests + [jnp.zeros((AUX_ROWS - N_EXPERT_GROUPS, t), F32)], axis=0)
    n_seg = t // LANES
    stacked = jnp.concatenate([ind[:, j * LANES:(j + 1) * LANES] for j in range(n_seg)], axis=0)
    upper = jnp.where(lax.broadcasted_iota(jnp.int32, (LANES, LANES), 0)
                      <= lax.broadcasted_iota(jnp.int32, (LANES, LANES), 1), 1.0, 0.0).astype(BF16)
    local = jnp.dot(stacked.astype(BF16), upper, preferred_element_type=F32)
    off = jnp.zeros((AUX_ROWS, 1), F32)
    pieces = []
    for j in range(n_seg):
        seg_cum = local[j * AUX_ROWS:(j + 1) * AUX_ROWS, :]
        pieces.append(seg_cum + off)
        off = off + seg_cum[:, LANES - 1:LANES]
    cum = jnp.concatenate(pieces, axis=1)
    counts = [off[g:g + 1, :] for g in range(N_EXPERT_GROUPS)]
    starts = [jnp.zeros((1, 1), F32)]
    for g in range(1, N_EXPERT_GROUPS):
        starts.append(starts[-1] + counts[g - 1])
    dest = sum(bests[g] * (starts[g] + cum[g:g + 1, :] - 1.0) for g in range(N_EXPERT_GROUPS))

    lane = lax.broadcasted_iota(jnp.int32, (1, t), 1)
    seg = sum(jnp.where(lane == k, v, 0.0) for k, v in enumerate(starts + counts))
    aux_ref[0] = jnp.concatenate([dest, seg, jnp.zeros((AUX_ROWS - 2, t), F32)], axis=0)
    p1 = [w.astype(BF16).astype(F32) for w in comb]
    r1 = [w - p for w, p in zip(comb, p1)]
    p2 = [r.astype(BF16).astype(F32) for r in r1]
    p3 = [r - p for r, p in zip(r1, p2)]
    pad = [jnp.zeros((PIECE_STRIDE - N_EXPERTS, t), F32)]
    table = jnp.concatenate(p1 + [dest] + [jnp.zeros((PIECE_STRIDE - N_EXPERTS - 1, t), F32)] + p2 + pad + p3 + pad
                            + [jnp.zeros((LANES - N_PIECES * PIECE_STRIDE, t), F32)], axis=0)
    rt_ref[...] = table.T


def _post_call(x, y, w, w_slot, mod, g, router_w, router_b, tokens_per_batch, tile=MOE_TILE):
    n, d = x.shape
    per_b = tokens_per_batch // tile
    rwh = router_w.astype(BF16)
    rwl = (router_w - rwh.astype(F32)).astype(BF16)
    rw = jnp.concatenate([rwh, rwl, jnp.zeros((d, LANES - 2 * N_EXPERTS), BF16)], axis=1)
    n_tiles = n // tile
    done = lambda s: jnp.maximum(s - 1, 0)
    row = lambda s: (done(s), 0)
    fixed = lambda s: (0, 0)
    return pl.pallas_call(
        functools.partial(_post_kernel, n_tiles=n_tiles),
        grid=(n_tiles + 1,),
        in_specs=[
            pl.BlockSpec((tile, d), row),
            pl.BlockSpec((tile, d), lambda s: (jnp.minimum(s, n_tiles - 1), 0)),
            pl.BlockSpec((None, d, d), lambda s: (w_slot, 0, 0)),
            pl.BlockSpec((1, N_MOD, d), lambda s: (done(s) // per_b, 0, 0)),
            pl.BlockSpec((1, d), fixed),
            pl.BlockSpec((d, LANES), fixed),
            pl.BlockSpec((N_EXPERTS, 1), fixed),
        ],
        out_specs=[
            pl.BlockSpec((tile, d), row),
            pl.BlockSpec((tile, d), row),
            pl.BlockSpec((tile, LANES), row),
            pl.BlockSpec((1, AUX_ROWS, tile), lambda s: (done(s), 0, 0)),
        ],
        out_shape=[
            jax.ShapeDtypeStruct((n, d), F32),
            jax.ShapeDtypeStruct((n, d), BF16),
            jax.ShapeDtypeStruct((n, LANES), F32),
            jax.ShapeDtypeStruct((n_tiles, AUX_ROWS, tile), F32),
        ],
        scratch_shapes=[
            pltpu.VMEM((d, d), BF16),
            pltpu.VMEM((tile, d), F32),
            pltpu.VMEM((tile, d), F32),
        ],
        compiler_params=_cparams(("arbitrary",)),
        name="mixer_out_router",
    )(x, y, w, mod, g.reshape(1, d), rw, router_b.reshape(N_EXPERTS, 1))


def _moe_kernel(seg_ref, h_ref, rt_ref, aux_ref, wg_ref, wu_ref, wd_ref, x_ref, mod_ref, ng_ref, nmod_ref,
                o_ref, *rest, final_norm):
    if final_norm:
        xs, cws, ys = rest
    else:
        hn_ref, xs, cws, ys = rest
    i = pl.program_id(0)
    g = pl.program_id(1)
    t = h_ref.shape[0]

    @pl.when(g == 0)
    def _():
        dest_row = aux_ref[0, 0:1, :]
        perm = jnp.where(lax.broadcasted_iota(jnp.int32, (t, t), 0).astype(F32) == dest_row,
                         1.0, 0.0).astype(BF16)
        xs[...] = jnp.dot(perm, h_ref[...], preferred_element_type=F32).astype(BF16)
        cws[...] = jnp.dot(perm, rt_ref[...].astype(BF16), preferred_element_type=F32)
        ys[...] = jnp.zeros_like(ys)

    start = seg_ref[i, g]
    end = start + seg_ref[i, N_EXPERT_GROUPS + g]
    wd = wd_ref[...].reshape(EXPERTS_PER_GROUP * D_EXPERT, wd_ref.shape[2])

    def window(w0, rows, lo, hi):
        sl = pl.ds(pl.multiple_of(w0, ROW_ALIGN), rows)
        xw = xs[sl, :]
        cw = cws[sl, :]
        r = w0 + lax.broadcasted_iota(jnp.int32, cw.shape, 0)
        lane = lax.broadcasted_iota(jnp.int32, cw.shape, 1)
        piece_lane = jnp.where(lane < N_PIECES * PIECE_STRIDE, lane & (PIECE_STRIDE - 1), -1)
        cw = jnp.where(r >= lo, cw, 0.0)
        cw = jnp.where(r < hi, cw, 0.0)
        acts = []
        for e in range(EXPERTS_PER_GROUP):
            ce = jnp.sum(jnp.where(piece_lane == g * EXPERTS_PER_GROUP + e, cw, 0.0), axis=1, keepdims=True)
            a = _silu_tanh(jnp.dot(xw, wg_ref[e], preferred_element_type=F32)) \
                * jnp.dot(xw, wu_ref[e], preferred_element_type=F32)
            acts.append((a * ce).astype(BF16))
        ys[sl, :] += jnp.dot(jnp.concatenate(acts, axis=1), wd, preferred_element_type=F32)

    w0 = jnp.minimum(start & -ROW_ALIGN, t - MOE_WINDOW)
    covered = w0 + MOE_WINDOW
    window(w0, MOE_WINDOW, start, jnp.minimum(end, covered))

    def extra(k, carry):
        lo = covered + k * MOE_EXTRA
        window(jnp.minimum(lo, t - MOE_EXTRA), MOE_EXTRA, lo, jnp.minimum(end, lo + MOE_EXTRA))
        return carry

    n_extra = jnp.maximum(end - covered + MOE_EXTRA - 1, 0) >> (MOE_EXTRA.bit_length() - 1)
    lax.fori_loop(0, n_extra, extra, 0)

    @pl.when(g == N_EXPERT_GROUPS - 1)
    def _():
        dest_col = rt_ref[:, DEST_LANE:DEST_LANE + 1]
        unperm = jnp.where(lax.broadcasted_iota(jnp.int32, (t, t), 1).astype(F32) == dest_col,
                           1.0, 0.0).astype(BF16)
        ff = jnp.dot(unperm, ys[...].astype(BF16), preferred_element_type=F32)
        xn = x_ref[...] + mod_ref[0, 5:6, :] * ff
        if final_norm:
            ms = jnp.mean(xn * xn, axis=-1, keepdims=True)
            xn = xn * lax.rsqrt(ms + EPS) * ng_ref[...]
        else:
            hn_ref[...] = _modulate(xn, ng_ref[...], nmod_ref[0, 0:1, :], nmod_ref[0, 1:2, :]).astype(BF16)
        o_ref[...] = xn


def _moe_call(h2, table, aux, wg, wu, wd, layer, x, mod, norm_g, next_mod, tokens_per_batch, final_norm):
    n, d = x.shape
    tile = MOE_TILE
    per_b = tokens_per_batch // tile
    seg = aux[:, 1, :2 * N_EXPERT_GROUPS].astype(jnp.int32)
    row = lambda i, g, seg: (i, 0)
    grp = lambda i, g, seg: (layer, g, 0, 0)
    tok = pl.BlockSpec((tile, d), row)
    x_out = jax.ShapeDtypeStruct((n, d), F32)
    grid_spec = pltpu.PrefetchScalarGridSpec(
        num_scalar_prefetch=1,
        grid=(n // tile, N_EXPERT_GROUPS),
        in_specs=[
            pl.BlockSpec((tile, d), row),
            pl.BlockSpec((tile, LANES), row),
            pl.BlockSpec((1, AUX_ROWS, tile), lambda i, g, seg: (i, 0, 0)),
            pl.BlockSpec((None, EXPERTS_PER_GROUP, d, D_EXPERT), grp),
            pl.BlockSpec((None, EXPERTS_PER_GROUP, d, D_EXPERT), grp),
            pl.BlockSpec((None, EXPERTS_PER_GROUP, D_EXPERT, d), grp),
            pl.BlockSpec((tile, d), row),
            pl.BlockSpec((1, N_MOD, d), lambda i, g, seg: (i // per_b, 0, 0)),
            pl.BlockSpec((1, d), lambda i, g, seg: (0, 0)),
            pl.BlockSpec((1, N_MOD, d), lambda i, g, seg: (i // per_b, 0, 0)),
        ],
        out_specs=tok if final_norm else [tok, tok],
        scratch_shapes=[
            pltpu.VMEM((tile, d), BF16),
            pltpu.VMEM((tile, LANES), F32),
            pltpu.VMEM((tile, d), F32),
        ],
    )
    return pl.pallas_call(
        functools.partial(_moe_kernel, final_norm=final_norm),
        grid_spec=grid_spec,
        out_shape=x_out if final_norm else [x_out, jax.ShapeDtypeStruct((n, d), BF16)],
        compiler_params=_cparams(("parallel", "arbitrary")),
        name="moe_ffn",
    )(seg, h2, table, aux, wg, wu, wd, x, mod, norm_g.reshape(1, d), next_mod)


def kernel(x, c, ctx, c_ctx, w_mod, b_mod, norm1_g, norm2_g, hg_w_in, hg_lb_fwd, hg_lb_bwd, hg_gnorm,
           hg_w_out, pool_w_in, pool_w_grp, pool_scale, pool_w_out, router_w, router_b, moe_w_gate,
           moe_w_up, moe_w_down, final_g):
    b, l, d = x.shape
    depth = w_mod.shape[0]
    n_mixers = 2

    cc = jnp.concatenate([c, c_ctx[None, :], jnp.zeros((MOD_ROWS - b - 1, d), F32)], axis=0)
    mods = _mod_call(cc, w_mod, b_mod)

    experts = (moe_w_gate.astype(BF16), moe_w_up.astype(BF16), moe_w_down.astype(BF16))
    x_lat = x.reshape(b * l, d)
    h_lat = None
    for i in range(depth):
        slot = i // n_mixers
        mod_lat = mods[i, :b].reshape(b, N_MOD, d)
        if i % n_mixers == 0:
            h_lat, g_lat = _norm_call(x_lat.reshape(b, l, d), norm1_g[i], mod_lat, 512,
                                      hg_w_in, slot, N_HG_PROJ - 1)
            mod_ctx = jnp.broadcast_to(mods[i, b].reshape(1, N_MOD, d), (b, N_MOD, d))
            h_ctx = _norm_call(ctx, norm1_g[i], mod_ctx, ctx.shape[1])
            y = _hgrn_call(h_ctx, h_lat, g_lat, hg_w_in, hg_lb_fwd, hg_lb_bwd, hg_gnorm[slot], slot)
            w_out = hg_w_out
        else:
            if h_lat is None:
                h_lat = _norm_call(x_lat.reshape(b, l, d), norm1_g[i], mod_lat, 512)
            y = _pool_call(h_lat, pool_w_in, pool_w_grp, pool_scale, slot)
            w_out = pool_w_out
        x_lat, h2, table, aux = _post_call(x_lat, y.reshape(b * l, d), w_out, slot, mod_lat,
                                           norm2_g[i], router_w, router_b, l)
        if i == depth - 1:
            x_lat = _moe_call(h2, table, aux, *experts, i, x_lat, mod_lat, final_g, mod_lat, l, final_norm=True)
        else:
            mod_next = mods[i + 1, :b].reshape(b, N_MOD, d)
            x_lat, h_next = _moe_call(h2, table, aux, *experts, i, x_lat, mod_lat, norm1_g[i + 1], mod_next, l,
                                      final_norm=False)
            h_lat = h_next.reshape(b, l, d) if (i + 1) % n_mixers != 0 else None
    return x_lat.reshape(b, l, d)
```

```python
import functools

import jax
import jax.numpy as jnp
from jax import lax
from jax.experimental import pallas as pl
from jax.experimental.pallas import tpu as pltpu

F32 = jnp.float32
BF16 = jnp.bfloat16

EPS = 1e-6
N_MOD = 6
HG_HEADS = 8
HG_DK = 128
HG_CHUNK = 64
N_HG_PROJ = 5
POOL_WINDOWS = (2, 4, 8, 16)
GRID_W = 64
GRID_SHIFT = GRID_W.bit_length() - 1
assert 1 << GRID_SHIFT == GRID_W
N_EXPERTS = 16
N_EXPERT_GROUPS = 4
EXPERTS_PER_GROUP = N_EXPERTS // N_EXPERT_GROUPS
D_EXPERT = 256

LANES = 128
MOD_ROWS = 8
VMEM_LIMIT = 56 * 1024 * 1024


def _cparams(sem, vmem=VMEM_LIMIT):
    return pltpu.CompilerParams(dimension_semantics=sem, vmem_limit_bytes=vmem)


def _sigmoid_pair(z):
    e = jnp.exp(-jnp.abs(z))
    r = 1.0 / (1.0 + e)
    er = e * r
    pos = z >= 0
    return jnp.where(pos, r, er), jnp.where(pos, er, r)


def _silu(z):
    return z * _sigmoid_pair(z)[0]


def _sigmoid_tanh(z):
    return 0.5 * jnp.tanh(0.5 * z) + 0.5


def _silu_tanh(z):
    return z * _sigmoid_tanh(z)


def _split_bf16(x):
    hi = x.astype(BF16)
    return hi, (x - hi.astype(F32)).astype(BF16)


def _mod_kernel(c_ref, w_ref, b_ref, o_ref):
    a = jnp.concatenate(_split_bf16(_silu(c_ref[...])), axis=0)
    w_hi, w_lo = _split_bf16(w_ref[0])
    acc = jnp.dot(a, w_hi, preferred_element_type=F32) + jnp.dot(a, w_lo, preferred_element_type=F32)
    o_ref[0] = acc[0:MOD_ROWS] + acc[MOD_ROWS:2 * MOD_ROWS] + b_ref[0]


def _mod_call(cc, w_mod, b_mod):
    depth, d, n = w_mod.shape
    tn = 1024
    return pl.pallas_call(
        _mod_kernel,
        grid=(depth, n // tn),
        in_specs=[
            pl.BlockSpec((MOD_ROWS, d), lambda i, j: (0, 0)),
            pl.BlockSpec((1, d, tn), lambda i, j: (i, 0, j)),
            pl.BlockSpec((1, 1, tn), lambda i, j: (i, 0, j)),
        ],
        out_specs=pl.BlockSpec((1, MOD_ROWS, tn), lambda i, j: (i, 0, j)),
        out_shape=jax.ShapeDtypeStruct((depth, MOD_ROWS, n), F32),
        compiler_params=_cparams(("parallel", "parallel")),
        name="mod_proj",
    )(cc, w_mod, b_mod.reshape(depth, 1, n))


def _modulate(x, g, shift, scale):
    ms = jnp.mean(x * x, axis=-1, keepdims=True)
    return (x * lax.rsqrt(ms + EPS) * g) * (1.0 + scale) + shift


def _norm_kernel(x_ref, g_ref, mod_ref, *rest):
    h = _modulate(x_ref[0], g_ref[...], mod_ref[0, 0:1, :], mod_ref[0, 1:2, :]).astype(BF16)
    if len(rest) == 1:
        (o_ref,) = rest
    else:
        w_ref, o_ref, p_ref, wbuf = rest

        @pl.when((pl.program_id(0) == 0) & (pl.program_id(1) == 0))
        def _():
            wbuf[...] = w_ref[...].astype(BF16)

        p_ref[0] = jnp.dot(h, wbuf[...], preferred_element_type=F32).astype(BF16)
    o_ref[0] = h


def _norm_call(x, g, mod, tile, w_proj=None, w_slot=0, w_col=0):
    b, t, d = x.shape
    tok = pl.BlockSpec((1, tile, d), lambda i, j: (i, j, 0))
    in_specs = [tok, pl.BlockSpec((1, d), lambda i, j: (0, 0)),
                pl.BlockSpec((1, N_MOD, d), lambda i, j: (i, 0, 0))]
    args = [x, g.reshape(1, d), mod]
    out_specs, out_shape = tok, jax.ShapeDtypeStruct((b, t, d), BF16)
    scratch = []
    if w_proj is not None:
        in_specs.append(pl.BlockSpec((None, d, d), lambda i, j: (w_slot, 0, w_col)))
        args.append(w_proj)
        out_specs, out_shape = [tok, tok], [out_shape, out_shape]
        scratch = [pltpu.VMEM((d, d), BF16)]
    return pl.pallas_call(
        _norm_kernel,
        grid=(b, t // tile),
        in_specs=in_specs,
        out_specs=out_specs,
        out_shape=out_shape,
        scratch_shapes=scratch,
        compiler_params=_cparams(("arbitrary", "arbitrary")),
        name="norm1",
    )(*args)


HG_BLOCK = 256
HG_CPB = HG_BLOCK // HG_CHUNK


def _chunk_prefix(x, row):
    for d in (1, 2, 4, 8, 16, 32):
        x = x + jnp.where(row >= d, pltpu.roll(x, d, axis=0), 0.0)
    return x


def _chunk_suffix(x, row):
    n = x.shape[0]
    for d in (1, 2, 4, 8, 16, 32):
        x = x + jnp.where(row < HG_CHUNK - d, pltpu.roll(x, n - d, axis=0), 0.0)
    return x


def _lower_bound(lb_ref, slot):
    rows = [lb_ref[j, 0] for j in range(lb_ref.shape[0])]
    m = functools.reduce(jnp.maximum, rows)
    es = [jnp.exp(r - m) for r in rows]
    return sum(es[:slot + 1]) / sum(es)


def _hgrn_kernel(hc_ref, hl_ref, g_ref, wq_ref, wv_ref, wf_ref, wb_ref, lbf_ref, lbb_ref, gn_ref, y_ref,
                 wbuf, pbuf0, pbuf1, pbuf2, pbuf3, oacc, qif, qib, kvf, kvb, decf, decb, spf, spb,
                 *, slot, n_ctx_chunks, n_lat_chunks):
    lb_f = _lower_bound(lbf_ref, slot)
    lb_b = _lower_bound(lbb_ref, slot)
    row = lax.broadcasted_iota(jnp.int32, (HG_BLOCK, HG_DK), 0) & (HG_CHUNK - 1)
    ci = lax.broadcasted_iota(jnp.int32, (HG_CPB, HG_CHUNK, HG_CHUNK), 1)
    si = lax.broadcasted_iota(jnp.int32, (HG_CPB, HG_CHUNK, HG_CHUNK), 2)

    def c3(t):
        return t.reshape(HG_CPB, HG_CHUNK, HG_DK)

    def direction(qs3, v3b, z, lb, fwd):
        sig = _sigmoid_tanh(z)
        f = lb + (1.0 - lb) * sig
        k3 = c3((1.0 - lb) * (1.0 - sig))
        lf = jnp.log(f)
        if fwd:
            cum = c3(_chunk_prefix(lf, row))
            ref = cum[:, HG_CHUNK // 2 - 1:HG_CHUNK // 2, :]
            last = cum[:, HG_CHUNK - 1:HG_CHUNK, :]
        else:
            cum = c3(_chunk_suffix(lf, row))
            ref = cum[:, HG_CHUNK // 2:HG_CHUNK // 2 + 1, :]
            last = cum[:, 0:1, :]
        dec = jnp.exp(last)
        if qs3 is None:
            kl = k3 * jnp.exp(last - cum)
        else:
            e1 = jnp.exp(cum - ref)
            qd = qs3 * e1
            qi = qd * jnp.exp(ref)
            kd = k3 * (1.0 / e1)
            kl = kd * jnp.exp(last - ref)
        kvt = jnp.einsum('ncv,nck->nvk', v3b, kl.astype(BF16), preferred_element_type=F32)
        if qs3 is None:
            return None, None, kvt, dec
        sc = jnp.einsum('nck,nsk->ncs', qd.astype(BF16), kd.astype(BF16), preferred_element_type=F32)
        sc = jnp.where((ci >= si) if fwd else (ci <= si), sc, 0.0)
        intra = jnp.einsum('ncs,nsv->ncv', sc.astype(BF16), v3b, preferred_element_type=F32)
        return intra, qi, kvt, dec

    for p, wp_ref in enumerate((wq_ref, wv_ref, wf_ref, wb_ref)):
        wbuf[:, p * HG_DK:(p + 1) * HG_DK] = wp_ref[...].astype(BF16)

    def project(hrows):
        return jnp.dot(hrows, wbuf[...], preferred_element_type=F32)

    def block(p, chunk0, lat_row0):
        v3b = c3(p[:, HG_DK:2 * HG_DK]).astype(BF16)
        zf = p[:, 2 * HG_DK:3 * HG_DK]
        zb = p[:, 3 * HG_DK:4 * HG_DK]
        if lat_row0 is None:
            qs3 = None
        else:
            qs3 = c3(_silu_tanh(p[:, 0:HG_DK]))
        in_f, qi_f, kv_f, dec_f = direction(qs3, v3b, zf, lb_f, True)
        in_b, qi_b, kv_b, dec_b = direction(qs3, v3b, zb, lb_b, False)
        kvf[pl.ds(chunk0, HG_CPB)] = kv_f
        kvb[pl.ds(chunk0, HG_CPB)] = kv_b
        decf[pl.ds(chunk0, HG_CPB)] = dec_f
        decb[pl.ds(chunk0, HG_CPB)] = dec_b
        if lat_row0 is not None:
            rows = pl.ds(lat_row0, HG_BLOCK)
            oacc[rows, :] = (in_f + in_b).reshape(HG_BLOCK, HG_DK)
            qif[rows, :] = qi_f.reshape(HG_BLOCK, HG_DK).astype(BF16)
            qib[rows, :] = qi_b.reshape(HG_BLOCK, HG_DK).astype(BF16)

    n_blocks = n_lat_chunks // HG_CPB

    def lat_rows(i):
        return hl_ref[0, pl.ds(pl.multiple_of(i * HG_BLOCK, HG_BLOCK), HG_BLOCK), :]

    def lat_terms(p_ref, i):
        block(p_ref[...], n_ctx_chunks + i * HG_CPB, pl.multiple_of(i * HG_BLOCK, HG_BLOCK))

    ctx_p = [project(hc_ref[0, i * HG_BLOCK:(i + 1) * HG_BLOCK, :]) for i in range(n_ctx_chunks // HG_CPB)]
    pbuf0[...] = project(lat_rows(0))
    for i, p in enumerate(ctx_p):
        block(p, i * HG_CPB, None)
    pbuf1[...] = project(lat_rows(1))

    def lat_quad(i0, last):
        pbuf2[...] = project(lat_rows(i0 + 2))
        lat_terms(pbuf0, i0)
        pbuf3[...] = project(lat_rows(i0 + 3))
        lat_terms(pbuf1, i0 + 1)
        if not last:
            pbuf0[...] = project(lat_rows(i0 + 4))
        lat_terms(pbuf2, i0 + 2)
        if not last:
            pbuf1[...] = project(lat_rows(i0 + 5))
        lat_terms(pbuf3, i0 + 3)

    def lat_body(t, carry):
        lat_quad(4 * t, False)
        return carry

    lax.fori_loop(0, n_blocks // 4 - 1, lat_body, 0)
    lat_quad(n_blocks - 4, True)

    def advance(s, kv_ref, dec_ref, n):
        return dec_ref[n] * s + kv_ref[n]

    sf = jnp.zeros((HG_DK, HG_DK), F32)
    for n in range(n_ctx_chunks):
        sf = advance(sf, kvf, decf, n)
    sb = jnp.zeros((HG_DK, HG_DK), F32)
    for n in reversed(range(n_ctx_chunks)):
        sb = advance(sb, kvb, decb, n)

    def scan_step(t, carry):
        sf, sb = carry
        jb = n_lat_chunks - 1 - t
        spf[t] = sf.astype(BF16)
        spb[jb] = sb.astype(BF16)
        return (advance(sf, kvf, decf, n_ctx_chunks + t), advance(sb, kvb, decb, n_ctx_chunks + jb))

    lax.fori_loop(0, n_lat_chunks, scan_step, (sf, sb), unroll=2)

    gn = gn_ref[...]

    def block_rows(i):
        return pl.ds(pl.multiple_of(i * HG_BLOCK, HG_BLOCK), HG_BLOCK)

    def inter(i):
        rows = block_rows(i)
        chunks = pl.ds(i * HG_CPB, HG_CPB)
        return (jnp.einsum('nck,nvk->ncv', c3(qif[rows, :]), spf[chunks], preferred_element_type=F32)
                + jnp.einsum('nck,nvk->ncv', c3(qib[rows, :]), spb[chunks], preferred_element_type=F32)
                ).reshape(HG_BLOCK, HG_DK)

    def readout(o_ref, i):
        rows = block_rows(i)
        o = oacc[rows, :] + o_ref[...]
        ms = jnp.mean(o * o, axis=-1, keepdims=True)
        o = o * lax.rsqrt(ms + EPS) * gn
        y_ref[0, rows, :] = (o * _silu_tanh(g_ref[0, rows, :].astype(F32))).astype(BF16)

    obuf0, obuf1 = pbuf0.at[:, 0:HG_DK], pbuf1.at[:, 0:HG_DK]
    obuf0[...] = inter(0)

    def readout_pair(t, carry):
        obuf1[...] = inter(2 * t + 1)
        readout(obuf0, 2 * t)
        obuf0[...] = inter(2 * t + 2)
        readout(obuf1, 2 * t + 1)
        return carry

    lax.fori_loop(0, n_blocks // 2 - 1, readout_pair, 0)
    obuf1[...] = inter(n_blocks - 1)
    readout(obuf0, n_blocks - 2)
    readout(obuf1, n_blocks - 1)


def _hgrn_call(h_ctx, h_lat, g_lat, w_in, lb_fwd, lb_bwd, gnorm, slot):
    b, lc, d = h_ctx.shape
    ll = h_lat.shape[1]
    nrow = lb_fwd.shape[0]
    ncc, nlc = lc // HG_CHUNK, ll // HG_CHUNK
    kern = functools.partial(_hgrn_kernel, slot=slot, n_ctx_chunks=ncc, n_lat_chunks=nlc)
    lb_spec = pl.BlockSpec((nrow, 1, 1, HG_DK), lambda i, h: (0, h, 0, 0))
    head_cols = pl.BlockSpec((1, ll, HG_DK), lambda i, h: (i, 0, h))
    n_rec = N_HG_PROJ - 1
    w_cols = [pl.BlockSpec((None, d, HG_DK), functools.partial(lambda i, h, p: (slot, 0, p * HG_HEADS + h), p=p))
              for p in range(n_rec)]
    return pl.pallas_call(
        kern,
        grid=(b, HG_HEADS),
        in_specs=[
            pl.BlockSpec((1, lc, d), lambda i, h: (i, 0, 0)),
            pl.BlockSpec((1, ll, d), lambda i, h: (i, 0, 0)),
            head_cols,
            *w_cols,
            lb_spec, lb_spec,
            pl.BlockSpec((1, HG_DK), lambda i, h: (0, 0)),
        ],
        out_specs=head_cols,
        out_shape=jax.ShapeDtypeStruct((b, ll, d), BF16),
        scratch_shapes=[
            pltpu.VMEM((d, n_rec * HG_DK), BF16),
            pltpu.VMEM((HG_BLOCK, n_rec * HG_DK), F32),
            pltpu.VMEM((HG_BLOCK, n_rec * HG_DK), F32),
            pltpu.VMEM((HG_BLOCK, n_rec * HG_DK), F32),
            pltpu.VMEM((HG_BLOCK, n_rec * HG_DK), F32),
            pltpu.VMEM((ll, HG_DK), F32),
            pltpu.VMEM((ll, HG_DK), BF16),
            pltpu.VMEM((ll, HG_DK), BF16),
            pltpu.VMEM((ncc + nlc, HG_DK, HG_DK), F32),
            pltpu.VMEM((ncc + nlc, HG_DK, HG_DK), F32),
            pltpu.VMEM((ncc + nlc, 1, HG_DK), F32),
            pltpu.VMEM((ncc + nlc, 1, HG_DK), F32),
            pltpu.VMEM((nlc, HG_DK, HG_DK), BF16),
            pltpu.VMEM((nlc, HG_DK, HG_DK), BF16),
        ],
        compiler_params=_cparams(("parallel", "arbitrary")),
        name="hgrn2",
    )(h_ctx, h_lat, g_lat, *([w_in] * n_rec),
      lb_fwd.reshape(nrow, HG_HEADS, 1, HG_DK), lb_bwd.reshape(nrow, HG_HEADS, 1, HG_DK),
      gnorm.reshape(1, HG_DK))


POOL_STRIP = 8
POOL_PAD = 8
POOL_BAND = 256


def _pool_group(h_ref, win_ref, wg_ref, ps_ref, o_ref, upad, *, win, rows):
    gd = win_ref.shape[1]
    half = win // 2
    u = jnp.dot(h_ref[0], win_ref[...].astype(BF16), preferred_element_type=F32)
    zeros = jnp.zeros((POOL_PAD, GRID_W, gd), F32)
    upad[0:POOL_PAD] = zeros
    upad[POOL_PAD + rows:POOL_PAD + rows + POOL_PAD] = zeros
    upad[POOL_PAD:POOL_PAD + rows] = u.reshape(rows, GRID_W, gd)

    tok = POOL_STRIP * GRID_W
    bi = lax.broadcasted_iota(jnp.int32, (POOL_BAND, POOL_BAND), 0)
    bj = lax.broadcasted_iota(jnp.int32, (POOL_BAND, POOL_BAND), 1)
    lo_c = (bi & (GRID_W - 1)) - half
    cj = bj & (GRID_W - 1)
    band = jnp.where(bi >> GRID_SHIFT == bj >> GRID_SHIFT, 1.0, 0.0)
    band = jnp.where(cj >= lo_c, band, 0.0)
    band = jnp.where(cj < lo_c + win, band, 0.0).astype(BF16)
    t = lax.broadcasted_iota(jnp.int32, (tok, LANES), 0)
    col = t & (GRID_W - 1)
    cnt_c = jnp.minimum(col - half + win, GRID_W) - jnp.maximum(col - half, 0)
    wg = wg_ref[0].astype(BF16)
    ps = ps_ref[...]

    def strip(i, carry):
        r0 = i * POOL_STRIP
        slab = upad[pl.ds(r0 + POOL_PAD - half, POOL_STRIP + win - 1)]
        span = 1
        while span < win:
            n = slab.shape[0] - span
            slab = slab[0:n] + slab[span:span + n]
            span *= 2
        rs = slab.reshape(tok, gd)
        hi = rs.astype(BF16)
        lo = (rs - hi.astype(F32)).astype(BF16)
        parts = []
        for k in range(tok // POOL_BAND):
            sl = slice(k * POOL_BAND, (k + 1) * POOL_BAND)
            parts.append(jnp.dot(band, hi[sl], preferred_element_type=F32)
                         + jnp.dot(band, lo[sl], preferred_element_type=F32))
        box = jnp.concatenate(parts, axis=0)
        r = r0 + (t >> GRID_SHIFT)
        cnt_r = jnp.minimum(r - half + win, rows) - jnp.maximum(r - half, 0)
        inv = 1.0 / (cnt_r * cnt_c).astype(F32)
        mean = box * jnp.concatenate([inv] * (gd // LANES), axis=1)
        ug = upad[pl.ds(r0 + POOL_PAD, POOL_STRIP)].reshape(tok, gd)
        z = jnp.dot((mean - ug).astype(BF16), wg, preferred_element_type=F32) * ps
        o_ref[0, pl.ds(pl.multiple_of(i * tok, tok), tok), :] = z.astype(BF16)
        return carry

    lax.fori_loop(0, rows // POOL_STRIP, strip, 0, unroll=2)


def _pool_kernel(h_ref, win_ref, wg_ref, ps_ref, o_ref, upad, *, rows):
    g = pl.program_id(1)
    for gi, win in enumerate(POOL_WINDOWS):
        @pl.when(g == gi)
        def _(win=win):
            _pool_group(h_ref, win_ref, wg_ref, ps_ref, o_ref, upad, win=win, rows=rows)


def _pool_call(h, w_in, w_grp, p_scale, slot):
    b, l, d = h.shape
    ng = len(POOL_WINDOWS)
    gd = d // ng
    rows = l // GRID_W
    return pl.pallas_call(
        functools.partial(_pool_kernel, rows=rows),
        grid=(b, ng),
        in_specs=[
            pl.BlockSpec((1, l, d), lambda i, g: (i, 0, 0)),
            pl.BlockSpec((None, d, gd), lambda i, g: (slot, 0, g)),
            pl.BlockSpec((None, 1, gd, gd), lambda i, g: (slot, g, 0, 0)),
            pl.BlockSpec((None, 1, gd), lambda i, g: (slot, 0, g)),
        ],
        out_specs=pl.BlockSpec((1, l, gd), lambda i, g: (i, 0, g)),
        out_shape=jax.ShapeDtypeStruct((b, l, d), BF16),
        scratch_shapes=[pltpu.VMEM((rows + 2 * POOL_PAD, GRID_W, gd), F32)],
        compiler_params=_cparams(("parallel", "arbitrary")),
        name="pool_mix",
    )(h, w_in, w_grp, p_scale.reshape(p_scale.shape[0], 1, d))


MOE_TILE = 1024
MOE_WINDOW = 304
MOE_EXTRA = 64
ROW_ALIGN = 16
DEST_LANE = N_EXPERTS
PIECE_STRIDE = 32
N_PIECES = 3
AUX_ROWS = 8


def _route(sel, s):
    keep = []
    gsum = []
    for g in range(N_EXPERT_GROUPS):
        a = sel[g * EXPERTS_PER_GROUP:(g + 1) * EXPERTS_PER_GROUP]
        beaten = [jnp.zeros_like(a[0]) for _ in a]
        for i in range(EXPERTS_PER_GROUP):
            for j in range(i + 1, EXPERTS_PER_GROUP):
                ge = jnp.where(a[i] >= a[j], 1.0, 0.0)
                beaten[j] = beaten[j] + ge
                beaten[i] = beaten[i] + (1.0 - ge)
        kg = [jnp.where(bt < 1.5, 1.0, 0.0) for bt in beaten]
        keep.append(kg)
        gsum.append(sum(k * x for k, x in zip(kg, a)))
    picked = []
    bests = []
    for g in range(N_EXPERT_GROUPS):
        better = jnp.zeros_like(gsum[0])
        for o in range(N_EXPERT_GROUPS):
            if o < g:
                better = better + jnp.where(gsum[o] >= gsum[g], 1.0, 0.0)
            elif o > g:
                better = better + jnp.where(gsum[o] > gsum[g], 1.0, 0.0)
        best = jnp.where(better < 0.5, 1.0, 0.0)
        bests.append(best)
        for i in range(EXPERTS_PER_GROUP):
            picked.append(best * keep[g][i] * s[g * EXPERTS_PER_GROUP + i])
    den = sum(picked)
    return [p / den for p in picked], bests


def _post_kernel(x_ref, y_ref, w_ref, mod_ref, g_ref, rw_ref, rb_ref,
                 xo_ref, h2_ref, rt_ref, aux_ref, wbuf):
    @pl.when(pl.program_id(0) == 0)
    def _():
        wbuf[...] = w_ref[...].astype(BF16)

    yw = jnp.dot(y_ref[...], wbuf[...], preferred_element_type=F32)
    xn = x_ref[...] + mod_ref[0, 2:3, :] * yw
    xo_ref[...] = xn
    h2 = _modulate(xn, g_ref[...], mod_ref[0, 3:4, :], mod_ref[0, 4:5, :])
    hi = h2.astype(BF16)
    h2_ref[...] = hi
    lo = (h2 - hi.astype(F32)).astype(BF16)
    rw = rw_ref[...]
    half = hi.shape[0] // 2
    prod = jnp.concatenate(
        [jnp.dot(hi[r:r + half], rw, preferred_element_type=F32)
         + jnp.dot(lo[r:r + half], rw, preferred_element_type=F32) for r in (0, half)], axis=0).T
    logits = prod[0:N_EXPERTS, :] + prod[N_EXPERTS:2 * N_EXPERTS, :]
    s = _sigmoid_pair(logits)[0]
    sel = s + rb_ref[...]
    comb, bests = _route([sel[e:e + 1, :] for e in range(N_EXPERTS)],
                         [s[e:e + 1, :] for e in range(N_EXPERTS)])
    t = logits.shape[1]

    ind = jnp.concatenate(bests + [jnp.zeros((AUX_ROWS - N_EXPERT_GROUPS, t), F32)], axis=0)
    n_seg = t // LANES
    stacked = jnp.concatenate([ind[:, j * LANES:(j + 1) * LANES] for j in range(n_seg)], axis=0)
    upper = jnp.where(lax.broadcasted_iota(jnp.int32, (LANES, LANES), 0)
                      <= lax.broadcasted_iota(jnp.int32, (LANES, LANES), 1), 1.0, 0.0).astype(BF16)
    local = jnp.dot(stacked.astype(BF16), upper, preferred_element_type=F32)
    off = jnp.zeros((AUX_ROWS, 1), F32)
    pieces = []
    for j in range(n_seg):
        seg_cum = local[j * AUX_ROWS:(j + 1) * AUX_ROWS, :]
        pieces.append(seg_cum + off)
        off = off + seg_cum[:, LANES - 1:LANES]
    cum = jnp.concatenate(pieces, axis=1)
    counts = [off[g:g + 1, :] for g in range(N_EXPERT_GROUPS)]
    starts = [jnp.zeros((1, 1), F32)]
    for g in range(1, N_EXPERT_GROUPS):
        starts.append(starts[-1] + counts[g - 1])
    dest = sum(bests[g] * (starts[g] + cum[g:g + 1, :] - 1.0) for g in range(N_EXPERT_GROUPS))

    lane = lax.broadcasted_iota(jnp.int32, (1, t), 1)
    seg = sum(jnp.where(lane == k, v, 0.0) for k, v in enumerate(starts + counts))
    aux_ref[0] = jnp.concatenate([dest, seg, jnp.zeros((AUX_ROWS - 2, t), F32)], axis=0)
    p1 = [w.astype(BF16).astype(F32) for w in comb]
    r1 = [w - p for w, p in zip(comb, p1)]
    p2 = [r.astype(BF16).astype(F32) for r in r1]
    p3 = [r - p for r, p in zip(r1, p2)]
    pad = [jnp.zeros((PIECE_STRIDE - N_EXPERTS, t), F32)]
    table = jnp.concatenate(p1 + [dest] + [jnp.zeros((PIECE_STRIDE - N_EXPERTS - 1, t), F32)] + p2 + pad + p3 + pad
                            + [jnp.zeros((LANES - N_PIECES * PIECE_STRIDE, t), F32)], axis=0)
    rt_ref[...] = table.T


def _post_call(x, y, w, w_slot, mod, g, router_w, router_b, tokens_per_batch, tile=MOE_TILE):
    n, d = x.shape
    per_b = tokens_per_batch // tile
    rwh = router_w.astype(BF16)
    rwl = (router_w - rwh.astype(F32)).astype(BF16)
    rw = jnp.concatenate([rwh, rwl, jnp.zeros((d, LANES - 2 * N_EXPERTS), BF16)], axis=1)
    row = lambda i: (i, 0)
    fixed = lambda i: (0, 0)
    return pl.pallas_call(
        _post_kernel,
        grid=(n // tile,),
        in_specs=[
            pl.BlockSpec((tile, d), row),
            pl.BlockSpec((tile, d), row),
            pl.BlockSpec((None, d, d), lambda i: (w_slot, 0, 0)),
            pl.BlockSpec((1, N_MOD, d), lambda i: (i // per_b, 0, 0)),
            pl.BlockSpec((1, d), fixed),
            pl.BlockSpec((d, LANES), fixed),
            pl.BlockSpec((N_EXPERTS, 1), fixed),
        ],
        out_specs=[
            pl.BlockSpec((tile, d), row),
            pl.BlockSpec((tile, d), row),
            pl.BlockSpec((tile, LANES), row),
            pl.BlockSpec((1, AUX_ROWS, tile), lambda i: (i, 0, 0)),
        ],
        out_shape=[
            jax.ShapeDtypeStruct((n, d), F32),
            jax.ShapeDtypeStruct((n, d), BF16),
            jax.ShapeDtypeStruct((n, LANES), F32),
            jax.ShapeDtypeStruct((n // tile, AUX_ROWS, tile), F32),
        ],
        scratch_shapes=[pltpu.VMEM((d, d), BF16)],
        compiler_params=_cparams(("arbitrary",)),
        name="mixer_out_router",
    )(x, y, w, mod, g.reshape(1, d), rw, router_b.reshape(N_EXPERTS, 1))


def _moe_kernel(seg_ref, h_ref, rt_ref, aux_ref, wg_ref, wu_ref, wd_ref, x_ref, mod_ref, ng_ref, nmod_ref,
                o_ref, *rest, final_norm):
    if final_norm:
        xs, cws, ys = rest
    else:
        hn_ref, xs, cws, ys = rest
    i = pl.program_id(0)
    g = pl.program_id(1)
    t = h_ref.shape[0]

    @pl.when(g == 0)
    def _():
        dest_row = aux_ref[0, 0:1, :]
        perm = jnp.where(lax.broadcasted_iota(jnp.int32, (t, t), 0).astype(F32) == dest_row,
                         1.0, 0.0).astype(BF16)
        xs[...] = jnp.dot(perm, h_ref[...], preferred_element_type=F32).astype(BF16)
        cws[...] = jnp.dot(perm, rt_ref[...].astype(BF16), preferred_element_type=F32)
        ys[...] = jnp.zeros_like(ys)

    start = seg_ref[i, g]
    end = start + seg_ref[i, N_EXPERT_GROUPS + g]
    wd = wd_ref[...].reshape(EXPERTS_PER_GROUP * D_EXPERT, wd_ref.shape[2])

    def window(w0, rows, lo, hi):
        sl = pl.ds(pl.multiple_of(w0, ROW_ALIGN), rows)
        xw = xs[sl, :]
        cw = cws[sl, :]
        r = w0 + lax.broadcasted_iota(jnp.int32, cw.shape, 0)
        lane = lax.broadcasted_iota(jnp.int32, cw.shape, 1)
        piece_lane = jnp.where(lane < N_PIECES * PIECE_STRIDE, lane & (PIECE_STRIDE - 1), -1)
        cw = jnp.where(r >= lo, cw, 0.0)
        cw = jnp.where(r < hi, cw, 0.0)
        acts = []
        for e in range(EXPERTS_PER_GROUP):
            ce = jnp.sum(jnp.where(piece_lane == g * EXPERTS_PER_GROUP + e, cw, 0.0), axis=1, keepdims=True)
            a = _silu_tanh(jnp.dot(xw, wg_ref[e], preferred_element_type=F32)) \
                * jnp.dot(xw, wu_ref[e], preferred_element_type=F32)
            acts.append((a * ce).astype(BF16))
        ys[sl, :] += jnp.dot(jnp.concatenate(acts, axis=1), wd, preferred_element_type=F32)

    w0 = jnp.minimum(start & -ROW_ALIGN, t - MOE_WINDOW)
    covered = w0 + MOE_WINDOW
    window(w0, MOE_WINDOW, start, jnp.minimum(end, covered))

    def extra(k, carry):
        lo = covered + k * MOE_EXTRA
        window(jnp.minimum(lo, t - MOE_EXTRA), MOE_EXTRA, lo, jnp.minimum(end, lo + MOE_EXTRA))
        return carry

    n_extra = jnp.maximum(end - covered + MOE_EXTRA - 1, 0) >> (MOE_EXTRA.bit_length() - 1)
    lax.fori_loop(0, n_extra, extra, 0)

    @pl.when(g == N_EXPERT_GROUPS - 1)
    def _():
        dest_col = rt_ref[:, DEST_LANE:DEST_LANE + 1]
        unperm = jnp.where(lax.broadcasted_iota(jnp.int32, (t, t), 1).astype(F32) == dest_col,
                           1.0, 0.0).astype(BF16)
        ff = jnp.dot(unperm, ys[...].astype(BF16), preferred_element_type=F32)
        xn = x_ref[...] + mod_ref[0, 5:6, :] * ff
        if final_norm:
            ms = jnp.mean(xn * xn, axis=-1, keepdims=True)
            xn = xn * lax.rsqrt(ms + EPS) * ng_ref[...]
        else:
            hn_ref[...] = _modulate(xn, ng_ref[...], nmod_ref[0, 0:1, :], nmod_ref[0, 1:2, :]).astype(BF16)
        o_ref[...] = xn


def _moe_call(h2, table, aux, wg, wu, wd, layer, x, mod, norm_g, next_mod, tokens_per_batch, final_norm):
    n, d = x.shape
    tile = MOE_TILE
    per_b = tokens_per_batch // tile
    seg = aux[:, 1, :2 * N_EXPERT_GROUPS].astype(jnp.int32)
    row = lambda i, g, seg: (i, 0)
    grp = lambda i, g, seg: (layer, g, 0, 0)
    tok = pl.BlockSpec((tile, d), row)
    x_out = jax.ShapeDtypeStruct((n, d), F32)
    grid_spec = pltpu.PrefetchScalarGridSpec(
        num_scalar_prefetch=1,
        grid=(n // tile, N_EXPERT_GROUPS),
        in_specs=[
            pl.BlockSpec((tile, d), row),
            pl.BlockSpec((tile, LANES), row),
            pl.BlockSpec((1, AUX_ROWS, tile), lambda i, g, seg: (i, 0, 0)),
            pl.BlockSpec((None, EXPERTS_PER_GROUP, d, D_EXPERT), grp),
            pl.BlockSpec((None, EXPERTS_PER_GROUP, d, D_EXPERT), grp),
            pl.BlockSpec((None, EXPERTS_PER_GROUP, D_EXPERT, d), grp),
            pl.BlockSpec((tile, d), row),
            pl.BlockSpec((1, N_MOD, d), lambda i, g, seg: (i // per_b, 0, 0)),
            pl.BlockSpec((1, d), lambda i, g, seg: (0, 0)),
            pl.BlockSpec((1, N_MOD, d), lambda i, g, seg: (i // per_b, 0, 0)),
        ],
        out_specs=tok if final_norm else [tok, tok],
        scratch_shapes=[
            pltpu.VMEM((tile, d), BF16),
            pltpu.VMEM((tile, LANES), F32),
            pltpu.VMEM((tile, d), F32),
        ],
    )
    return pl.pallas_call(
        functools.partial(_moe_kernel, final_norm=final_norm),
        grid_spec=grid_spec,
        out_shape=x_out if final_norm else [x_out, jax.ShapeDtypeStruct((n, d), BF16)],
        compiler_params=_cparams(("parallel", "arbitrary")),
        name="moe_ffn",
    )(seg, h2, table, aux, wg, wu, wd, x, mod, norm_g.reshape(1, d), next_mod)


def kernel(x, c, ctx, c_ctx, w_mod, b_mod, norm1_g, norm2_g, hg_w_in, hg_lb_fwd, hg_lb_bwd, hg_gnorm,
           hg_w_out, pool_w_in, pool_w_grp, pool_scale, pool_w_out, router_w, router_b, moe_w_gate,
           moe_w_up, moe_w_down, final_g):
    b, l, d = x.shape
    depth = w_mod.shape[0]
    n_mixers = 2

    cc = jnp.concatenate([c, c_ctx[None, :], jnp.zeros((MOD_ROWS - b - 1, d), F32)], axis=0)
    mods = _mod_call(cc, w_mod, b_mod)

    experts = (moe_w_gate.astype(BF16), moe_w_up.astype(BF16), moe_w_down.astype(BF16))
    x_lat = x.reshape(b * l, d)
    h_lat = None
    for i in range(depth):
        slot = i // n_mixers
        mod_lat = mods[i, :b].reshape(b, N_MOD, d)
        if i % n_mixers == 0:
            h_lat, g_lat = _norm_call(x_lat.reshape(b, l, d), norm1_g[i], mod_lat, 512,
                                      hg_w_in, slot, N_HG_PROJ - 1)
            mod_ctx = jnp.broadcast_to(mods[i, b].reshape(1, N_MOD, d), (b, N_MOD, d))
            h_ctx = _norm_call(ctx, norm1_g[i], mod_ctx, ctx.shape[1])
            y = _hgrn_call(h_ctx, h_lat, g_lat, hg_w_in, hg_lb_fwd, hg_lb_bwd, hg_gnorm[slot], slot)
            w_out = hg_w_out
        else:
            if h_lat is None:
                h_lat = _norm_call(x_lat.reshape(b, l, d), norm1_g[i], mod_lat, 512)
            y = _pool_call(h_lat, pool_w_in, pool_w_grp, pool_scale, slot)
            w_out = pool_w_out
        x_lat, h2, table, aux = _post_call(x_lat, y.reshape(b * l, d), w_out, slot, mod_lat,
                                           norm2_g[i], router_w, router_b, l)
        if i == depth - 1:
            x_lat = _moe_call(h2, table, aux, *experts, i, x_lat, mod_lat, final_g, mod_lat, l, final_norm=True)
        else:
            mod_next = mods[i + 1, :b].reshape(b, N_MOD, d)
            x_lat, h_next = _moe_call(h2, table, aux, *experts, i, x_lat, mod_lat, norm1_g[i + 1], mod_next, l,
                                      final_norm=False)
            h_lat = h_next.reshape(b, l, d) if (i + 1) % n_mixers != 0 else None
    return x_lat.reshape(b, l, d)
```

```python
import functools

import jax
import jax.numpy as jnp
from jax import lax
from jax.experimental import pallas as pl
from jax.experimental.pallas import tpu as pltpu

F32 = jnp.float32
BF16 = jnp.bfloat16

EPS = 1e-6
N_MOD = 6
HG_HEADS = 8
HG_DK = 128
HG_CHUNK = 64
N_HG_PROJ = 5
POOL_WINDOWS = (2, 4, 8, 16)
GRID_W = 64
GRID_SHIFT = GRID_W.bit_length() - 1
assert 1 << GRID_SHIFT == GRID_W
N_EXPERTS = 16
N_EXPERT_GROUPS = 4
EXPERTS_PER_GROUP = N_EXPERTS // N_EXPERT_GROUPS
D_EXPERT = 256

LANES = 128
MOD_ROWS = 8
VMEM_LIMIT = 56 * 1024 * 1024
HGRN_VMEM_LIMIT = 60 * 1024 * 1024


def _cparams(sem, vmem=VMEM_LIMIT):
    return pltpu.CompilerParams(dimension_semantics=sem, vmem_limit_bytes=vmem)


def _sigmoid_pair(z):
    e = jnp.exp(-jnp.abs(z))
    r = 1.0 / (1.0 + e)
    er = e * r
    pos = z >= 0
    return jnp.where(pos, r, er), jnp.where(pos, er, r)


def _silu(z):
    return z * _sigmoid_pair(z)[0]


def _sigmoid_tanh(z):
    return 0.5 * jnp.tanh(0.5 * z) + 0.5


def _silu_tanh(z):
    return z * _sigmoid_tanh(z)


def _split_bf16(x):
    hi = x.astype(BF16)
    return hi, (x - hi.astype(F32)).astype(BF16)


def _mod_kernel(c_ref, w_ref, b_ref, o_ref):
    a = jnp.concatenate(_split_bf16(_silu(c_ref[...])), axis=0)
    w_hi, w_lo = _split_bf16(w_ref[0])
    acc = jnp.dot(a, w_hi, preferred_element_type=F32) + jnp.dot(a, w_lo, preferred_element_type=F32)
    o_ref[0] = acc[0:MOD_ROWS] + acc[MOD_ROWS:2 * MOD_ROWS] + b_ref[0]


def _mod_call(cc, w_mod, b_mod):
    depth, d, n = w_mod.shape
    tn = 1024
    return pl.pallas_call(
        _mod_kernel,
        grid=(depth, n // tn),
        in_specs=[
            pl.BlockSpec((MOD_ROWS, d), lambda i, j: (0, 0)),
            pl.BlockSpec((1, d, tn), lambda i, j: (i, 0, j)),
            pl.BlockSpec((1, 1, tn), lambda i, j: (i, 0, j)),
        ],
        out_specs=pl.BlockSpec((1, MOD_ROWS, tn), lambda i, j: (i, 0, j)),
        out_shape=jax.ShapeDtypeStruct((depth, MOD_ROWS, n), F32),
        compiler_params=_cparams(("parallel", "parallel")),
        name="mod_proj",
    )(cc, w_mod, b_mod.reshape(depth, 1, n))


def _modulate(x, g, shift, scale):
    ms = jnp.mean(x * x, axis=-1, keepdims=True)
    return (x * lax.rsqrt(ms + EPS) * g) * (1.0 + scale) + shift


def _norm_kernel(x_ref, g_ref, mod_ref, *rest):
    h = _modulate(x_ref[0], g_ref[...], mod_ref[0, 0:1, :], mod_ref[0, 1:2, :]).astype(BF16)
    if len(rest) == 1:
        (o_ref,) = rest
    else:
        w_ref, o_ref, p_ref, wbuf = rest

        @pl.when((pl.program_id(0) == 0) & (pl.program_id(1) == 0))
        def _():
            wbuf[...] = w_ref[...].astype(BF16)

        p_ref[0] = jnp.dot(h, wbuf[...], preferred_element_type=F32).astype(BF16)
    o_ref[0] = h


def _norm_call(x, g, mod, tile, w_proj=None, w_slot=0, w_col=0):
    b, t, d = x.shape
    tok = pl.BlockSpec((1, tile, d), lambda i, j: (i, j, 0))
    in_specs = [tok, pl.BlockSpec((1, d), lambda i, j: (0, 0)),
                pl.BlockSpec((1, N_MOD, d), lambda i, j: (i, 0, 0))]
    args = [x, g.reshape(1, d), mod]
    out_specs, out_shape = tok, jax.ShapeDtypeStruct((b, t, d), BF16)
    scratch = []
    if w_proj is not None:
        in_specs.append(pl.BlockSpec((None, d, d), lambda i, j: (w_slot, 0, w_col)))
        args.append(w_proj)
        out_specs, out_shape = [tok, tok], [out_shape, out_shape]
        scratch = [pltpu.VMEM((d, d), BF16)]
    return pl.pallas_call(
        _norm_kernel,
        grid=(b, t // tile),
        in_specs=in_specs,
        out_specs=out_specs,
        out_shape=out_shape,
        scratch_shapes=scratch,
        compiler_params=_cparams(("arbitrary", "arbitrary")),
        name="norm1",
    )(*args)


HG_BLOCK = 256
HG_CPB = HG_BLOCK // HG_CHUNK


def _chunk_prefix(x, row):
    for d in (1, 2, 4, 8, 16, 32):
        x = x + jnp.where(row >= d, pltpu.roll(x, d, axis=0), 0.0)
    return x


def _chunk_suffix(x, row):
    n = x.shape[0]
    for d in (1, 2, 4, 8, 16, 32):
        x = x + jnp.where(row < HG_CHUNK - d, pltpu.roll(x, n - d, axis=0), 0.0)
    return x


def _lower_bound(lb_ref, slot):
    rows = [lb_ref[j, 0] for j in range(lb_ref.shape[0])]
    m = functools.reduce(jnp.maximum, rows)
    es = [jnp.exp(r - m) for r in rows]
    return sum(es[:slot + 1]) / sum(es)


N_HGRN_IN = 10


def _hgrn_kernel(*refs, slot, n_ctx_chunks, n_lat_chunks, n_cast):
    hc_ref, hl_ref, g_ref, wq_ref, wv_ref, wf_ref, wb_ref, lbf_ref, lbb_ref, gn_ref = refs[:N_HGRN_IN]
    cast_in = refs[N_HGRN_IN:N_HGRN_IN + n_cast]
    y_ref = refs[N_HGRN_IN + n_cast]
    cast_out = refs[N_HGRN_IN + n_cast + 1:N_HGRN_IN + 2 * n_cast + 1]
    (wbuf, pbuf0, pbuf1, pbuf2, pbuf3, oacc, qif, qib,
     kvf, kvb, decf, decb, spf, spb) = refs[N_HGRN_IN + 2 * n_cast + 1:]
    for c_ref, o_ref in zip(cast_in, cast_out):
        o_ref[...] = c_ref[...].astype(BF16)

    lb_f = _lower_bound(lbf_ref, slot)
    lb_b = _lower_bound(lbb_ref, slot)
    row = lax.broadcasted_iota(jnp.int32, (HG_BLOCK, HG_DK), 0) & (HG_CHUNK - 1)
    ci = lax.broadcasted_iota(jnp.int32, (HG_CPB, HG_CHUNK, HG_CHUNK), 1)
    si = lax.broadcasted_iota(jnp.int32, (HG_CPB, HG_CHUNK, HG_CHUNK), 2)

    def c3(t):
        return t.reshape(HG_CPB, HG_CHUNK, HG_DK)

    def direction(qs3, v3b, z, lb, fwd):
        sig = _sigmoid_tanh(z)
        f = lb + (1.0 - lb) * sig
        k3 = c3((1.0 - lb) * (1.0 - sig))
        lf = jnp.log(f)
        if fwd:
            cum = c3(_chunk_prefix(lf, row))
            ref = cum[:, HG_CHUNK // 2 - 1:HG_CHUNK // 2, :]
            last = cum[:, HG_CHUNK - 1:HG_CHUNK, :]
        else:
            cum = c3(_chunk_suffix(lf, row))
            ref = cum[:, HG_CHUNK // 2:HG_CHUNK // 2 + 1, :]
            last = cum[:, 0:1, :]
        dec = jnp.exp(last)
        if qs3 is None:
            kl = k3 * jnp.exp(last - cum)
        else:
            e1 = jnp.exp(cum - ref)
            qd = qs3 * e1
            qi = qd * jnp.exp(ref)
            kd = k3 * (1.0 / e1)
            kl = kd * jnp.exp(last - ref)
        kvt = jnp.einsum('ncv,nck->nvk', v3b, kl.astype(BF16), preferred_element_type=F32)
        if qs3 is None:
            return None, None, kvt, dec
        sc = jnp.einsum('nck,nsk->ncs', qd.astype(BF16), kd.astype(BF16), preferred_element_type=F32)
        sc = jnp.where((ci >= si) if fwd else (ci <= si), sc, 0.0)
        intra = jnp.einsum('ncs,nsv->ncv', sc.astype(BF16), v3b, preferred_element_type=F32)
        return intra, qi, kvt, dec

    for p, wp_ref in enumerate((wq_ref, wv_ref, wf_ref, wb_ref)):
        wbuf[:, p * HG_DK:(p + 1) * HG_DK] = wp_ref[...].astype(BF16)

    def project(hrows):
        return jnp.dot(hrows, wbuf[...], preferred_element_type=F32)

    def block(p, chunk0, lat_row0):
        v3b = c3(p[:, HG_DK:2 * HG_DK]).astype(BF16)
        zf = p[:, 2 * HG_DK:3 * HG_DK]
        zb = p[:, 3 * HG_DK:4 * HG_DK]
        if lat_row0 is None:
            qs3 = None
        else:
            qs3 = c3(_silu_tanh(p[:, 0:HG_DK]))
        in_f, qi_f, kv_f, dec_f = direction(qs3, v3b, zf, lb_f, True)
        in_b, qi_b, kv_b, dec_b = direction(qs3, v3b, zb, lb_b, False)
        kvf[pl.ds(chunk0, HG_CPB)] = kv_f
        kvb[pl.ds(chunk0, HG_CPB)] = kv_b
        decf[pl.ds(chunk0, HG_CPB)] = dec_f
        decb[pl.ds(chunk0, HG_CPB)] = dec_b
        if lat_row0 is not None:
            rows = pl.ds(lat_row0, HG_BLOCK)
            oacc[rows, :] = (in_f + in_b).reshape(HG_BLOCK, HG_DK)
            qif[rows, :] = qi_f.reshape(HG_BLOCK, HG_DK).astype(BF16)
            qib[rows, :] = qi_b.reshape(HG_BLOCK, HG_DK).astype(BF16)

    n_blocks = n_lat_chunks // HG_CPB

    def lat_rows(i):
        return hl_ref[0, pl.ds(pl.multiple_of(i * HG_BLOCK, HG_BLOCK), HG_BLOCK), :]

    def lat_terms(p_ref, i):
        block(p_ref[...], n_ctx_chunks + i * HG_CPB, pl.multiple_of(i * HG_BLOCK, HG_BLOCK))

    ctx_p = [project(hc_ref[0, i * HG_BLOCK:(i + 1) * HG_BLOCK, :]) for i in range(n_ctx_chunks // HG_CPB)]
    pbuf0[...] = project(lat_rows(0))
    for i, p in enumerate(ctx_p):
        block(p, i * HG_CPB, None)
    pbuf1[...] = project(lat_rows(1))

    def lat_quad(i0, last):
        pbuf2[...] = project(lat_rows(i0 + 2))
        lat_terms(pbuf0, i0)
        pbuf3[...] = project(lat_rows(i0 + 3))
        lat_terms(pbuf1, i0 + 1)
        if not last:
            pbuf0[...] = project(lat_rows(i0 + 4))
        lat_terms(pbuf2, i0 + 2)
        if not last:
            pbuf1[...] = project(lat_rows(i0 + 5))
        lat_terms(pbuf3, i0 + 3)

    def lat_body(t, carry):
        lat_quad(4 * t, False)
        return carry

    lax.fori_loop(0, n_blocks // 4 - 1, lat_body, 0)
    lat_quad(n_blocks - 4, True)

    def advance(s, kv_ref, dec_ref, n):
        return dec_ref[n] * s + kv_ref[n]

    sf = jnp.zeros((HG_DK, HG_DK), F32)
    for n in range(n_ctx_chunks):
        sf = advance(sf, kvf, decf, n)
    sb = jnp.zeros((HG_DK, HG_DK), F32)
    for n in reversed(range(n_ctx_chunks)):
        sb = advance(sb, kvb, decb, n)

    def scan_step(t, carry):
        sf, sb = carry
        jb = n_lat_chunks - 1 - t
        spf[t] = sf.astype(BF16)
        spb[jb] = sb.astype(BF16)
        return (advance(sf, kvf, decf, n_ctx_chunks + t), advance(sb, kvb, decb, n_ctx_chunks + jb))

    lax.fori_loop(0, n_lat_chunks, scan_step, (sf, sb), unroll=2)

    gn = gn_ref[...]

    def block_rows(i):
        return pl.ds(pl.multiple_of(i * HG_BLOCK, HG_BLOCK), HG_BLOCK)

    def inter(i):
        rows = block_rows(i)
        chunks = pl.ds(i * HG_CPB, HG_CPB)
        return (jnp.einsum('nck,nvk->ncv', c3(qif[rows, :]), spf[chunks], preferred_element_type=F32)
                + jnp.einsum('nck,nvk->ncv', c3(qib[rows, :]), spb[chunks], preferred_element_type=F32)
                ).reshape(HG_BLOCK, HG_DK)

    def readout(o_ref, i):
        rows = block_rows(i)
        o = oacc[rows, :] + o_ref[...]
        ms = jnp.mean(o * o, axis=-1, keepdims=True)
        o = o * lax.rsqrt(ms + EPS) * gn
        y_ref[0, rows, :] = (o * _silu_tanh(g_ref[0, rows, :].astype(F32))).astype(BF16)

    obuf0, obuf1 = pbuf0.at[:, 0:HG_DK], pbuf1.at[:, 0:HG_DK]
    obuf0[...] = inter(0)

    def readout_pair(t, carry):
        obuf1[...] = inter(2 * t + 1)
        readout(obuf0, 2 * t)
        obuf0[...] = inter(2 * t + 2)
        readout(obuf1, 2 * t + 1)
        return carry

    lax.fori_loop(0, n_blocks // 2 - 1, readout_pair, 0)
    obuf1[...] = inter(n_blocks - 1)
    readout(obuf0, n_blocks - 2)
    readout(obuf1, n_blocks - 1)


def _hgrn_call(h_ctx, h_lat, g_lat, w_in, lb_fwd, lb_bwd, gnorm, slot, to_cast):
    b, lc, d = h_ctx.shape
    ll = h_lat.shape[1]
    nrow = lb_fwd.shape[0]
    ncc, nlc = lc // HG_CHUNK, ll // HG_CHUNK
    kern = functools.partial(_hgrn_kernel, slot=slot, n_ctx_chunks=ncc, n_lat_chunks=nlc, n_cast=len(to_cast))
    lb_spec = pl.BlockSpec((nrow, 1, 1, HG_DK), lambda i, h: (0, h, 0, 0))
    head_cols = pl.BlockSpec((1, ll, HG_DK), lambda i, h: (i, 0, h))
    n_rec = N_HG_PROJ - 1
    w_cols = [pl.BlockSpec((None, d, HG_DK), functools.partial(lambda i, h, p: (slot, 0, p * HG_HEADS + h), p=p))
              for p in range(n_rec)]
    steps = b * HG_HEADS
    slabs = [a.reshape(steps, -1, *a.shape[-2:]) for a in to_cast]
    slab_specs = [pl.BlockSpec((1,) + a.shape[1:], lambda i, h: (i * HG_HEADS + h, 0, 0, 0)) for a in slabs]
    outs = pl.pallas_call(
        kern,
        grid=(b, HG_HEADS),
        in_specs=[
            pl.BlockSpec((1, lc, d), lambda i, h: (i, 0, 0)),
            pl.BlockSpec((1, ll, d), lambda i, h: (i, 0, 0)),
            head_cols,
            *w_cols,
            lb_spec, lb_spec,
            pl.BlockSpec((1, HG_DK), lambda i, h: (0, 0)),
            *slab_specs,
        ],
        out_specs=[head_cols, *slab_specs],
        out_shape=[jax.ShapeDtypeStruct((b, ll, d), BF16)]
        + [jax.ShapeDtypeStruct(a.shape, BF16) for a in slabs],
        scratch_shapes=[
            pltpu.VMEM((d, n_rec * HG_DK), BF16),
            pltpu.VMEM((HG_BLOCK, n_rec * HG_DK), F32),
            pltpu.VMEM((HG_BLOCK, n_rec * HG_DK), F32),
            pltpu.VMEM((HG_BLOCK, n_rec * HG_DK), F32),
            pltpu.VMEM((HG_BLOCK, n_rec * HG_DK), F32),
            pltpu.VMEM((ll, HG_DK), F32),
            pltpu.VMEM((ll, HG_DK), BF16),
            pltpu.VMEM((ll, HG_DK), BF16),
            pltpu.VMEM((ncc + nlc, HG_DK, HG_DK), F32),
            pltpu.VMEM((ncc + nlc, HG_DK, HG_DK), F32),
            pltpu.VMEM((ncc + nlc, 1, HG_DK), F32),
            pltpu.VMEM((ncc + nlc, 1, HG_DK), F32),
            pltpu.VMEM((nlc, HG_DK, HG_DK), BF16),
            pltpu.VMEM((nlc, HG_DK, HG_DK), BF16),
        ],
        compiler_params=_cparams(("parallel", "arbitrary"), HGRN_VMEM_LIMIT),
        name="hgrn2",
    )(h_ctx, h_lat, g_lat, *([w_in] * n_rec),
      lb_fwd.reshape(nrow, HG_HEADS, 1, HG_DK), lb_bwd.reshape(nrow, HG_HEADS, 1, HG_DK),
      gnorm.reshape(1, HG_DK), *slabs)
    return outs[0], [o.reshape(a.shape) for o, a in zip(outs[1:], to_cast)]


POOL_STRIP = 8
POOL_PAD = 8
POOL_BAND = 256


def _pool_group(h_ref, win_ref, wg_ref, ps_ref, o_ref, upad, *, win, rows):
    gd = win_ref.shape[1]
    half = win // 2
    u = jnp.dot(h_ref[0], win_ref[...].astype(BF16), preferred_element_type=F32)
    zeros = jnp.zeros((POOL_PAD, GRID_W, gd), F32)
    upad[0:POOL_PAD] = zeros
    upad[POOL_PAD + rows:POOL_PAD + rows + POOL_PAD] = zeros
    upad[POOL_PAD:POOL_PAD + rows] = u.reshape(rows, GRID_W, gd)

    tok = POOL_STRIP * GRID_W
    bi = lax.broadcasted_iota(jnp.int32, (POOL_BAND, POOL_BAND), 0)
    bj = lax.broadcasted_iota(jnp.int32, (POOL_BAND, POOL_BAND), 1)
    lo_c = (bi & (GRID_W - 1)) - half
    cj = bj & (GRID_W - 1)
    band = jnp.where(bi >> GRID_SHIFT == bj >> GRID_SHIFT, 1.0, 0.0)
    band = jnp.where(cj >= lo_c, band, 0.0)
    band = jnp.where(cj < lo_c + win, band, 0.0).astype(BF16)
    t = lax.broadcasted_iota(jnp.int32, (tok, LANES), 0)
    col = t & (GRID_W - 1)
    cnt_c = jnp.minimum(col - half + win, GRID_W) - jnp.maximum(col - half, 0)
    wg = wg_ref[0].astype(BF16)
    ps = ps_ref[...]

    def strip(i, carry):
        r0 = i * POOL_STRIP
        slab = upad[pl.ds(r0 + POOL_PAD - half, POOL_STRIP + win - 1)]
        span = 1
        while span < win:
            n = slab.shape[0] - span
            slab = slab[0:n] + slab[span:span + n]
            span *= 2
        rs = slab.reshape(tok, gd)
        hi = rs.astype(BF16)
        lo = (rs - hi.astype(F32)).astype(BF16)
        parts = []
        for k in range(tok // POOL_BAND):
            sl = slice(k * POOL_BAND, (k + 1) * POOL_BAND)
            parts.append(jnp.dot(band, hi[sl], preferred_element_type=F32)
                         + jnp.dot(band, lo[sl], preferred_element_type=F32))
        box = jnp.concatenate(parts, axis=0)
        r = r0 + (t >> GRID_SHIFT)
        cnt_r = jnp.minimum(r - half + win, rows) - jnp.maximum(r - half, 0)
        inv = 1.0 / (cnt_r * cnt_c).astype(F32)
        mean = box * jnp.concatenate([inv] * (gd // LANES), axis=1)
        ug = upad[pl.ds(r0 + POOL_PAD, POOL_STRIP)].reshape(tok, gd)
        z = jnp.dot((mean - ug).astype(BF16), wg, preferred_element_type=F32) * ps
        o_ref[0, pl.ds(pl.multiple_of(i * tok, tok), tok), :] = z.astype(BF16)
        return carry

    lax.fori_loop(0, rows // POOL_STRIP, strip, 0, unroll=2)


def _pool_kernel(h_ref, win_ref, wg_ref, ps_ref, o_ref, upad, *, rows):
    g = pl.program_id(1)
    for gi, win in enumerate(POOL_WINDOWS):
        @pl.when(g == gi)
        def _(win=win):
            _pool_group(h_ref, win_ref, wg_ref, ps_ref, o_ref, upad, win=win, rows=rows)


def _pool_call(h, w_in, w_grp, p_scale, slot):
    b, l, d = h.shape
    ng = len(POOL_WINDOWS)
    gd = d // ng
    rows = l // GRID_W
    return pl.pallas_call(
        functools.partial(_pool_kernel, rows=rows),
        grid=(b, ng),
        in_specs=[
            pl.BlockSpec((1, l, d), lambda i, g: (i, 0, 0)),
            pl.BlockSpec((None, d, gd), lambda i, g: (slot, 0, g)),
            pl.BlockSpec((None, 1, gd, gd), lambda i, g: (slot, g, 0, 0)),
            pl.BlockSpec((None, 1, gd), lambda i, g: (slot, 0, g)),
        ],
        out_specs=pl.BlockSpec((1, l, gd), lambda i, g: (i, 0, g)),
        out_shape=jax.ShapeDtypeStruct((b, l, d), BF16),
        scratch_shapes=[pltpu.VMEM((rows + 2 * POOL_PAD, GRID_W, gd), F32)],
        compiler_params=_cparams(("parallel", "arbitrary")),
        name="pool_mix",
    )(h, w_in, w_grp, p_scale.reshape(p_scale.shape[0], 1, d))


MOE_TILE = 1024
MOE_WINDOW = 304
MOE_EXTRA = 64
ROW_ALIGN = 16
DEST_LANE = N_EXPERTS
PIECE_STRIDE = 32
N_PIECES = 3
AUX_ROWS = 8


def _route(sel, s):
    keep = []
    gsum = []
    for g in range(N_EXPERT_GROUPS):
        a = sel[g * EXPERTS_PER_GROUP:(g + 1) * EXPERTS_PER_GROUP]
        beaten = [jnp.zeros_like(a[0]) for _ in a]
        for i in range(EXPERTS_PER_GROUP):
            for j in range(i + 1, EXPERTS_PER_GROUP):
                ge = jnp.where(a[i] >= a[j], 1.0, 0.0)
                beaten[j] = beaten[j] + ge
                beaten[i] = beaten[i] + (1.0 - ge)
        kg = [jnp.where(bt < 1.5, 1.0, 0.0) for bt in beaten]
        keep.append(kg)
        gsum.append(sum(k * x for k, x in zip(kg, a)))
    picked = []
    bests = []
    for g in range(N_EXPERT_GROUPS):
        better = jnp.zeros_like(gsum[0])
        for o in range(N_EXPERT_GROUPS):
            if o < g:
                better = better + jnp.where(gsum[o] >= gsum[g], 1.0, 0.0)
            elif o > g:
                better = better + jnp.where(gsum[o] > gsum[g], 1.0, 0.0)
        best = jnp.where(better < 0.5, 1.0, 0.0)
        bests.append(best)
        for i in range(EXPERTS_PER_GROUP):
            picked.append(best * keep[g][i] * s[g * EXPERTS_PER_GROUP + i])
    den = sum(picked)
    return [p / den for p in picked], bests


def _post_kernel(x_ref, y_ref, w_ref, mod_ref, g_ref, rw_ref, rb_ref,
                 xo_ref, h2_ref, rt_ref, aux_ref, wbuf):
    @pl.when(pl.program_id(0) == 0)
    def _():
        wbuf[...] = w_ref[...].astype(BF16)

    yw = jnp.dot(y_ref[...], wbuf[...], preferred_element_type=F32)
    xn = x_ref[...] + mod_ref[0, 2:3, :] * yw
    xo_ref[...] = xn
    h2 = _modulate(xn, g_ref[...], mod_ref[0, 3:4, :], mod_ref[0, 4:5, :])
    hi = h2.astype(BF16)
    h2_ref[...] = hi
    lo = (h2 - hi.astype(F32)).astype(BF16)
    rw = rw_ref[...]
    half = hi.shape[0] // 2
    prod = jnp.concatenate(
        [jnp.dot(hi[r:r + half], rw, preferred_element_type=F32)
         + jnp.dot(lo[r:r + half], rw, preferred_element_type=F32) for r in (0, half)], axis=0).T
    logits = prod[0:N_EXPERTS, :] + prod[N_EXPERTS:2 * N_EXPERTS, :]
    s = _sigmoid_pair(logits)[0]
    sel = s + rb_ref[...]
    comb, bests = _route([sel[e:e + 1, :] for e in range(N_EXPERTS)],
                         [s[e:e + 1, :] for e in range(N_EXPERTS)])
    t = logits.shape[1]

    ind = jnp.concatenate(bests + [jnp.zeros((AUX_ROWS - N_EXPERT_GROUPS, t), F32)], axis=0)
    n_seg = t // LANES
    stacked = jnp.concatenate([ind[:, j * LANES:(j + 1) * LANES] for j in range(n_seg)], axis=0)
    upper = jnp.where(lax.broadcasted_iota(jnp.int32, (LANES, LANES), 0)
                      <= lax.broadcasted_iota(jnp.int32, (LANES, LANES), 1), 1.0, 0.0).astype(BF16)
    local = jnp.dot(stacked.astype(BF16), upper, preferred_element_type=F32)
    off = jnp.zeros((AUX_ROWS, 1), F32)
    pieces = []
    for j in range(n_seg):
        seg_cum = local[j * AUX_ROWS:(j + 1) * AUX_ROWS, :]
        pieces.append(seg_cum + off)
        off = off + seg_cum[:, LANES - 1:LANES]
    cum = jnp.concatenate(pieces, axis=1)
    counts = [off[g:g + 1, :] for g in range(N_EXPERT_GROUPS)]
    starts = [jnp.zeros((1, 1), F32)]
    for g in range(1, N_EXPERT_GROUPS):
        starts.append(starts[-1] + counts[g - 1])
    dest = sum(bests[g] * (starts[g] + cum[g:g + 1, :] - 1.0) for g in range(N_EXPERT_GROUPS))

    lane = lax.broadcasted_iota(jnp.int32, (1, t), 1)
    seg = sum(jnp.where(lane == k, v, 0.0) for k, v in enumerate(starts + counts))
    aux_ref[0] = jnp.concatenate([dest, seg, jnp.zeros((AUX_ROWS - 2, t), F32)], axis=0)
    p1 = [w.astype(BF16).astype(F32) for w in comb]
    r1 = [w - p for w, p in zip(comb, p1)]
    p2 = [r.astype(BF16).astype(F32) for r in r1]
    p3 = [r - p for r, p in zip(r1, p2)]
    pad = [jnp.zeros((PIECE_STRIDE - N_EXPERTS, t), F32)]
    table = jnp.concatenate(p1 + [dest] + [jnp.zeros((PIECE_STRIDE - N_EXPERTS - 1, t), F32)] + p2 + pad + p3 + pad
                            + [jnp.zeros((LANES - N_PIECES * PIECE_STRIDE, t), F32)], axis=0)
    rt_ref[...] = table.T


def _post_call(x, y, w, w_slot, mod, g, router_w, router_b, tokens_per_batch, tile=MOE_TILE):
    n, d = x.shape
    per_b = tokens_per_batch // tile
    rwh = router_w.astype(BF16)
    rwl = (router_w - rwh.astype(F32)).astype(BF16)
    rw = jnp.concatenate([rwh, rwl, jnp.zeros((d, LANES - 2 * N_EXPERTS), BF16)], axis=1)
    row = lambda i: (i, 0)
    fixed = lambda i: (0, 0)
    return pl.pallas_call(
        _post_kernel,
        grid=(n // tile,),
        in_specs=[
            pl.BlockSpec((tile, d), row),
            pl.BlockSpec((tile, d), row),
            pl.BlockSpec((None, d, d), lambda i: (w_slot, 0, 0)),
            pl.BlockSpec((1, N_MOD, d), lambda i: (i // per_b, 0, 0)),
            pl.BlockSpec((1, d), fixed),
            pl.BlockSpec((d, LANES), fixed),
            pl.BlockSpec((N_EXPERTS, 1), fixed),
        ],
        out_specs=[
            pl.BlockSpec((tile, d), row),
            pl.BlockSpec((tile, d), row),
            pl.BlockSpec((tile, LANES), row),
            pl.BlockSpec((1, AUX_ROWS, tile), lambda i: (i, 0, 0)),
        ],
        out_shape=[
            jax.ShapeDtypeStruct((n, d), F32),
            jax.ShapeDtypeStruct((n, d), BF16),
            jax.ShapeDtypeStruct((n, LANES), F32),
            jax.ShapeDtypeStruct((n // tile, AUX_ROWS, tile), F32),
        ],
        scratch_shapes=[pltpu.VMEM((d, d), BF16)],
        compiler_params=_cparams(("arbitrary",)),
        name="mixer_out_router",
    )(x, y, w, mod, g.reshape(1, d), rw, router_b.reshape(N_EXPERTS, 1))


def _moe_kernel(seg_ref, h_ref, rt_ref, aux_ref, wg_ref, wu_ref, wd_ref, x_ref, mod_ref, ng_ref, nmod_ref,
                o_ref, *rest, final_norm):
    if final_norm:
        xs, cws, ys = rest
    else:
        hn_ref, xs, cws, ys = rest
    i = pl.program_id(0)
    g = pl.program_id(1)
    t = h_ref.shape[0]

    @pl.when(g == 0)
    def _():
        dest_row = aux_ref[0, 0:1, :]
        perm = jnp.where(lax.broadcasted_iota(jnp.int32, (t, t), 0).astype(F32) == dest_row,
                         1.0, 0.0).astype(BF16)
        xs[...] = jnp.dot(perm, h_ref[...], preferred_element_type=F32).astype(BF16)
        cws[...] = jnp.dot(perm, rt_ref[...].astype(BF16), preferred_element_type=F32)
        ys[...] = jnp.zeros_like(ys)

    start = seg_ref[i, g]
    end = start + seg_ref[i, N_EXPERT_GROUPS + g]
    wd = wd_ref[...].reshape(EXPERTS_PER_GROUP * D_EXPERT, wd_ref.shape[2])

    def window(w0, rows, lo, hi):
        sl = pl.ds(pl.multiple_of(w0, ROW_ALIGN), rows)
        xw = xs[sl, :]
        cw = cws[sl, :]
        r = w0 + lax.broadcasted_iota(jnp.int32, cw.shape, 0)
        lane = lax.broadcasted_iota(jnp.int32, cw.shape, 1)
        piece_lane = jnp.where(lane < N_PIECES * PIECE_STRIDE, lane & (PIECE_STRIDE - 1), -1)
        cw = jnp.where(r >= lo, cw, 0.0)
        cw = jnp.where(r < hi, cw, 0.0)
        acts = []
        for e in range(EXPERTS_PER_GROUP):
            ce = jnp.sum(jnp.where(piece_lane == g * EXPERTS_PER_GROUP + e, cw, 0.0), axis=1, keepdims=True)
            a = _silu_tanh(jnp.dot(xw, wg_ref[e], preferred_element_type=F32)) \
                * jnp.dot(xw, wu_ref[e], preferred_element_type=F32)
            acts.append((a * ce).astype(BF16))
        ys[sl, :] += jnp.dot(jnp.concatenate(acts, axis=1), wd, preferred_element_type=F32)

    w0 = jnp.minimum(start & -ROW_ALIGN, t - MOE_WINDOW)
    covered = w0 + MOE_WINDOW
    window(w0, MOE_WINDOW, start, jnp.minimum(end, covered))

    def extra(k, carry):
        lo = covered + k * MOE_EXTRA
        window(jnp.minimum(lo, t - MOE_EXTRA), MOE_EXTRA, lo, jnp.minimum(end, lo + MOE_EXTRA))
        return carry

    n_extra = jnp.maximum(end - covered + MOE_EXTRA - 1, 0) >> (MOE_EXTRA.bit_length() - 1)
    lax.fori_loop(0, n_extra, extra, 0)

    @pl.when(g == N_EXPERT_GROUPS - 1)
    def _():
        dest_col = rt_ref[:, DEST_LANE:DEST_LANE + 1]
        unperm = jnp.where(lax.broadcasted_iota(jnp.int32, (t, t), 1).astype(F32) == dest_col,
                           1.0, 0.0).astype(BF16)
        ff = jnp.dot(unperm, ys[...].astype(BF16), preferred_element_type=F32)
        xn = x_ref[...] + mod_ref[0, 5:6, :] * ff
        if final_norm:
            ms = jnp.mean(xn * xn, axis=-1, keepdims=True)
            xn = xn * lax.rsqrt(ms + EPS) * ng_ref[...]
        else:
            hn_ref[...] = _modulate(xn, ng_ref[...], nmod_ref[0, 0:1, :], nmod_ref[0, 1:2, :]).astype(BF16)
        o_ref[...] = xn


def _moe_call(h2, table, aux, wg, wu, wd, layer, x, mod, norm_g, next_mod, tokens_per_batch, final_norm):
    n, d = x.shape
    tile = MOE_TILE
    per_b = tokens_per_batch // tile
    seg = aux[:, 1, :2 * N_EXPERT_GROUPS].astype(jnp.int32)
    row = lambda i, g, seg: (i, 0)
    grp = lambda i, g, seg: (layer, g, 0, 0)
    tok = pl.BlockSpec((tile, d), row)
    x_out = jax.ShapeDtypeStruct((n, d), F32)
    grid_spec = pltpu.PrefetchScalarGridSpec(
        num_scalar_prefetch=1,
        grid=(n // tile, N_EXPERT_GROUPS),
        in_specs=[
            pl.BlockSpec((tile, d), row),
            pl.BlockSpec((tile, LANES), row),
            pl.BlockSpec((1, AUX_ROWS, tile), lambda i, g, seg: (i, 0, 0)),
            pl.BlockSpec((None, EXPERTS_PER_GROUP, d, D_EXPERT), grp),
            pl.BlockSpec((None, EXPERTS_PER_GROUP, d, D_EXPERT), grp),
            pl.BlockSpec((None, EXPERTS_PER_GROUP, D_EXPERT, d), grp),
            pl.BlockSpec((tile, d), row),
            pl.BlockSpec((1, N_MOD, d), lambda i, g, seg: (i // per_b, 0, 0)),
            pl.BlockSpec((1, d), lambda i, g, seg: (0, 0)),
            pl.BlockSpec((1, N_MOD, d), lambda i, g, seg: (i // per_b, 0, 0)),
        ],
        out_specs=tok if final_norm else [tok, tok],
        scratch_shapes=[
            pltpu.VMEM((tile, d), BF16),
            pltpu.VMEM((tile, LANES), F32),
            pltpu.VMEM((tile, d), F32),
        ],
    )
    return pl.pallas_call(
        functools.partial(_moe_kernel, final_norm=final_norm),
        grid_spec=grid_spec,
        out_shape=x_out if final_norm else [x_out, jax.ShapeDtypeStruct((n, d), BF16)],
        compiler_params=_cparams(("parallel", "arbitrary")),
        name="moe_ffn",
    )(seg, h2, table, aux, wg, wu, wd, x, mod, norm_g.reshape(1, d), next_mod)


def kernel(x, c, ctx, c_ctx, w_mod, b_mod, norm1_g, norm2_g, hg_w_in, hg_lb_fwd, hg_lb_bwd, hg_gnorm,
           hg_w_out, pool_w_in, pool_w_grp, pool_scale, pool_w_out, router_w, router_b, moe_w_gate,
           moe_w_up, moe_w_down, final_g):
    b, l, d = x.shape
    depth = w_mod.shape[0]
    n_mixers = 2

    cc = jnp.concatenate([c, c_ctx[None, :], jnp.zeros((MOD_ROWS - b - 1, d), F32)], axis=0)
    mods = _mod_call(cc, w_mod, b_mod)

    experts = None
    x_lat = x.reshape(b * l, d)
    h_lat = None
    for i in range(depth):
        slot = i // n_mixers
        mod_lat = mods[i, :b].reshape(b, N_MOD, d)
        if i % n_mixers == 0:
            h_lat, g_lat = _norm_call(x_lat.reshape(b, l, d), norm1_g[i], mod_lat, 1024,
                                      hg_w_in, slot, N_HG_PROJ - 1)
            mod_ctx = jnp.broadcast_to(mods[i, b].reshape(1, N_MOD, d), (b, N_MOD, d))
            h_ctx = _norm_call(ctx, norm1_g[i], mod_ctx, ctx.shape[1])
            to_cast = (moe_w_gate, moe_w_up, moe_w_down) if experts is None else ()
            y, cast = _hgrn_call(h_ctx, h_lat, g_lat, hg_w_in, hg_lb_fwd, hg_lb_bwd, hg_gnorm[slot], slot, to_cast)
            experts = experts or tuple(cast)
            w_out = hg_w_out
        else:
            if h_lat is None:
                h_lat = _norm_call(x_lat.reshape(b, l, d), norm1_g[i], mod_lat, 512)
            y = _pool_call(h_lat, pool_w_in, pool_w_grp, pool_scale, slot)
            w_out = pool_w_out
        x_lat, h2, table, aux = _post_call(x_lat, y.reshape(b * l, d), w_out, slot, mod_lat,
                                           norm2_g[i], router_w, router_b, l)
        if experts is None:
            experts = (moe_w_gate.astype(BF16), moe_w_up.astype(BF16), moe_w_down.astype(BF16))
        if i == depth - 1:
            x_lat = _moe_call(h2, table, aux, *experts, i, x_lat, mod_lat, final_g, mod_lat, l, final_norm=True)
        else:
            mod_next = mods[i + 1, :b].reshape(b, N_MOD, d)
            x_lat, h_next = _moe_call(h2, table, aux, *experts, i, x_lat, mod_lat, norm1_g[i + 1], mod_next, l,
                                      final_norm=False)
            h_lat = h_next.reshape(b, l, d) if (i + 1) % n_mixers != 0 else None
    return x_lat.reshape(b, l, d)
```

```python
import functools

import jax
import jax.numpy as jnp
from jax import lax
from jax.experimental import pallas as pl
from jax.experimental.pallas import tpu as pltpu

F32 = jnp.float32
BF16 = jnp.bfloat16

EPS = 1e-6
N_MOD = 6
HG_HEADS = 8
HG_DK = 128
HG_CHUNK = 64
N_HG_PROJ = 5
POOL_WINDOWS = (2, 4, 8, 16)
GRID_W = 64
GRID_SHIFT = GRID_W.bit_length() - 1
assert 1 << GRID_SHIFT == GRID_W
N_EXPERTS = 16
N_EXPERT_GROUPS = 4
EXPERTS_PER_GROUP = N_EXPERTS // N_EXPERT_GROUPS
D_EXPERT = 256

LANES = 128
MOD_ROWS = 8
VMEM_LIMIT = 56 * 1024 * 1024
HGRN_VMEM_LIMIT = 60 * 1024 * 1024


def _cparams(sem, vmem=VMEM_LIMIT):
    return pltpu.CompilerParams(dimension_semantics=sem, vmem_limit_bytes=vmem)


def _sigmoid_pair(z):
    e = jnp.exp(-jnp.abs(z))
    r = 1.0 / (1.0 + e)
    er = e * r
    pos = z >= 0
    return jnp.where(pos, r, er), jnp.where(pos, er, r)


def _silu(z):
    return z * _sigmoid_pair(z)[0]


def _sigmoid_tanh(z):
    return 0.5 * jnp.tanh(0.5 * z) + 0.5


def _silu_tanh(z):
    return z * _sigmoid_tanh(z)


def _split_bf16(x):
    hi = x.astype(BF16)
    return hi, (x - hi.astype(F32)).astype(BF16)


def _mod_kernel(c_ref, w_ref, b_ref, o_ref):
    a = jnp.concatenate(_split_bf16(_silu(c_ref[...])), axis=0)
    w_hi, w_lo = _split_bf16(w_ref[0])
    acc = jnp.dot(a, w_hi, preferred_element_type=F32) + jnp.dot(a, w_lo, preferred_element_type=F32)
    o_ref[0] = acc[0:MOD_ROWS] + acc[MOD_ROWS:2 * MOD_ROWS] + b_ref[0]


def _mod_call(cc, w_mod, b_mod):
    depth, d, n = w_mod.shape
    tn = 1024
    return pl.pallas_call(
        _mod_kernel,
        grid=(depth, n // tn),
        in_specs=[
            pl.BlockSpec((MOD_ROWS, d), lambda i, j: (0, 0)),
            pl.BlockSpec((1, d, tn), lambda i, j: (i, 0, j)),
            pl.BlockSpec((1, 1, tn), lambda i, j: (i, 0, j)),
        ],
        out_specs=pl.BlockSpec((1, MOD_ROWS, tn), lambda i, j: (i, 0, j)),
        out_shape=jax.ShapeDtypeStruct((depth, MOD_ROWS, n), F32),
        compiler_params=_cparams(("parallel", "parallel")),
        name="mod_proj",
    )(cc, w_mod, b_mod.reshape(depth, 1, n))


def _modulate(x, g, shift, scale):
    ms = jnp.mean(x * x, axis=-1, keepdims=True)
    return (x * lax.rsqrt(ms + EPS) * g) * (1.0 + scale) + shift


def _norm_kernel(x_ref, g_ref, mod_ref, *rest):
    h = _modulate(x_ref[0], g_ref[...], mod_ref[0, 0:1, :], mod_ref[0, 1:2, :]).astype(BF16)
    if len(rest) == 1:
        (o_ref,) = rest
    else:
        w_ref, o_ref, p_ref, wbuf = rest

        @pl.when((pl.program_id(0) == 0) & (pl.program_id(1) == 0))
        def _():
            wbuf[...] = w_ref[...].astype(BF16)

        p_ref[0] = jnp.dot(h, wbuf[...], preferred_element_type=F32).astype(BF16)
    o_ref[0] = h


def _norm_call(x, g, mod, tile, w_proj=None, w_slot=0, w_col=0):
    b, t, d = x.shape
    tok = pl.BlockSpec((1, tile, d), lambda i, j: (i, j, 0))
    in_specs = [tok, pl.BlockSpec((1, d), lambda i, j: (0, 0)),
                pl.BlockSpec((1, N_MOD, d), lambda i, j: (i, 0, 0))]
    args = [x, g.reshape(1, d), mod]
    out_specs, out_shape = tok, jax.ShapeDtypeStruct((b, t, d), BF16)
    scratch = []
    if w_proj is not None:
        in_specs.append(pl.BlockSpec((None, d, d), lambda i, j: (w_slot, 0, w_col)))
        args.append(w_proj)
        out_specs, out_shape = [tok, tok], [out_shape, out_shape]
        scratch = [pltpu.VMEM((d, d), BF16)]
    return pl.pallas_call(
        _norm_kernel,
        grid=(b, t // tile),
        in_specs=in_specs,
        out_specs=out_specs,
        out_shape=out_shape,
        scratch_shapes=scratch,
        compiler_params=_cparams(("arbitrary", "arbitrary")),
        name="norm1",
    )(*args)


HG_BLOCK = 256
HG_CPB = HG_BLOCK // HG_CHUNK


def _chunk_prefix(x, row):
    for d in (1, 2, 4, 8, 16, 32):
        x = x + jnp.where(row >= d, pltpu.roll(x, d, axis=0), 0.0)
    return x


def _chunk_suffix(x, row):
    n = x.shape[0]
    for d in (1, 2, 4, 8, 16, 32):
        x = x + jnp.where(row < HG_CHUNK - d, pltpu.roll(x, n - d, axis=0), 0.0)
    return x


def _lower_bound(lb_ref, slot):
    rows = [lb_ref[j, 0] for j in range(lb_ref.shape[0])]
    m = functools.reduce(jnp.maximum, rows)
    es = [jnp.exp(r - m) for r in rows]
    return sum(es[:slot + 1]) / sum(es)


N_HGRN_IN = 10


def _hgrn_kernel(*refs, slot, n_ctx_chunks, n_lat_chunks, n_cast):
    hc_ref, hl_ref, g_ref, wq_ref, wv_ref, wf_ref, wb_ref, lbf_ref, lbb_ref, gn_ref = refs[:N_HGRN_IN]
    cast_in = refs[N_HGRN_IN:N_HGRN_IN + n_cast]
    y_ref = refs[N_HGRN_IN + n_cast]
    cast_out = refs[N_HGRN_IN + n_cast + 1:N_HGRN_IN + 2 * n_cast + 1]
    (wbuf, pbuf0, pbuf1, pbuf2, pbuf3, oacc, qif, qib,
     kvf, kvb, decf, decb, spf, spb) = refs[N_HGRN_IN + 2 * n_cast + 1:]
    for c_ref, o_ref in zip(cast_in, cast_out):
        o_ref[...] = c_ref[...].astype(BF16)

    lb_f = _lower_bound(lbf_ref, slot)
    lb_b = _lower_bound(lbb_ref, slot)
    row = lax.broadcasted_iota(jnp.int32, (HG_BLOCK, HG_DK), 0) & (HG_CHUNK - 1)
    ci = lax.broadcasted_iota(jnp.int32, (HG_CPB, HG_CHUNK, HG_CHUNK), 1)
    si = lax.broadcasted_iota(jnp.int32, (HG_CPB, HG_CHUNK, HG_CHUNK), 2)

    def c3(t):
        return t.reshape(HG_CPB, HG_CHUNK, HG_DK)

    def direction(qs3, v3b, z, lb, fwd):
        sig = _sigmoid_tanh(z)
        f = lb + (1.0 - lb) * sig
        k3 = c3((1.0 - lb) * (1.0 - sig))
        lf = jnp.log(f)
        if fwd:
            cum = c3(_chunk_prefix(lf, row))
            ref = cum[:, HG_CHUNK // 2 - 1:HG_CHUNK // 2, :]
            last = cum[:, HG_CHUNK - 1:HG_CHUNK, :]
        else:
            cum = c3(_chunk_suffix(lf, row))
            ref = cum[:, HG_CHUNK // 2:HG_CHUNK // 2 + 1, :]
            last = cum[:, 0:1, :]
        dec = jnp.exp(last)
        if qs3 is None:
            kl = k3 * jnp.exp(last - cum)
        else:
            e1 = jnp.exp(cum - ref)
            qd = qs3 * e1
            qi = qd * jnp.exp(ref)
            kd = k3 * (1.0 / e1)
            kl = kd * jnp.exp(last - ref)
        kvt = jnp.einsum('ncv,nck->nvk', v3b, kl.astype(BF16), preferred_element_type=F32)
        if qs3 is None:
            return None, None, kvt, dec
        sc = jnp.einsum('nck,nsk->ncs', qd.astype(BF16), kd.astype(BF16), preferred_element_type=F32)
        sc = jnp.where((ci >= si) if fwd else (ci <= si), sc, 0.0)
        intra = jnp.einsum('ncs,nsv->ncv', sc.astype(BF16), v3b, preferred_element_type=F32)
        return intra, qi, kvt, dec

    for p, wp_ref in enumerate((wq_ref, wv_ref, wf_ref, wb_ref)):
        wbuf[:, p * HG_DK:(p + 1) * HG_DK] = wp_ref[...].astype(BF16)

    def project(hrows):
        return jnp.dot(hrows, wbuf[...], preferred_element_type=F32)

    def block(p, chunk0, lat_row0):
        v3b = c3(p[:, HG_DK:2 * HG_DK]).astype(BF16)
        zf = p[:, 2 * HG_DK:3 * HG_DK]
        zb = p[:, 3 * HG_DK:4 * HG_DK]
        if lat_row0 is None:
            qs3 = None
        else:
            qs3 = c3(_silu_tanh(p[:, 0:HG_DK]))
        in_f, qi_f, kv_f, dec_f = direction(qs3, v3b, zf, lb_f, True)
        in_b, qi_b, kv_b, dec_b = direction(qs3, v3b, zb, lb_b, False)
        kvf[pl.ds(chunk0, HG_CPB)] = kv_f
        kvb[pl.ds(chunk0, HG_CPB)] = kv_b
        decf[pl.ds(chunk0, HG_CPB)] = dec_f
        decb[pl.ds(chunk0, HG_CPB)] = dec_b
        if lat_row0 is not None:
            rows = pl.ds(lat_row0, HG_BLOCK)
            oacc[rows, :] = (in_f + in_b).reshape(HG_BLOCK, HG_DK)
            qif[rows, :] = qi_f.reshape(HG_BLOCK, HG_DK).astype(BF16)
            qib[rows, :] = qi_b.reshape(HG_BLOCK, HG_DK).astype(BF16)

    n_blocks = n_lat_chunks // HG_CPB

    def lat_rows(i):
        return hl_ref[0, pl.ds(pl.multiple_of(i * HG_BLOCK, HG_BLOCK), HG_BLOCK), :]

    def lat_terms(p_ref, i):
        block(p_ref[...], n_ctx_chunks + i * HG_CPB, pl.multiple_of(i * HG_BLOCK, HG_BLOCK))

    ctx_p = [project(hc_ref[0, i * HG_BLOCK:(i + 1) * HG_BLOCK, :]) for i in range(n_ctx_chunks // HG_CPB)]
    pbuf0[...] = project(lat_rows(0))
    for i, p in enumerate(ctx_p):
        block(p, i * HG_CPB, None)
    pbuf1[...] = project(lat_rows(1))

    def lat_quad(i0, last):
        pbuf2[...] = project(lat_rows(i0 + 2))
        lat_terms(pbuf0, i0)
        pbuf3[...] = project(lat_rows(i0 + 3))
        lat_terms(pbuf1, i0 + 1)
        if not last:
            pbuf0[...] = project(lat_rows(i0 + 4))
        lat_terms(pbuf2, i0 + 2)
        if not last:
            pbuf1[...] = project(lat_rows(i0 + 5))
        lat_terms(pbuf3, i0 + 3)

    def lat_body(t, carry):
        lat_quad(4 * t, False)
        return carry

    lax.fori_loop(0, n_blocks // 4 - 1, lat_body, 0)
    lat_quad(n_blocks - 4, True)

    def advance(s, kv_ref, dec_ref, n):
        return dec_ref[n] * s + kv_ref[n]

    sf = jnp.zeros((HG_DK, HG_DK), F32)
    for n in range(n_ctx_chunks):
        sf = advance(sf, kvf, decf, n)
    sb = jnp.zeros((HG_DK, HG_DK), F32)
    for n in reversed(range(n_ctx_chunks)):
        sb = advance(sb, kvb, decb, n)

    def scan_step(t, carry):
        sf, sb = carry
        jb = n_lat_chunks - 1 - t
        spf[t] = sf.astype(BF16)
        spb[jb] = sb.astype(BF16)
        return (advance(sf, kvf, decf, n_ctx_chunks + t), advance(sb, kvb, decb, n_ctx_chunks + jb))

    lax.fori_loop(0, n_lat_chunks, scan_step, (sf, sb), unroll=2)

    gn = gn_ref[...]

    def block_rows(i):
        return pl.ds(pl.multiple_of(i * HG_BLOCK, HG_BLOCK), HG_BLOCK)

    def inter(i):
        rows = block_rows(i)
        chunks = pl.ds(i * HG_CPB, HG_CPB)
        return (jnp.einsum('nck,nvk->ncv', c3(qif[rows, :]), spf[chunks], preferred_element_type=F32)
                + jnp.einsum('nck,nvk->ncv', c3(qib[rows, :]), spb[chunks], preferred_element_type=F32)
                ).reshape(HG_BLOCK, HG_DK)

    def readout(o_ref, i):
        rows = block_rows(i)
        o = oacc[rows, :] + o_ref[...]
        ms = jnp.mean(o * o, axis=-1, keepdims=True)
        o = o * lax.rsqrt(ms + EPS) * gn
        y_ref[0, rows, :] = (o * _silu_tanh(g_ref[0, rows, :].astype(F32))).astype(BF16)

    obuf0, obuf1 = pbuf0.at[:, 0:HG_DK], pbuf1.at[:, 0:HG_DK]
    obuf0[...] = inter(0)

    def readout_pair(t, carry):
        obuf1[...] = inter(2 * t + 1)
        readout(obuf0, 2 * t)
        obuf0[...] = inter(2 * t + 2)
        readout(obuf1, 2 * t + 1)
        return carry

    lax.fori_loop(0, n_blocks // 2 - 1, readout_pair, 0)
    obuf1[...] = inter(n_blocks - 1)
    readout(obuf0, n_blocks - 2)
    readout(obuf1, n_blocks - 1)


def _hgrn_call(h_ctx, h_lat, g_lat, w_in, lb_fwd, lb_bwd, gnorm, slot, to_cast):
    b, lc, d = h_ctx.shape
    ll = h_lat.shape[1]
    nrow = lb_fwd.shape[0]
    ncc, nlc = lc // HG_CHUNK, ll // HG_CHUNK
    kern = functools.partial(_hgrn_kernel, slot=slot, n_ctx_chunks=ncc, n_lat_chunks=nlc, n_cast=len(to_cast))
    lb_spec = pl.BlockSpec((nrow, 1, 1, HG_DK), lambda i, h: (0, h, 0, 0))
    head_cols = pl.BlockSpec((1, ll, HG_DK), lambda i, h: (i, 0, h))
    n_rec = N_HG_PROJ - 1
    w_cols = [pl.BlockSpec((None, d, HG_DK), functools.partial(lambda i, h, p: (slot, 0, p * HG_HEADS + h), p=p))
              for p in range(n_rec)]
    steps = b * HG_HEADS
    slabs = [a.reshape(steps, -1, *a.shape[-2:]) for a in to_cast]
    slab_specs = [pl.BlockSpec((1,) + a.shape[1:], lambda i, h: (i * HG_HEADS + h, 0, 0, 0)) for a in slabs]
    outs = pl.pallas_call(
        kern,
        grid=(b, HG_HEADS),
        in_specs=[
            pl.BlockSpec((1, lc, d), lambda i, h: (i, 0, 0)),
            pl.BlockSpec((1, ll, d), lambda i, h: (i, 0, 0)),
            head_cols,
            *w_cols,
            lb_spec, lb_spec,
            pl.BlockSpec((1, HG_DK), lambda i, h: (0, 0)),
            *slab_specs,
        ],
        out_specs=[head_cols, *slab_specs],
        out_shape=[jax.ShapeDtypeStruct((b, ll, d), BF16)]
        + [jax.ShapeDtypeStruct(a.shape, BF16) for a in slabs],
        scratch_shapes=[
            pltpu.VMEM((d, n_rec * HG_DK), BF16),
            pltpu.VMEM((HG_BLOCK, n_rec * HG_DK), F32),
            pltpu.VMEM((HG_BLOCK, n_rec * HG_DK), F32),
            pltpu.VMEM((HG_BLOCK, n_rec * HG_DK), F32),
            pltpu.VMEM((HG_BLOCK, n_rec * HG_DK), F32),
            pltpu.VMEM((ll, HG_DK), F32),
            pltpu.VMEM((ll, HG_DK), BF16),
            pltpu.VMEM((ll, HG_DK), BF16),
            pltpu.VMEM((ncc + nlc, HG_DK, HG_DK), F32),
            pltpu.VMEM((ncc + nlc, HG_DK, HG_DK), F32),
            pltpu.VMEM((ncc + nlc, 1, HG_DK), F32),
            pltpu.VMEM((ncc + nlc, 1, HG_DK), F32),
            pltpu.VMEM((nlc, HG_DK, HG_DK), BF16),
            pltpu.VMEM((nlc, HG_DK, HG_DK), BF16),
        ],
        compiler_params=_cparams(("parallel", "arbitrary"), HGRN_VMEM_LIMIT),
        name="hgrn2",
    )(h_ctx, h_lat, g_lat, *([w_in] * n_rec),
      lb_fwd.reshape(nrow, HG_HEADS, 1, HG_DK), lb_bwd.reshape(nrow, HG_HEADS, 1, HG_DK),
      gnorm.reshape(1, HG_DK), *slabs)
    return outs[0], [o.reshape(a.shape) for o, a in zip(outs[1:], to_cast)]


POOL_STRIP = 8
POOL_PAD = 8
POOL_BAND = 256


def _pool_group(h_ref, win_ref, wg_ref, ps_ref, o_ref, upad, *, win, rows):
    gd = win_ref.shape[1]
    half = win // 2
    u = jnp.dot(h_ref[0], win_ref[...].astype(BF16), preferred_element_type=F32)
    zeros = jnp.zeros((POOL_PAD, GRID_W, gd), F32)
    upad[0:POOL_PAD] = zeros
    upad[POOL_PAD + rows:POOL_PAD + rows + POOL_PAD] = zeros
    upad[POOL_PAD:POOL_PAD + rows] = u.reshape(rows, GRID_W, gd)

    tok = POOL_STRIP * GRID_W
    bi = lax.broadcasted_iota(jnp.int32, (POOL_BAND, POOL_BAND), 0)
    bj = lax.broadcasted_iota(jnp.int32, (POOL_BAND, POOL_BAND), 1)
    lo_c = (bi & (GRID_W - 1)) - half
    cj = bj & (GRID_W - 1)
    band = jnp.where(bi >> GRID_SHIFT == bj >> GRID_SHIFT, 1.0, 0.0)
    band = jnp.where(cj >= lo_c, band, 0.0)
    band = jnp.where(cj < lo_c + win, band, 0.0).astype(BF16)
    t = lax.broadcasted_iota(jnp.int32, (tok, LANES), 0)
    col = t & (GRID_W - 1)
    cnt_c = jnp.minimum(col - half + win, GRID_W) - jnp.maximum(col - half, 0)
    wg = wg_ref[0].astype(BF16)
    ps = ps_ref[...]

    def strip(i, carry):
        r0 = i * POOL_STRIP
        slab = upad[pl.ds(r0 + POOL_PAD - half, POOL_STRIP + win - 1)]
        span = 1
        while span < win:
            n = slab.shape[0] - span
            slab = slab[0:n] + slab[span:span + n]
            span *= 2
        rs = slab.reshape(tok, gd)
        hi = rs.astype(BF16)
        lo = (rs - hi.astype(F32)).astype(BF16)
        parts = []
        for k in range(tok // POOL_BAND):
            sl = slice(k * POOL_BAND, (k + 1) * POOL_BAND)
            parts.append(jnp.dot(band, hi[sl], preferred_element_type=F32)
                         + jnp.dot(band, lo[sl], preferred_element_type=F32))
        box = jnp.concatenate(parts, axis=0)
        r = r0 + (t >> GRID_SHIFT)
        cnt_r = jnp.minimum(r - half + win, rows) - jnp.maximum(r - half, 0)
        inv = 1.0 / (cnt_r * cnt_c).astype(F32)
        mean = box * jnp.concatenate([inv] * (gd // LANES), axis=1)
        ug = upad[pl.ds(r0 + POOL_PAD, POOL_STRIP)].reshape(tok, gd)
        z = jnp.dot((mean - ug).astype(BF16), wg, preferred_element_type=F32) * ps
        o_ref[0, pl.ds(pl.multiple_of(i * tok, tok), tok), :] = z.astype(BF16)
        return carry

    lax.fori_loop(0, rows // POOL_STRIP, strip, 0, unroll=2)


def _pool_kernel(h_ref, win_ref, wg_ref, ps_ref, o_ref, upad, *, rows):
    g = pl.program_id(1)
    for gi, win in enumerate(POOL_WINDOWS):
        @pl.when(g == gi)
        def _(win=win):
            _pool_group(h_ref, win_ref, wg_ref, ps_ref, o_ref, upad, win=win, rows=rows)


def _pool_call(h, w_in, w_grp, p_scale, slot):
    b, l, d = h.shape
    ng = len(POOL_WINDOWS)
    gd = d // ng
    rows = l // GRID_W
    return pl.pallas_call(
        functools.partial(_pool_kernel, rows=rows),
        grid=(b, ng),
        in_specs=[
            pl.BlockSpec((1, l, d), lambda i, g: (i, 0, 0)),
            pl.BlockSpec((None, d, gd), lambda i, g: (slot, 0, g)),
            pl.BlockSpec((None, 1, gd, gd), lambda i, g: (slot, g, 0, 0)),
            pl.BlockSpec((None, 1, gd), lambda i, g: (slot, 0, g)),
        ],
        out_specs=pl.BlockSpec((1, l, gd), lambda i, g: (i, 0, g)),
        out_shape=jax.ShapeDtypeStruct((b, l, d), BF16),
        scratch_shapes=[pltpu.VMEM((rows + 2 * POOL_PAD, GRID_W, gd), F32)],
        compiler_params=_cparams(("parallel", "arbitrary")),
        name="pool_mix",
    )(h, w_in, w_grp, p_scale.reshape(p_scale.shape[0], 1, d))


MOE_TILE = 1024
MOE_WINDOWS = (256, 272, 288, 320)
MOE_EXTRA = 64
ROW_ALIGN = 16
DEST_LANE = N_EXPERTS
PIECE_STRIDE = 32
N_PIECES = 3
AUX_ROWS = 8


def _route(sel, s):
    keep = []
    gsum = []
    for g in range(N_EXPERT_GROUPS):
        a = sel[g * EXPERTS_PER_GROUP:(g + 1) * EXPERTS_PER_GROUP]
        beaten = [jnp.zeros_like(a[0]) for _ in a]
        for i in range(EXPERTS_PER_GROUP):
            for j in range(i + 1, EXPERTS_PER_GROUP):
                ge = jnp.where(a[i] >= a[j], 1.0, 0.0)
                beaten[j] = beaten[j] + ge
                beaten[i] = beaten[i] + (1.0 - ge)
        kg = [jnp.where(bt < 1.5, 1.0, 0.0) for bt in beaten]
        keep.append(kg)
        gsum.append(sum(k * x for k, x in zip(kg, a)))
    picked = []
    bests = []
    for g in range(N_EXPERT_GROUPS):
        better = jnp.zeros_like(gsum[0])
        for o in range(N_EXPERT_GROUPS):
            if o < g:
                better = better + jnp.where(gsum[o] >= gsum[g], 1.0, 0.0)
            elif o > g:
                better = better + jnp.where(gsum[o] > gsum[g], 1.0, 0.0)
        best = jnp.where(better < 0.5, 1.0, 0.0)
        bests.append(best)
        for i in range(EXPERTS_PER_GROUP):
            picked.append(best * keep[g][i] * s[g * EXPERTS_PER_GROUP + i])
    den = sum(picked)
    return [p / den for p in picked], bests


def _post_kernel(x_ref, y_ref, w_ref, mod_ref, g_ref, rw_ref, rb_ref,
                 xo_ref, h2_ref, rt_ref, aux_ref, wbuf):
    @pl.when(pl.program_id(0) == 0)
    def _():
        wbuf[...] = w_ref[...].astype(BF16)

    yw = jnp.dot(y_ref[...], wbuf[...], preferred_element_type=F32)
    xn = x_ref[...] + mod_ref[0, 2:3, :] * yw
    xo_ref[...] = xn
    h2 = _modulate(xn, g_ref[...], mod_ref[0, 3:4, :], mod_ref[0, 4:5, :])
    hi = h2.astype(BF16)
    h2_ref[...] = hi
    lo = (h2 - hi.astype(F32)).astype(BF16)
    rw = rw_ref[...]
    half = hi.shape[0] // 2
    prod = jnp.concatenate(
        [jnp.dot(hi[r:r + half], rw, preferred_element_type=F32)
         + jnp.dot(lo[r:r + half], rw, preferred_element_type=F32) for r in (0, half)], axis=0).T
    logits = prod[0:N_EXPERTS, :] + prod[N_EXPERTS:2 * N_EXPERTS, :]
    s = _sigmoid_pair(logits)[0]
    sel = s + rb_ref[...]
    comb, bests = _route([sel[e:e + 1, :] for e in range(N_EXPERTS)],
                         [s[e:e + 1, :] for e in range(N_EXPERTS)])
    t = logits.shape[1]

    ind = jnp.concatenate(bests + [jnp.zeros((AUX_ROWS - N_EXPERT_GROUPS, t), F32)], axis=0)
    n_seg = t // LANES
    stacked = jnp.concatenate([ind[:, j * LANES:(j + 1) * LANES] for j in range(n_seg)], axis=0)
    upper = jnp.where(lax.broadcasted_iota(jnp.int32, (LANES, LANES), 0)
                      <= lax.broadcasted_iota(jnp.int32, (LANES, LANES), 1), 1.0, 0.0).astype(BF16)
    local = jnp.dot(stacked.astype(BF16), upper, preferred_element_type=F32)
    off = jnp.zeros((AUX_ROWS, 1), F32)
    pieces = []
    for j in range(n_seg):
        seg_cum = local[j * AUX_ROWS:(j + 1) * AUX_ROWS, :]
        pieces.append(seg_cum + off)
        off = off + seg_cum[:, LANES - 1:LANES]
    cum = jnp.concatenate(pieces, axis=1)
    counts = [off[g:g + 1, :] for g in range(N_EXPERT_GROUPS)]
    starts = [jnp.zeros((1, 1), F32)]
    for g in range(1, N_EXPERT_GROUPS):
        starts.append(starts[-1] + counts[g - 1])
    dest = sum(bests[g] * (starts[g] + cum[g:g + 1, :] - 1.0) for g in range(N_EXPERT_GROUPS))

    lane = lax.broadcasted_iota(jnp.int32, (1, t), 1)
    seg = sum(jnp.where(lane == k, v, 0.0) for k, v in enumerate(starts + counts))
    aux_ref[0] = jnp.concatenate([dest, seg, jnp.zeros((AUX_ROWS - 2, t), F32)], axis=0)
    p1 = [w.astype(BF16).astype(F32) for w in comb]
    r1 = [w - p for w, p in zip(comb, p1)]
    p2 = [r.astype(BF16).astype(F32) for r in r1]
    p3 = [r - p for r, p in zip(r1, p2)]
    pad = [jnp.zeros((PIECE_STRIDE - N_EXPERTS, t), F32)]
    table = jnp.concatenate(p1 + [dest] + [jnp.zeros((PIECE_STRIDE - N_EXPERTS - 1, t), F32)] + p2 + pad + p3 + pad
                            + [jnp.zeros((LANES - N_PIECES * PIECE_STRIDE, t), F32)], axis=0)
    rt_ref[...] = table.T


def _post_call(x, y, w, w_slot, mod, g, router_w, router_b, tokens_per_batch, tile=MOE_TILE):
    n, d = x.shape
    per_b = tokens_per_batch // tile
    rwh = router_w.astype(BF16)
    rwl = (router_w - rwh.astype(F32)).astype(BF16)
    rw = jnp.concatenate([rwh, rwl, jnp.zeros((d, LANES - 2 * N_EXPERTS), BF16)], axis=1)
    row = lambda i: (i, 0)
    fixed = lambda i: (0, 0)
    return pl.pallas_call(
        _post_kernel,
        grid=(n // tile,),
        in_specs=[
            pl.BlockSpec((tile, d), row),
            pl.BlockSpec((tile, d), row),
            pl.BlockSpec((None, d, d), lambda i: (w_slot, 0, 0)),
            pl.BlockSpec((1, N_MOD, d), lambda i: (i // per_b, 0, 0)),
            pl.BlockSpec((1, d), fixed),
            pl.BlockSpec((d, LANES), fixed),
            pl.BlockSpec((N_EXPERTS, 1), fixed),
        ],
        out_specs=[
            pl.BlockSpec((tile, d), row),
            pl.BlockSpec((tile, d), row),
            pl.BlockSpec((tile, LANES), row),
            pl.BlockSpec((1, AUX_ROWS, tile), lambda i: (i, 0, 0)),
        ],
        out_shape=[
            jax.ShapeDtypeStruct((n, d), F32),
            jax.ShapeDtypeStruct((n, d), BF16),
            jax.ShapeDtypeStruct((n, LANES), F32),
            jax.ShapeDtypeStruct((n // tile, AUX_ROWS, tile), F32),
        ],
        scratch_shapes=[pltpu.VMEM((d, d), BF16)],
        compiler_params=_cparams(("arbitrary",)),
        name="mixer_out_router",
    )(x, y, w, mod, g.reshape(1, d), rw, router_b.reshape(N_EXPERTS, 1))


def _moe_kernel(seg_ref, h_ref, rt_ref, aux_ref, wg_ref, wu_ref, wd_ref, x_ref, mod_ref, ng_ref, nmod_ref,
                o_ref, *rest, final_norm):
    if final_norm:
        xs, cws, ys = rest
    else:
        hn_ref, xs, cws, ys = rest
    i = pl.program_id(0)
    g = pl.program_id(1)
    t = h_ref.shape[0]

    @pl.when(g == 0)
    def _():
        dest_row = aux_ref[0, 0:1, :]
        perm = jnp.where(lax.broadcasted_iota(jnp.int32, (t, t), 0).astype(F32) == dest_row,
                         1.0, 0.0).astype(BF16)
        xs[...] = jnp.dot(perm, h_ref[...], preferred_element_type=F32).astype(BF16)
        cws[...] = jnp.dot(perm, rt_ref[...].astype(BF16), preferred_element_type=F32)
        ys[...] = jnp.zeros_like(ys)

    start = seg_ref[i, g]
    end = start + seg_ref[i, N_EXPERT_GROUPS + g]
    wd = wd_ref[...].reshape(EXPERTS_PER_GROUP * D_EXPERT, wd_ref.shape[2])

    def window(w0, rows, lo, hi):
        sl = pl.ds(pl.multiple_of(w0, ROW_ALIGN), rows)
        xw = xs[sl, :]
        cw = cws[sl, :]
        r = w0 + lax.broadcasted_iota(jnp.int32, cw.shape, 0)
        lane = lax.broadcasted_iota(jnp.int32, cw.shape, 1)
        piece_lane = jnp.where(lane < N_PIECES * PIECE_STRIDE, lane & (PIECE_STRIDE - 1), -1)
        cw = jnp.where(r >= lo, cw, 0.0)
        cw = jnp.where(r < hi, cw, 0.0)
        acts = []
        for e in range(EXPERTS_PER_GROUP):
            ce = jnp.sum(jnp.where(piece_lane == g * EXPERTS_PER_GROUP + e, cw, 0.0), axis=1, keepdims=True)
            a = _silu_tanh(jnp.dot(xw, wg_ref[e], preferred_element_type=F32)) \
                * jnp.dot(xw, wu_ref[e], preferred_element_type=F32)
            acts.append((a * ce).astype(BF16))
        ys[sl, :] += jnp.dot(jnp.concatenate(acts, axis=1), wd, preferred_element_type=F32)

    aligned = start & -ROW_ALIGN
    fits = [end <= jnp.minimum(aligned, t - m) + m for m in MOE_WINDOWS[:-1]]
    taken = False
    for m, fit in zip(MOE_WINDOWS[:-1], fits):
        @pl.when(fit if taken is False else fit & ~taken)
        def _(m=m):
            window(jnp.minimum(aligned, t - m), m, start, end)
        taken = fit if taken is False else taken | fit

    @pl.when(~taken)
    def _():
        m = MOE_WINDOWS[-1]
        w0 = jnp.minimum(aligned, t - m)
        covered = w0 + m
        window(w0, m, start, jnp.minimum(end, covered))

        def extra(k, carry):
            lo = covered + k * MOE_EXTRA
            window(jnp.minimum(lo, t - MOE_EXTRA), MOE_EXTRA, lo, jnp.minimum(end, lo + MOE_EXTRA))
            return carry

        n_extra = jnp.maximum(end - covered + MOE_EXTRA - 1, 0) >> (MOE_EXTRA.bit_length() - 1)
        lax.fori_loop(0, n_extra, extra, 0)

    @pl.when(g == N_EXPERT_GROUPS - 1)
    def _():
        dest_col = rt_ref[:, DEST_LANE:DEST_LANE + 1]
        unperm = jnp.where(lax.broadcasted_iota(jnp.int32, (t, t), 1).astype(F32) == dest_col,
                           1.0, 0.0).astype(BF16)
        ff = jnp.dot(unperm, ys[...].astype(BF16), preferred_element_type=F32)
        xn = x_ref[...] + mod_ref[0, 5:6, :] * ff
        if final_norm:
            ms = jnp.mean(xn * xn, axis=-1, keepdims=True)
            xn = xn * lax.rsqrt(ms + EPS) * ng_ref[...]
        else:
            hn_ref[...] = _modulate(xn, ng_ref[...], nmod_ref[0, 0:1, :], nmod_ref[0, 1:2, :]).astype(BF16)
        o_ref[...] = xn


def _moe_call(h2, table, aux, wg, wu, wd, layer, x, mod, norm_g, next_mod, tokens_per_batch, final_norm):
    n, d = x.shape
    tile = MOE_TILE
    per_b = tokens_per_batch // tile
    seg = aux[:, 1, :2 * N_EXPERT_GROUPS].astype(jnp.int32)
    row = lambda i, g, seg: (i, 0)
    grp = lambda i, g, seg: (layer, g, 0, 0)
    tok = pl.BlockSpec((tile, d), row)
    x_out = jax.ShapeDtypeStruct((n, d), F32)
    grid_spec = pltpu.PrefetchScalarGridSpec(
        num_scalar_prefetch=1,
        grid=(n // tile, N_EXPERT_GROUPS),
        in_specs=[
            pl.BlockSpec((tile, d), row),
            pl.BlockSpec((tile, LANES), row),
            pl.BlockSpec((1, AUX_ROWS, tile), lambda i, g, seg: (i, 0, 0)),
            pl.BlockSpec((None, EXPERTS_PER_GROUP, d, D_EXPERT), grp),
            pl.BlockSpec((None, EXPERTS_PER_GROUP, d, D_EXPERT), grp),
            pl.BlockSpec((None, EXPERTS_PER_GROUP, D_EXPERT, d), grp),
            pl.BlockSpec((tile, d), row),
            pl.BlockSpec((1, N_MOD, d), lambda i, g, seg: (i // per_b, 0, 0)),
            pl.BlockSpec((1, d), lambda i, g, seg: (0, 0)),
            pl.BlockSpec((1, N_MOD, d), lambda i, g, seg: (i // per_b, 0, 0)),
        ],
        out_specs=tok if final_norm else [tok, tok],
        scratch_shapes=[
            pltpu.VMEM((tile, d), BF16),
            pltpu.VMEM((tile, LANES), F32),
            pltpu.VMEM((tile, d), F32),
        ],
    )
    return pl.pallas_call(
        functools.partial(_moe_kernel, final_norm=final_norm),
        grid_spec=grid_spec,
        out_shape=x_out if final_norm else [x_out, jax.ShapeDtypeStruct((n, d), BF16)],
        compiler_params=_cparams(("parallel", "arbitrary")),
        name="moe_ffn",
    )(seg, h2, table, aux, wg, wu, wd, x, mod, norm_g.reshape(1, d), next_mod)


def kernel(x, c, ctx, c_ctx, w_mod, b_mod, norm1_g, norm2_g, hg_w_in, hg_lb_fwd, hg_lb_bwd, hg_gnorm,
           hg_w_out, pool_w_in, pool_w_grp, pool_scale, pool_w_out, router_w, router_b, moe_w_gate,
           moe_w_up, moe_w_down, final_g):
    b, l, d = x.shape
    depth = w_mod.shape[0]
    n_mixers = 2

    cc = jnp.concatenate([c, c_ctx[None, :], jnp.zeros((MOD_ROWS - b - 1, d), F32)], axis=0)
    mods = _mod_call(cc, w_mod, b_mod)

    experts = None
    x_lat = x.reshape(b * l, d)
    h_lat = None
    for i in range(depth):
        slot = i // n_mixers
        mod_lat = mods[i, :b].reshape(b, N_MOD, d)
        if i % n_mixers == 0:
            h_lat, g_lat = _norm_call(x_lat.reshape(b, l, d), norm1_g[i], mod_lat, 1024,
                                      hg_w_in, slot, N_HG_PROJ - 1)
            mod_ctx = jnp.broadcast_to(mods[i, b].reshape(1, N_MOD, d), (b, N_MOD, d))
            h_ctx = _norm_call(ctx, norm1_g[i], mod_ctx, ctx.shape[1])
            to_cast = (moe_w_gate, moe_w_up, moe_w_down) if experts is None else ()
            y, cast = _hgrn_call(h_ctx, h_lat, g_lat, hg_w_in, hg_lb_fwd, hg_lb_bwd, hg_gnorm[slot], slot, to_cast)
            experts = experts or tuple(cast)
            w_out = hg_w_out
        else:
            if h_lat is None:
                h_lat = _norm_call(x_lat.reshape(b, l, d), norm1_g[i], mod_lat, 512)
            y = _pool_call(h_lat, pool_w_in, pool_w_grp, pool_scale, slot)
            w_out = pool_w_out
        x_lat, h2, table, aux = _post_call(x_lat, y.reshape(b * l, d), w_out, slot, mod_lat,
                                           norm2_g[i], router_w, router_b, l)
        if experts is None:
            experts = (moe_w_gate.astype(BF16), moe_w_up.astype(BF16), moe_w_down.astype(BF16))
        if i == depth - 1:
            x_lat = _moe_call(h2, table, aux, *experts, i, x_lat, mod_lat, final_g, mod_lat, l, final_norm=True)
        else:
            mod_next = mods[i + 1, :b].reshape(b, N_MOD, d)
            x_lat, h_next = _moe_call(h2, table, aux, *experts, i, x_lat, mod_lat, norm1_g[i + 1], mod_next, l,
                                      final_norm=False)
            h_lat = h_next.reshape(b, l, d) if (i + 1) % n_mixers != 0 else None
    return x_lat.reshape(b, l, d)
```

```python
import functools

import jax
import jax.numpy as jnp
from jax import lax
from jax.experimental import pallas as pl
from jax.experimental.pallas import tpu as pltpu

F32 = jnp.float32
BF16 = jnp.bfloat16

EPS = 1e-6
N_MOD = 6
HG_HEADS = 8
HG_DK = 128
HG_CHUNK = 64
N_HG_PROJ = 5
POOL_WINDOWS = (2, 4, 8, 16)
GRID_W = 64
GRID_SHIFT = GRID_W.bit_length() - 1
assert 1 << GRID_SHIFT == GRID_W
N_EXPERTS = 16
N_EXPERT_GROUPS = 4
EXPERTS_PER_GROUP = N_EXPERTS // N_EXPERT_GROUPS
D_EXPERT = 256

LANES = 128
MOD_ROWS = 8
VMEM_LIMIT = 56 * 1024 * 1024
HGRN_VMEM_LIMIT = 60 * 1024 * 1024
MOE_VMEM_LIMIT = 60 * 1024 * 1024


def _cparams(sem, vmem=VMEM_LIMIT):
    return pltpu.CompilerParams(dimension_semantics=sem, vmem_limit_bytes=vmem)


def _sigmoid_pair(z):
    e = jnp.exp(-jnp.abs(z))
    r = 1.0 / (1.0 + e)
    er = e * r
    pos = z >= 0
    return jnp.where(pos, r, er), jnp.where(pos, er, r)


def _silu(z):
    return z * _sigmoid_pair(z)[0]


def _sigmoid_tanh(z):
    return 0.5 * jnp.tanh(0.5 * z) + 0.5


def _silu_tanh(z):
    return z * _sigmoid_tanh(z)


def _split_bf16(x):
    hi = x.astype(BF16)
    return hi, (x - hi.astype(F32)).astype(BF16)


def _mod_kernel(c_ref, w_ref, b_ref, o_ref):
    a = jnp.concatenate(_split_bf16(_silu(c_ref[...])), axis=0)
    w_hi, w_lo = _split_bf16(w_ref[0])
    acc = jnp.dot(a, w_hi, preferred_element_type=F32) + jnp.dot(a, w_lo, preferred_element_type=F32)
    o_ref[0] = acc[0:MOD_ROWS] + acc[MOD_ROWS:2 * MOD_ROWS] + b_ref[0]


def _mod_call(cc, w_mod, b_mod):
    depth, d, n = w_mod.shape
    tn = 1024
    return pl.pallas_call(
        _mod_kernel,
        grid=(depth, n // tn),
        in_specs=[
            pl.BlockSpec((MOD_ROWS, d), lambda i, j: (0, 0)),
            pl.BlockSpec((1, d, tn), lambda i, j: (i, 0, j)),
            pl.BlockSpec((1, 1, tn), lambda i, j: (i, 0, j)),
        ],
        out_specs=pl.BlockSpec((1, MOD_ROWS, tn), lambda i, j: (i, 0, j)),
        out_shape=jax.ShapeDtypeStruct((depth, MOD_ROWS, n), F32),
        compiler_params=_cparams(("parallel", "parallel")),
        name="mod_proj",
    )(cc, w_mod, b_mod.reshape(depth, 1, n))


def _modulate(x, g, shift, scale):
    ms = jnp.mean(x * x, axis=-1, keepdims=True)
    return (x * lax.rsqrt(ms + EPS) * g) * (1.0 + scale) + shift


def _norm_kernel(x_ref, g_ref, mod_ref, *rest):
    h = _modulate(x_ref[0], g_ref[...], mod_ref[0, 0:1, :], mod_ref[0, 1:2, :]).astype(BF16)
    if len(rest) == 1:
        (o_ref,) = rest
    else:
        w_ref, o_ref, p_ref, wbuf = rest

        @pl.when((pl.program_id(0) == 0) & (pl.program_id(1) == 0))
        def _():
            wbuf[...] = w_ref[...].astype(BF16)

        p_ref[0] = jnp.dot(h, wbuf[...], preferred_element_type=F32).astype(BF16)
    o_ref[0] = h


def _norm_call(x, g, mod, tile, w_proj=None, w_slot=0, w_col=0):
    b, t, d = x.shape
    tok = pl.BlockSpec((1, tile, d), lambda i, j: (i, j, 0))
    in_specs = [tok, pl.BlockSpec((1, d), lambda i, j: (0, 0)),
                pl.BlockSpec((1, N_MOD, d), lambda i, j: (i, 0, 0))]
    args = [x, g.reshape(1, d), mod]
    out_specs, out_shape = tok, jax.ShapeDtypeStruct((b, t, d), BF16)
    scratch = []
    if w_proj is not None:
        in_specs.append(pl.BlockSpec((None, d, d), lambda i, j: (w_slot, 0, w_col)))
        args.append(w_proj)
        out_specs, out_shape = [tok, tok], [out_shape, out_shape]
        scratch = [pltpu.VMEM((d, d), BF16)]
    return pl.pallas_call(
        _norm_kernel,
        grid=(b, t // tile),
        in_specs=in_specs,
        out_specs=out_specs,
        out_shape=out_shape,
        scratch_shapes=scratch,
        compiler_params=_cparams(("arbitrary", "arbitrary")),
        name="norm1",
    )(*args)


HG_BLOCK = 256
HG_CPB = HG_BLOCK // HG_CHUNK


def _chunk_prefix(x, row):
    for d in (1, 2, 4, 8, 16, 32):
        x = x + jnp.where(row >= d, pltpu.roll(x, d, axis=0), 0.0)
    return x


def _chunk_suffix(x, row):
    n = x.shape[0]
    for d in (1, 2, 4, 8, 16, 32):
        x = x + jnp.where(row < HG_CHUNK - d, pltpu.roll(x, n - d, axis=0), 0.0)
    return x


def _lower_bound(lb_ref, slot):
    rows = [lb_ref[j, 0] for j in range(lb_ref.shape[0])]
    m = functools.reduce(jnp.maximum, rows)
    es = [jnp.exp(r - m) for r in rows]
    return sum(es[:slot + 1]) / sum(es)


N_HGRN_IN = 10


def _hgrn_kernel(*refs, slot, n_ctx_chunks, n_lat_chunks, n_cast):
    hc_ref, hl_ref, g_ref, wq_ref, wv_ref, wf_ref, wb_ref, lbf_ref, lbb_ref, gn_ref = refs[:N_HGRN_IN]
    cast_in = refs[N_HGRN_IN:N_HGRN_IN + n_cast]
    y_ref = refs[N_HGRN_IN + n_cast]
    cast_out = refs[N_HGRN_IN + n_cast + 1:N_HGRN_IN + 2 * n_cast + 1]
    (wbuf, pbuf0, pbuf1, pbuf2, pbuf3, oacc, qif, qib,
     kvf, kvb, decf, decb, spf, spb) = refs[N_HGRN_IN + 2 * n_cast + 1:]
    for c_ref, o_ref in zip(cast_in, cast_out):
        o_ref[...] = c_ref[...].astype(BF16)

    lb_f = _lower_bound(lbf_ref, slot)
    lb_b = _lower_bound(lbb_ref, slot)
    row = lax.broadcasted_iota(jnp.int32, (HG_BLOCK, HG_DK), 0) & (HG_CHUNK - 1)
    ci = lax.broadcasted_iota(jnp.int32, (HG_CPB, HG_CHUNK, HG_CHUNK), 1)
    si = lax.broadcasted_iota(jnp.int32, (HG_CPB, HG_CHUNK, HG_CHUNK), 2)

    def c3(t):
        return t.reshape(HG_CPB, HG_CHUNK, HG_DK)

    def direction(qs3, v3b, z, lb, fwd):
        sig = _sigmoid_tanh(z)
        f = lb + (1.0 - lb) * sig
        k3 = c3((1.0 - lb) * (1.0 - sig))
        lf = jnp.log(f)
        if fwd:
            cum = c3(_chunk_prefix(lf, row))
            ref = cum[:, HG_CHUNK // 2 - 1:HG_CHUNK // 2, :]
            last = cum[:, HG_CHUNK - 1:HG_CHUNK, :]
        else:
            cum = c3(_chunk_suffix(lf, row))
            ref = cum[:, HG_CHUNK // 2:HG_CHUNK // 2 + 1, :]
            last = cum[:, 0:1, :]
        dec = jnp.exp(last)
        if qs3 is None:
            kl = k3 * jnp.exp(last - cum)
        else:
            e1 = jnp.exp(cum - ref)
            qd = qs3 * e1
            qi = qd * jnp.exp(ref)
            kd = k3 * (1.0 / e1)
            kl = kd * jnp.exp(last - ref)
        kvt = jnp.einsum('ncv,nck->nvk', v3b, kl.astype(BF16), preferred_element_type=F32)
        if qs3 is None:
            return None, None, kvt, dec
        sc = jnp.einsum('nck,nsk->ncs', qd.astype(BF16), kd.astype(BF16), preferred_element_type=F32)
        sc = jnp.where((ci >= si) if fwd else (ci <= si), sc, 0.0)
        intra = jnp.einsum('ncs,nsv->ncv', sc.astype(BF16), v3b, preferred_element_type=F32)
        return intra, qi, kvt, dec

    for p, wp_ref in enumerate((wq_ref, wv_ref, wf_ref, wb_ref)):
        wbuf[:, p * HG_DK:(p + 1) * HG_DK] = wp_ref[...].astype(BF16)

    def project(hrows):
        return jnp.dot(hrows, wbuf[...], preferred_element_type=F32)

    def block(p, chunk0, lat_row0):
        v3b = c3(p[:, HG_DK:2 * HG_DK]).astype(BF16)
        zf = p[:, 2 * HG_DK:3 * HG_DK]
        zb = p[:, 3 * HG_DK:4 * HG_DK]
        if lat_row0 is None:
            qs3 = None
        else:
            qs3 = c3(_silu_tanh(p[:, 0:HG_DK]))
        in_f, qi_f, kv_f, dec_f = direction(qs3, v3b, zf, lb_f, True)
        in_b, qi_b, kv_b, dec_b = direction(qs3, v3b, zb, lb_b, False)
        kvf[pl.ds(chunk0, HG_CPB)] = kv_f
        kvb[pl.ds(chunk0, HG_CPB)] = kv_b
        decf[pl.ds(chunk0, HG_CPB)] = dec_f
        decb[pl.ds(chunk0, HG_CPB)] = dec_b
        if lat_row0 is not None:
            rows = pl.ds(lat_row0, HG_BLOCK)
            oacc[rows, :] = (in_f + in_b).reshape(HG_BLOCK, HG_DK)
            qif[rows, :] = qi_f.reshape(HG_BLOCK, HG_DK).astype(BF16)
            qib[rows, :] = qi_b.reshape(HG_BLOCK, HG_DK).astype(BF16)

    n_blocks = n_lat_chunks // HG_CPB

    def lat_rows(i):
        return hl_ref[0, pl.ds(pl.multiple_of(i * HG_BLOCK, HG_BLOCK), HG_BLOCK), :]

    def lat_terms(p_ref, i):
        block(p_ref[...], n_ctx_chunks + i * HG_CPB, pl.multiple_of(i * HG_BLOCK, HG_BLOCK))

    ctx_p = [project(hc_ref[0, i * HG_BLOCK:(i + 1) * HG_BLOCK, :]) for i in range(n_ctx_chunks // HG_CPB)]
    pbuf0[...] = project(lat_rows(0))
    for i, p in enumerate(ctx_p):
        block(p, i * HG_CPB, None)
    pbuf1[...] = project(lat_rows(1))

    def lat_quad(i0, last):
        pbuf2[...] = project(lat_rows(i0 + 2))
        lat_terms(pbuf0, i0)
        pbuf3[...] = project(lat_rows(i0 + 3))
        lat_terms(pbuf1, i0 + 1)
        if not last:
            pbuf0[...] = project(lat_rows(i0 + 4))
        lat_terms(pbuf2, i0 + 2)
        if not last:
            pbuf1[...] = project(lat_rows(i0 + 5))
        lat_terms(pbuf3, i0 + 3)

    def lat_body(t, carry):
        lat_quad(4 * t, False)
        return carry

    lax.fori_loop(0, n_blocks // 4 - 1, lat_body, 0)
    lat_quad(n_blocks - 4, True)

    def advance(s, kv_ref, dec_ref, n):
        return dec_ref[n] * s + kv_ref[n]

    sf = jnp.zeros((HG_DK, HG_DK), F32)
    for n in range(n_ctx_chunks):
        sf = advance(sf, kvf, decf, n)
    sb = jnp.zeros((HG_DK, HG_DK), F32)
    for n in reversed(range(n_ctx_chunks)):
        sb = advance(sb, kvb, decb, n)

    def scan_step(t, carry):
        sf, sb = carry
        jb = n_lat_chunks - 1 - t
        spf[t] = sf.astype(BF16)
        spb[jb] = sb.astype(BF16)
        return (advance(sf, kvf, decf, n_ctx_chunks + t), advance(sb, kvb, decb, n_ctx_chunks + jb))

    lax.fori_loop(0, n_lat_chunks, scan_step, (sf, sb), unroll=2)

    gn = gn_ref[...]

    def block_rows(i):
        return pl.ds(pl.multiple_of(i * HG_BLOCK, HG_BLOCK), HG_BLOCK)

    def inter(i):
        rows = block_rows(i)
        chunks = pl.ds(i * HG_CPB, HG_CPB)
        return (jnp.einsum('nck,nvk->ncv', c3(qif[rows, :]), spf[chunks], preferred_element_type=F32)
                + jnp.einsum('nck,nvk->ncv', c3(qib[rows, :]), spb[chunks], preferred_element_type=F32)
                ).reshape(HG_BLOCK, HG_DK)

    def readout(o_ref, i):
        rows = block_rows(i)
        o = oacc[rows, :] + o_ref[...]
        ms = jnp.mean(o * o, axis=-1, keepdims=True)
        o = o * lax.rsqrt(ms + EPS) * gn
        y_ref[0, rows, :] = (o * _silu_tanh(g_ref[0, rows, :].astype(F32))).astype(BF16)

    obuf0, obuf1 = pbuf0.at[:, 0:HG_DK], pbuf1.at[:, 0:HG_DK]
    obuf0[...] = inter(0)

    def readout_pair(t, carry):
        obuf1[...] = inter(2 * t + 1)
        readout(obuf0, 2 * t)
        obuf0[...] = inter(2 * t + 2)
        readout(obuf1, 2 * t + 1)
        return carry

    lax.fori_loop(0, n_blocks // 2 - 1, readout_pair, 0)
    obuf1[...] = inter(n_blocks - 1)
    readout(obuf0, n_blocks - 2)
    readout(obuf1, n_blocks - 1)


def _hgrn_call(h_ctx, h_lat, g_lat, w_in, lb_fwd, lb_bwd, gnorm, slot, to_cast):
    b, lc, d = h_ctx.shape
    ll = h_lat.shape[1]
    nrow = lb_fwd.shape[0]
    ncc, nlc = lc // HG_CHUNK, ll // HG_CHUNK
    kern = functools.partial(_hgrn_kernel, slot=slot, n_ctx_chunks=ncc, n_lat_chunks=nlc, n_cast=len(to_cast))
    lb_spec = pl.BlockSpec((nrow, 1, 1, HG_DK), lambda i, h: (0, h, 0, 0))
    head_cols = pl.BlockSpec((1, ll, HG_DK), lambda i, h: (i, 0, h))
    n_rec = N_HG_PROJ - 1
    w_cols = [pl.BlockSpec((None, d, HG_DK), functools.partial(lambda i, h, p: (slot, 0, p * HG_HEADS + h), p=p))
              for p in range(n_rec)]
    steps = b * HG_HEADS
    slabs = [a.reshape(steps, -1, *a.shape[-2:]) for a in to_cast]
    slab_specs = [pl.BlockSpec((1,) + a.shape[1:], lambda i, h: (i * HG_HEADS + h, 0, 0, 0)) for a in slabs]
    outs = pl.pallas_call(
        kern,
        grid=(b, HG_HEADS),
        in_specs=[
            pl.BlockSpec((1, lc, d), lambda i, h: (i, 0, 0)),
            pl.BlockSpec((1, ll, d), lambda i, h: (i, 0, 0)),
            head_cols,
            *w_cols,
            lb_spec, lb_spec,
            pl.BlockSpec((1, HG_DK), lambda i, h: (0, 0)),
            *slab_specs,
        ],
        out_specs=[head_cols, *slab_specs],
        out_shape=[jax.ShapeDtypeStruct((b, ll, d), BF16)]
        + [jax.ShapeDtypeStruct(a.shape, BF16) for a in slabs],
        scratch_shapes=[
            pltpu.VMEM((d, n_rec * HG_DK), BF16),
            pltpu.VMEM((HG_BLOCK, n_rec * HG_DK), F32),
            pltpu.VMEM((HG_BLOCK, n_rec * HG_DK), F32),
            pltpu.VMEM((HG_BLOCK, n_rec * HG_DK), F32),
            pltpu.VMEM((HG_BLOCK, n_rec * HG_DK), F32),
            pltpu.VMEM((ll, HG_DK), F32),
            pltpu.VMEM((ll, HG_DK), BF16),
            pltpu.VMEM((ll, HG_DK), BF16),
            pltpu.VMEM((ncc + nlc, HG_DK, HG_DK), F32),
            pltpu.VMEM((ncc + nlc, HG_DK, HG_DK), F32),
            pltpu.VMEM((ncc + nlc, 1, HG_DK), F32),
            pltpu.VMEM((ncc + nlc, 1, HG_DK), F32),
            pltpu.VMEM((nlc, HG_DK, HG_DK), BF16),
            pltpu.VMEM((nlc, HG_DK, HG_DK), BF16),
        ],
        compiler_params=_cparams(("parallel", "arbitrary"), HGRN_VMEM_LIMIT),
        name="hgrn2",
    )(h_ctx, h_lat, g_lat, *([w_in] * n_rec),
      lb_fwd.reshape(nrow, HG_HEADS, 1, HG_DK), lb_bwd.reshape(nrow, HG_HEADS, 1, HG_DK),
      gnorm.reshape(1, HG_DK), *slabs)
    return outs[0], [o.reshape(a.shape) for o, a in zip(outs[1:], to_cast)]


POOL_STRIP = 8
POOL_PAD = 8
POOL_BAND = 256


def _pool_group(h_ref, win_ref, wg_ref, ps_ref, o_ref, upad, *, win, rows):
    gd = win_ref.shape[1]
    half = win // 2
    u = jnp.dot(h_ref[0], win_ref[...].astype(BF16), preferred_element_type=F32)
    zeros = jnp.zeros((POOL_PAD, GRID_W, gd), F32)
    upad[0:POOL_PAD] = zeros
    upad[POOL_PAD + rows:POOL_PAD + rows + POOL_PAD] = zeros
    upad[POOL_PAD:POOL_PAD + rows] = u.reshape(rows, GRID_W, gd)

    tok = POOL_STRIP * GRID_W
    bi = lax.broadcasted_iota(jnp.int32, (POOL_BAND, POOL_BAND), 0)
    bj = lax.broadcasted_iota(jnp.int32, (POOL_BAND, POOL_BAND), 1)
    lo_c = (bi & (GRID_W - 1)) - half
    cj = bj & (GRID_W - 1)
    band = jnp.where(bi >> GRID_SHIFT == bj >> GRID_SHIFT, 1.0, 0.0)
    band = jnp.where(cj >= lo_c, band, 0.0)
    band = jnp.where(cj < lo_c + win, band, 0.0).astype(BF16)
    t = lax.broadcasted_iota(jnp.int32, (tok, LANES), 0)
    col = t & (GRID_W - 1)
    cnt_c = jnp.minimum(col - half + win, GRID_W) - jnp.maximum(col - half, 0)
    wg = wg_ref[0].astype(BF16)
    ps = ps_ref[...]

    def strip(i, carry):
        r0 = i * POOL_STRIP
        slab = upad[pl.ds(r0 + POOL_PAD - half, POOL_STRIP + win - 1)]
        span = 1
        while span < win:
            n = slab.shape[0] - span
            slab = slab[0:n] + slab[span:span + n]
            span *= 2
        rs = slab.reshape(tok, gd)
        hi = rs.astype(BF16)
        lo = (rs - hi.astype(F32)).astype(BF16)
        parts = []
        for k in range(tok // POOL_BAND):
            sl = slice(k * POOL_BAND, (k + 1) * POOL_BAND)
            parts.append(jnp.dot(band, hi[sl], preferred_element_type=F32)
                         + jnp.dot(band, lo[sl], preferred_element_type=F32))
        box = jnp.concatenate(parts, axis=0)
        r = r0 + (t >> GRID_SHIFT)
        cnt_r = jnp.minimum(r - half + win, rows) - jnp.maximum(r - half, 0)
        inv = 1.0 / (cnt_r * cnt_c).astype(F32)
        mean = box * jnp.concatenate([inv] * (gd // LANES), axis=1)
        ug = upad[pl.ds(r0 + POOL_PAD, POOL_STRIP)].reshape(tok, gd)
        z = jnp.dot((mean - ug).astype(BF16), wg, preferred_element_type=F32) * ps
        o_ref[0, pl.ds(pl.multiple_of(i * tok, tok), tok), :] = z.astype(BF16)
        return carry

    lax.fori_loop(0, rows // POOL_STRIP, strip, 0, unroll=2)


def _pool_kernel(h_ref, win_ref, wg_ref, ps_ref, o_ref, upad, *, rows):
    g = pl.program_id(1)
    for gi, win in enumerate(POOL_WINDOWS):
        @pl.when(g == gi)
        def _(win=win):
            _pool_group(h_ref, win_ref, wg_ref, ps_ref, o_ref, upad, win=win, rows=rows)


def _pool_call(h, w_in, w_grp, p_scale, slot):
    b, l, d = h.shape
    ng = len(POOL_WINDOWS)
    gd = d // ng
    rows = l // GRID_W
    return pl.pallas_call(
        functools.partial(_pool_kernel, rows=rows),
        grid=(b, ng),
        in_specs=[
            pl.BlockSpec((1, l, d), lambda i, g: (i, 0, 0)),
            pl.BlockSpec((None, d, gd), lambda i, g: (slot, 0, g)),
            pl.BlockSpec((None, 1, gd, gd), lambda i, g: (slot, g, 0, 0)),
            pl.BlockSpec((None, 1, gd), lambda i, g: (slot, 0, g)),
        ],
        out_specs=pl.BlockSpec((1, l, gd), lambda i, g: (i, 0, g)),
        out_shape=jax.ShapeDtypeStruct((b, l, d), BF16),
        scratch_shapes=[pltpu.VMEM((rows + 2 * POOL_PAD, GRID_W, gd), F32)],
        compiler_params=_cparams(("parallel", "arbitrary")),
        name="pool_mix",
    )(h, w_in, w_grp, p_scale.reshape(p_scale.shape[0], 1, d))


MOE_TILE = 1024
MOE_WINDOWS = (256, 272, 288, 320)
MOE_EXTRA = 64
ROW_ALIGN = 16
DEST_LANE = N_EXPERTS
PIECE_STRIDE = 32
N_PIECES = 3
AUX_ROWS = 8


def _route(sel, s):
    keep = []
    gsum = []
    for g in range(N_EXPERT_GROUPS):
        a = sel[g * EXPERTS_PER_GROUP:(g + 1) * EXPERTS_PER_GROUP]
        beaten = [jnp.zeros_like(a[0]) for _ in a]
        for i in range(EXPERTS_PER_GROUP):
            for j in range(i + 1, EXPERTS_PER_GROUP):
                ge = jnp.where(a[i] >= a[j], 1.0, 0.0)
                beaten[j] = beaten[j] + ge
                beaten[i] = beaten[i] + (1.0 - ge)
        kg = [jnp.where(bt < 1.5, 1.0, 0.0) for bt in beaten]
        keep.append(kg)
        gsum.append(sum(k * x for k, x in zip(kg, a)))
    picked = []
    bests = []
    for g in range(N_EXPERT_GROUPS):
        better = jnp.zeros_like(gsum[0])
        for o in range(N_EXPERT_GROUPS):
            if o < g:
                better = better + jnp.where(gsum[o] >= gsum[g], 1.0, 0.0)
            elif o > g:
                better = better + jnp.where(gsum[o] > gsum[g], 1.0, 0.0)
        best = jnp.where(better < 0.5, 1.0, 0.0)
        bests.append(best)
        for i in range(EXPERTS_PER_GROUP):
            picked.append(best * keep[g][i] * s[g * EXPERTS_PER_GROUP + i])
    den = sum(picked)
    return [p / den for p in picked], bests


def _post_kernel(x_ref, y_ref, w_ref, mod_ref, g_ref, rw_ref, rb_ref,
                 xo_ref, h2_ref, rt_ref, aux_ref, wbuf):
    @pl.when(pl.program_id(0) == 0)
    def _():
        wbuf[...] = w_ref[...].astype(BF16)

    yw = jnp.dot(y_ref[...], wbuf[...], preferred_element_type=F32)
    xn = x_ref[...] + mod_ref[0, 2:3, :] * yw
    xo_ref[...] = xn
    h2 = _modulate(xn, g_ref[...], mod_ref[0, 3:4, :], mod_ref[0, 4:5, :])
    hi = h2.astype(BF16)
    h2_ref[...] = hi
    lo = (h2 - hi.astype(F32)).astype(BF16)
    rw = rw_ref[...]
    half = hi.shape[0] // 2
    prod = jnp.concatenate(
        [jnp.dot(hi[r:r + half], rw, preferred_element_type=F32)
         + jnp.dot(lo[r:r + half], rw, preferred_element_type=F32) for r in (0, half)], axis=0).T
    logits = prod[0:N_EXPERTS, :] + prod[N_EXPERTS:2 * N_EXPERTS, :]
    s = _sigmoid_pair(logits)[0]
    sel = s + rb_ref[...]
    comb, bests = _route([sel[e:e + 1, :] for e in range(N_EXPERTS)],
                         [s[e:e + 1, :] for e in range(N_EXPERTS)])
    t = logits.shape[1]

    ind = jnp.concatenate(bests + [jnp.zeros((AUX_ROWS - N_EXPERT_GROUPS, t), F32)], axis=0)
    n_seg = t // LANES
    stacked = jnp.concatenate([ind[:, j * LANES:(j + 1) * LANES] for j in range(n_seg)], axis=0)
    upper = jnp.where(lax.broadcasted_iota(jnp.int32, (LANES, LANES), 0)
                      <= lax.broadcasted_iota(jnp.int32, (LANES, LANES), 1), 1.0, 0.0).astype(BF16)
    local = jnp.dot(stacked.astype(BF16), upper, preferred_element_type=F32)
    off = jnp.zeros((AUX_ROWS, 1), F32)
    pieces = []
    for j in range(n_seg):
        seg_cum = local[j * AUX_ROWS:(j + 1) * AUX_ROWS, :]
        pieces.append(seg_cum + off)
        off = off + seg_cum[:, LANES - 1:LANES]
    cum = jnp.concatenate(pieces, axis=1)
    counts = [off[g:g + 1, :] for g in range(N_EXPERT_GROUPS)]
    starts = [jnp.zeros((1, 1), F32)]
    for g in range(1, N_EXPERT_GROUPS):
        starts.append(starts[-1] + counts[g - 1])
    dest = sum(bests[g] * (starts[g] + cum[g:g + 1, :] - 1.0) for g in range(N_EXPERT_GROUPS))

    lane = lax.broadcasted_iota(jnp.int32, (1, t), 1)
    seg = sum(jnp.where(lane == k, v, 0.0) for k, v in enumerate(starts + counts))
    aux_ref[0] = jnp.concatenate([dest, seg, jnp.zeros((AUX_ROWS - 2, t), F32)], axis=0)
    p1 = [w.astype(BF16).astype(F32) for w in comb]
    r1 = [w - p for w, p in zip(comb, p1)]
    p2 = [r.astype(BF16).astype(F32) for r in r1]
    p3 = [r - p for r, p in zip(r1, p2)]
    pad = [jnp.zeros((PIECE_STRIDE - N_EXPERTS, t), F32)]
    table = jnp.concatenate(p1 + [dest] + [jnp.zeros((PIECE_STRIDE - N_EXPERTS - 1, t), F32)] + p2 + pad + p3 + pad
                            + [jnp.zeros((LANES - N_PIECES * PIECE_STRIDE, t), F32)], axis=0)
    rt_ref[...] = table.T


def _post_call(x, y, w, w_slot, mod, g, router_w, router_b, tokens_per_batch, tile=MOE_TILE):
    n, d = x.shape
    per_b = tokens_per_batch // tile
    rwh = router_w.astype(BF16)
    rwl = (router_w - rwh.astype(F32)).astype(BF16)
    rw = jnp.concatenate([rwh, rwl, jnp.zeros((d, LANES - 2 * N_EXPERTS), BF16)], axis=1)
    row = lambda i: (i, 0)
    fixed = lambda i: (0, 0)
    return pl.pallas_call(
        _post_kernel,
        grid=(n // tile,),
        in_specs=[
            pl.BlockSpec((tile, d), row),
            pl.BlockSpec((tile, d), row),
            pl.BlockSpec((None, d, d), lambda i: (w_slot, 0, 0)),
            pl.BlockSpec((1, N_MOD, d), lambda i: (i // per_b, 0, 0)),
            pl.BlockSpec((1, d), fixed),
            pl.BlockSpec((d, LANES), fixed),
            pl.BlockSpec((N_EXPERTS, 1), fixed),
        ],
        out_specs=[
            pl.BlockSpec((tile, d), row),
            pl.BlockSpec((tile, d), row),
            pl.BlockSpec((tile, LANES), row),
            pl.BlockSpec((1, AUX_ROWS, tile), lambda i: (i, 0, 0)),
        ],
        out_shape=[
            jax.ShapeDtypeStruct((n, d), F32),
            jax.ShapeDtypeStruct((n, d), BF16),
            jax.ShapeDtypeStruct((n, LANES), F32),
            jax.ShapeDtypeStruct((n // tile, AUX_ROWS, tile), F32),
        ],
        scratch_shapes=[pltpu.VMEM((d, d), BF16)],
        compiler_params=_cparams(("arbitrary",)),
        name="mixer_out_router",
    )(x, y, w, mod, g.reshape(1, d), rw, router_b.reshape(N_EXPERTS, 1))


def _moe_kernel(seg_ref, h_ref, rt_ref, aux_ref, wg_ref, wu_ref, wd_ref, x_ref, mod_ref, ng_ref, nmod_ref,
                o_ref, *rest, final_norm):
    if final_norm:
        xs, cws, ys = rest
    else:
        hn_ref, xs, cws, ys = rest
    i = pl.program_id(0)
    g = pl.program_id(1)
    t = h_ref.shape[0]

    @pl.when(g == 0)
    def _():
        dest_row = aux_ref[0, 0:1, :]
        perm = jnp.where(lax.broadcasted_iota(jnp.int32, (t, t), 0).astype(F32) == dest_row,
                         1.0, 0.0).astype(BF16)
        xs[...] = jnp.dot(perm, h_ref[...], preferred_element_type=F32).astype(BF16)
        cws[...] = jnp.dot(perm, rt_ref[...].astype(BF16), preferred_element_type=F32)
        ys[...] = jnp.zeros_like(ys)

    start = seg_ref[i, g]
    end = start + seg_ref[i, N_EXPERT_GROUPS + g]
    wd = wd_ref[...].reshape(EXPERTS_PER_GROUP * D_EXPERT, wd_ref.shape[2])

    def window(w0, rows, lo, hi):
        sl = pl.ds(pl.multiple_of(w0, ROW_ALIGN), rows)
        xw = xs[sl, :]
        cw = cws[sl, :]
        r = w0 + lax.broadcasted_iota(jnp.int32, cw.shape, 0)
        lane = lax.broadcasted_iota(jnp.int32, cw.shape, 1)
        piece_lane = jnp.where(lane < N_PIECES * PIECE_STRIDE, lane & (PIECE_STRIDE - 1), -1)
        cw = jnp.where(r >= lo, cw, 0.0)
        cw = jnp.where(r < hi, cw, 0.0)
        acts = []
        for e in range(EXPERTS_PER_GROUP):
            ce = jnp.sum(jnp.where(piece_lane == g * EXPERTS_PER_GROUP + e, cw, 0.0), axis=1, keepdims=True)
            ge = g * EXPERTS_PER_GROUP + e
            a = _silu_tanh(jnp.dot(xw, wg_ref[ge], preferred_element_type=F32)) \
                * jnp.dot(xw, wu_ref[ge], preferred_element_type=F32)
            acts.append((a * ce).astype(BF16))
        ys[sl, :] += jnp.dot(jnp.concatenate(acts, axis=1), wd, preferred_element_type=F32)

    aligned = start & -ROW_ALIGN
    fits = [end <= jnp.minimum(aligned, t - m) + m for m in MOE_WINDOWS[:-1]]
    taken = False
    for m, fit in zip(MOE_WINDOWS[:-1], fits):
        @pl.when(fit if taken is False else fit & ~taken)
        def _(m=m):
            window(jnp.minimum(aligned, t - m), m, start, end)
        taken = fit if taken is False else taken | fit

    @pl.when(~taken)
    def _():
        m = MOE_WINDOWS[-1]
        w0 = jnp.minimum(aligned, t - m)
        covered = w0 + m
        window(w0, m, start, jnp.minimum(end, covered))

        def extra(k, carry):
            lo = covered + k * MOE_EXTRA
            window(jnp.minimum(lo, t - MOE_EXTRA), MOE_EXTRA, lo, jnp.minimum(end, lo + MOE_EXTRA))
            return carry

        n_extra = jnp.maximum(end - covered + MOE_EXTRA - 1, 0) >> (MOE_EXTRA.bit_length() - 1)
        lax.fori_loop(0, n_extra, extra, 0)

    @pl.when(g == N_EXPERT_GROUPS - 1)
    def _():
        dest_col = rt_ref[:, DEST_LANE:DEST_LANE + 1]
        unperm = jnp.where(lax.broadcasted_iota(jnp.int32, (t, t), 1).astype(F32) == dest_col,
                           1.0, 0.0).astype(BF16)
        ff = jnp.dot(unperm, ys[...].astype(BF16), preferred_element_type=F32)
        xn = x_ref[...] + mod_ref[0, 5:6, :] * ff
        if final_norm:
            ms = jnp.mean(xn * xn, axis=-1, keepdims=True)
            xn = xn * lax.rsqrt(ms + EPS) * ng_ref[...]
        else:
            hn_ref[...] = _modulate(xn, ng_ref[...], nmod_ref[0, 0:1, :], nmod_ref[0, 1:2, :]).astype(BF16)
        o_ref[...] = xn


def _moe_call(h2, table, aux, wg, wu, wd, layer, x, mod, norm_g, next_mod, tokens_per_batch, final_norm):
    n, d = x.shape
    tile = MOE_TILE
    per_b = tokens_per_batch // tile
    seg = aux[:, 1, :2 * N_EXPERT_GROUPS].astype(jnp.int32)
    row = lambda i, g, seg: (i, 0)
    grp = lambda i, g, seg: (layer, g, 0, 0)
    tok = pl.BlockSpec((tile, d), row)
    x_out = jax.ShapeDtypeStruct((n, d), F32)
    grid_spec = pltpu.PrefetchScalarGridSpec(
        num_scalar_prefetch=1,
        grid=(n // tile, N_EXPERT_GROUPS),
        in_specs=[
            pl.BlockSpec((tile, d), row),
            pl.BlockSpec((tile, LANES), row),
            pl.BlockSpec((1, AUX_ROWS, tile), lambda i, g, seg: (i, 0, 0)),
            pl.BlockSpec((None, N_EXPERTS, d, D_EXPERT), lambda i, g, seg: (layer, 0, 0, 0)),
            pl.BlockSpec((None, N_EXPERTS, d, D_EXPERT), lambda i, g, seg: (layer, 0, 0, 0)),
            pl.BlockSpec((None, EXPERTS_PER_GROUP, D_EXPERT, d), grp),
            pl.BlockSpec((tile, d), row),
            pl.BlockSpec((1, N_MOD, d), lambda i, g, seg: (i // per_b, 0, 0)),
            pl.BlockSpec((1, d), lambda i, g, seg: (0, 0)),
            pl.BlockSpec((1, N_MOD, d), lambda i, g, seg: (i // per_b, 0, 0)),
        ],
        out_specs=tok if final_norm else [tok, tok],
        scratch_shapes=[
            pltpu.VMEM((tile, d), BF16),
            pltpu.VMEM((tile, LANES), F32),
            pltpu.VMEM((tile, d), F32),
        ],
    )
    return pl.pallas_call(
        functools.partial(_moe_kernel, final_norm=final_norm),
        grid_spec=grid_spec,
        out_shape=x_out if final_norm else [x_out, jax.ShapeDtypeStruct((n, d), BF16)],
        compiler_params=_cparams(("parallel", "arbitrary"), MOE_VMEM_LIMIT),
        name="moe_ffn",
    )(seg, h2, table, aux, wg, wu, wd, x, mod, norm_g.reshape(1, d), next_mod)


def kernel(x, c, ctx, c_ctx, w_mod, b_mod, norm1_g, norm2_g, hg_w_in, hg_lb_fwd, hg_lb_bwd, hg_gnorm,
           hg_w_out, pool_w_in, pool_w_grp, pool_scale, pool_w_out, router_w, router_b, moe_w_gate,
           moe_w_up, moe_w_down, final_g):
    b, l, d = x.shape
    depth = w_mod.shape[0]
    n_mixers = 2

    cc = jnp.concatenate([c, c_ctx[None, :], jnp.zeros((MOD_ROWS - b - 1, d), F32)], axis=0)
    mods = _mod_call(cc, w_mod, b_mod)

    experts = None
    x_lat = x.reshape(b * l, d)
    h_lat = None
    for i in range(depth):
        slot = i // n_mixers
        mod_lat = mods[i, :b].reshape(b, N_MOD, d)
        if i % n_mixers == 0:
            h_lat, g_lat = _norm_call(x_lat.reshape(b, l, d), norm1_g[i], mod_lat, 1024,
                                      hg_w_in, slot, N_HG_PROJ - 1)
            mod_ctx = jnp.broadcast_to(mods[i, b].reshape(1, N_MOD, d), (b, N_MOD, d))
            h_ctx = _norm_call(ctx, norm1_g[i], mod_ctx, ctx.shape[1])
            to_cast = (moe_w_gate, moe_w_up, moe_w_down) if experts is None else ()
            y, cast = _hgrn_call(h_ctx, h_lat, g_lat, hg_w_in, hg_lb_fwd, hg_lb_bwd, hg_gnorm[slot], slot, to_cast)
            experts = experts or tuple(cast)
            w_out = hg_w_out
        else:
            if h_lat is None:
                h_lat = _norm_call(x_lat.reshape(b, l, d), norm1_g[i], mod_lat, 512)
            y = _pool_call(h_lat, pool_w_in, pool_w_grp, pool_scale, slot)
            w_out = pool_w_out
        x_lat, h2, table, aux = _post_call(x_lat, y.reshape(b * l, d), w_out, slot, mod_lat,
                                           norm2_g[i], router_w, router_b, l)
        if experts is None:
            experts = (moe_w_gate.astype(BF16), moe_w_up.astype(BF16), moe_w_down.astype(BF16))
        if i == depth - 1:
            x_lat = _moe_call(h2, table, aux, *experts, i, x_lat, mod_lat, final_g, mod_lat, l, final_norm=True)
        else:
            mod_next = mods[i + 1, :b].reshape(b, N_MOD, d)
            x_lat, h_next = _moe_call(h2, table, aux, *experts, i, x_lat, mod_lat, norm1_g[i + 1], mod_next, l,
                                      final_norm=False)
            h_lat = h_next.reshape(b, l, d) if (i + 1) % n_mixers != 0 else None
    return x_lat.reshape(b, l, d)
```

```python
import functools

import jax
import jax.numpy as jnp
from jax import lax
from jax.experimental import pallas as pl
from jax.experimental.pallas import tpu as pltpu

F32 = jnp.float32
BF16 = jnp.bfloat16

EPS = 1e-6
N_MOD = 6
HG_HEADS = 8
HG_DK = 128
HG_CHUNK = 64
N_HG_PROJ = 5
POOL_WINDOWS = (2, 4, 8, 16)
GRID_W = 64
GRID_SHIFT = GRID_W.bit_length() - 1
assert 1 << GRID_SHIFT == GRID_W
N_EXPERTS = 16
N_EXPERT_GROUPS = 4
EXPERTS_PER_GROUP = N_EXPERTS // N_EXPERT_GROUPS
D_EXPERT = 256

LANES = 128
MOD_ROWS = 8
VMEM_LIMIT = 56 * 1024 * 1024
HGRN_VMEM_LIMIT = 60 * 1024 * 1024
MOE_VMEM_LIMIT = 60 * 1024 * 1024


def _cparams(sem, vmem=VMEM_LIMIT):
    return pltpu.CompilerParams(dimension_semantics=sem, vmem_limit_bytes=vmem)


def _sigmoid_pair(z):
    e = jnp.exp(-jnp.abs(z))
    r = 1.0 / (1.0 + e)
    er = e * r
    pos = z >= 0
    return jnp.where(pos, r, er), jnp.where(pos, er, r)


def _silu(z):
    return z * _sigmoid_pair(z)[0]


def _sigmoid_tanh(z):
    return 0.5 * jnp.tanh(0.5 * z) + 0.5


def _silu_tanh(z):
    return z * _sigmoid_tanh(z)


def _split_bf16(x):
    hi = x.astype(BF16)
    return hi, (x - hi.astype(F32)).astype(BF16)


def _mod_kernel(c_ref, w_ref, b_ref, o_ref):
    a = jnp.concatenate(_split_bf16(_silu(c_ref[...])), axis=0)
    w_hi, w_lo = _split_bf16(w_ref[0])
    acc = jnp.dot(a, w_hi, preferred_element_type=F32) + jnp.dot(a, w_lo, preferred_element_type=F32)
    o_ref[0] = acc[0:MOD_ROWS] + acc[MOD_ROWS:2 * MOD_ROWS] + b_ref[0]


def _mod_call(cc, w_mod, b_mod):
    depth, d, n = w_mod.shape
    tn = 1024
    return pl.pallas_call(
        _mod_kernel,
        grid=(depth, n // tn),
        in_specs=[
            pl.BlockSpec((MOD_ROWS, d), lambda i, j: (0, 0)),
            pl.BlockSpec((1, d, tn), lambda i, j: (i, 0, j)),
            pl.BlockSpec((1, 1, tn), lambda i, j: (i, 0, j)),
        ],
        out_specs=pl.BlockSpec((1, MOD_ROWS, tn), lambda i, j: (i, 0, j)),
        out_shape=jax.ShapeDtypeStruct((depth, MOD_ROWS, n), F32),
        compiler_params=_cparams(("parallel", "parallel")),
        name="mod_proj",
    )(cc, w_mod, b_mod.reshape(depth, 1, n))


def _modulate(x, g, shift, scale):
    ms = jnp.mean(x * x, axis=-1, keepdims=True)
    return (x * lax.rsqrt(ms + EPS) * g) * (1.0 + scale) + shift


def _norm_kernel(x_ref, g_ref, mod_ref, *rest):
    h = _modulate(x_ref[0], g_ref[...], mod_ref[0, 0:1, :], mod_ref[0, 1:2, :]).astype(BF16)
    if len(rest) == 1:
        (o_ref,) = rest
    else:
        w_ref, o_ref, p_ref, wbuf = rest

        @pl.when((pl.program_id(0) == 0) & (pl.program_id(1) == 0))
        def _():
            wbuf[...] = w_ref[...].astype(BF16)

        p_ref[0] = jnp.dot(h, wbuf[...], preferred_element_type=F32).astype(BF16)
    o_ref[0] = h


def _norm_call(x, g, mod, tile, w_proj=None, w_slot=0, w_col=0):
    b, t, d = x.shape
    tok = pl.BlockSpec((1, tile, d), lambda i, j: (i, j, 0))
    in_specs = [tok, pl.BlockSpec((1, d), lambda i, j: (0, 0)),
                pl.BlockSpec((1, N_MOD, d), lambda i, j: (i, 0, 0))]
    args = [x, g.reshape(1, d), mod]
    out_specs, out_shape = tok, jax.ShapeDtypeStruct((b, t, d), BF16)
    scratch = []
    if w_proj is not None:
        in_specs.append(pl.BlockSpec((None, d, d), lambda i, j: (w_slot, 0, w_col)))
        args.append(w_proj)
        out_specs, out_shape = [tok, tok], [out_shape, out_shape]
        scratch = [pltpu.VMEM((d, d), BF16)]
    return pl.pallas_call(
        _norm_kernel,
        grid=(b, t // tile),
        in_specs=in_specs,
        out_specs=out_specs,
        out_shape=out_shape,
        scratch_shapes=scratch,
        compiler_params=_cparams(("arbitrary", "arbitrary")),
        name="norm1",
    )(*args)


HG_BLOCK = 256
HG_CPB = HG_BLOCK // HG_CHUNK


def _chunk_prefix(x, row):
    for d in (1, 2, 4, 8, 16, 32):
        x = x + jnp.where(row >= d, pltpu.roll(x, d, axis=0), 0.0)
    return x


def _chunk_suffix(x, row):
    n = x.shape[0]
    for d in (1, 2, 4, 8, 16, 32):
        x = x + jnp.where(row < HG_CHUNK - d, pltpu.roll(x, n - d, axis=0), 0.0)
    return x


def _lower_bound(lb_ref, slot):
    rows = [lb_ref[j, 0] for j in range(lb_ref.shape[0])]
    m = functools.reduce(jnp.maximum, rows)
    es = [jnp.exp(r - m) for r in rows]
    return sum(es[:slot + 1]) / sum(es)


N_HGRN_IN = 10


def _hgrn_kernel(*refs, slot, n_ctx_chunks, n_lat_chunks, n_cast):
    hc_ref, hl_ref, g_ref, wq_ref, wv_ref, wf_ref, wb_ref, lbf_ref, lbb_ref, gn_ref = refs[:N_HGRN_IN]
    cast_in = refs[N_HGRN_IN:N_HGRN_IN + n_cast]
    y_ref = refs[N_HGRN_IN + n_cast]
    cast_out = refs[N_HGRN_IN + n_cast + 1:N_HGRN_IN + 2 * n_cast + 1]
    (wbuf, pbuf0, pbuf1, pbuf2, pbuf3, oacc, qif, qib,
     kvf, kvb, decf, decb, spf, spb) = refs[N_HGRN_IN + 2 * n_cast + 1:]
    for c_ref, o_ref in zip(cast_in, cast_out):
        o_ref[...] = c_ref[...].astype(BF16)

    lb_f = _lower_bound(lbf_ref, slot)
    lb_b = _lower_bound(lbb_ref, slot)
    row = lax.broadcasted_iota(jnp.int32, (HG_BLOCK, HG_DK), 0) & (HG_CHUNK - 1)
    ci = lax.broadcasted_iota(jnp.int32, (HG_CPB, HG_CHUNK, HG_CHUNK), 1)
    si = lax.broadcasted_iota(jnp.int32, (HG_CPB, HG_CHUNK, HG_CHUNK), 2)

    def c3(t):
        return t.reshape(HG_CPB, HG_CHUNK, HG_DK)

    def direction(qs3, v3b, z, lb, fwd):
        sig = _sigmoid_tanh(z)
        f = lb + (1.0 - lb) * sig
        k3 = c3((1.0 - lb) * (1.0 - sig))
        lf = jnp.log(f)
        if fwd:
            cum = c3(_chunk_prefix(lf, row))
            ref = cum[:, HG_CHUNK // 2 - 1:HG_CHUNK // 2, :]
            last = cum[:, HG_CHUNK - 1:HG_CHUNK, :]
        else:
            cum = c3(_chunk_suffix(lf, row))
            ref = cum[:, HG_CHUNK // 2:HG_CHUNK // 2 + 1, :]
            last = cum[:, 0:1, :]
        dec = jnp.exp(last)
        if qs3 is None:
            kl = k3 * jnp.exp(last - cum)
        else:
            e1 = jnp.exp(cum - ref)
            qd = qs3 * e1
            qi = qd * jnp.exp(ref)
            kd = k3 * (1.0 / e1)
            kl = kd * jnp.exp(last - ref)
        kvt = jnp.einsum('ncv,nck->nvk', v3b, kl.astype(BF16), preferred_element_type=F32)
        if qs3 is None:
            return None, None, kvt, dec
        sc = jnp.einsum('nck,nsk->ncs', qd.astype(BF16), kd.astype(BF16), preferred_element_type=F32)
        sc = jnp.where((ci >= si) if fwd else (ci <= si), sc, 0.0)
        intra = jnp.einsum('ncs,nsv->ncv', sc.astype(BF16), v3b, preferred_element_type=F32)
        return intra, qi, kvt, dec

    for p, wp_ref in enumerate((wq_ref, wv_ref, wf_ref, wb_ref)):
        wbuf[:, p * HG_DK:(p + 1) * HG_DK] = wp_ref[...].astype(BF16)

    def project(hrows):
        return jnp.dot(hrows, wbuf[...], preferred_element_type=F32)

    def block(p, chunk0, lat_row0):
        v3b = c3(p[:, HG_DK:2 * HG_DK]).astype(BF16)
        zf = p[:, 2 * HG_DK:3 * HG_DK]
        zb = p[:, 3 * HG_DK:4 * HG_DK]
        if lat_row0 is None:
            qs3 = None
        else:
            qs3 = c3(_silu_tanh(p[:, 0:HG_DK]))
        in_f, qi_f, kv_f, dec_f = direction(qs3, v3b, zf, lb_f, True)
        in_b, qi_b, kv_b, dec_b = direction(qs3, v3b, zb, lb_b, False)
        kvf[pl.ds(chunk0, HG_CPB)] = kv_f
        kvb[pl.ds(chunk0, HG_CPB)] = kv_b
        decf[pl.ds(chunk0, HG_CPB)] = dec_f
        decb[pl.ds(chunk0, HG_CPB)] = dec_b
        if lat_row0 is not None:
            rows = pl.ds(lat_row0, HG_BLOCK)
            oacc[rows, :] = (in_f + in_b).reshape(HG_BLOCK, HG_DK)
            qif[rows, :] = qi_f.reshape(HG_BLOCK, HG_DK).astype(BF16)
            qib[rows, :] = qi_b.reshape(HG_BLOCK, HG_DK).astype(BF16)

    n_blocks = n_lat_chunks // HG_CPB

    def lat_rows(i):
        return hl_ref[0, pl.ds(pl.multiple_of(i * HG_BLOCK, HG_BLOCK), HG_BLOCK), :]

    def lat_terms(p_ref, i):
        block(p_ref[...], n_ctx_chunks + i * HG_CPB, pl.multiple_of(i * HG_BLOCK, HG_BLOCK))

    ctx_p = [project(hc_ref[0, i * HG_BLOCK:(i + 1) * HG_BLOCK, :]) for i in range(n_ctx_chunks // HG_CPB)]
    pbuf0[...] = project(lat_rows(0))
    for i, p in enumerate(ctx_p):
        block(p, i * HG_CPB, None)
    pbuf1[...] = project(lat_rows(1))

    def lat_quad(i0, last):
        pbuf2[...] = project(lat_rows(i0 + 2))
        lat_terms(pbuf0, i0)
        pbuf3[...] = project(lat_rows(i0 + 3))
        lat_terms(pbuf1, i0 + 1)
        if not last:
            pbuf0[...] = project(lat_rows(i0 + 4))
        lat_terms(pbuf2, i0 + 2)
        if not last:
            pbuf1[...] = project(lat_rows(i0 + 5))
        lat_terms(pbuf3, i0 + 3)

    def lat_body(t, carry):
        lat_quad(4 * t, False)
        return carry

    lax.fori_loop(0, n_blocks // 4 - 1, lat_body, 0)
    lat_quad(n_blocks - 4, True)

    def advance(s, kv_ref, dec_ref, n):
        return dec_ref[n] * s + kv_ref[n]

    sf = jnp.zeros((HG_DK, HG_DK), F32)
    for n in range(n_ctx_chunks):
        sf = advance(sf, kvf, decf, n)
    sb = jnp.zeros((HG_DK, HG_DK), F32)
    for n in reversed(range(n_ctx_chunks)):
        sb = advance(sb, kvb, decb, n)

    def scan_step(t, carry):
        sf, sb = carry
        jb = n_lat_chunks - 1 - t
        spf[t] = sf.astype(BF16)
        spb[jb] = sb.astype(BF16)
        return (advance(sf, kvf, decf, n_ctx_chunks + t), advance(sb, kvb, decb, n_ctx_chunks + jb))

    lax.fori_loop(0, n_lat_chunks, scan_step, (sf, sb), unroll=2)

    gn = gn_ref[...]

    def block_rows(i):
        return pl.ds(pl.multiple_of(i * HG_BLOCK, HG_BLOCK), HG_BLOCK)

    def inter(i):
        rows = block_rows(i)
        chunks = pl.ds(i * HG_CPB, HG_CPB)
        return (jnp.einsum('nck,nvk->ncv', c3(qif[rows, :]), spf[chunks], preferred_element_type=F32)
                + jnp.einsum('nck,nvk->ncv', c3(qib[rows, :]), spb[chunks], preferred_element_type=F32)
                ).reshape(HG_BLOCK, HG_DK)

    def readout(o_ref, i):
        rows = block_rows(i)
        o = oacc[rows, :] + o_ref[...]
        ms = jnp.mean(o * o, axis=-1, keepdims=True)
        o = o * lax.rsqrt(ms + EPS) * gn
        y_ref[0, rows, :] = (o * _silu_tanh(g_ref[0, rows, :].astype(F32))).astype(BF16)

    obuf0, obuf1 = pbuf0.at[:, 0:HG_DK], pbuf1.at[:, 0:HG_DK]
    obuf0[...] = inter(0)

    def readout_pair(t, carry):
        obuf1[...] = inter(2 * t + 1)
        readout(obuf0, 2 * t)
        obuf0[...] = inter(2 * t + 2)
        readout(obuf1, 2 * t + 1)
        return carry

    lax.fori_loop(0, n_blocks // 2 - 1, readout_pair, 0)
    obuf1[...] = inter(n_blocks - 1)
    readout(obuf0, n_blocks - 2)
    readout(obuf1, n_blocks - 1)


def _hgrn_call(h_ctx, h_lat, g_lat, w_in, lb_fwd, lb_bwd, gnorm, slot, to_cast):
    b, lc, d = h_ctx.shape
    ll = h_lat.shape[1]
    nrow = lb_fwd.shape[0]
    ncc, nlc = lc // HG_CHUNK, ll // HG_CHUNK
    kern = functools.partial(_hgrn_kernel, slot=slot, n_ctx_chunks=ncc, n_lat_chunks=nlc, n_cast=len(to_cast))
    lb_spec = pl.BlockSpec((nrow, 1, 1, HG_DK), lambda i, h: (0, h, 0, 0))
    head_cols = pl.BlockSpec((1, ll, HG_DK), lambda i, h: (i, 0, h))
    n_rec = N_HG_PROJ - 1
    w_cols = [pl.BlockSpec((None, d, HG_DK), functools.partial(lambda i, h, p: (slot, 0, p * HG_HEADS + h), p=p))
              for p in range(n_rec)]
    steps = b * HG_HEADS
    slabs = [a.reshape(steps, -1, *a.shape[-2:]) for a in to_cast]
    slab_specs = [pl.BlockSpec((1,) + a.shape[1:], lambda i, h: (i * HG_HEADS + h, 0, 0, 0)) for a in slabs]
    outs = pl.pallas_call(
        kern,
        grid=(b, HG_HEADS),
        in_specs=[
            pl.BlockSpec((1, lc, d), lambda i, h: (i, 0, 0)),
            pl.BlockSpec((1, ll, d), lambda i, h: (i, 0, 0)),
            head_cols,
            *w_cols,
            lb_spec, lb_spec,
            pl.BlockSpec((1, HG_DK), lambda i, h: (0, 0)),
            *slab_specs,
        ],
        out_specs=[head_cols, *slab_specs],
        out_shape=[jax.ShapeDtypeStruct((b, ll, d), BF16)]
        + [jax.ShapeDtypeStruct(a.shape, BF16) for a in slabs],
        scratch_shapes=[
            pltpu.VMEM((d, n_rec * HG_DK), BF16),
            pltpu.VMEM((HG_BLOCK, n_rec * HG_DK), F32),
            pltpu.VMEM((HG_BLOCK, n_rec * HG_DK), F32),
            pltpu.VMEM((HG_BLOCK, n_rec * HG_DK), F32),
            pltpu.VMEM((HG_BLOCK, n_rec * HG_DK), F32),
            pltpu.VMEM((ll, HG_DK), F32),
            pltpu.VMEM((ll, HG_DK), BF16),
            pltpu.VMEM((ll, HG_DK), BF16),
            pltpu.VMEM((ncc + nlc, HG_DK, HG_DK), F32),
            pltpu.VMEM((ncc + nlc, HG_DK, HG_DK), F32),
            pltpu.VMEM((ncc + nlc, 1, HG_DK), F32),
            pltpu.VMEM((ncc + nlc, 1, HG_DK), F32),
            pltpu.VMEM((nlc, HG_DK, HG_DK), BF16),
            pltpu.VMEM((nlc, HG_DK, HG_DK), BF16),
        ],
        compiler_params=_cparams(("parallel", "arbitrary"), HGRN_VMEM_LIMIT),
        name="hgrn2",
    )(h_ctx, h_lat, g_lat, *([w_in] * n_rec),
      lb_fwd.reshape(nrow, HG_HEADS, 1, HG_DK), lb_bwd.reshape(nrow, HG_HEADS, 1, HG_DK),
      gnorm.reshape(1, HG_DK), *slabs)
    return outs[0], [o.reshape(a.shape) for o, a in zip(outs[1:], to_cast)]


POOL_STRIP = 8
POOL_PAD = 8
POOL_BAND = 256


def _pool_group(h_ref, win_ref, wg_ref, ps_ref, o_ref, upad, *, win, rows):
    gd = win_ref.shape[1]
    half = win // 2
    u = jnp.dot(h_ref[0], win_ref[...].astype(BF16), preferred_element_type=F32)
    zeros = jnp.zeros((POOL_PAD, GRID_W, gd), F32)
    upad[0:POOL_PAD] = zeros
    upad[POOL_PAD + rows:POOL_PAD + rows + POOL_PAD] = zeros
    upad[POOL_PAD:POOL_PAD + rows] = u.reshape(rows, GRID_W, gd)

    tok = POOL_STRIP * GRID_W
    bi = lax.broadcasted_iota(jnp.int32, (POOL_BAND, POOL_BAND), 0)
    bj = lax.broadcasted_iota(jnp.int32, (POOL_BAND, POOL_BAND), 1)
    lo_c = (bi & (GRID_W - 1)) - half
    cj = bj & (GRID_W - 1)
    band = jnp.where(bi >> GRID_SHIFT == bj >> GRID_SHIFT, 1.0, 0.0)
    band = jnp.where(cj >= lo_c, band, 0.0)
    band = jnp.where(cj < lo_c + win, band, 0.0).astype(BF16)
    t = lax.broadcasted_iota(jnp.int32, (tok, LANES), 0)
    col = t & (GRID_W - 1)
    cnt_c = jnp.minimum(col - half + win, GRID_W) - jnp.maximum(col - half, 0)
    wg = wg_ref[0].astype(BF16)
    ps = ps_ref[...]

    def strip(i, carry):
        r0 = i * POOL_STRIP
        slab = upad[pl.ds(r0 + POOL_PAD - half, POOL_STRIP + win - 1)]
        span = 1
        while span < win:
            n = slab.shape[0] - span
            slab = slab[0:n] + slab[span:span + n]
            span *= 2
        rs = slab.reshape(tok, gd)
        hi = rs.astype(BF16)
        lo = (rs - hi.astype(F32)).astype(BF16)
        parts = []
        for k in range(tok // POOL_BAND):
            sl = slice(k * POOL_BAND, (k + 1) * POOL_BAND)
            parts.append(jnp.dot(band, hi[sl], preferred_element_type=F32)
                         + jnp.dot(band, lo[sl], preferred_element_type=F32))
        box = jnp.concatenate(parts, axis=0)
        r = r0 + (t >> GRID_SHIFT)
        cnt_r = jnp.minimum(r - half + win, rows) - jnp.maximum(r - half, 0)
        inv = 1.0 / (cnt_r * cnt_c).astype(F32)
        mean = box * jnp.concatenate([inv] * (gd // LANES), axis=1)
        ug = upad[pl.ds(r0 + POOL_PAD, POOL_STRIP)].reshape(tok, gd)
        z = jnp.dot((mean - ug).astype(BF16), wg, preferred_element_type=F32) * ps
        o_ref[0, pl.ds(pl.multiple_of(i * tok, tok), tok), :] = z.astype(BF16)
        return carry

    lax.fori_loop(0, rows // POOL_STRIP, strip, 0, unroll=2)


def _pool_kernel(h_ref, win_ref, wg_ref, ps_ref, o_ref, upad, *, rows):
    g = pl.program_id(1)
    for gi, win in enumerate(POOL_WINDOWS):
        @pl.when(g == gi)
        def _(win=win):
            _pool_group(h_ref, win_ref, wg_ref, ps_ref, o_ref, upad, win=win, rows=rows)


def _pool_call(h, w_in, w_grp, p_scale, slot):
    b, l, d = h.shape
    ng = len(POOL_WINDOWS)
    gd = d // ng
    rows = l // GRID_W
    return pl.pallas_call(
        functools.partial(_pool_kernel, rows=rows),
        grid=(b, ng),
        in_specs=[
            pl.BlockSpec((1, l, d), lambda i, g: (i, 0, 0)),
            pl.BlockSpec((None, d, gd), lambda i, g: (slot, 0, g)),
            pl.BlockSpec((None, 1, gd, gd), lambda i, g: (slot, g, 0, 0)),
            pl.BlockSpec((None, 1, gd), lambda i, g: (slot, 0, g)),
        ],
        out_specs=pl.BlockSpec((1, l, gd), lambda i, g: (i, 0, g)),
        out_shape=jax.ShapeDtypeStruct((b, l, d), BF16),
        scratch_shapes=[pltpu.VMEM((rows + 2 * POOL_PAD, GRID_W, gd), F32)],
        compiler_params=_cparams(("parallel", "arbitrary")),
        name="pool_mix",
    )(h, w_in, w_grp, p_scale.reshape(p_scale.shape[0], 1, d))


MOE_TILE = 1024
MOE_WINDOWS = (128, 192, 256, 320, 384)
MOE_FULL = 256
ROW_ALIGN = 16
DEST_LANE = N_EXPERTS
PIECE_STRIDE = 32
N_PIECES = 3
AUX_ROWS = 8


def _route(sel, s):
    keep = []
    gsum = []
    for g in range(N_EXPERT_GROUPS):
        a = sel[g * EXPERTS_PER_GROUP:(g + 1) * EXPERTS_PER_GROUP]
        beaten = [jnp.zeros_like(a[0]) for _ in a]
        for i in range(EXPERTS_PER_GROUP):
            for j in range(i + 1, EXPERTS_PER_GROUP):
                ge = jnp.where(a[i] >= a[j], 1.0, 0.0)
                beaten[j] = beaten[j] + ge
                beaten[i] = beaten[i] + (1.0 - ge)
        kg = [jnp.where(bt < 1.5, 1.0, 0.0) for bt in beaten]
        keep.append(kg)
        gsum.append(sum(k * x for k, x in zip(kg, a)))
    picked = []
    bests = []
    for g in range(N_EXPERT_GROUPS):
        better = jnp.zeros_like(gsum[0])
        for o in range(N_EXPERT_GROUPS):
            if o < g:
                better = better + jnp.where(gsum[o] >= gsum[g], 1.0, 0.0)
            elif o > g:
                better = better + jnp.where(gsum[o] > gsum[g], 1.0, 0.0)
        best = jnp.where(better < 0.5, 1.0, 0.0)
        bests.append(best)
        for i in range(EXPERTS_PER_GROUP):
            picked.append(best * keep[g][i] * s[g * EXPERTS_PER_GROUP + i])
    den = sum(picked)
    return [p / den for p in picked], bests


def _post_kernel(x_ref, y_ref, w_ref, mod_ref, g_ref, rw_ref, rb_ref,
                 xo_ref, h2_ref, rt_ref, aux_ref, wbuf):
    @pl.when(pl.program_id(0) == 0)
    def _():
        wbuf[...] = w_ref[...].astype(BF16)

    yw = jnp.dot(y_ref[...], wbuf[...], preferred_element_type=F32)
    xn = x_ref[...] + mod_ref[0, 2:3, :] * yw
    xo_ref[...] = xn
    h2 = _modulate(xn, g_ref[...], mod_ref[0, 3:4, :], mod_ref[0, 4:5, :])
    hi = h2.astype(BF16)
    h2_ref[...] = hi
    lo = (h2 - hi.astype(F32)).astype(BF16)
    rw = rw_ref[...]
    half = hi.shape[0] // 2
    prod = jnp.concatenate(
        [jnp.dot(hi[r:r + half], rw, preferred_element_type=F32)
         + jnp.dot(lo[r:r + half], rw, preferred_element_type=F32) for r in (0, half)], axis=0).T
    logits = prod[0:N_EXPERTS, :] + prod[N_EXPERTS:2 * N_EXPERTS, :]
    s = _sigmoid_pair(logits)[0]
    sel = s + rb_ref[...]
    comb, bests = _route([sel[e:e + 1, :] for e in range(N_EXPERTS)],
                         [s[e:e + 1, :] for e in range(N_EXPERTS)])
    t = logits.shape[1]

    ind = jnp.concatenate(bests + [jnp.zeros((AUX_ROWS - N_EXPERT_GROUPS, t), F32)], axis=0)
    n_seg = t // LANES
    stacked = jnp.concatenate([ind[:, j * LANES:(j + 1) * LANES] for j in range(n_seg)], axis=0)
    upper = jnp.where(lax.broadcasted_iota(jnp.int32, (LANES, LANES), 0)
                      <= lax.broadcasted_iota(jnp.int32, (LANES, LANES), 1), 1.0, 0.0).astype(BF16)
    local = jnp.dot(stacked.astype(BF16), upper, preferred_element_type=F32)
    off = jnp.zeros((AUX_ROWS, 1), F32)
    pieces = []
    for j in range(n_seg):
        seg_cum = local[j * AUX_ROWS:(j + 1) * AUX_ROWS, :]
        pieces.append(seg_cum + off)
        off = off + seg_cum[:, LANES - 1:LANES]
    cum = jnp.concatenate(pieces, axis=1)
    counts = [off[g:g + 1, :] for g in range(N_EXPERT_GROUPS)]
    starts = [jnp.zeros((1, 1), F32)]
    for g in range(1, N_EXPERT_GROUPS):
        starts.append(starts[-1] + counts[g - 1])
    dest = sum(bests[g] * (starts[g] + cum[g:g + 1, :] - 1.0) for g in range(N_EXPERT_GROUPS))

    lane = lax.broadcasted_iota(jnp.int32, (1, t), 1)
    seg = sum(jnp.where(lane == k, v, 0.0) for k, v in enumerate(starts + counts))
    aux_ref[0] = jnp.concatenate([dest, seg, jnp.zeros((AUX_ROWS - 2, t), F32)], axis=0)
    p1 = [w.astype(BF16).astype(F32) for w in comb]
    r1 = [w - p for w, p in zip(comb, p1)]
    p2 = [r.astype(BF16).astype(F32) for r in r1]
    p3 = [r - p for r, p in zip(r1, p2)]
    pad = [jnp.zeros((PIECE_STRIDE - N_EXPERTS, t), F32)]
    table = jnp.concatenate(p1 + [dest] + [jnp.zeros((PIECE_STRIDE - N_EXPERTS - 1, t), F32)] + p2 + pad + p3 + pad
                            + [jnp.zeros((LANES - N_PIECES * PIECE_STRIDE, t), F32)], axis=0)
    rt_ref[...] = table.T


def _post_call(x, y, w, w_slot, mod, g, router_w, router_b, tokens_per_batch, tile=MOE_TILE):
    n, d = x.shape
    per_b = tokens_per_batch // tile
    rwh = router_w.astype(BF16)
    rwl = (router_w - rwh.astype(F32)).astype(BF16)
    rw = jnp.concatenate([rwh, rwl, jnp.zeros((d, LANES - 2 * N_EXPERTS), BF16)], axis=1)
    row = lambda i: (i, 0)
    fixed = lambda i: (0, 0)
    return pl.pallas_call(
        _post_kernel,
        grid=(n // tile,),
        in_specs=[
            pl.BlockSpec((tile, d), row),
            pl.BlockSpec((tile, d), row),
            pl.BlockSpec((None, d, d), lambda i: (w_slot, 0, 0)),
            pl.BlockSpec((1, N_MOD, d), lambda i: (i // per_b, 0, 0)),
            pl.BlockSpec((1, d), fixed),
            pl.BlockSpec((d, LANES), fixed),
            pl.BlockSpec((N_EXPERTS, 1), fixed),
        ],
        out_specs=[
            pl.BlockSpec((tile, d), row),
            pl.BlockSpec((tile, d), row),
            pl.BlockSpec((tile, LANES), row),
            pl.BlockSpec((1, AUX_ROWS, tile), lambda i: (i, 0, 0)),
        ],
        out_shape=[
            jax.ShapeDtypeStruct((n, d), F32),
            jax.ShapeDtypeStruct((n, d), BF16),
            jax.ShapeDtypeStruct((n, LANES), F32),
            jax.ShapeDtypeStruct((n // tile, AUX_ROWS, tile), F32),
        ],
        scratch_shapes=[pltpu.VMEM((d, d), BF16)],
        compiler_params=_cparams(("arbitrary",)),
        name="mixer_out_router",
    )(x, y, w, mod, g.reshape(1, d), rw, router_b.reshape(N_EXPERTS, 1))


def _moe_kernel(seg_ref, h_ref, rt_ref, aux_ref, wg_ref, wu_ref, wd_ref, x_ref, mod_ref, ng_ref, nmod_ref,
                o_ref, *rest, final_norm):
    if final_norm:
        xs, cws, ys = rest
    else:
        hn_ref, xs, cws, ys = rest
    i = pl.program_id(0)
    g = pl.program_id(1)
    t = h_ref.shape[0]

    @pl.when(g == 0)
    def _():
        dest_row = aux_ref[0, 0:1, :]
        perm = jnp.where(lax.broadcasted_iota(jnp.int32, (t, t), 0).astype(F32) == dest_row,
                         1.0, 0.0).astype(BF16)
        xs[...] = jnp.dot(perm, h_ref[...], preferred_element_type=F32).astype(BF16)
        cws[...] = jnp.dot(perm, rt_ref[...].astype(BF16), preferred_element_type=F32)
        ys[...] = jnp.zeros_like(ys)

    start = seg_ref[i, g]
    end = start + seg_ref[i, N_EXPERT_GROUPS + g]
    wd = wd_ref[...].reshape(EXPERTS_PER_GROUP * D_EXPERT, wd_ref.shape[2])

    def window(w0, rows, lo, hi):
        sl = pl.ds(pl.multiple_of(w0, ROW_ALIGN), rows)
        xw = xs[sl, :]
        cw = cws[sl, :]
        r = w0 + lax.broadcasted_iota(jnp.int32, cw.shape, 0)
        lane = lax.broadcasted_iota(jnp.int32, cw.shape, 1)
        piece_lane = jnp.where(lane < N_PIECES * PIECE_STRIDE, lane & (PIECE_STRIDE - 1), -1)
        cw = jnp.where(r >= lo, cw, 0.0)
        cw = jnp.where(r < hi, cw, 0.0)
        acts = []
        for e in range(EXPERTS_PER_GROUP):
            ce = jnp.sum(jnp.where(piece_lane == g * EXPERTS_PER_GROUP + e, cw, 0.0), axis=1, keepdims=True)
            ge = g * EXPERTS_PER_GROUP + e
            a = _silu_tanh(jnp.dot(xw, wg_ref[ge], preferred_element_type=F32)) \
                * jnp.dot(xw, wu_ref[ge], preferred_element_type=F32)
            acts.append((a * ce).astype(BF16))
        ys[sl, :] += jnp.dot(jnp.concatenate(acts, axis=1), wd, preferred_element_type=F32)

    aligned = start & -ROW_ALIGN
    need = end - aligned
    biggest = MOE_WINDOWS[-1]
    n_full = jnp.where(need > biggest, (need - (MOE_WINDOWS[0] + 1)) >> (MOE_FULL.bit_length() - 1), 0)

    def full(k, carry):
        w0 = aligned + k * MOE_FULL
        window(w0, MOE_FULL, jnp.maximum(start, w0), w0 + MOE_FULL)
        return carry

    lax.fori_loop(0, n_full, full, 0)
    rest0 = aligned + n_full * MOE_FULL
    rest = end - rest0
    for smaller, m in zip((0,) + MOE_WINDOWS[:-1], MOE_WINDOWS):
        @pl.when((end > start) & (rest > smaller) & (rest <= m))
        def _(m=m):
            window(jnp.minimum(rest0, t - m), m, jnp.maximum(start, rest0), end)

    @pl.when(g == N_EXPERT_GROUPS - 1)
    def _():
        dest_col = rt_ref[:, DEST_LANE:DEST_LANE + 1]
        unperm = jnp.where(lax.broadcasted_iota(jnp.int32, (t, t), 1).astype(F32) == dest_col,
                           1.0, 0.0).astype(BF16)
        ff = jnp.dot(unperm, ys[...].astype(BF16), preferred_element_type=F32)
        xn = x_ref[...] + mod_ref[0, 5:6, :] * ff
        if final_norm:
            ms = jnp.mean(xn * xn, axis=-1, keepdims=True)
            xn = xn * lax.rsqrt(ms + EPS) * ng_ref[...]
        else:
            hn_ref[...] = _modulate(xn, ng_ref[...], nmod_ref[0, 0:1, :], nmod_ref[0, 1:2, :]).astype(BF16)
        o_ref[...] = xn


def _moe_call(h2, table, aux, wg, wu, wd, layer, x, mod, norm_g, next_mod, tokens_per_batch, final_norm):
    n, d = x.shape
    tile = MOE_TILE
    per_b = tokens_per_batch // tile
    seg = aux[:, 1, :2 * N_EXPERT_GROUPS].astype(jnp.int32)
    row = lambda i, g, seg: (i, 0)
    grp = lambda i, g, seg: (layer, g, 0, 0)
    tok = pl.BlockSpec((tile, d), row)
    x_out = jax.ShapeDtypeStruct((n, d), F32)
    grid_spec = pltpu.PrefetchScalarGridSpec(
        num_scalar_prefetch=1,
        grid=(n // tile, N_EXPERT_GROUPS),
        in_specs=[
            pl.BlockSpec((tile, d), row),
            pl.BlockSpec((tile, LANES), row),
            pl.BlockSpec((1, AUX_ROWS, tile), lambda i, g, seg: (i, 0, 0)),
            pl.BlockSpec((None, N_EXPERTS, d, D_EXPERT), lambda i, g, seg: (layer, 0, 0, 0)),
            pl.BlockSpec((None, N_EXPERTS, d, D_EXPERT), lambda i, g, seg: (layer, 0, 0, 0)),
            pl.BlockSpec((None, EXPERTS_PER_GROUP, D_EXPERT, d), grp),
            pl.BlockSpec((tile, d), row),
            pl.BlockSpec((1, N_MOD, d), lambda i, g, seg: (i // per_b, 0, 0)),
            pl.BlockSpec((1, d), lambda i, g, seg: (0, 0)),
            pl.BlockSpec((1, N_MOD, d), lambda i, g, seg: (i // per_b, 0, 0)),
        ],
        out_specs=tok if final_norm else [tok, tok],
        scratch_shapes=[
            pltpu.VMEM((tile, d), BF16),
            pltpu.VMEM((tile, LANES), F32),
            pltpu.VMEM((tile, d), F32),
        ],
    )
    return pl.pallas_call(
        functools.partial(_moe_kernel, final_norm=final_norm),
        grid_spec=grid_spec,
        out_shape=x_out if final_norm else [x_out, jax.ShapeDtypeStruct((n, d), BF16)],
        compiler_params=_cparams(("parallel", "arbitrary"), MOE_VMEM_LIMIT),
        name="moe_ffn",
    )(seg, h2, table, aux, wg, wu, wd, x, mod, norm_g.reshape(1, d), next_mod)


def kernel(x, c, ctx, c_ctx, w_mod, b_mod, norm1_g, norm2_g, hg_w_in, hg_lb_fwd, hg_lb_bwd, hg_gnorm,
           hg_w_out, pool_w_in, pool_w_grp, pool_scale, pool_w_out, router_w, router_b, moe_w_gate,
           moe_w_up, moe_w_down, final_g):
    b, l, d = x.shape
    depth = w_mod.shape[0]
    n_mixers = 2

    cc = jnp.concatenate([c, c_ctx[None, :], jnp.zeros((MOD_ROWS - b - 1, d), F32)], axis=0)
    mods = _mod_call(cc, w_mod, b_mod)

    experts = None
    x_lat = x.reshape(b * l, d)
    h_lat = None
    for i in range(depth):
        slot = i // n_mixers
        mod_lat = mods[i, :b].reshape(b, N_MOD, d)
        if i % n_mixers == 0:
            h_lat, g_lat = _norm_call(x_lat.reshape(b, l, d), norm1_g[i], mod_lat, 1024,
                                      hg_w_in, slot, N_HG_PROJ - 1)
            mod_ctx = jnp.broadcast_to(mods[i, b].reshape(1, N_MOD, d), (b, N_MOD, d))
            h_ctx = _norm_call(ctx, norm1_g[i], mod_ctx, ctx.shape[1])
            to_cast = (moe_w_gate, moe_w_up, moe_w_down) if experts is None else ()
            y, cast = _hgrn_call(h_ctx, h_lat, g_lat, hg_w_in, hg_lb_fwd, hg_lb_bwd, hg_gnorm[slot], slot, to_cast)
            experts = experts or tuple(cast)
            w_out = hg_w_out
        else:
            if h_lat is None:
                h_lat = _norm_call(x_lat.reshape(b, l, d), norm1_g[i], mod_lat, 512)
            y = _pool_call(h_lat, pool_w_in, pool_w_grp, pool_scale, slot)
            w_out = pool_w_out
        x_lat, h2, table, aux = _post_call(x_lat, y.reshape(b * l, d), w_out, slot, mod_lat,
                                           norm2_g[i], router_w, router_b, l)
        if experts is None:
            experts = (moe_w_gate.astype(BF16), moe_w_up.astype(BF16), moe_w_down.astype(BF16))
        if i == depth - 1:
            x_lat = _moe_call(h2, table, aux, *experts, i, x_lat, mod_lat, final_g, mod_lat, l, final_norm=True)
        else:
            mod_next = mods[i + 1, :b].reshape(b, N_MOD, d)
            x_lat, h_next = _moe_call(h2, table, aux, *experts, i, x_lat, mod_lat, norm1_g[i + 1], mod_next, l,
                                      final_norm=False)
            h_lat = h_next.reshape(b, l, d) if (i + 1) % n_mixers != 0 else None
    return x_lat.reshape(b, l, d)
```

```python
import functools

import jax
import jax.numpy as jnp
from jax import lax
from jax.experimental import pallas as pl
from jax.experimental.pallas import tpu as pltpu

F32 = jnp.float32
BF16 = jnp.bfloat16

EPS = 1e-6
N_MOD = 6
HG_HEADS = 8
HG_DK = 128
HG_CHUNK = 64
N_HG_PROJ = 5
POOL_WINDOWS = (2, 4, 8, 16)
GRID_W = 64
GRID_SHIFT = GRID_W.bit_length() - 1
assert 1 << GRID_SHIFT == GRID_W
N_EXPERTS = 16
N_EXPERT_GROUPS = 4
EXPERTS_PER_GROUP = N_EXPERTS // N_EXPERT_GROUPS
D_EXPERT = 256

LANES = 128
MOD_ROWS = 8
VMEM_LIMIT = 56 * 1024 * 1024
HGRN_VMEM_LIMIT = 60 * 1024 * 1024
MOE_VMEM_LIMIT = 60 * 1024 * 1024


def _cparams(sem, vmem=VMEM_LIMIT):
    return pltpu.CompilerParams(dimension_semantics=sem, vmem_limit_bytes=vmem)


def _sigmoid_pair(z):
    e = jnp.exp(-jnp.abs(z))
    r = 1.0 / (1.0 + e)
    er = e * r
    pos = z >= 0
    return jnp.where(pos, r, er), jnp.where(pos, er, r)


def _silu(z):
    return z * _sigmoid_pair(z)[0]


def _sigmoid_tanh(z):
    return 0.5 * jnp.tanh(0.5 * z) + 0.5


def _silu_tanh(z):
    return z * _sigmoid_tanh(z)


def _split_bf16(x):
    hi = x.astype(BF16)
    return hi, (x - hi.astype(F32)).astype(BF16)


def _mod_kernel(c_ref, w_ref, b_ref, o_ref):
    a = jnp.concatenate(_split_bf16(_silu(c_ref[...])), axis=0)
    w_hi, w_lo = _split_bf16(w_ref[0])
    acc = jnp.dot(a, w_hi, preferred_element_type=F32) + jnp.dot(a, w_lo, preferred_element_type=F32)
    o_ref[0] = acc[0:MOD_ROWS] + acc[MOD_ROWS:2 * MOD_ROWS] + b_ref[0]


def _mod_call(cc, w_mod, b_mod):
    depth, d, n = w_mod.shape
    tn = 1024
    return pl.pallas_call(
        _mod_kernel,
        grid=(depth, n // tn),
        in_specs=[
            pl.BlockSpec((MOD_ROWS, d), lambda i, j: (0, 0)),
            pl.BlockSpec((1, d, tn), lambda i, j: (i, 0, j)),
            pl.BlockSpec((1, 1, tn), lambda i, j: (i, 0, j)),
        ],
        out_specs=pl.BlockSpec((1, MOD_ROWS, tn), lambda i, j: (i, 0, j)),
        out_shape=jax.ShapeDtypeStruct((depth, MOD_ROWS, n), F32),
        compiler_params=_cparams(("parallel", "parallel")),
        name="mod_proj",
    )(cc, w_mod, b_mod.reshape(depth, 1, n))


def _modulate(x, g, shift, scale):
    ms = jnp.mean(x * x, axis=-1, keepdims=True)
    return (x * lax.rsqrt(ms + EPS) * g) * (1.0 + scale) + shift


def _norm_kernel(x_ref, g_ref, mod_ref, *rest):
    h = _modulate(x_ref[0], g_ref[...], mod_ref[0, 0:1, :], mod_ref[0, 1:2, :]).astype(BF16)
    if len(rest) == 1:
        (o_ref,) = rest
    else:
        w_ref, o_ref, p_ref, wbuf = rest

        @pl.when((pl.program_id(0) == 0) & (pl.program_id(1) == 0))
        def _():
            wbuf[...] = w_ref[...].astype(BF16)

        p_ref[0] = jnp.dot(h, wbuf[...], preferred_element_type=F32).astype(BF16)
    o_ref[0] = h


def _norm_call(x, g, mod, tile, w_proj=None, w_slot=0, w_col=0):
    b, t, d = x.shape
    tok = pl.BlockSpec((1, tile, d), lambda i, j: (i, j, 0))
    in_specs = [tok, pl.BlockSpec((1, d), lambda i, j: (0, 0)),
                pl.BlockSpec((1, N_MOD, d), lambda i, j: (i, 0, 0))]
    args = [x, g.reshape(1, d), mod]
    out_specs, out_shape = tok, jax.ShapeDtypeStruct((b, t, d), BF16)
    scratch = []
    if w_proj is not None:
        in_specs.append(pl.BlockSpec((None, d, d), lambda i, j: (w_slot, 0, w_col)))
        args.append(w_proj)
        out_specs, out_shape = [tok, tok], [out_shape, out_shape]
        scratch = [pltpu.VMEM((d, d), BF16)]
    return pl.pallas_call(
        _norm_kernel,
        grid=(b, t // tile),
        in_specs=in_specs,
        out_specs=out_specs,
        out_shape=out_shape,
        scratch_shapes=scratch,
        compiler_params=_cparams(("arbitrary", "arbitrary")),
        name="norm1",
    )(*args)


HG_BLOCK = 256
HG_CPB = HG_BLOCK // HG_CHUNK


def _chunk_prefix(x, row):
    for d in (1, 2, 4, 8, 16, 32):
        x = x + jnp.where(row >= d, pltpu.roll(x, d, axis=0), 0.0)
    return x


def _chunk_suffix(x, row):
    n = x.shape[0]
    for d in (1, 2, 4, 8, 16, 32):
        x = x + jnp.where(row < HG_CHUNK - d, pltpu.roll(x, n - d, axis=0), 0.0)
    return x


def _lower_bound(lb_ref, slot):
    rows = [lb_ref[j, 0] for j in range(lb_ref.shape[0])]
    m = functools.reduce(jnp.maximum, rows)
    es = [jnp.exp(r - m) for r in rows]
    return sum(es[:slot + 1]) / sum(es)


N_HGRN_IN = 10


def _hgrn_kernel(*refs, slot, n_ctx_chunks, n_lat_chunks, n_cast):
    hc_ref, hl_ref, g_ref, wq_ref, wv_ref, wf_ref, wb_ref, lbf_ref, lbb_ref, gn_ref = refs[:N_HGRN_IN]
    cast_in = refs[N_HGRN_IN:N_HGRN_IN + n_cast]
    y_ref = refs[N_HGRN_IN + n_cast]
    cast_out = refs[N_HGRN_IN + n_cast + 1:N_HGRN_IN + 2 * n_cast + 1]
    (wbuf, pbuf0, pbuf1, pbuf2, pbuf3, oacc, qif, qib,
     kvf, kvb, decf, decb, spf, spb) = refs[N_HGRN_IN + 2 * n_cast + 1:]
    for c_ref, o_ref in zip(cast_in, cast_out):
        o_ref[...] = c_ref[...].astype(BF16)

    lb_f = _lower_bound(lbf_ref, slot)
    lb_b = _lower_bound(lbb_ref, slot)
    row = lax.broadcasted_iota(jnp.int32, (HG_BLOCK, HG_DK), 0) & (HG_CHUNK - 1)
    ci = lax.broadcasted_iota(jnp.int32, (HG_CPB, HG_CHUNK, HG_CHUNK), 1)
    si = lax.broadcasted_iota(jnp.int32, (HG_CPB, HG_CHUNK, HG_CHUNK), 2)

    def c3(t):
        return t.reshape(HG_CPB, HG_CHUNK, HG_DK)

    def direction(qs3, v3b, z, lb, fwd):
        sig = _sigmoid_tanh(z)
        f = lb + (1.0 - lb) * sig
        k3 = c3((1.0 - lb) * (1.0 - sig))
        lf = jnp.log(f)
        if fwd:
            cum = c3(_chunk_prefix(lf, row))
            ref = cum[:, HG_CHUNK // 2 - 1:HG_CHUNK // 2, :]
            last = cum[:, HG_CHUNK - 1:HG_CHUNK, :]
        else:
            cum = c3(_chunk_suffix(lf, row))
            ref = cum[:, HG_CHUNK // 2:HG_CHUNK // 2 + 1, :]
            last = cum[:, 0:1, :]
        dec = jnp.exp(last)
        if qs3 is None:
            kl = k3 * jnp.exp(last - cum)
        else:
            e1 = jnp.exp(cum - ref)
            qd = qs3 * e1
            qi = qd * jnp.exp(ref)
            kd = k3 * (1.0 / e1)
            kl = kd * jnp.exp(last - ref)
        kvt = jnp.einsum('ncv,nck->nvk', v3b, kl.astype(BF16), preferred_element_type=F32)
        if qs3 is None:
            return None, None, kvt, dec
        sc = jnp.einsum('nck,nsk->ncs', qd.astype(BF16), kd.astype(BF16), preferred_element_type=F32)
        sc = jnp.where((ci >= si) if fwd else (ci <= si), sc, 0.0)
        intra = jnp.einsum('ncs,nsv->ncv', sc.astype(BF16), v3b, preferred_element_type=F32)
        return intra, qi, kvt, dec

    for p, wp_ref in enumerate((wq_ref, wv_ref, wf_ref, wb_ref)):
        wbuf[:, p * HG_DK:(p + 1) * HG_DK] = wp_ref[...].astype(BF16)

    def project(hrows):
        return jnp.dot(hrows, wbuf[...], preferred_element_type=F32)

    def block(p, chunk0, lat_row0, between=None):
        v3b = c3(p[:, HG_DK:2 * HG_DK]).astype(BF16)
        zf = p[:, 2 * HG_DK:3 * HG_DK]
        zb = p[:, 3 * HG_DK:4 * HG_DK]
        if lat_row0 is None:
            qs3 = None
        else:
            qs3 = c3(_silu_tanh(p[:, 0:HG_DK]))
        in_f, qi_f, kv_f, dec_f = direction(qs3, v3b, zf, lb_f, True)
        if between is not None:
            between()
        in_b, qi_b, kv_b, dec_b = direction(qs3, v3b, zb, lb_b, False)
        kvf[pl.ds(chunk0, HG_CPB)] = kv_f
        kvb[pl.ds(chunk0, HG_CPB)] = kv_b
        decf[pl.ds(chunk0, HG_CPB)] = dec_f
        decb[pl.ds(chunk0, HG_CPB)] = dec_b
        if lat_row0 is not None:
            rows = pl.ds(lat_row0, HG_BLOCK)
            oacc[rows, :] = (in_f + in_b).reshape(HG_BLOCK, HG_DK)
            qif[rows, :] = qi_f.reshape(HG_BLOCK, HG_DK).astype(BF16)
            qib[rows, :] = qi_b.reshape(HG_BLOCK, HG_DK).astype(BF16)

    n_blocks = n_lat_chunks // HG_CPB

    def lat_rows(i):
        return hl_ref[0, pl.ds(pl.multiple_of(i * HG_BLOCK, HG_BLOCK), HG_BLOCK), :]

    def lat_terms(p_ref, i, between=None):
        block(p_ref[...], n_ctx_chunks + i * HG_CPB, pl.multiple_of(i * HG_BLOCK, HG_BLOCK), between)

    half_n = 2 * HG_DK

    def project_half(dst, i, k):
        c = slice(k * half_n, (k + 1) * half_n)
        dst[:, c] = jnp.dot(lat_rows(i), wbuf[:, c], preferred_element_type=F32)

    ctx_p = [project(hc_ref[0, i * HG_BLOCK:(i + 1) * HG_BLOCK, :]) for i in range(n_ctx_chunks // HG_CPB)]
    pbuf0[...] = project(lat_rows(0))
    for i, p in enumerate(ctx_p):
        block(p, i * HG_CPB, None)
    pbuf1[...] = project(lat_rows(1))

    def lat_quad(i0, last):
        project_half(pbuf2, i0 + 2, 0)
        lat_terms(pbuf0, i0, lambda: project_half(pbuf2, i0 + 2, 1))
        project_half(pbuf3, i0 + 3, 0)
        lat_terms(pbuf1, i0 + 1, lambda: project_half(pbuf3, i0 + 3, 1))
        if not last:
            project_half(pbuf0, i0 + 4, 0)
        lat_terms(pbuf2, i0 + 2, None if last else (lambda: project_half(pbuf0, i0 + 4, 1)))
        if not last:
            project_half(pbuf1, i0 + 5, 0)
        lat_terms(pbuf3, i0 + 3, None if last else (lambda: project_half(pbuf1, i0 + 5, 1)))

    def lat_body(t, carry):
        lat_quad(4 * t, False)
        return carry

    lax.fori_loop(0, n_blocks // 4 - 1, lat_body, 0)
    lat_quad(n_blocks - 4, True)

    def advance(s, kv_ref, dec_ref, n):
        return dec_ref[n] * s + kv_ref[n]

    sf = jnp.zeros((HG_DK, HG_DK), F32)
    for n in range(n_ctx_chunks):
        sf = advance(sf, kvf, decf, n)
    sb = jnp.zeros((HG_DK, HG_DK), F32)
    for n in reversed(range(n_ctx_chunks)):
        sb = advance(sb, kvb, decb, n)

    def scan_step(t, carry):
        sf, sb = carry
        jb = n_lat_chunks - 1 - t
        spf[t] = sf.astype(BF16)
        spb[jb] = sb.astype(BF16)
        return (advance(sf, kvf, decf, n_ctx_chunks + t), advance(sb, kvb, decb, n_ctx_chunks + jb))

    lax.fori_loop(0, n_lat_chunks, scan_step, (sf, sb), unroll=2)

    gn = gn_ref[...]

    def block_rows(i):
        return pl.ds(pl.multiple_of(i * HG_BLOCK, HG_BLOCK), HG_BLOCK)

    def inter(i):
        rows = block_rows(i)
        chunks = pl.ds(i * HG_CPB, HG_CPB)
        return (jnp.einsum('nck,nvk->ncv', c3(qif[rows, :]), spf[chunks], preferred_element_type=F32)
                + jnp.einsum('nck,nvk->ncv', c3(qib[rows, :]), spb[chunks], preferred_element_type=F32)
                ).reshape(HG_BLOCK, HG_DK)

    def readout(o_ref, i):
        rows = block_rows(i)
        o = oacc[rows, :] + o_ref[...]
        ms = jnp.mean(o * o, axis=-1, keepdims=True)
        o = o * lax.rsqrt(ms + EPS) * gn
        y_ref[0, rows, :] = (o * _silu_tanh(g_ref[0, rows, :].astype(F32))).astype(BF16)

    obuf0, obuf1 = pbuf0.at[:, 0:HG_DK], pbuf1.at[:, 0:HG_DK]
    obuf0[...] = inter(0)

    def readout_pair(t, carry):
        obuf1[...] = inter(2 * t + 1)
        readout(obuf0, 2 * t)
        obuf0[...] = inter(2 * t + 2)
        readout(obuf1, 2 * t + 1)
        return carry

    lax.fori_loop(0, n_blocks // 2 - 1, readout_pair, 0)
    obuf1[...] = inter(n_blocks - 1)
    readout(obuf0, n_blocks - 2)
    readout(obuf1, n_blocks - 1)


def _hgrn_call(h_ctx, h_lat, g_lat, w_in, lb_fwd, lb_bwd, gnorm, slot, to_cast):
    b, lc, d = h_ctx.shape
    ll = h_lat.shape[1]
    nrow = lb_fwd.shape[0]
    ncc, nlc = lc // HG_CHUNK, ll // HG_CHUNK
    kern = functools.partial(_hgrn_kernel, slot=slot, n_ctx_chunks=ncc, n_lat_chunks=nlc, n_cast=len(to_cast))
    lb_spec = pl.BlockSpec((nrow, 1, 1, HG_DK), lambda i, h: (0, h, 0, 0))
    head_cols = pl.BlockSpec((1, ll, HG_DK), lambda i, h: (i, 0, h))
    n_rec = N_HG_PROJ - 1
    w_cols = [pl.BlockSpec((None, d, HG_DK), functools.partial(lambda i, h, p: (slot, 0, p * HG_HEADS + h), p=p))
              for p in range(n_rec)]
    steps = b * HG_HEADS
    slabs = [a.reshape(steps, -1, *a.shape[-2:]) for a in to_cast]
    slab_specs = [pl.BlockSpec((1,) + a.shape[1:], lambda i, h: (i * HG_HEADS + h, 0, 0, 0)) for a in slabs]
    outs = pl.pallas_call(
        kern,
        grid=(b, HG_HEADS),
        in_specs=[
            pl.BlockSpec((1, lc, d), lambda i, h: (i, 0, 0)),
            pl.BlockSpec((1, ll, d), lambda i, h: (i, 0, 0)),
            head_cols,
            *w_cols,
            lb_spec, lb_spec,
            pl.BlockSpec((1, HG_DK), lambda i, h: (0, 0)),
            *slab_specs,
        ],
        out_specs=[head_cols, *slab_specs],
        out_shape=[jax.ShapeDtypeStruct((b, ll, d), BF16)]
        + [jax.ShapeDtypeStruct(a.shape, BF16) for a in slabs],
        scratch_shapes=[
            pltpu.VMEM((d, n_rec * HG_DK), BF16),
            pltpu.VMEM((HG_BLOCK, n_rec * HG_DK), F32),
            pltpu.VMEM((HG_BLOCK, n_rec * HG_DK), F32),
            pltpu.VMEM((HG_BLOCK, n_rec * HG_DK), F32),
            pltpu.VMEM((HG_BLOCK, n_rec * HG_DK), F32),
            pltpu.VMEM((ll, HG_DK), F32),
            pltpu.VMEM((ll, HG_DK), BF16),
            pltpu.VMEM((ll, HG_DK), BF16),
            pltpu.VMEM((ncc + nlc, HG_DK, HG_DK), F32),
            pltpu.VMEM((ncc + nlc, HG_DK, HG_DK), F32),
            pltpu.VMEM((ncc + nlc, 1, HG_DK), F32),
            pltpu.VMEM((ncc + nlc, 1, HG_DK), F32),
            pltpu.VMEM((nlc, HG_DK, HG_DK), BF16),
            pltpu.VMEM((nlc, HG_DK, HG_DK), BF16),
        ],
        compiler_params=_cparams(("parallel", "arbitrary"), HGRN_VMEM_LIMIT),
        name="hgrn2",
    )(h_ctx, h_lat, g_lat, *([w_in] * n_rec),
      lb_fwd.reshape(nrow, HG_HEADS, 1, HG_DK), lb_bwd.reshape(nrow, HG_HEADS, 1, HG_DK),
      gnorm.reshape(1, HG_DK), *slabs)
    return outs[0], [o.reshape(a.shape) for o, a in zip(outs[1:], to_cast)]


POOL_STRIP = 8
POOL_PAD = 8
POOL_BAND = 256


def _pool_group(h_ref, win_ref, wg_ref, ps_ref, o_ref, upad, *, win, rows):
    gd = win_ref.shape[1]
    half = win // 2
    u = jnp.dot(h_ref[0], win_ref[...].astype(BF16), preferred_element_type=F32)
    zeros = jnp.zeros((POOL_PAD, GRID_W, gd), F32)
    upad[0:POOL_PAD] = zeros
    upad[POOL_PAD + rows:POOL_PAD + rows + POOL_PAD] = zeros
    upad[POOL_PAD:POOL_PAD + rows] = u.reshape(rows, GRID_W, gd)

    tok = POOL_STRIP * GRID_W
    bi = lax.broadcasted_iota(jnp.int32, (POOL_BAND, POOL_BAND), 0)
    bj = lax.broadcasted_iota(jnp.int32, (POOL_BAND, POOL_BAND), 1)
    lo_c = (bi & (GRID_W - 1)) - half
    cj = bj & (GRID_W - 1)
    band = jnp.where(bi >> GRID_SHIFT == bj >> GRID_SHIFT, 1.0, 0.0)
    band = jnp.where(cj >= lo_c, band, 0.0)
    band = jnp.where(cj < lo_c + win, band, 0.0).astype(BF16)
    t = lax.broadcasted_iota(jnp.int32, (tok, LANES), 0)
    col = t & (GRID_W - 1)
    cnt_c = jnp.minimum(col - half + win, GRID_W) - jnp.maximum(col - half, 0)
    wg = wg_ref[0].astype(BF16)
    ps = ps_ref[...]

    def strip(i, carry):
        r0 = i * POOL_STRIP
        slab = upad[pl.ds(r0 + POOL_PAD - half, POOL_STRIP + win - 1)]
        span = 1
        while span < win:
            n = slab.shape[0] - span
            slab = slab[0:n] + slab[span:span + n]
            span *= 2
        rs = slab.reshape(tok, gd)
        hi = rs.astype(BF16)
        lo = (rs - hi.astype(F32)).astype(BF16)
        parts = []
        for k in range(tok // POOL_BAND):
            sl = slice(k * POOL_BAND, (k + 1) * POOL_BAND)
            parts.append(jnp.dot(band, hi[sl], preferred_element_type=F32)
                         + jnp.dot(band, lo[sl], preferred_element_type=F32))
        box = jnp.concatenate(parts, axis=0)
        r = r0 + (t >> GRID_SHIFT)
        cnt_r = jnp.minimum(r - half + win, rows) - jnp.maximum(r - half, 0)
        inv = 1.0 / (cnt_r * cnt_c).astype(F32)
        mean = box * jnp.concatenate([inv] * (gd // LANES), axis=1)
        ug = upad[pl.ds(r0 + POOL_PAD, POOL_STRIP)].reshape(tok, gd)
        z = jnp.dot((mean - ug).astype(BF16), wg, preferred_element_type=F32) * ps
        o_ref[0, pl.ds(pl.multiple_of(i * tok, tok), tok), :] = z.astype(BF16)
        return carry

    lax.fori_loop(0, rows // POOL_STRIP, strip, 0, unroll=2)


def _pool_kernel(h_ref, win_ref, wg_ref, ps_ref, o_ref, upad, *, rows):
    g = pl.program_id(1)
    for gi, win in enumerate(POOL_WINDOWS):
        @pl.when(g == gi)
        def _(win=win):
            _pool_group(h_ref, win_ref, wg_ref, ps_ref, o_ref, upad, win=win, rows=rows)


def _pool_call(h, w_in, w_grp, p_scale, slot):
    b, l, d = h.shape
    ng = len(POOL_WINDOWS)
    gd = d // ng
    rows = l // GRID_W
    return pl.pallas_call(
        functools.partial(_pool_kernel, rows=rows),
        grid=(b, ng),
        in_specs=[
            pl.BlockSpec((1, l, d), lambda i, g: (i, 0, 0)),
            pl.BlockSpec((None, d, gd), lambda i, g: (slot, 0, g)),
            pl.BlockSpec((None, 1, gd, gd), lambda i, g: (slot, g, 0, 0)),
            pl.BlockSpec((None, 1, gd), lambda i, g: (slot, 0, g)),
        ],
        out_specs=pl.BlockSpec((1, l, gd), lambda i, g: (i, 0, g)),
        out_shape=jax.ShapeDtypeStruct((b, l, d), BF16),
        scratch_shapes=[pltpu.VMEM((rows + 2 * POOL_PAD, GRID_W, gd), F32)],
        compiler_params=_cparams(("parallel", "arbitrary")),
        name="pool_mix",
    )(h, w_in, w_grp, p_scale.reshape(p_scale.shape[0], 1, d))


MOE_TILE = 1024
MOE_WINDOWS = (128, 160, 192, 224, 256, 288, 320, 352, 384)
MOE_FULL = 256
ROW_ALIGN = 16
DEST_LANE = N_EXPERTS
PIECE_STRIDE = 32
N_PIECES = 3
AUX_ROWS = 8


def _route(sel, s):
    keep = []
    gsum = []
    for g in range(N_EXPERT_GROUPS):
        a = sel[g * EXPERTS_PER_GROUP:(g + 1) * EXPERTS_PER_GROUP]
        beaten = [jnp.zeros_like(a[0]) for _ in a]
        for i in range(EXPERTS_PER_GROUP):
            for j in range(i + 1, EXPERTS_PER_GROUP):
                ge = jnp.where(a[i] >= a[j], 1.0, 0.0)
                beaten[j] = beaten[j] + ge
                beaten[i] = beaten[i] + (1.0 - ge)
        kg = [jnp.where(bt < 1.5, 1.0, 0.0) for bt in beaten]
        keep.append(kg)
        gsum.append(sum(k * x for k, x in zip(kg, a)))
    picked = []
    bests = []
    for g in range(N_EXPERT_GROUPS):
        better = jnp.zeros_like(gsum[0])
        for o in range(N_EXPERT_GROUPS):
            if o < g:
                better = better + jnp.where(gsum[o] >= gsum[g], 1.0, 0.0)
            elif o > g:
                better = better + jnp.where(gsum[o] > gsum[g], 1.0, 0.0)
        best = jnp.where(better < 0.5, 1.0, 0.0)
        bests.append(best)
        for i in range(EXPERTS_PER_GROUP):
            picked.append(best * keep[g][i] * s[g * EXPERTS_PER_GROUP + i])
    den = sum(picked)
    return [p / den for p in picked], bests


def _post_kernel(x_ref, y_ref, w_ref, mod_ref, g_ref, rw_ref, rb_ref,
                 xo_ref, h2_ref, rt_ref, aux_ref, wbuf):
    @pl.when(pl.program_id(0) == 0)
    def _():
        wbuf[...] = w_ref[...].astype(BF16)

    yw = jnp.dot(y_ref[...], wbuf[...], preferred_element_type=F32)
    xn = x_ref[...] + mod_ref[0, 2:3, :] * yw
    xo_ref[...] = xn
    h2 = _modulate(xn, g_ref[...], mod_ref[0, 3:4, :], mod_ref[0, 4:5, :])
    hi = h2.astype(BF16)
    h2_ref[...] = hi
    lo = (h2 - hi.astype(F32)).astype(BF16)
    rw = rw_ref[...]
    half = hi.shape[0] // 2
    prod = jnp.concatenate(
        [jnp.dot(hi[r:r + half], rw, preferred_element_type=F32)
         + jnp.dot(lo[r:r + half], rw, preferred_element_type=F32) for r in (0, half)], axis=0).T
    logits = prod[0:N_EXPERTS, :] + prod[N_EXPERTS:2 * N_EXPERTS, :]
    s = _sigmoid_pair(logits)[0]
    sel = s + rb_ref[...]
    comb, bests = _route([sel[e:e + 1, :] for e in range(N_EXPERTS)],
                         [s[e:e + 1, :] for e in range(N_EXPERTS)])
    t = logits.shape[1]

    ind = jnp.concatenate(bests + [jnp.zeros((AUX_ROWS - N_EXPERT_GROUPS, t), F32)], axis=0)
    n_seg = t // LANES
    stacked = jnp.concatenate([ind[:, j * LANES:(j + 1) * LANES] for j in range(n_seg)], axis=0)
    upper = jnp.where(lax.broadcasted_iota(jnp.int32, (LANES, LANES), 0)
                      <= lax.broadcasted_iota(jnp.int32, (LANES, LANES), 1), 1.0, 0.0).astype(BF16)
    local = jnp.dot(stacked.astype(BF16), upper, preferred_element_type=F32)
    off = jnp.zeros((AUX_ROWS, 1), F32)
    pieces = []
    for j in range(n_seg):
        seg_cum = local[j * AUX_ROWS:(j + 1) * AUX_ROWS, :]
        pieces.append(seg_cum + off)
        off = off + seg_cum[:, LANES - 1:LANES]
    cum = jnp.concatenate(pieces, axis=1)
    counts = [off[g:g + 1, :] for g in range(N_EXPERT_GROUPS)]
    starts = [jnp.zeros((1, 1), F32)]
    for g in range(1, N_EXPERT_GROUPS):
        starts.append(starts[-1] + counts[g - 1])
    dest = sum(bests[g] * (starts[g] + cum[g:g + 1, :] - 1.0) for g in range(N_EXPERT_GROUPS))

    lane = lax.broadcasted_iota(jnp.int32, (1, t), 1)
    seg = sum(jnp.where(lane == k, v, 0.0) for k, v in enumerate(starts + counts))
    aux_ref[0] = jnp.concatenate([dest, seg, jnp.zeros((AUX_ROWS - 2, t), F32)], axis=0)
    p1 = [w.astype(BF16).astype(F32) for w in comb]
    r1 = [w - p for w, p in zip(comb, p1)]
    p2 = [r.astype(BF16).astype(F32) for r in r1]
    p3 = [r - p for r, p in zip(r1, p2)]
    pad = [jnp.zeros((PIECE_STRIDE - N_EXPERTS, t), F32)]
    table = jnp.concatenate(p1 + [dest] + [jnp.zeros((PIECE_STRIDE - N_EXPERTS - 1, t), F32)] + p2 + pad + p3 + pad
                            + [jnp.zeros((LANES - N_PIECES * PIECE_STRIDE, t), F32)], axis=0)
    rt_ref[...] = table.T


def _post_call(x, y, w, w_slot, mod, g, router_w, router_b, tokens_per_batch, tile=MOE_TILE):
    n, d = x.shape
    per_b = tokens_per_batch // tile
    rwh = router_w.astype(BF16)
    rwl = (router_w - rwh.astype(F32)).astype(BF16)
    rw = jnp.concatenate([rwh, rwl, jnp.zeros((d, LANES - 2 * N_EXPERTS), BF16)], axis=1)
    row = lambda i: (i, 0)
    fixed = lambda i: (0, 0)
    return pl.pallas_call(
        _post_kernel,
        grid=(n // tile,),
        in_specs=[
            pl.BlockSpec((tile, d), row),
            pl.BlockSpec((tile, d), row),
            pl.BlockSpec((None, d, d), lambda i: (w_slot, 0, 0)),
            pl.BlockSpec((1, N_MOD, d), lambda i: (i // per_b, 0, 0)),
            pl.BlockSpec((1, d), fixed),
            pl.BlockSpec((d, LANES), fixed),
            pl.BlockSpec((N_EXPERTS, 1), fixed),
        ],
        out_specs=[
            pl.BlockSpec((tile, d), row),
            pl.BlockSpec((tile, d), row),
            pl.BlockSpec((tile, LANES), row),
            pl.BlockSpec((1, AUX_ROWS, tile), lambda i: (i, 0, 0)),
        ],
        out_shape=[
            jax.ShapeDtypeStruct((n, d), F32),
            jax.ShapeDtypeStruct((n, d), BF16),
            jax.ShapeDtypeStruct((n, LANES), F32),
            jax.ShapeDtypeStruct((n // tile, AUX_ROWS, tile), F32),
        ],
        scratch_shapes=[pltpu.VMEM((d, d), BF16)],
        compiler_params=_cparams(("arbitrary",)),
        name="mixer_out_router",
    )(x, y, w, mod, g.reshape(1, d), rw, router_b.reshape(N_EXPERTS, 1))


def _moe_kernel(seg_ref, h_ref, rt_ref, aux_ref, wg_ref, wu_ref, wd_ref, x_ref, mod_ref, ng_ref, nmod_ref,
                o_ref, *rest, final_norm):
    if final_norm:
        xs, cws, ys = rest
    else:
        hn_ref, xs, cws, ys = rest
    i = pl.program_id(0)
    g = pl.program_id(1)
    t = h_ref.shape[0]

    @pl.when(g == 0)
    def _():
        dest_row = aux_ref[0, 0:1, :]
        perm = jnp.where(lax.broadcasted_iota(jnp.int32, (t, t), 0).astype(F32) == dest_row,
                         1.0, 0.0).astype(BF16)
        xs[...] = jnp.dot(perm, h_ref[...], preferred_element_type=F32).astype(BF16)
        cws[...] = jnp.dot(perm, rt_ref[...].astype(BF16), preferred_element_type=F32)
        ys[...] = jnp.zeros_like(ys)

    start = seg_ref[i, g]
    end = start + seg_ref[i, N_EXPERT_GROUPS + g]
    wd = wd_ref[...].reshape(EXPERTS_PER_GROUP * D_EXPERT, wd_ref.shape[2])

    def window(w0, rows, lo, hi):
        sl = pl.ds(pl.multiple_of(w0, ROW_ALIGN), rows)
        xw = xs[sl, :]
        cw = cws[sl, :]
        r = w0 + lax.broadcasted_iota(jnp.int32, cw.shape, 0)
        lane = lax.broadcasted_iota(jnp.int32, cw.shape, 1)
        piece_lane = jnp.where(lane < N_PIECES * PIECE_STRIDE, lane & (PIECE_STRIDE - 1), -1)
        cw = jnp.where(r >= lo, cw, 0.0)
        cw = jnp.where(r < hi, cw, 0.0)
        acts = []
        for e in range(EXPERTS_PER_GROUP):
            ce = jnp.sum(jnp.where(piece_lane == g * EXPERTS_PER_GROUP + e, cw, 0.0), axis=1, keepdims=True)
            ge = g * EXPERTS_PER_GROUP + e
            a = _silu_tanh(jnp.dot(xw, wg_ref[ge], preferred_element_type=F32)) \
                * jnp.dot(xw, wu_ref[ge], preferred_element_type=F32)
            acts.append((a * ce).astype(BF16))
        ys[sl, :] += jnp.dot(jnp.concatenate(acts, axis=1), wd, preferred_element_type=F32)

    aligned = start & -ROW_ALIGN
    need = end - aligned
    biggest = MOE_WINDOWS[-1]
    n_full = jnp.where(need > biggest, (need - (MOE_WINDOWS[0] + 1)) >> (MOE_FULL.bit_length() - 1), 0)

    def full(k, carry):
        w0 = aligned + k * MOE_FULL
        window(w0, MOE_FULL, jnp.maximum(start, w0), w0 + MOE_FULL)
        return carry

    lax.fori_loop(0, n_full, full, 0)
    rest0 = aligned + n_full * MOE_FULL
    rest = end - rest0
    for smaller, m in zip((0,) + MOE_WINDOWS[:-1], MOE_WINDOWS):
        @pl.when((end > start) & (rest > smaller) & (rest <= m))
        def _(m=m):
            window(jnp.minimum(rest0, t - m), m, jnp.maximum(start, rest0), end)

    @pl.when(g == N_EXPERT_GROUPS - 1)
    def _():
        dest_col = rt_ref[:, DEST_LANE:DEST_LANE + 1]
        unperm = jnp.where(lax.broadcasted_iota(jnp.int32, (t, t), 1).astype(F32) == dest_col,
                           1.0, 0.0).astype(BF16)
        ff = jnp.dot(unperm, ys[...].astype(BF16), preferred_element_type=F32)
        xn = x_ref[...] + mod_ref[0, 5:6, :] * ff
        if final_norm:
            ms = jnp.mean(xn * xn, axis=-1, keepdims=True)
            xn = xn * lax.rsqrt(ms + EPS) * ng_ref[...]
        else:
            hn_ref[...] = _modulate(xn, ng_ref[...], nmod_ref[0, 0:1, :], nmod_ref[0, 1:2, :]).astype(BF16)
        o_ref[...] = xn


def _moe_call(h2, table, aux, wg, wu, wd, layer, x, mod, norm_g, next_mod, tokens_per_batch, final_norm):
    n, d = x.shape
    tile = MOE_TILE
    per_b = tokens_per_batch // tile
    seg = aux[:, 1, :2 * N_EXPERT_GROUPS].astype(jnp.int32)
    row = lambda i, g, seg: (i, 0)
    grp = lambda i, g, seg: (layer, g, 0, 0)
    tok = pl.BlockSpec((tile, d), row)
    x_out = jax.ShapeDtypeStruct((n, d), F32)
    grid_spec = pltpu.PrefetchScalarGridSpec(
        num_scalar_prefetch=1,
        grid=(n // tile, N_EXPERT_GROUPS),
        in_specs=[
            pl.BlockSpec((tile, d), row),
            pl.BlockSpec((tile, LANES), row),
            pl.BlockSpec((1, AUX_ROWS, tile), lambda i, g, seg: (i, 0, 0)),
            pl.BlockSpec((None, N_EXPERTS, d, D_EXPERT), lambda i, g, seg: (layer, 0, 0, 0)),
            pl.BlockSpec((None, N_EXPERTS, d, D_EXPERT), lambda i, g, seg: (layer, 0, 0, 0)),
            pl.BlockSpec((None, EXPERTS_PER_GROUP, D_EXPERT, d), grp),
            pl.BlockSpec((tile, d), row),
            pl.BlockSpec((1, N_MOD, d), lambda i, g, seg: (i // per_b, 0, 0)),
            pl.BlockSpec((1, d), lambda i, g, seg: (0, 0)),
            pl.BlockSpec((1, N_MOD, d), lambda i, g, seg: (i // per_b, 0, 0)),
        ],
        out_specs=tok if final_norm else [tok, tok],
        scratch_shapes=[
            pltpu.VMEM((tile, d), BF16),
            pltpu.VMEM((tile, LANES), F32),
            pltpu.VMEM((tile, d), F32),
        ],
    )
    return pl.pallas_call(
        functools.partial(_moe_kernel, final_norm=final_norm),
        grid_spec=grid_spec,
        out_shape=x_out if final_norm else [x_out, jax.ShapeDtypeStruct((n, d), BF16)],
        compiler_params=_cparams(("parallel", "arbitrary"), MOE_VMEM_LIMIT),
        name="moe_ffn",
    )(seg, h2, table, aux, wg, wu, wd, x, mod, norm_g.reshape(1, d), next_mod)


def kernel(x, c, ctx, c_ctx, w_mod, b_mod, norm1_g, norm2_g, hg_w_in, hg_lb_fwd, hg_lb_bwd, hg_gnorm,
           hg_w_out, pool_w_in, pool_w_grp, pool_scale, pool_w_out, router_w, router_b, moe_w_gate,
           moe_w_up, moe_w_down, final_g):
    b, l, d = x.shape
    depth = w_mod.shape[0]
    n_mixers = 2

    cc = jnp.concatenate([c, c_ctx[None, :], jnp.zeros((MOD_ROWS - b - 1, d), F32)], axis=0)
    mods = _mod_call(cc, w_mod, b_mod)

    experts = None
    x_lat = x.reshape(b * l, d)
    h_lat = None
    for i in range(depth):
        slot = i // n_mixers
        mod_lat = mods[i, :b].reshape(b, N_MOD, d)
        if i % n_mixers == 0:
            h_lat, g_lat = _norm_call(x_lat.reshape(b, l, d), norm1_g[i], mod_lat, 1024,
                                      hg_w_in, slot, N_HG_PROJ - 1)
            mod_ctx = jnp.broadcast_to(mods[i, b].reshape(1, N_MOD, d), (b, N_MOD, d))
            h_ctx = _norm_call(ctx, norm1_g[i], mod_ctx, ctx.shape[1])
            to_cast = (moe_w_gate, moe_w_up, moe_w_down) if experts is None else ()
            y, cast = _hgrn_call(h_ctx, h_lat, g_lat, hg_w_in, hg_lb_fwd, hg_lb_bwd, hg_gnorm[slot], slot, to_cast)
            experts = experts or tuple(cast)
            w_out = hg_w_out
        else:
            if h_lat is None:
                h_lat = _norm_call(x_lat.reshape(b, l, d), norm1_g[i], mod_lat, 512)
            y = _pool_call(h_lat, pool_w_in, pool_w_grp, pool_scale, slot)
            w_out = pool_w_out
        x_lat, h2, table, aux = _post_call(x_lat, y.reshape(b * l, d), w_out, slot, mod_lat,
                                           norm2_g[i], router_w, router_b, l)
        if experts is None:
            experts = (moe_w_gate.astype(BF16), moe_w_up.astype(BF16), moe_w_down.astype(BF16))
        if i == depth - 1:
            x_lat = _moe_call(h2, table, aux, *experts, i, x_lat, mod_lat, final_g, mod_lat, l, final_norm=True)
        else:
            mod_next = mods[i + 1, :b].reshape(b, N_MOD, d)
            x_lat, h_next = _moe_call(h2, table, aux, *experts, i, x_lat, mod_lat, norm1_g[i + 1], mod_next, l,
                                      final_norm=False)
            h_lat = h_next.reshape(b, l, d) if (i + 1) % n_mixers != 0 else None
    return x_lat.reshape(b, l, d)
```

```python
import functools

import jax
import jax.numpy as jnp
from jax import lax
from jax.experimental import pallas as pl
from jax.experimental.pallas import tpu as pltpu

F32 = jnp.float32
BF16 = jnp.bfloat16

EPS = 1e-6
N_MOD = 6
HG_HEADS = 8
HG_DK = 128
HG_CHUNK = 64
N_HG_PROJ = 5
POOL_WINDOWS = (2, 4, 8, 16)
GRID_W = 64
GRID_SHIFT = GRID_W.bit_length() - 1
assert 1 << GRID_SHIFT == GRID_W
N_EXPERTS = 16
N_EXPERT_GROUPS = 4
EXPERTS_PER_GROUP = N_EXPERTS // N_EXPERT_GROUPS
D_EXPERT = 256

LANES = 128
MOD_ROWS = 8
VMEM_LIMIT = 56 * 1024 * 1024
HGRN_VMEM_LIMIT = 60 * 1024 * 1024
MOE_VMEM_LIMIT = 60 * 1024 * 1024


def _cparams(sem, vmem=VMEM_LIMIT):
    return pltpu.CompilerParams(dimension_semantics=sem, vmem_limit_bytes=vmem)


def _sigmoid_pair(z):
    e = jnp.exp(-jnp.abs(z))
    r = 1.0 / (1.0 + e)
    er = e * r
    pos = z >= 0
    return jnp.where(pos, r, er), jnp.where(pos, er, r)


def _silu(z):
    return z * _sigmoid_pair(z)[0]


def _sigmoid_tanh(z):
    return 0.5 * jnp.tanh(0.5 * z) + 0.5


def _silu_tanh(z):
    return z * _sigmoid_tanh(z)


def _split_bf16(x):
    hi = x.astype(BF16)
    return hi, (x - hi.astype(F32)).astype(BF16)


def _mod_kernel(c_ref, w_ref, b_ref, o_ref):
    a = jnp.concatenate(_split_bf16(_silu(c_ref[...])), axis=0)
    w_hi, w_lo = _split_bf16(w_ref[0])
    acc = jnp.dot(a, w_hi, preferred_element_type=F32) + jnp.dot(a, w_lo, preferred_element_type=F32)
    o_ref[0] = acc[0:MOD_ROWS] + acc[MOD_ROWS:2 * MOD_ROWS] + b_ref[0]


def _mod_call(cc, w_mod, b_mod):
    depth, d, n = w_mod.shape
    tn = 1024
    return pl.pallas_call(
        _mod_kernel,
        grid=(depth, n // tn),
        in_specs=[
            pl.BlockSpec((MOD_ROWS, d), lambda i, j: (0, 0)),
            pl.BlockSpec((1, d, tn), lambda i, j: (i, 0, j)),
            pl.BlockSpec((1, 1, tn), lambda i, j: (i, 0, j)),
        ],
        out_specs=pl.BlockSpec((1, MOD_ROWS, tn), lambda i, j: (i, 0, j)),
        out_shape=jax.ShapeDtypeStruct((depth, MOD_ROWS, n), F32),
        compiler_params=_cparams(("parallel", "parallel")),
        name="mod_proj",
    )(cc, w_mod, b_mod.reshape(depth, 1, n))


def _modulate(x, g, shift, scale):
    ms = jnp.mean(x * x, axis=-1, keepdims=True)
    return (x * lax.rsqrt(ms + EPS) * g) * (1.0 + scale) + shift


def _norm_kernel(x_ref, g_ref, mod_ref, *rest):
    h = _modulate(x_ref[0], g_ref[...], mod_ref[0, 0:1, :], mod_ref[0, 1:2, :]).astype(BF16)
    if len(rest) == 1:
        (o_ref,) = rest
    else:
        w_ref, o_ref, p_ref, wbuf = rest

        @pl.when((pl.program_id(0) == 0) & (pl.program_id(1) == 0))
        def _():
            wbuf[...] = w_ref[...].astype(BF16)

        p_ref[0] = jnp.dot(h, wbuf[...], preferred_element_type=F32).astype(BF16)
    o_ref[0] = h


def _norm_call(x, g, mod, tile, w_proj=None, w_slot=0, w_col=0):
    b, t, d = x.shape
    tok = pl.BlockSpec((1, tile, d), lambda i, j: (i, j, 0))
    in_specs = [tok, pl.BlockSpec((1, d), lambda i, j: (0, 0)),
                pl.BlockSpec((1, N_MOD, d), lambda i, j: (i, 0, 0))]
    args = [x, g.reshape(1, d), mod]
    out_specs, out_shape = tok, jax.ShapeDtypeStruct((b, t, d), BF16)
    scratch = []
    if w_proj is not None:
        in_specs.append(pl.BlockSpec((None, d, d), lambda i, j: (w_slot, 0, w_col)))
        args.append(w_proj)
        out_specs, out_shape = [tok, tok], [out_shape, out_shape]
        scratch = [pltpu.VMEM((d, d), BF16)]
    return pl.pallas_call(
        _norm_kernel,
        grid=(b, t // tile),
        in_specs=in_specs,
        out_specs=out_specs,
        out_shape=out_shape,
        scratch_shapes=scratch,
        compiler_params=_cparams(("arbitrary", "arbitrary")),
        name="norm1",
    )(*args)


HG_BLOCK = 256
HG_CPB = HG_BLOCK // HG_CHUNK


def _chunk_prefix(x, row):
    for d in (1, 2, 4, 8, 16, 32):
        x = x + jnp.where(row >= d, pltpu.roll(x, d, axis=0), 0.0)
    return x


def _chunk_suffix(x, row):
    n = x.shape[0]
    for d in (1, 2, 4, 8, 16, 32):
        x = x + jnp.where(row < HG_CHUNK - d, pltpu.roll(x, n - d, axis=0), 0.0)
    return x


def _lower_bound(lb_ref, slot):
    rows = [lb_ref[j, 0] for j in range(lb_ref.shape[0])]
    m = functools.reduce(jnp.maximum, rows)
    es = [jnp.exp(r - m) for r in rows]
    return sum(es[:slot + 1]) / sum(es)


N_HGRN_IN = 10


def _hgrn_kernel(*refs, slot, n_ctx_chunks, n_lat_chunks, n_cast):
    hc_ref, hl_ref, g_ref, wq_ref, wv_ref, wf_ref, wb_ref, lbf_ref, lbb_ref, gn_ref = refs[:N_HGRN_IN]
    cast_in = refs[N_HGRN_IN:N_HGRN_IN + n_cast]
    y_ref = refs[N_HGRN_IN + n_cast]
    cast_out = refs[N_HGRN_IN + n_cast + 1:N_HGRN_IN + 2 * n_cast + 1]
    (wbuf, pbuf0, pbuf1, pbuf2, pbuf3, oacc, qif, qib,
     kvf, kvb, decf, decb, spf, spb) = refs[N_HGRN_IN + 2 * n_cast + 1:]
    for c_ref, o_ref in zip(cast_in, cast_out):
        o_ref[...] = c_ref[...].astype(BF16)

    lb_f = _lower_bound(lbf_ref, slot)
    lb_b = _lower_bound(lbb_ref, slot)
    row = lax.broadcasted_iota(jnp.int32, (HG_BLOCK, HG_DK), 0) & (HG_CHUNK - 1)
    ci = lax.broadcasted_iota(jnp.int32, (HG_CPB, HG_CHUNK, HG_CHUNK), 1)
    si = lax.broadcasted_iota(jnp.int32, (HG_CPB, HG_CHUNK, HG_CHUNK), 2)

    def c3(t):
        return t.reshape(HG_CPB, HG_CHUNK, HG_DK)

    def gates(qs3, z, lb, fwd):
        sig = _sigmoid_tanh(z)
        f = lb + (1.0 - lb) * sig
        k3 = c3((1.0 - lb) * (1.0 - sig))
        lf = jnp.log(f)
        if fwd:
            cum = c3(_chunk_prefix(lf, row))
            ref = cum[:, HG_CHUNK // 2 - 1:HG_CHUNK // 2, :]
            last = cum[:, HG_CHUNK - 1:HG_CHUNK, :]
        else:
            cum = c3(_chunk_suffix(lf, row))
            ref = cum[:, HG_CHUNK // 2:HG_CHUNK // 2 + 1, :]
            last = cum[:, 0:1, :]
        dec = jnp.exp(last)
        if qs3 is None:
            return None, None, None, (k3 * jnp.exp(last - cum)).astype(BF16), dec
        e1 = jnp.exp(cum - ref)
        qd = qs3 * e1
        qi = qd * jnp.exp(ref)
        kd = k3 * (1.0 / e1)
        kl = kd * jnp.exp(last - ref)
        return qd.astype(BF16), kd.astype(BF16), qi.astype(BF16), kl.astype(BF16), dec

    def matmuls(qd, kd, kl, v3b, fwd):
        kvt = jnp.einsum('ncv,nck->nvk', v3b, kl, preferred_element_type=F32)
        if qd is None:
            return None, kvt
        sc = jnp.einsum('nck,nsk->ncs', qd, kd, preferred_element_type=F32)
        sc = jnp.where((ci >= si) if fwd else (ci <= si), sc, 0.0)
        return jnp.einsum('ncs,nsv->ncv', sc.astype(BF16), v3b, preferred_element_type=F32), kvt

    for p, wp_ref in enumerate((wq_ref, wv_ref, wf_ref, wb_ref)):
        wbuf[:, p * HG_DK:(p + 1) * HG_DK] = wp_ref[...].astype(BF16)

    def project(hrows):
        return jnp.dot(hrows, wbuf[...], preferred_element_type=F32)

    def block(p, chunk0, lat_row0, hooks=(None, None)):
        v3b = c3(p[:, HG_DK:2 * HG_DK]).astype(BF16)
        zf = p[:, 2 * HG_DK:3 * HG_DK]
        zb = p[:, 3 * HG_DK:4 * HG_DK]
        if lat_row0 is None:
            qs3 = None
        else:
            qs3 = c3(_silu_tanh(p[:, 0:HG_DK]))
        qd_f, kd_f, qi_f, kl_f, dec_f = gates(qs3, zf, lb_f, True)
        if hooks[0] is not None:
            hooks[0]()
        qd_b, kd_b, qi_b, kl_b, dec_b = gates(qs3, zb, lb_b, False)
        in_f, kv_f = matmuls(qd_f, kd_f, kl_f, v3b, True)
        if hooks[1] is not None:
            hooks[1]()
        in_b, kv_b = matmuls(qd_b, kd_b, kl_b, v3b, False)
        kvf[pl.ds(chunk0, HG_CPB)] = kv_f
        kvb[pl.ds(chunk0, HG_CPB)] = kv_b
        decf[pl.ds(chunk0, HG_CPB)] = dec_f
        decb[pl.ds(chunk0, HG_CPB)] = dec_b
        if lat_row0 is not None:
            rows = pl.ds(lat_row0, HG_BLOCK)
            oacc[rows, :] = (in_f + in_b).reshape(HG_BLOCK, HG_DK)
            qif[rows, :] = qi_f.reshape(HG_BLOCK, HG_DK)
            qib[rows, :] = qi_b.reshape(HG_BLOCK, HG_DK)

    n_blocks = n_lat_chunks // HG_CPB

    def lat_rows(i):
        return hl_ref[0, pl.ds(pl.multiple_of(i * HG_BLOCK, HG_BLOCK), HG_BLOCK), :]

    def lat_terms(p_ref, i, hooks=(None, None)):
        block(p_ref[...], n_ctx_chunks + i * HG_CPB, pl.multiple_of(i * HG_BLOCK, HG_BLOCK), hooks)

    half_n = 2 * HG_DK

    def project_halves(dst, i):
        def half(k):
            c = slice(k * half_n, (k + 1) * half_n)
            dst[:, c] = jnp.dot(lat_rows(i), wbuf[:, c], preferred_element_type=F32)
        return (lambda: half(0)), (lambda: half(1))

    ctx_p = [project(hc_ref[0, i * HG_BLOCK:(i + 1) * HG_BLOCK, :]) for i in range(n_ctx_chunks // HG_CPB)]
    pbuf0[...] = project(lat_rows(0))
    for i, p in enumerate(ctx_p):
        block(p, i * HG_CPB, None)
    pbuf1[...] = project(lat_rows(1))

    def lat_quad(i0, last):
        lat_terms(pbuf0, i0, project_halves(pbuf2, i0 + 2))
        lat_terms(pbuf1, i0 + 1, project_halves(pbuf3, i0 + 3))
        lat_terms(pbuf2, i0 + 2, (None, None) if last else project_halves(pbuf0, i0 + 4))
        lat_terms(pbuf3, i0 + 3, (None, None) if last else project_halves(pbuf1, i0 + 5))

    def lat_body(t, carry):
        lat_quad(4 * t, False)
        return carry

    lax.fori_loop(0, n_blocks // 4 - 1, lat_body, 0)
    lat_quad(n_blocks - 4, True)

    def advance(s, kv_ref, dec_ref, n):
        return dec_ref[n] * s + kv_ref[n]

    sf = jnp.zeros((HG_DK, HG_DK), F32)
    for n in range(n_ctx_chunks):
        sf = advance(sf, kvf, decf, n)
    sb = jnp.zeros((HG_DK, HG_DK), F32)
    for n in reversed(range(n_ctx_chunks)):
        sb = advance(sb, kvb, decb, n)

    def scan_step(t, carry):
        sf, sb = carry
        jb = n_lat_chunks - 1 - t
        spf[t] = sf.astype(BF16)
        spb[jb] = sb.astype(BF16)
        return (advance(sf, kvf, decf, n_ctx_chunks + t), advance(sb, kvb, decb, n_ctx_chunks + jb))

    lax.fori_loop(0, n_lat_chunks, scan_step, (sf, sb), unroll=2)

    gn = gn_ref[...]

    def block_rows(i):
        return pl.ds(pl.multiple_of(i * HG_BLOCK, HG_BLOCK), HG_BLOCK)

    def inter(i):
        rows = block_rows(i)
        chunks = pl.ds(i * HG_CPB, HG_CPB)
        return (jnp.einsum('nck,nvk->ncv', c3(qif[rows, :]), spf[chunks], preferred_element_type=F32)
                + jnp.einsum('nck,nvk->ncv', c3(qib[rows, :]), spb[chunks], preferred_element_type=F32)
                ).reshape(HG_BLOCK, HG_DK)

    def readout(o_ref, i):
        rows = block_rows(i)
        o = oacc[rows, :] + o_ref[...]
        ms = jnp.mean(o * o, axis=-1, keepdims=True)
        o = o * lax.rsqrt(ms + EPS) * gn
        y_ref[0, rows, :] = (o * _silu_tanh(g_ref[0, rows, :].astype(F32))).astype(BF16)

    obuf0, obuf1 = pbuf0.at[:, 0:HG_DK], pbuf1.at[:, 0:HG_DK]
    obuf0[...] = inter(0)

    def readout_pair(t, carry):
        obuf1[...] = inter(2 * t + 1)
        readout(obuf0, 2 * t)
        obuf0[...] = inter(2 * t + 2)
        readout(obuf1, 2 * t + 1)
        return carry

    lax.fori_loop(0, n_blocks // 2 - 1, readout_pair, 0)
    obuf1[...] = inter(n_blocks - 1)
    readout(obuf0, n_blocks - 2)
    readout(obuf1, n_blocks - 1)


def _hgrn_call(h_ctx, h_lat, g_lat, w_in, lb_fwd, lb_bwd, gnorm, slot, to_cast):
    b, lc, d = h_ctx.shape
    ll = h_lat.shape[1]
    nrow = lb_fwd.shape[0]
    ncc, nlc = lc // HG_CHUNK, ll // HG_CHUNK
    kern = functools.partial(_hgrn_kernel, slot=slot, n_ctx_chunks=ncc, n_lat_chunks=nlc, n_cast=len(to_cast))
    lb_spec = pl.BlockSpec((nrow, 1, 1, HG_DK), lambda i, h: (0, h, 0, 0))
    head_cols = pl.BlockSpec((1, ll, HG_DK), lambda i, h: (i, 0, h))
    n_rec = N_HG_PROJ - 1
    w_cols = [pl.BlockSpec((None, d, HG_DK), functools.partial(lambda i, h, p: (slot, 0, p * HG_HEADS + h), p=p))
              for p in range(n_rec)]
    steps = b * HG_HEADS
    slabs = [a.reshape(steps, -1, *a.shape[-2:]) for a in to_cast]
    slab_specs = [pl.BlockSpec((1,) + a.shape[1:], lambda i, h: (i * HG_HEADS + h, 0, 0, 0)) for a in slabs]
    outs = pl.pallas_call(
        kern,
        grid=(b, HG_HEADS),
        in_specs=[
            pl.BlockSpec((1, lc, d), lambda i, h: (i, 0, 0)),
            pl.BlockSpec((1, ll, d), lambda i, h: (i, 0, 0)),
            head_cols,
            *w_cols,
            lb_spec, lb_spec,
            pl.BlockSpec((1, HG_DK), lambda i, h: (0, 0)),
            *slab_specs,
        ],
        out_specs=[head_cols, *slab_specs],
        out_shape=[jax.ShapeDtypeStruct((b, ll, d), BF16)]
        + [jax.ShapeDtypeStruct(a.shape, BF16) for a in slabs],
        scratch_shapes=[
            pltpu.VMEM((d, n_rec * HG_DK), BF16),
            pltpu.VMEM((HG_BLOCK, n_rec * HG_DK), F32),
            pltpu.VMEM((HG_BLOCK, n_rec * HG_DK), F32),
            pltpu.VMEM((HG_BLOCK, n_rec * HG_DK), F32),
            pltpu.VMEM((HG_BLOCK, n_rec * HG_DK), F32),
            pltpu.VMEM((ll, HG_DK), F32),
            pltpu.VMEM((ll, HG_DK), BF16),
            pltpu.VMEM((ll, HG_DK), BF16),
            pltpu.VMEM((ncc + nlc, HG_DK, HG_DK), F32),
            pltpu.VMEM((ncc + nlc, HG_DK, HG_DK), F32),
            pltpu.VMEM((ncc + nlc, 1, HG_DK), F32),
            pltpu.VMEM((ncc + nlc, 1, HG_DK), F32),
            pltpu.VMEM((nlc, HG_DK, HG_DK), BF16),
            pltpu.VMEM((nlc, HG_DK, HG_DK), BF16),
        ],
        compiler_params=_cparams(("parallel", "arbitrary"), HGRN_VMEM_LIMIT),
        name="hgrn2",
    )(h_ctx, h_lat, g_lat, *([w_in] * n_rec),
      lb_fwd.reshape(nrow, HG_HEADS, 1, HG_DK), lb_bwd.reshape(nrow, HG_HEADS, 1, HG_DK),
      gnorm.reshape(1, HG_DK), *slabs)
    return outs[0], [o.reshape(a.shape) for o, a in zip(outs[1:], to_cast)]


POOL_STRIP = 8
POOL_PAD = 8
POOL_BAND = 256


def _pool_group(h_ref, win_ref, wg_ref, ps_ref, o_ref, upad, *, win, rows):
    gd = win_ref.shape[1]
    half = win // 2
    u = jnp.dot(h_ref[0], win_ref[...].astype(BF16), preferred_element_type=F32)
    zeros = jnp.zeros((POOL_PAD, GRID_W, gd), F32)
    upad[0:POOL_PAD] = zeros
    upad[POOL_PAD + rows:POOL_PAD + rows + POOL_PAD] = zeros
    upad[POOL_PAD:POOL_PAD + rows] = u.reshape(rows, GRID_W, gd)

    tok = POOL_STRIP * GRID_W
    bi = lax.broadcasted_iota(jnp.int32, (POOL_BAND, POOL_BAND), 0)
    bj = lax.broadcasted_iota(jnp.int32, (POOL_BAND, POOL_BAND), 1)
    lo_c = (bi & (GRID_W - 1)) - half
    cj = bj & (GRID_W - 1)
    band = jnp.where(bi >> GRID_SHIFT == bj >> GRID_SHIFT, 1.0, 0.0)
    band = jnp.where(cj >= lo_c, band, 0.0)
    band = jnp.where(cj < lo_c + win, band, 0.0).astype(BF16)
    t = lax.broadcasted_iota(jnp.int32, (tok, LANES), 0)
    col = t & (GRID_W - 1)
    cnt_c = jnp.minimum(col - half + win, GRID_W) - jnp.maximum(col - half, 0)
    wg = wg_ref[0].astype(BF16)
    ps = ps_ref[...]

    def strip(i, carry):
        r0 = i * POOL_STRIP
        slab = upad[pl.ds(r0 + POOL_PAD - half, POOL_STRIP + win - 1)]
        span = 1
        while span < win:
            n = slab.shape[0] - span
            slab = slab[0:n] + slab[span:span + n]
            span *= 2
        rs = slab.reshape(tok, gd)
        hi = rs.astype(BF16)
        lo = (rs - hi.astype(F32)).astype(BF16)
        parts = []
        for k in range(tok // POOL_BAND):
            sl = slice(k * POOL_BAND, (k + 1) * POOL_BAND)
            parts.append(jnp.dot(band, hi[sl], preferred_element_type=F32)
                         + jnp.dot(band, lo[sl], preferred_element_type=F32))
        box = jnp.concatenate(parts, axis=0)
        r = r0 + (t >> GRID_SHIFT)
        cnt_r = jnp.minimum(r - half + win, rows) - jnp.maximum(r - half, 0)
        inv = 1.0 / (cnt_r * cnt_c).astype(F32)
        mean = box * jnp.concatenate([inv] * (gd // LANES), axis=1)
        ug = upad[pl.ds(r0 + POOL_PAD, POOL_STRIP)].reshape(tok, gd)
        z = jnp.dot((mean - ug).astype(BF16), wg, preferred_element_type=F32) * ps
        o_ref[0, pl.ds(pl.multiple_of(i * tok, tok), tok), :] = z.astype(BF16)
        return carry

    lax.fori_loop(0, rows // POOL_STRIP, strip, 0, unroll=2)


def _pool_kernel(h_ref, win_ref, wg_ref, ps_ref, o_ref, upad, *, rows):
    g = pl.program_id(1)
    for gi, win in enumerate(POOL_WINDOWS):
        @pl.when(g == gi)
        def _(win=win):
            _pool_group(h_ref, win_ref, wg_ref, ps_ref, o_ref, upad, win=win, rows=rows)


def _pool_call(h, w_in, w_grp, p_scale, slot):
    b, l, d = h.shape
    ng = len(POOL_WINDOWS)
    gd = d // ng
    rows = l // GRID_W
    return pl.pallas_call(
        functools.partial(_pool_kernel, rows=rows),
        grid=(b, ng),
        in_specs=[
            pl.BlockSpec((1, l, d), lambda i, g: (i, 0, 0)),
            pl.BlockSpec((None, d, gd), lambda i, g: (slot, 0, g)),
            pl.BlockSpec((None, 1, gd, gd), lambda i, g: (slot, g, 0, 0)),
            pl.BlockSpec((None, 1, gd), lambda i, g: (slot, 0, g)),
        ],
        out_specs=pl.BlockSpec((1, l, gd), lambda i, g: (i, 0, g)),
        out_shape=jax.ShapeDtypeStruct((b, l, d), BF16),
        scratch_shapes=[pltpu.VMEM((rows + 2 * POOL_PAD, GRID_W, gd), F32)],
        compiler_params=_cparams(("parallel", "arbitrary")),
        name="pool_mix",
    )(h, w_in, w_grp, p_scale.reshape(p_scale.shape[0], 1, d))


MOE_TILE = 1024
MOE_WINDOWS = (128, 160, 192, 224, 256, 288, 320, 352, 384)
MOE_FULL = 256
ROW_ALIGN = 16
DEST_LANE = N_EXPERTS
PIECE_STRIDE = 32
N_PIECES = 3
AUX_ROWS = 8


def _route(sel, s):
    keep = []
    gsum = []
    for g in range(N_EXPERT_GROUPS):
        a = sel[g * EXPERTS_PER_GROUP:(g + 1) * EXPERTS_PER_GROUP]
        beaten = [jnp.zeros_like(a[0]) for _ in a]
        for i in range(EXPERTS_PER_GROUP):
            for j in range(i + 1, EXPERTS_PER_GROUP):
                ge = jnp.where(a[i] >= a[j], 1.0, 0.0)
                beaten[j] = beaten[j] + ge
                beaten[i] = beaten[i] + (1.0 - ge)
        kg = [jnp.where(bt < 1.5, 1.0, 0.0) for bt in beaten]
        keep.append(kg)
        gsum.append(sum(k * x for k, x in zip(kg, a)))
    picked = []
    bests = []
    for g in range(N_EXPERT_GROUPS):
        better = jnp.zeros_like(gsum[0])
        for o in range(N_EXPERT_GROUPS):
            if o < g:
                better = better + jnp.where(gsum[o] >= gsum[g], 1.0, 0.0)
            elif o > g:
                better = better + jnp.where(gsum[o] > gsum[g], 1.0, 0.0)
        best = jnp.where(better < 0.5, 1.0, 0.0)
        bests.append(best)
        for i in range(EXPERTS_PER_GROUP):
            picked.append(best * keep[g][i] * s[g * EXPERTS_PER_GROUP + i])
    den = sum(picked)
    return [p / den for p in picked], bests


def _post_kernel(x_ref, y_ref, w_ref, mod_ref, g_ref, rw_ref, rb_ref,
                 xo_ref, h2_ref, rt_ref, aux_ref, wbuf):
    @pl.when(pl.program_id(0) == 0)
    def _():
        wbuf[...] = w_ref[...].astype(BF16)

    yw = jnp.dot(y_ref[...], wbuf[...], preferred_element_type=F32)
    xn = x_ref[...] + mod_ref[0, 2:3, :] * yw
    xo_ref[...] = xn
    h2 = _modulate(xn, g_ref[...], mod_ref[0, 3:4, :], mod_ref[0, 4:5, :])
    hi = h2.astype(BF16)
    h2_ref[...] = hi
    lo = (h2 - hi.astype(F32)).astype(BF16)
    rw = rw_ref[...]
    half = hi.shape[0] // 2
    prod = jnp.concatenate(
        [jnp.dot(hi[r:r + half], rw, preferred_element_type=F32)
         + jnp.dot(lo[r:r + half], rw, preferred_element_type=F32) for r in (0, half)], axis=0).T
    logits = prod[0:N_EXPERTS, :] + prod[N_EXPERTS:2 * N_EXPERTS, :]
    s = _sigmoid_pair(logits)[0]
    sel = s + rb_ref[...]
    comb, bests = _route([sel[e:e + 1, :] for e in range(N_EXPERTS)],
                         [s[e:e + 1, :] for e in range(N_EXPERTS)])
    t = logits.shape[1]

    ind = jnp.concatenate(bests + [jnp.zeros((AUX_ROWS - N_EXPERT_GROUPS, t), F32)], axis=0)
    n_seg = t // LANES
    stacked = jnp.concatenate([ind[:, j * LANES:(j + 1) * LANES] for j in range(n_seg)], axis=0)
    upper = jnp.where(lax.broadcasted_iota(jnp.int32, (LANES, LANES), 0)
                      <= lax.broadcasted_iota(jnp.int32, (LANES, LANES), 1), 1.0, 0.0).astype(BF16)
    local = jnp.dot(stacked.astype(BF16), upper, preferred_element_type=F32)
    off = jnp.zeros((AUX_ROWS, 1), F32)
    pieces = []
    for j in range(n_seg):
        seg_cum = local[j * AUX_ROWS:(j + 1) * AUX_ROWS, :]
        pieces.append(seg_cum + off)
        off = off + seg_cum[:, LANES - 1:LANES]
    cum = jnp.concatenate(pieces, axis=1)
    counts = [off[g:g + 1, :] for g in range(N_EXPERT_GROUPS)]
    starts = [jnp.zeros((1, 1), F32)]
    for g in range(1, N_EXPERT_GROUPS):
        starts.append(starts[-1] + counts[g - 1])
    dest = sum(bests[g] * (starts[g] + cum[g:g + 1, :] - 1.0) for g in range(N_EXPERT_GROUPS))

    lane = lax.broadcasted_iota(jnp.int32, (1, t), 1)
    seg = sum(jnp.where(lane == k, v, 0.0) for k, v in enumerate(starts + counts))
    aux_ref[0] = jnp.concatenate([dest, seg, jnp.zeros((AUX_ROWS - 2, t), F32)], axis=0)
    p1 = [w.astype(BF16).astype(F32) for w in comb]
    r1 = [w - p for w, p in zip(comb, p1)]
    p2 = [r.astype(BF16).astype(F32) for r in r1]
    p3 = [r - p for r, p in zip(r1, p2)]
    pad = [jnp.zeros((PIECE_STRIDE - N_EXPERTS, t), F32)]
    table = jnp.concatenate(p1 + [dest] + [jnp.zeros((PIECE_STRIDE - N_EXPERTS - 1, t), F32)] + p2 + pad + p3 + pad
                            + [jnp.zeros((LANES - N_PIECES * PIECE_STRIDE, t), F32)], axis=0)
    rt_ref[...] = table.T


def _post_call(x, y, w, w_slot, mod, g, router_w, router_b, tokens_per_batch, tile=MOE_TILE):
    n, d = x.shape
    per_b = tokens_per_batch // tile
    rwh = router_w.astype(BF16)
    rwl = (router_w - rwh.astype(F32)).astype(BF16)
    rw = jnp.concatenate([rwh, rwl, jnp.zeros((d, LANES - 2 * N_EXPERTS), BF16)], axis=1)
    row = lambda i: (i, 0)
    fixed = lambda i: (0, 0)
    return pl.pallas_call(
        _post_kernel,
        grid=(n // tile,),
        in_specs=[
            pl.BlockSpec((tile, d), row),
            pl.BlockSpec((tile, d), row),
            pl.BlockSpec((None, d, d), lambda i: (w_slot, 0, 0)),
            pl.BlockSpec((1, N_MOD, d), lambda i: (i // per_b, 0, 0)),
            pl.BlockSpec((1, d), fixed),
            pl.BlockSpec((d, LANES), fixed),
            pl.BlockSpec((N_EXPERTS, 1), fixed),
        ],
        out_specs=[
            pl.BlockSpec((tile, d), row),
            pl.BlockSpec((tile, d), row),
            pl.BlockSpec((tile, LANES), row),
            pl.BlockSpec((1, AUX_ROWS, tile), lambda i: (i, 0, 0)),
        ],
        out_shape=[
            jax.ShapeDtypeStruct((n, d), F32),
            jax.ShapeDtypeStruct((n, d), BF16),
            jax.ShapeDtypeStruct((n, LANES), F32),
            jax.ShapeDtypeStruct((n // tile, AUX_ROWS, tile), F32),
        ],
        scratch_shapes=[pltpu.VMEM((d, d), BF16)],
        compiler_params=_cparams(("arbitrary",)),
        name="mixer_out_router",
    )(x, y, w, mod, g.reshape(1, d), rw, router_b.reshape(N_EXPERTS, 1))


def _moe_kernel(seg_ref, h_ref, rt_ref, aux_ref, wg_ref, wu_ref, wd_ref, x_ref, mod_ref, ng_ref, nmod_ref,
                o_ref, *rest, final_norm):
    if final_norm:
        xs, cws, ys = rest
    else:
        hn_ref, xs, cws, ys = rest
    i = pl.program_id(0)
    g = pl.program_id(1)
    t = h_ref.shape[0]

    @pl.when(g == 0)
    def _():
        dest_row = aux_ref[0, 0:1, :]
        perm = jnp.where(lax.broadcasted_iota(jnp.int32, (t, t), 0).astype(F32) == dest_row,
                         1.0, 0.0).astype(BF16)
        xs[...] = jnp.dot(perm, h_ref[...], preferred_element_type=F32).astype(BF16)
        cws[...] = jnp.dot(perm, rt_ref[...].astype(BF16), preferred_element_type=F32)
        ys[...] = jnp.zeros_like(ys)

    start = seg_ref[i, g]
    end = start + seg_ref[i, N_EXPERT_GROUPS + g]
    wd = wd_ref[...].reshape(EXPERTS_PER_GROUP * D_EXPERT, wd_ref.shape[2])

    def window(w0, rows, lo, hi):
        sl = pl.ds(pl.multiple_of(w0, ROW_ALIGN), rows)
        xw = xs[sl, :]
        cw = cws[sl, :]
        r = w0 + lax.broadcasted_iota(jnp.int32, cw.shape, 0)
        lane = lax.broadcasted_iota(jnp.int32, cw.shape, 1)
        piece_lane = jnp.where(lane < N_PIECES * PIECE_STRIDE, lane & (PIECE_STRIDE - 1), -1)
        cw = jnp.where(r >= lo, cw, 0.0)
        cw = jnp.where(r < hi, cw, 0.0)
        acts = []
        for e in range(EXPERTS_PER_GROUP):
            ce = jnp.sum(jnp.where(piece_lane == g * EXPERTS_PER_GROUP + e, cw, 0.0), axis=1, keepdims=True)
            ge = g * EXPERTS_PER_GROUP + e
            a = _silu_tanh(jnp.dot(xw, wg_ref[ge], preferred_element_type=F32)) \
                * jnp.dot(xw, wu_ref[ge], preferred_element_type=F32)
            acts.append((a * ce).astype(BF16))
        ys[sl, :] += jnp.dot(jnp.concatenate(acts, axis=1), wd, preferred_element_type=F32)

    aligned = start & -ROW_ALIGN
    need = end - aligned
    biggest = MOE_WINDOWS[-1]
    n_full = jnp.where(need > biggest, (need - (MOE_WINDOWS[0] + 1)) >> (MOE_FULL.bit_length() - 1), 0)

    def full(k, carry):
        w0 = aligned + k * MOE_FULL
        window(w0, MOE_FULL, jnp.maximum(start, w0), w0 + MOE_FULL)
        return carry

    lax.fori_loop(0, n_full, full, 0)
    rest0 = aligned + n_full * MOE_FULL
    rest = end - rest0
    for smaller, m in zip((0,) + MOE_WINDOWS[:-1], MOE_WINDOWS):
        @pl.when((end > start) & (rest > smaller) & (rest <= m))
        def _(m=m):
            window(jnp.minimum(rest0, t - m), m, jnp.maximum(start, rest0), end)

    @pl.when(g == N_EXPERT_GROUPS - 1)
    def _():
        dest_col = rt_ref[:, DEST_LANE:DEST_LANE + 1]
        unperm = jnp.where(lax.broadcasted_iota(jnp.int32, (t, t), 1).astype(F32) == dest_col,
                           1.0, 0.0).astype(BF16)
        ff = jnp.dot(unperm, ys[...].astype(BF16), preferred_element_type=F32)
        xn = x_ref[...] + mod_ref[0, 5:6, :] * ff
        if final_norm:
            ms = jnp.mean(xn * xn, axis=-1, keepdims=True)
            xn = xn * lax.rsqrt(ms + EPS) * ng_ref[...]
        else:
            hn_ref[...] = _modulate(xn, ng_ref[...], nmod_ref[0, 0:1, :], nmod_ref[0, 1:2, :]).astype(BF16)
        o_ref[...] = xn


def _moe_call(h2, table, aux, wg, wu, wd, layer, x, mod, norm_g, next_mod, tokens_per_batch, final_norm):
    n, d = x.shape
    tile = MOE_TILE
    per_b = tokens_per_batch // tile
    seg = aux[:, 1, :2 * N_EXPERT_GROUPS].astype(jnp.int32)
    row = lambda i, g, seg: (i, 0)
    grp = lambda i, g, seg: (layer, g, 0, 0)
    tok = pl.BlockSpec((tile, d), row)
    x_out = jax.ShapeDtypeStruct((n, d), F32)
    grid_spec = pltpu.PrefetchScalarGridSpec(
        num_scalar_prefetch=1,
        grid=(n // tile, N_EXPERT_GROUPS),
        in_specs=[
            pl.BlockSpec((tile, d), row),
            pl.BlockSpec((tile, LANES), row),
            pl.BlockSpec((1, AUX_ROWS, tile), lambda i, g, seg: (i, 0, 0)),
            pl.BlockSpec((None, N_EXPERTS, d, D_EXPERT), lambda i, g, seg: (layer, 0, 0, 0)),
            pl.BlockSpec((None, N_EXPERTS, d, D_EXPERT), lambda i, g, seg: (layer, 0, 0, 0)),
            pl.BlockSpec((None, EXPERTS_PER_GROUP, D_EXPERT, d), grp),
            pl.BlockSpec((tile, d), row),
            pl.BlockSpec((1, N_MOD, d), lambda i, g, seg: (i // per_b, 0, 0)),
            pl.BlockSpec((1, d), lambda i, g, seg: (0, 0)),
            pl.BlockSpec((1, N_MOD, d), lambda i, g, seg: (i // per_b, 0, 0)),
        ],
        out_specs=tok if final_norm else [tok, tok],
        scratch_shapes=[
            pltpu.VMEM((tile, d), BF16),
            pltpu.VMEM((tile, LANES), F32),
            pltpu.VMEM((tile, d), F32),
        ],
    )
    return pl.pallas_call(
        functools.partial(_moe_kernel, final_norm=final_norm),
        grid_spec=grid_spec,
        out_shape=x_out if final_norm else [x_out, jax.ShapeDtypeStruct((n, d), BF16)],
        compiler_params=_cparams(("parallel", "arbitrary"), MOE_VMEM_LIMIT),
        name="moe_ffn",
    )(seg, h2, table, aux, wg, wu, wd, x, mod, norm_g.reshape(1, d), next_mod)


def kernel(x, c, ctx, c_ctx, w_mod, b_mod, norm1_g, norm2_g, hg_w_in, hg_lb_fwd, hg_lb_bwd, hg_gnorm,
           hg_w_out, pool_w_in, pool_w_grp, pool_scale, pool_w_out, router_w, router_b, moe_w_gate,
           moe_w_up, moe_w_down, final_g):
    b, l, d = x.shape
    depth = w_mod.shape[0]
    n_mixers = 2

    cc = jnp.concatenate([c, c_ctx[None, :], jnp.zeros((MOD_ROWS - b - 1, d), F32)], axis=0)
    mods = _mod_call(cc, w_mod, b_mod)

    experts = None
    x_lat = x.reshape(b * l, d)
    h_lat = None
    for i in range(depth):
        slot = i // n_mixers
        mod_lat = mods[i, :b].reshape(b, N_MOD, d)
        if i % n_mixers == 0:
            h_lat, g_lat = _norm_call(x_lat.reshape(b, l, d), norm1_g[i], mod_lat, 1024,
                                      hg_w_in, slot, N_HG_PROJ - 1)
            mod_ctx = jnp.broadcast_to(mods[i, b].reshape(1, N_MOD, d), (b, N_MOD, d))
            h_ctx = _norm_call(ctx, norm1_g[i], mod_ctx, ctx.shape[1])
            to_cast = (moe_w_gate, moe_w_up, moe_w_down) if experts is None else ()
            y, cast = _hgrn_call(h_ctx, h_lat, g_lat, hg_w_in, hg_lb_fwd, hg_lb_bwd, hg_gnorm[slot], slot, to_cast)
            experts = experts or tuple(cast)
            w_out = hg_w_out
        else:
            if h_lat is None:
                h_lat = _norm_call(x_lat.reshape(b, l, d), norm1_g[i], mod_lat, 512)
            y = _pool_call(h_lat, pool_w_in, pool_w_grp, pool_scale, slot)
            w_out = pool_w_out
        x_lat, h2, table, aux = _post_call(x_lat, y.reshape(b * l, d), w_out, slot, mod_lat,
                                           norm2_g[i], router_w, router_b, l)
        if experts is None:
            experts = (moe_w_gate.astype(BF16), moe_w_up.astype(BF16), moe_w_down.astype(BF16))
        if i == depth - 1:
            x_lat = _moe_call(h2, table, aux, *experts, i, x_lat, mod_lat, final_g, mod_lat, l, final_norm=True)
        else:
            mod_next = mods[i + 1, :b].reshape(b, N_MOD, d)
            x_lat, h_next = _moe_call(h2, table, aux, *experts, i, x_lat, mod_lat, norm1_g[i + 1], mod_next, l,
                                      final_norm=False)
            h_lat = h_next.reshape(b, l, d) if (i + 1) % n_mixers != 0 else None
    return x_lat.reshape(b, l, d)
```

```python
import functools

import jax
import jax.numpy as jnp
from jax import lax
from jax.experimental import pallas as pl
from jax.experimental.pallas import tpu as pltpu

F32 = jnp.float32
BF16 = jnp.bfloat16

EPS = 1e-6
N_MOD = 6
HG_HEADS = 8
HG_DK = 128
HG_CHUNK = 64
N_HG_PROJ = 5
POOL_WINDOWS = (2, 4, 8, 16)
GRID_W = 64
GRID_SHIFT = GRID_W.bit_length() - 1
assert 1 << GRID_SHIFT == GRID_W
N_EXPERTS = 16
N_EXPERT_GROUPS = 4
EXPERTS_PER_GROUP = N_EXPERTS // N_EXPERT_GROUPS
D_EXPERT = 256

LANES = 128
MOD_ROWS = 8
VMEM_LIMIT = 56 * 1024 * 1024
HGRN_VMEM_LIMIT = 60 * 1024 * 1024
MOE_VMEM_LIMIT = 60 * 1024 * 1024


def _cparams(sem, vmem=VMEM_LIMIT):
    return pltpu.CompilerParams(dimension_semantics=sem, vmem_limit_bytes=vmem)


def _sigmoid_pair(z):
    e = jnp.exp(-jnp.abs(z))
    r = 1.0 / (1.0 + e)
    er = e * r
    pos = z >= 0
    return jnp.where(pos, r, er), jnp.where(pos, er, r)


def _silu(z):
    return z * _sigmoid_pair(z)[0]


def _sigmoid_tanh(z):
    return 0.5 * jnp.tanh(0.5 * z) + 0.5


def _silu_tanh(z):
    return z * _sigmoid_tanh(z)


def _split_bf16(x):
    hi = x.astype(BF16)
    return hi, (x - hi.astype(F32)).astype(BF16)


def _mod_kernel(c_ref, w_ref, b_ref, o_ref):
    a = jnp.concatenate(_split_bf16(_silu(c_ref[...])), axis=0)
    w_hi, w_lo = _split_bf16(w_ref[0])
    acc = jnp.dot(a, w_hi, preferred_element_type=F32) + jnp.dot(a, w_lo, preferred_element_type=F32)
    o_ref[0] = acc[0:MOD_ROWS] + acc[MOD_ROWS:2 * MOD_ROWS] + b_ref[0]


def _mod_call(cc, w_mod, b_mod):
    depth, d, n = w_mod.shape
    tn = 1024
    return pl.pallas_call(
        _mod_kernel,
        grid=(depth, n // tn),
        in_specs=[
            pl.BlockSpec((MOD_ROWS, d), lambda i, j: (0, 0)),
            pl.BlockSpec((1, d, tn), lambda i, j: (i, 0, j)),
            pl.BlockSpec((1, 1, tn), lambda i, j: (i, 0, j)),
        ],
        out_specs=pl.BlockSpec((1, MOD_ROWS, tn), lambda i, j: (i, 0, j)),
        out_shape=jax.ShapeDtypeStruct((depth, MOD_ROWS, n), F32),
        compiler_params=_cparams(("parallel", "parallel")),
        name="mod_proj",
    )(cc, w_mod, b_mod.reshape(depth, 1, n))


def _modulate(x, g, shift, scale):
    ms = jnp.mean(x * x, axis=-1, keepdims=True)
    return (x * lax.rsqrt(ms + EPS) * g) * (1.0 + scale) + shift


def _norm_kernel(x_ref, g_ref, mod_ref, *rest):
    h = _modulate(x_ref[0], g_ref[...], mod_ref[0, 0:1, :], mod_ref[0, 1:2, :]).astype(BF16)
    if len(rest) == 1:
        (o_ref,) = rest
    else:
        w_ref, o_ref, p_ref, wbuf = rest

        @pl.when((pl.program_id(0) == 0) & (pl.program_id(1) == 0))
        def _():
            wbuf[...] = w_ref[...].astype(BF16)

        p_ref[0] = jnp.dot(h, wbuf[...], preferred_element_type=F32).astype(BF16)
    o_ref[0] = h


def _norm_call(x, g, mod, tile, w_proj=None, w_slot=0, w_col=0):
    b, t, d = x.shape
    tok = pl.BlockSpec((1, tile, d), lambda i, j: (i, j, 0))
    in_specs = [tok, pl.BlockSpec((1, d), lambda i, j: (0, 0)),
                pl.BlockSpec((1, N_MOD, d), lambda i, j: (i, 0, 0))]
    args = [x, g.reshape(1, d), mod]
    out_specs, out_shape = tok, jax.ShapeDtypeStruct((b, t, d), BF16)
    scratch = []
    if w_proj is not None:
        in_specs.append(pl.BlockSpec((None, d, d), lambda i, j: (w_slot, 0, w_col)))
        args.append(w_proj)
        out_specs, out_shape = [tok, tok], [out_shape, out_shape]
        scratch = [pltpu.VMEM((d, d), BF16)]
    return pl.pallas_call(
        _norm_kernel,
        grid=(b, t // tile),
        in_specs=in_specs,
        out_specs=out_specs,
        out_shape=out_shape,
        scratch_shapes=scratch,
        compiler_params=_cparams(("arbitrary", "arbitrary")),
        name="norm1",
    )(*args)


HG_BLOCK = 256
HG_CPB = HG_BLOCK // HG_CHUNK


def _chunk_prefix(x, row):
    for d in (1, 2, 4, 8, 16, 32):
        x = x + jnp.where(row >= d, pltpu.roll(x, d, axis=0), 0.0)
    return x


def _chunk_suffix(x, row):
    n = x.shape[0]
    for d in (1, 2, 4, 8, 16, 32):
        x = x + jnp.where(row < HG_CHUNK - d, pltpu.roll(x, n - d, axis=0), 0.0)
    return x


def _lower_bound(lb_ref, slot):
    rows = [lb_ref[j, 0] for j in range(lb_ref.shape[0])]
    m = functools.reduce(jnp.maximum, rows)
    es = [jnp.exp(r - m) for r in rows]
    return sum(es[:slot + 1]) / sum(es)


N_HGRN_IN = 10


def _hgrn_kernel(*refs, slot, n_ctx_chunks, n_lat_chunks, n_cast):
    hc_ref, hl_ref, g_ref, wq_ref, wv_ref, wf_ref, wb_ref, lbf_ref, lbb_ref, gn_ref = refs[:N_HGRN_IN]
    cast_in = refs[N_HGRN_IN:N_HGRN_IN + n_cast]
    y_ref = refs[N_HGRN_IN + n_cast]
    cast_out = refs[N_HGRN_IN + n_cast + 1:N_HGRN_IN + 2 * n_cast + 1]
    (wbuf, pbuf0, pbuf1, pbuf2, pbuf3, oacc, qif, qib,
     kvf, kvb, decf, decb, spf, spb) = refs[N_HGRN_IN + 2 * n_cast + 1:]
    for c_ref, o_ref in zip(cast_in, cast_out):
        o_ref[...] = c_ref[...].astype(BF16)

    lb_f = _lower_bound(lbf_ref, slot)
    lb_b = _lower_bound(lbb_ref, slot)
    row = lax.broadcasted_iota(jnp.int32, (HG_BLOCK, HG_DK), 0) & (HG_CHUNK - 1)
    ci = lax.broadcasted_iota(jnp.int32, (HG_CPB, HG_CHUNK, HG_CHUNK), 1)
    si = lax.broadcasted_iota(jnp.int32, (HG_CPB, HG_CHUNK, HG_CHUNK), 2)

    def c3(t):
        return t.reshape(HG_CPB, HG_CHUNK, HG_DK)

    def direction(qs3, v3b, z, lb, fwd):
        sig = _sigmoid_tanh(z)
        f = lb + (1.0 - lb) * sig
        k3 = c3((1.0 - lb) * (1.0 - sig))
        lf = jnp.log(f)
        if fwd:
            cum = c3(_chunk_prefix(lf, row))
            ref = cum[:, HG_CHUNK // 2 - 1:HG_CHUNK // 2, :]
            last = cum[:, HG_CHUNK - 1:HG_CHUNK, :]
        else:
            cum = c3(_chunk_suffix(lf, row))
            ref = cum[:, HG_CHUNK // 2:HG_CHUNK // 2 + 1, :]
            last = cum[:, 0:1, :]
        dec = jnp.exp(last)
        if qs3 is None:
            kl = k3 * jnp.exp(last - cum)
        else:
            e1 = jnp.exp(cum - ref)
            qd = qs3 * e1
            qi = qd * jnp.exp(ref)
            kd = k3 * (1.0 / e1)
            kl = kd * jnp.exp(last - ref)
        kvt = jnp.einsum('ncv,nck->nvk', v3b, kl.astype(BF16), preferred_element_type=F32)
        if qs3 is None:
            return None, None, kvt, dec
        sc = jnp.einsum('nck,nsk->ncs', qd.astype(BF16), kd.astype(BF16), preferred_element_type=F32)
        sc = jnp.where((ci >= si) if fwd else (ci <= si), sc, 0.0)
        intra = jnp.einsum('ncs,nsv->ncv', sc.astype(BF16), v3b, preferred_element_type=F32)
        return intra, qi, kvt, dec

    for p, wp_ref in enumerate((wq_ref, wv_ref, wf_ref, wb_ref)):
        wbuf[:, p * HG_DK:(p + 1) * HG_DK] = wp_ref[...].astype(BF16)

    def project(hrows):
        return jnp.dot(hrows, wbuf[...], preferred_element_type=F32)

    def block(p, chunk0, lat_row0, between=None):
        v3b = c3(p[:, HG_DK:2 * HG_DK]).astype(BF16)
        zf = p[:, 2 * HG_DK:3 * HG_DK]
        zb = p[:, 3 * HG_DK:4 * HG_DK]
        if lat_row0 is None:
            qs3 = None
        else:
            qs3 = c3(_silu_tanh(p[:, 0:HG_DK]))
        in_f, qi_f, kv_f, dec_f = direction(qs3, v3b, zf, lb_f, True)
        if between is not None:
            between()
        in_b, qi_b, kv_b, dec_b = direction(qs3, v3b, zb, lb_b, False)
        kvf[pl.ds(chunk0, HG_CPB)] = kv_f
        kvb[pl.ds(chunk0, HG_CPB)] = kv_b
        decf[pl.ds(chunk0, HG_CPB)] = dec_f
        decb[pl.ds(chunk0, HG_CPB)] = dec_b
        if lat_row0 is not None:
            rows = pl.ds(lat_row0, HG_BLOCK)
            oacc[rows, :] = (in_f + in_b).reshape(HG_BLOCK, HG_DK)
            qif[rows, :] = qi_f.reshape(HG_BLOCK, HG_DK).astype(BF16)
            qib[rows, :] = qi_b.reshape(HG_BLOCK, HG_DK).astype(BF16)

    n_blocks = n_lat_chunks // HG_CPB

    def lat_rows(i):
        return hl_ref[0, pl.ds(pl.multiple_of(i * HG_BLOCK, HG_BLOCK), HG_BLOCK), :]

    def lat_terms(p_ref, i, between=None):
        block(p_ref[...], n_ctx_chunks + i * HG_CPB, pl.multiple_of(i * HG_BLOCK, HG_BLOCK), between)

    half_n = 2 * HG_DK

    def project_half(dst, i, k):
        c = slice(k * half_n, (k + 1) * half_n)
        dst[:, c] = jnp.dot(lat_rows(i), wbuf[:, c], preferred_element_type=F32)

    ctx_p = [project(hc_ref[0, i * HG_BLOCK:(i + 1) * HG_BLOCK, :]) for i in range(n_ctx_chunks // HG_CPB)]
    pbuf0[...] = project(lat_rows(0))
    for i, p in enumerate(ctx_p):
        block(p, i * HG_CPB, None)
    pbuf1[...] = project(lat_rows(1))

    def lat_quad(i0, last):
        project_half(pbuf2, i0 + 2, 0)
        lat_terms(pbuf0, i0, lambda: project_half(pbuf2, i0 + 2, 1))
        project_half(pbuf3, i0 + 3, 0)
        lat_terms(pbuf1, i0 + 1, lambda: project_half(pbuf3, i0 + 3, 1))
        if not last:
            project_half(pbuf0, i0 + 4, 0)
        lat_terms(pbuf2, i0 + 2, None if last else (lambda: project_half(pbuf0, i0 + 4, 1)))
        if not last:
            project_half(pbuf1, i0 + 5, 0)
        lat_terms(pbuf3, i0 + 3, None if last else (lambda: project_half(pbuf1, i0 + 5, 1)))

    def lat_body(t, carry):
        lat_quad(4 * t, False)
        return carry

    lax.fori_loop(0, n_blocks // 4 - 1, lat_body, 0)
    lat_quad(n_blocks - 4, True)

    def advance(s, kv_ref, dec_ref, n):
        return dec_ref[n] * s + kv_ref[n]

    sf = jnp.zeros((HG_DK, HG_DK), F32)
    for n in range(n_ctx_chunks):
        sf = advance(sf, kvf, decf, n)
    sb = jnp.zeros((HG_DK, HG_DK), F32)
    for n in reversed(range(n_ctx_chunks)):
        sb = advance(sb, kvb, decb, n)

    def scan_step(t, carry):
        sf, sb = carry
        jb = n_lat_chunks - 1 - t
        spf[t] = sf.astype(BF16)
        spb[jb] = sb.astype(BF16)
        return (advance(sf, kvf, decf, n_ctx_chunks + t), advance(sb, kvb, decb, n_ctx_chunks + jb))

    lax.fori_loop(0, n_lat_chunks, scan_step, (sf, sb), unroll=2)

    gn = gn_ref[...]

    def block_rows(i):
        return pl.ds(pl.multiple_of(i * HG_BLOCK, HG_BLOCK), HG_BLOCK)

    def inter(i):
        rows = block_rows(i)
        chunks = pl.ds(i * HG_CPB, HG_CPB)
        return (jnp.einsum('nck,nvk->ncv', c3(qif[rows, :]), spf[chunks], preferred_element_type=F32)
                + jnp.einsum('nck,nvk->ncv', c3(qib[rows, :]), spb[chunks], preferred_element_type=F32)
                ).reshape(HG_BLOCK, HG_DK)

    def readout(o_ref, i):
        rows = block_rows(i)
        o = oacc[rows, :] + o_ref[...]
        ms = jnp.mean(o * o, axis=-1, keepdims=True)
        o = o * lax.rsqrt(ms + EPS) * gn
        y_ref[0, rows, :] = (o * _silu_tanh(g_ref[0, rows, :].astype(F32))).astype(BF16)

    obuf0, obuf1 = pbuf0.at[:, 0:HG_DK], pbuf1.at[:, 0:HG_DK]
    obuf0[...] = inter(0)

    def readout_pair(t, carry):
        obuf1[...] = inter(2 * t + 1)
        readout(obuf0, 2 * t)
        obuf0[...] = inter(2 * t + 2)
        readout(obuf1, 2 * t + 1)
        return carry

    lax.fori_loop(0, n_blocks // 2 - 1, readout_pair, 0)
    obuf1[...] = inter(n_blocks - 1)
    readout(obuf0, n_blocks - 2)
    readout(obuf1, n_blocks - 1)


def _hgrn_call(h_ctx, h_lat, g_lat, w_in, lb_fwd, lb_bwd, gnorm, slot, to_cast):
    b, lc, d = h_ctx.shape
    ll = h_lat.shape[1]
    nrow = lb_fwd.shape[0]
    ncc, nlc = lc // HG_CHUNK, ll // HG_CHUNK
    kern = functools.partial(_hgrn_kernel, slot=slot, n_ctx_chunks=ncc, n_lat_chunks=nlc, n_cast=len(to_cast))
    lb_spec = pl.BlockSpec((nrow, 1, 1, HG_DK), lambda i, h: (0, h, 0, 0))
    head_cols = pl.BlockSpec((1, ll, HG_DK), lambda i, h: (i, 0, h))
    n_rec = N_HG_PROJ - 1
    w_cols = [pl.BlockSpec((None, d, HG_DK), functools.partial(lambda i, h, p: (slot, 0, p * HG_HEADS + h), p=p))
              for p in range(n_rec)]
    steps = b * HG_HEADS
    slabs = [a.reshape(steps, -1, *a.shape[-2:]) for a in to_cast]
    slab_specs = [pl.BlockSpec((1,) + a.shape[1:], lambda i, h: (i * HG_HEADS + h, 0, 0, 0)) for a in slabs]
    outs = pl.pallas_call(
        kern,
        grid=(b, HG_HEADS),
        in_specs=[
            pl.BlockSpec((1, lc, d), lambda i, h: (i, 0, 0)),
            pl.BlockSpec((1, ll, d), lambda i, h: (i, 0, 0)),
            head_cols,
            *w_cols,
            lb_spec, lb_spec,
            pl.BlockSpec((1, HG_DK), lambda i, h: (0, 0)),
            *slab_specs,
        ],
        out_specs=[head_cols, *slab_specs],
        out_shape=[jax.ShapeDtypeStruct((b, ll, d), BF16)]
        + [jax.ShapeDtypeStruct(a.shape, BF16) for a in slabs],
        scratch_shapes=[
            pltpu.VMEM((d, n_rec * HG_DK), BF16),
            pltpu.VMEM((HG_BLOCK, n_rec * HG_DK), F32),
            pltpu.VMEM((HG_BLOCK, n_rec * HG_DK), F32),
            pltpu.VMEM((HG_BLOCK, n_rec * HG_DK), F32),
            pltpu.VMEM((HG_BLOCK, n_rec * HG_DK), F32),
            pltpu.VMEM((ll, HG_DK), F32),
            pltpu.VMEM((ll, HG_DK), BF16),
            pltpu.VMEM((ll, HG_DK), BF16),
            pltpu.VMEM((ncc + nlc, HG_DK, HG_DK), F32),
            pltpu.VMEM((ncc + nlc, HG_DK, HG_DK), F32),
            pltpu.VMEM((ncc + nlc, 1, HG_DK), F32),
            pltpu.VMEM((ncc + nlc, 1, HG_DK), F32),
            pltpu.VMEM((nlc, HG_DK, HG_DK), BF16),
            pltpu.VMEM((nlc, HG_DK, HG_DK), BF16),
        ],
        compiler_params=_cparams(("parallel", "arbitrary"), HGRN_VMEM_LIMIT),
        name="hgrn2",
    )(h_ctx, h_lat, g_lat, *([w_in] * n_rec),
      lb_fwd.reshape(nrow, HG_HEADS, 1, HG_DK), lb_bwd.reshape(nrow, HG_HEADS, 1, HG_DK),
      gnorm.reshape(1, HG_DK), *slabs)
    return outs[0], [o.reshape(a.shape) for o, a in zip(outs[1:], to_cast)]


POOL_STRIP = 8
POOL_PAD = 8
POOL_BAND = 256


def _pool_group(h_ref, win_ref, wg_ref, ps_ref, o_ref, upad, *, win, rows):
    gd = win_ref.shape[1]
    half = win // 2
    zeros = jnp.zeros((POOL_PAD, GRID_W, gd), F32)
    upad[0:POOL_PAD] = zeros
    upad[POOL_PAD + rows:POOL_PAD + rows + POOL_PAD] = zeros
    w_in = win_ref[...].astype(BF16)
    tok = POOL_STRIP * GRID_W
    n_strips = rows // POOL_STRIP

    def project(c):
        u = jnp.dot(h_ref[0, c * tok:(c + 1) * tok, :], w_in, preferred_element_type=F32)
        upad[POOL_PAD + c * POOL_STRIP:POOL_PAD + (c + 1) * POOL_STRIP] = u.reshape(POOL_STRIP, GRID_W, gd)

    bi = lax.broadcasted_iota(jnp.int32, (POOL_BAND, POOL_BAND), 0)
    bj = lax.broadcasted_iota(jnp.int32, (POOL_BAND, POOL_BAND), 1)
    lo_c = (bi & (GRID_W - 1)) - half
    cj = bj & (GRID_W - 1)
    band = jnp.where(bi >> GRID_SHIFT == bj >> GRID_SHIFT, 1.0, 0.0)
    band = jnp.where(cj >= lo_c, band, 0.0)
    band = jnp.where(cj < lo_c + win, band, 0.0).astype(BF16)
    t = lax.broadcasted_iota(jnp.int32, (tok, LANES), 0)
    col = t & (GRID_W - 1)
    cnt_c = jnp.minimum(col - half + win, GRID_W) - jnp.maximum(col - half, 0)
    wg = wg_ref[0].astype(BF16)
    ps = ps_ref[...]

    def strip(i):
        r0 = i * POOL_STRIP
        slab = upad[r0 + POOL_PAD - half:r0 + POOL_PAD - half + POOL_STRIP + win - 1]
        span = 1
        while span < win:
            n = slab.shape[0] - span
            slab = slab[0:n] + slab[span:span + n]
            span *= 2
        rs = slab.reshape(tok, gd)
        hi = rs.astype(BF16)
        lo = (rs - hi.astype(F32)).astype(BF16)
        parts = []
        for k in range(tok // POOL_BAND):
            sl = slice(k * POOL_BAND, (k + 1) * POOL_BAND)
            parts.append(jnp.dot(band, hi[sl], preferred_element_type=F32)
                         + jnp.dot(band, lo[sl], preferred_element_type=F32))
        box = jnp.concatenate(parts, axis=0)
        r = r0 + (t >> GRID_SHIFT)
        cnt_r = jnp.minimum(r - half + win, rows) - jnp.maximum(r - half, 0)
        inv = 1.0 / (cnt_r * cnt_c).astype(F32)
        mean = box * jnp.concatenate([inv] * (gd // LANES), axis=1)
        ug = upad[r0 + POOL_PAD:r0 + POOL_PAD + POOL_STRIP].reshape(tok, gd)
        z = jnp.dot((mean - ug).astype(BF16), wg, preferred_element_type=F32) * ps
        o_ref[0, i * tok:(i + 1) * tok, :] = z.astype(BF16)

    project(0)
    project(1)
    for i in range(n_strips):
        if i + 2 < n_strips:
            project(i + 2)
        strip(i)


def _pool_kernel(h_ref, win_ref, wg_ref, ps_ref, o_ref, upad, *, rows):
    g = pl.program_id(1)
    for gi, win in enumerate(POOL_WINDOWS):
        @pl.when(g == gi)
        def _(win=win):
            _pool_group(h_ref, win_ref, wg_ref, ps_ref, o_ref, upad, win=win, rows=rows)


def _pool_call(h, w_in, w_grp, p_scale, slot):
    b, l, d = h.shape
    ng = len(POOL_WINDOWS)
    gd = d // ng
    rows = l // GRID_W
    return pl.pallas_call(
        functools.partial(_pool_kernel, rows=rows),
        grid=(b, ng),
        in_specs=[
            pl.BlockSpec((1, l, d), lambda i, g: (i, 0, 0)),
            pl.BlockSpec((None, d, gd), lambda i, g: (slot, 0, g)),
            pl.BlockSpec((None, 1, gd, gd), lambda i, g: (slot, g, 0, 0)),
            pl.BlockSpec((None, 1, gd), lambda i, g: (slot, 0, g)),
        ],
        out_specs=pl.BlockSpec((1, l, gd), lambda i, g: (i, 0, g)),
        out_shape=jax.ShapeDtypeStruct((b, l, d), BF16),
        scratch_shapes=[pltpu.VMEM((rows + 2 * POOL_PAD, GRID_W, gd), F32)],
        compiler_params=_cparams(("parallel", "arbitrary")),
        name="pool_mix",
    )(h, w_in, w_grp, p_scale.reshape(p_scale.shape[0], 1, d))


MOE_TILE = 1024
MOE_WINDOWS = (128, 160, 192, 224, 256, 288, 320, 352, 384)
MOE_FULL = 256
ROW_ALIGN = 16
DEST_LANE = N_EXPERTS
PIECE_STRIDE = 32
N_PIECES = 3
AUX_ROWS = 8


def _route(sel, s):
    keep = []
    gsum = []
    for g in range(N_EXPERT_GROUPS):
        a = sel[g * EXPERTS_PER_GROUP:(g + 1) * EXPERTS_PER_GROUP]
        beaten = [jnp.zeros_like(a[0]) for _ in a]
        for i in range(EXPERTS_PER_GROUP):
            for j in range(i + 1, EXPERTS_PER_GROUP):
                ge = jnp.where(a[i] >= a[j], 1.0, 0.0)
                beaten[j] = beaten[j] + ge
                beaten[i] = beaten[i] + (1.0 - ge)
        kg = [jnp.where(bt < 1.5, 1.0, 0.0) for bt in beaten]
        keep.append(kg)
        gsum.append(sum(k * x for k, x in zip(kg, a)))
    picked = []
    bests = []
    for g in range(N_EXPERT_GROUPS):
        better = jnp.zeros_like(gsum[0])
        for o in range(N_EXPERT_GROUPS):
            if o < g:
                better = better + jnp.where(gsum[o] >= gsum[g], 1.0, 0.0)
            elif o > g:
                better = better + jnp.where(gsum[o] > gsum[g], 1.0, 0.0)
        best = jnp.where(better < 0.5, 1.0, 0.0)
        bests.append(best)
        for i in range(EXPERTS_PER_GROUP):
            picked.append(best * keep[g][i] * s[g * EXPERTS_PER_GROUP + i])
    den = sum(picked)
    return [p / den for p in picked], bests


def _post_kernel(x_ref, y_ref, w_ref, mod_ref, g_ref, rw_ref, rb_ref,
                 xo_ref, h2_ref, rt_ref, aux_ref, wbuf):
    @pl.when(pl.program_id(0) == 0)
    def _():
        wbuf[...] = w_ref[...].astype(BF16)

    yw = jnp.dot(y_ref[...], wbuf[...], preferred_element_type=F32)
    xn = x_ref[...] + mod_ref[0, 2:3, :] * yw
    xo_ref[...] = xn
    h2 = _modulate(xn, g_ref[...], mod_ref[0, 3:4, :], mod_ref[0, 4:5, :])
    hi = h2.astype(BF16)
    h2_ref[...] = hi
    lo = (h2 - hi.astype(F32)).astype(BF16)
    rw = rw_ref[...]
    half = hi.shape[0] // 2
    prod = jnp.concatenate(
        [jnp.dot(hi[r:r + half], rw, preferred_element_type=F32)
         + jnp.dot(lo[r:r + half], rw, preferred_element_type=F32) for r in (0, half)], axis=0).T
    logits = prod[0:N_EXPERTS, :] + prod[N_EXPERTS:2 * N_EXPERTS, :]
    s = _sigmoid_pair(logits)[0]
    sel = s + rb_ref[...]
    comb, bests = _route([sel[e:e + 1, :] for e in range(N_EXPERTS)],
                         [s[e:e + 1, :] for e in range(N_EXPERTS)])
    t = logits.shape[1]

    ind = jnp.concatenate(bests + [jnp.zeros((AUX_ROWS - N_EXPERT_GROUPS, t), F32)], axis=0)
    n_seg = t // LANES
    stacked = jnp.concatenate([ind[:, j * LANES:(j + 1) * LANES] for j in range(n_seg)], axis=0)
    upper = jnp.where(lax.broadcasted_iota(jnp.int32, (LANES, LANES), 0)
                      <= lax.broadcasted_iota(jnp.int32, (LANES, LANES), 1), 1.0, 0.0).astype(BF16)
    local = jnp.dot(stacked.astype(BF16), upper, preferred_element_type=F32)
    off = jnp.zeros((AUX_ROWS, 1), F32)
    pieces = []
    for j in range(n_seg):
        seg_cum = local[j * AUX_ROWS:(j + 1) * AUX_ROWS, :]
        pieces.append(seg_cum + off)
        off = off + seg_cum[:, LANES - 1:LANES]
    cum = jnp.concatenate(pieces, axis=1)
    counts = [off[g:g + 1, :] for g in range(N_EXPERT_GROUPS)]
    starts = [jnp.zeros((1, 1), F32)]
    for g in range(1, N_EXPERT_GROUPS):
        starts.append(starts[-1] + counts[g - 1])
    dest = sum(bests[g] * (starts[g] + cum[g:g + 1, :] - 1.0) for g in range(N_EXPERT_GROUPS))

    lane = lax.broadcasted_iota(jnp.int32, (1, t), 1)
    seg = sum(jnp.where(lane == k, v, 0.0) for k, v in enumerate(starts + counts))
    aux_ref[0] = jnp.concatenate([dest, seg, jnp.zeros((AUX_ROWS - 2, t), F32)], axis=0)
    p1 = [w.astype(BF16).astype(F32) for w in comb]
    r1 = [w - p for w, p in zip(comb, p1)]
    p2 = [r.astype(BF16).astype(F32) for r in r1]
    p3 = [r - p for r, p in zip(r1, p2)]
    pad = [jnp.zeros((PIECE_STRIDE - N_EXPERTS, t), F32)]
    table = jnp.concatenate(p1 + [dest] + [jnp.zeros((PIECE_STRIDE - N_EXPERTS - 1, t), F32)] + p2 + pad + p3 + pad
                            + [jnp.zeros((LANES - N_PIECES * PIECE_STRIDE, t), F32)], axis=0)
    rt_ref[...] = table.T


def _post_call(x, y, w, w_slot, mod, g, router_w, router_b, tokens_per_batch, tile=MOE_TILE):
    n, d = x.shape
    per_b = tokens_per_batch // tile
    rwh = router_w.astype(BF16)
    rwl = (router_w - rwh.astype(F32)).astype(BF16)
    rw = jnp.concatenate([rwh, rwl, jnp.zeros((d, LANES - 2 * N_EXPERTS), BF16)], axis=1)
    row = lambda i: (i, 0)
    fixed = lambda i: (0, 0)
    return pl.pallas_call(
        _post_kernel,
        grid=(n // tile,),
        in_specs=[
            pl.BlockSpec((tile, d), row),
            pl.BlockSpec((tile, d), row),
            pl.BlockSpec((None, d, d), lambda i: (w_slot, 0, 0)),
            pl.BlockSpec((1, N_MOD, d), lambda i: (i // per_b, 0, 0)),
            pl.BlockSpec((1, d), fixed),
            pl.BlockSpec((d, LANES), fixed),
            pl.BlockSpec((N_EXPERTS, 1), fixed),
        ],
        out_specs=[
            pl.BlockSpec((tile, d), row),
            pl.BlockSpec((tile, d), row),
            pl.BlockSpec((tile, LANES), row),
            pl.BlockSpec((1, AUX_ROWS, tile), lambda i: (i, 0, 0)),
        ],
        out_shape=[
            jax.ShapeDtypeStruct((n, d), F32),
            jax.ShapeDtypeStruct((n, d), BF16),
            jax.ShapeDtypeStruct((n, LANES), F32),
            jax.ShapeDtypeStruct((n // tile, AUX_ROWS, tile), F32),
        ],
        scratch_shapes=[pltpu.VMEM((d, d), BF16)],
        compiler_params=_cparams(("arbitrary",)),
        name="mixer_out_router",
    )(x, y, w, mod, g.reshape(1, d), rw, router_b.reshape(N_EXPERTS, 1))


def _moe_kernel(seg_ref, h_ref, rt_ref, aux_ref, wg_ref, wu_ref, wd_ref, x_ref, mod_ref, ng_ref, nmod_ref,
                o_ref, *rest, final_norm):
    if final_norm:
        xs, cws, ys = rest
    else:
        hn_ref, xs, cws, ys = rest
    i = pl.program_id(0)
    g = pl.program_id(1)
    t = h_ref.shape[0]

    @pl.when(g == 0)
    def _():
        dest_row = aux_ref[0, 0:1, :]
        perm = jnp.where(lax.broadcasted_iota(jnp.int32, (t, t), 0).astype(F32) == dest_row,
                         1.0, 0.0).astype(BF16)
        xs[...] = jnp.dot(perm, h_ref[...], preferred_element_type=F32).astype(BF16)
        cws[...] = jnp.dot(perm, rt_ref[...].astype(BF16), preferred_element_type=F32)
        ys[...] = jnp.zeros_like(ys)

    start = seg_ref[i, g]
    end = start + seg_ref[i, N_EXPERT_GROUPS + g]
    wd = wd_ref[...].reshape(EXPERTS_PER_GROUP * D_EXPERT, wd_ref.shape[2])

    def window(w0, rows, lo, hi):
        sl = pl.ds(pl.multiple_of(w0, ROW_ALIGN), rows)
        xw = xs[sl, :]
        cw = cws[sl, :]
        r = w0 + lax.broadcasted_iota(jnp.int32, cw.shape, 0)
        lane = lax.broadcasted_iota(jnp.int32, cw.shape, 1)
        piece_lane = jnp.where(lane < N_PIECES * PIECE_STRIDE, lane & (PIECE_STRIDE - 1), -1)
        cw = jnp.where(r >= lo, cw, 0.0)
        cw = jnp.where(r < hi, cw, 0.0)
        acts = []
        for e in range(EXPERTS_PER_GROUP):
            ce = jnp.sum(jnp.where(piece_lane == g * EXPERTS_PER_GROUP + e, cw, 0.0), axis=1, keepdims=True)
            ge = g * EXPERTS_PER_GROUP + e
            a = _silu_tanh(jnp.dot(xw, wg_ref[ge], preferred_element_type=F32)) \
                * jnp.dot(xw, wu_ref[ge], preferred_element_type=F32)
            acts.append((a * ce).astype(BF16))
        ys[sl, :] += jnp.dot(jnp.concatenate(acts, axis=1), wd, preferred_element_type=F32)

    aligned = start & -ROW_ALIGN
    need = end - aligned
    biggest = MOE_WINDOWS[-1]
    n_full = jnp.where(need > biggest, (need - (MOE_WINDOWS[0] + 1)) >> (MOE_FULL.bit_length() - 1), 0)

    def full(k, carry):
        w0 = aligned + k * MOE_FULL
        window(w0, MOE_FULL, jnp.maximum(start, w0), w0 + MOE_FULL)
        return carry

    lax.fori_loop(0, n_full, full, 0)
    rest0 = aligned + n_full * MOE_FULL
    rest = end - rest0
    for smaller, m in zip((0,) + MOE_WINDOWS[:-1], MOE_WINDOWS):
        @pl.when((end > start) & (rest > smaller) & (rest <= m))
        def _(m=m):
            window(jnp.minimum(rest0, t - m), m, jnp.maximum(start, rest0), end)

    @pl.when(g == N_EXPERT_GROUPS - 1)
    def _():
        dest_col = rt_ref[:, DEST_LANE:DEST_LANE + 1]
        unperm = jnp.where(lax.broadcasted_iota(jnp.int32, (t, t), 1).astype(F32) == dest_col,
                           1.0, 0.0).astype(BF16)
        ff = jnp.dot(unperm, ys[...].astype(BF16), preferred_element_type=F32)
        xn = x_ref[...] + mod_ref[0, 5:6, :] * ff
        if final_norm:
            ms = jnp.mean(xn * xn, axis=-1, keepdims=True)
            xn = xn * lax.rsqrt(ms + EPS) * ng_ref[...]
        else:
            hn_ref[...] = _modulate(xn, ng_ref[...], nmod_ref[0, 0:1, :], nmod_ref[0, 1:2, :]).astype(BF16)
        o_ref[...] = xn


def _moe_call(h2, table, aux, wg, wu, wd, layer, x, mod, norm_g, next_mod, tokens_per_batch, final_norm):
    n, d = x.shape
    tile = MOE_TILE
    per_b = tokens_per_batch // tile
    seg = aux[:, 1, :2 * N_EXPERT_GROUPS].astype(jnp.int32)
    row = lambda i, g, seg: (i, 0)
    grp = lambda i, g, seg: (layer, g, 0, 0)
    tok = pl.BlockSpec((tile, d), row)
    x_out = jax.ShapeDtypeStruct((n, d), F32)
    grid_spec = pltpu.PrefetchScalarGridSpec(
        num_scalar_prefetch=1,
        grid=(n // tile, N_EXPERT_GROUPS),
        in_specs=[
            pl.BlockSpec((tile, d), row),
            pl.BlockSpec((tile, LANES), row),
            pl.BlockSpec((1, AUX_ROWS, tile), lambda i, g, seg: (i, 0, 0)),
            pl.BlockSpec((None, N_EXPERTS, d, D_EXPERT), lambda i, g, seg: (layer, 0, 0, 0)),
            pl.BlockSpec((None, N_EXPERTS, d, D_EXPERT), lambda i, g, seg: (layer, 0, 0, 0)),
            pl.BlockSpec((None, EXPERTS_PER_GROUP, D_EXPERT, d), grp),
            pl.BlockSpec((tile, d), row),
            pl.BlockSpec((1, N_MOD, d), lambda i, g, seg: (i // per_b, 0, 0)),
            pl.BlockSpec((1, d), lambda i, g, seg: (0, 0)),
            pl.BlockSpec((1, N_MOD, d), lambda i, g, seg: (i // per_b, 0, 0)),
        ],
        out_specs=tok if final_norm else [tok, tok],
        scratch_shapes=[
            pltpu.VMEM((tile, d), BF16),
            pltpu.VMEM((tile, LANES), F32),
            pltpu.VMEM((tile, d), F32),
        ],
    )
    return pl.pallas_call(
        functools.partial(_moe_kernel, final_norm=final_norm),
        grid_spec=grid_spec,
        out_shape=x_out if final_norm else [x_out, jax.ShapeDtypeStruct((n, d), BF16)],
        compiler_params=_cparams(("parallel", "arbitrary"), MOE_VMEM_LIMIT),
        name="moe_ffn",
    )(seg, h2, table, aux, wg, wu, wd, x, mod, norm_g.reshape(1, d), next_mod)


def kernel(x, c, ctx, c_ctx, w_mod, b_mod, norm1_g, norm2_g, hg_w_in, hg_lb_fwd, hg_lb_bwd, hg_gnorm,
           hg_w_out, pool_w_in, pool_w_grp, pool_scale, pool_w_out, router_w, router_b, moe_w_gate,
           moe_w_up, moe_w_down, final_g):
    b, l, d = x.shape
    depth = w_mod.shape[0]
    n_mixers = 2

    cc = jnp.concatenate([c, c_ctx[None, :], jnp.zeros((MOD_ROWS - b - 1, d), F32)], axis=0)
    mods = _mod_call(cc, w_mod, b_mod)

    experts = None
    x_lat = x.reshape(b * l, d)
    h_lat = None
    for i in range(depth):
        slot = i // n_mixers
        mod_lat = mods[i, :b].reshape(b, N_MOD, d)
        if i % n_mixers == 0:
            h_lat, g_lat = _norm_call(x_lat.reshape(b, l, d), norm1_g[i], mod_lat, 1024,
                                      hg_w_in, slot, N_HG_PROJ - 1)
            mod_ctx = jnp.broadcast_to(mods[i, b].reshape(1, N_MOD, d), (b, N_MOD, d))
            h_ctx = _norm_call(ctx, norm1_g[i], mod_ctx, ctx.shape[1])
            to_cast = (moe_w_gate, moe_w_up, moe_w_down) if experts is None else ()
            y, cast = _hgrn_call(h_ctx, h_lat, g_lat, hg_w_in, hg_lb_fwd, hg_lb_bwd, hg_gnorm[slot], slot, to_cast)
            experts = experts or tuple(cast)
            w_out = hg_w_out
        else:
            if h_lat is None:
                h_lat = _norm_call(x_lat.reshape(b, l, d), norm1_g[i], mod_lat, 512)
            y = _pool_call(h_lat, pool_w_in, pool_w_grp, pool_scale, slot)
            w_out = pool_w_out
        x_lat, h2, table, aux = _post_call(x_lat, y.reshape(b * l, d), w_out, slot, mod_lat,
                                           norm2_g[i], router_w, router_b, l)
        if experts is None:
            experts = (moe_w_gate.astype(BF16), moe_w_up.astype(BF16), moe_w_down.astype(BF16))
        if i == depth - 1:
            x_lat = _moe_call(h2, table, aux, *experts, i, x_lat, mod_lat, final_g, mod_lat, l, final_norm=True)
        else:
            mod_next = mods[i + 1, :b].reshape(b, N_MOD, d)
            x_lat, h_next = _moe_call(h2, table, aux, *experts, i, x_lat, mod_lat, norm1_g[i + 1], mod_next, l,
                                      final_norm=False)
            h_lat = h_next.reshape(b, l, d) if (i + 1) % n_mixers != 0 else None
    return x_lat.reshape(b, l, d)
```

```python
import functools

import jax
import jax.numpy as jnp
from jax import lax
from jax.experimental import pallas as pl
from jax.experimental.pallas import tpu as pltpu

F32 = jnp.float32
BF16 = jnp.bfloat16

EPS = 1e-6
N_MOD = 6
HG_HEADS = 8
HG_DK = 128
HG_CHUNK = 64
N_HG_PROJ = 5
POOL_WINDOWS = (2, 4, 8, 16)
GRID_W = 64
GRID_SHIFT = GRID_W.bit_length() - 1
assert 1 << GRID_SHIFT == GRID_W
N_EXPERTS = 16
N_EXPERT_GROUPS = 4
EXPERTS_PER_GROUP = N_EXPERTS // N_EXPERT_GROUPS
D_EXPERT = 256

LANES = 128
MOD_ROWS = 8
VMEM_LIMIT = 56 * 1024 * 1024
HGRN_VMEM_LIMIT = 60 * 1024 * 1024
MOE_VMEM_LIMIT = 60 * 1024 * 1024


def _cparams(sem, vmem=VMEM_LIMIT):
    return pltpu.CompilerParams(dimension_semantics=sem, vmem_limit_bytes=vmem)


def _sigmoid_pair(z):
    e = jnp.exp(-jnp.abs(z))
    r = 1.0 / (1.0 + e)
    er = e * r
    pos = z >= 0
    return jnp.where(pos, r, er), jnp.where(pos, er, r)


def _silu(z):
    return z * _sigmoid_pair(z)[0]


def _sigmoid_tanh(z):
    return 0.5 * jnp.tanh(0.5 * z) + 0.5


def _silu_tanh(z):
    return z * _sigmoid_tanh(z)


def _split_bf16(x):
    hi = x.astype(BF16)
    return hi, (x - hi.astype(F32)).astype(BF16)


def _mod_kernel(c_ref, w_ref, b_ref, o_ref):
    a = jnp.concatenate(_split_bf16(_silu(c_ref[...])), axis=0)
    w_hi, w_lo = _split_bf16(w_ref[0])
    acc = jnp.dot(a, w_hi, preferred_element_type=F32) + jnp.dot(a, w_lo, preferred_element_type=F32)
    o_ref[0] = acc[0:MOD_ROWS] + acc[MOD_ROWS:2 * MOD_ROWS] + b_ref[0]


def _mod_call(cc, w_mod, b_mod):
    depth, d, n = w_mod.shape
    tn = 1024
    return pl.pallas_call(
        _mod_kernel,
        grid=(depth, n // tn),
        in_specs=[
            pl.BlockSpec((MOD_ROWS, d), lambda i, j: (0, 0)),
            pl.BlockSpec((1, d, tn), lambda i, j: (i, 0, j)),
            pl.BlockSpec((1, 1, tn), lambda i, j: (i, 0, j)),
        ],
        out_specs=pl.BlockSpec((1, MOD_ROWS, tn), lambda i, j: (i, 0, j)),
        out_shape=jax.ShapeDtypeStruct((depth, MOD_ROWS, n), F32),
        compiler_params=_cparams(("parallel", "parallel")),
        name="mod_proj",
    )(cc, w_mod, b_mod.reshape(depth, 1, n))


def _modulate(x, g, shift, scale):
    ms = jnp.mean(x * x, axis=-1, keepdims=True)
    return (x * lax.rsqrt(ms + EPS) * g) * (1.0 + scale) + shift


def _norm_kernel(x_ref, g_ref, mod_ref, *rest):
    h = _modulate(x_ref[0], g_ref[...], mod_ref[0, 0:1, :], mod_ref[0, 1:2, :]).astype(BF16)
    if len(rest) == 1:
        (o_ref,) = rest
    else:
        w_ref, o_ref, p_ref, wbuf = rest

        @pl.when((pl.program_id(0) == 0) & (pl.program_id(1) == 0))
        def _():
            wbuf[...] = w_ref[...].astype(BF16)

        p_ref[0] = jnp.dot(h, wbuf[...], preferred_element_type=F32).astype(BF16)
    o_ref[0] = h


def _norm_call(x, g, mod, tile, w_proj=None, w_slot=0, w_col=0):
    b, t, d = x.shape
    tok = pl.BlockSpec((1, tile, d), lambda i, j: (i, j, 0))
    in_specs = [tok, pl.BlockSpec((1, d), lambda i, j: (0, 0)),
                pl.BlockSpec((1, N_MOD, d), lambda i, j: (i, 0, 0))]
    args = [x, g.reshape(1, d), mod]
    out_specs, out_shape = tok, jax.ShapeDtypeStruct((b, t, d), BF16)
    scratch = []
    if w_proj is not None:
        in_specs.append(pl.BlockSpec((None, d, d), lambda i, j: (w_slot, 0, w_col)))
        args.append(w_proj)
        out_specs, out_shape = [tok, tok], [out_shape, out_shape]
        scratch = [pltpu.VMEM((d, d), BF16)]
    return pl.pallas_call(
        _norm_kernel,
        grid=(b, t // tile),
        in_specs=in_specs,
        out_specs=out_specs,
        out_shape=out_shape,
        scratch_shapes=scratch,
        compiler_params=_cparams(("arbitrary", "arbitrary")),
        name="norm1",
    )(*args)


HG_BLOCK = 256
HG_CPB = HG_BLOCK // HG_CHUNK


def _chunk_prefix(x, row):
    for d in (1, 2, 4, 8, 16, 32):
        x = x + jnp.where(row >= d, pltpu.roll(x, d, axis=0), 0.0)
    return x


def _chunk_suffix(x, row):
    n = x.shape[0]
    for d in (1, 2, 4, 8, 16, 32):
        x = x + jnp.where(row < HG_CHUNK - d, pltpu.roll(x, n - d, axis=0), 0.0)
    return x


def _lower_bound(lb_ref, slot):
    rows = [lb_ref[j, 0] for j in range(lb_ref.shape[0])]
    m = functools.reduce(jnp.maximum, rows)
    es = [jnp.exp(r - m) for r in rows]
    return sum(es[:slot + 1]) / sum(es)


N_HGRN_IN = 10


def _hgrn_kernel(*refs, slot, n_ctx_chunks, n_lat_chunks, n_cast):
    hc_ref, hl_ref, g_ref, wq_ref, wv_ref, wf_ref, wb_ref, lbf_ref, lbb_ref, gn_ref = refs[:N_HGRN_IN]
    cast_in = refs[N_HGRN_IN:N_HGRN_IN + n_cast]
    y_ref = refs[N_HGRN_IN + n_cast]
    cast_out = refs[N_HGRN_IN + n_cast + 1:N_HGRN_IN + 2 * n_cast + 1]
    (wbuf, pbuf0, pbuf1, pbuf2, pbuf3, oacc, qif, qib,
     kvf, kvb, decf, decb, spf, spb) = refs[N_HGRN_IN + 2 * n_cast + 1:]
    for c_ref, o_ref in zip(cast_in, cast_out):
        o_ref[...] = c_ref[...].astype(BF16)

    lb_f = _lower_bound(lbf_ref, slot)
    lb_b = _lower_bound(lbb_ref, slot)
    row = lax.broadcasted_iota(jnp.int32, (HG_BLOCK, HG_DK), 0) & (HG_CHUNK - 1)
    ci = lax.broadcasted_iota(jnp.int32, (HG_CPB, HG_CHUNK, HG_CHUNK), 1)
    si = lax.broadcasted_iota(jnp.int32, (HG_CPB, HG_CHUNK, HG_CHUNK), 2)

    def c3(t):
        return t.reshape(HG_CPB, HG_CHUNK, HG_DK)

    def direction(qs3, v3b, z, lb, fwd):
        sig = _sigmoid_tanh(z)
        f = lb + (1.0 - lb) * sig
        k3 = c3((1.0 - lb) * (1.0 - sig))
        lf = jnp.log(f)
        if fwd:
            cum = c3(_chunk_prefix(lf, row))
            ref = cum[:, HG_CHUNK // 2 - 1:HG_CHUNK // 2, :]
            last = cum[:, HG_CHUNK - 1:HG_CHUNK, :]
        else:
            cum = c3(_chunk_suffix(lf, row))
            ref = cum[:, HG_CHUNK // 2:HG_CHUNK // 2 + 1, :]
            last = cum[:, 0:1, :]
        dec = jnp.exp(last)
        if qs3 is None:
            kl = k3 * jnp.exp(last - cum)
        else:
            e1 = jnp.exp(cum - ref)
            qd = qs3 * e1
            qi = qd * jnp.exp(ref)
            kd = k3 * (1.0 / e1)
            kl = kd * jnp.exp(last - ref)
        kvt = jnp.einsum('ncv,nck->nvk', v3b, kl.astype(BF16), preferred_element_type=F32)
        if qs3 is None:
            return None, None, kvt, dec
        sc = jnp.einsum('nck,nsk->ncs', qd.astype(BF16), kd.astype(BF16), preferred_element_type=F32)
        sc = jnp.where((ci >= si) if fwd else (ci <= si), sc, 0.0)
        intra = jnp.einsum('ncs,nsv->ncv', sc.astype(BF16), v3b, preferred_element_type=F32)
        return intra, qi, kvt, dec

    for p, wp_ref in enumerate((wq_ref, wv_ref, wf_ref, wb_ref)):
        wbuf[:, p * HG_DK:(p + 1) * HG_DK] = wp_ref[...].astype(BF16)

    def project(hrows):
        return jnp.dot(hrows, wbuf[...], preferred_element_type=F32)

    def block(p, chunk0, lat_row0, between=None):
        v3b = c3(p[:, HG_DK:2 * HG_DK]).astype(BF16)
        zf = p[:, 2 * HG_DK:3 * HG_DK]
        zb = p[:, 3 * HG_DK:4 * HG_DK]
        if lat_row0 is None:
            qs3 = None
        else:
            qs3 = c3(_silu_tanh(p[:, 0:HG_DK]))
        in_f, qi_f, kv_f, dec_f = direction(qs3, v3b, zf, lb_f, True)
        if between is not None:
            between()
        in_b, qi_b, kv_b, dec_b = direction(qs3, v3b, zb, lb_b, False)
        kvf[pl.ds(chunk0, HG_CPB)] = kv_f
        kvb[pl.ds(chunk0, HG_CPB)] = kv_b
        decf[pl.ds(chunk0, HG_CPB)] = dec_f
        decb[pl.ds(chunk0, HG_CPB)] = dec_b
        if lat_row0 is not None:
            rows = pl.ds(lat_row0, HG_BLOCK)
            oacc[rows, :] = (in_f + in_b).reshape(HG_BLOCK, HG_DK)
            qif[rows, :] = qi_f.reshape(HG_BLOCK, HG_DK).astype(BF16)
            qib[rows, :] = qi_b.reshape(HG_BLOCK, HG_DK).astype(BF16)

    n_blocks = n_lat_chunks // HG_CPB

    def lat_rows(i):
        return hl_ref[0, pl.ds(pl.multiple_of(i * HG_BLOCK, HG_BLOCK), HG_BLOCK), :]

    def lat_terms(p_ref, i, between=None):
        block(p_ref[...], n_ctx_chunks + i * HG_CPB, pl.multiple_of(i * HG_BLOCK, HG_BLOCK), between)

    half_n = 2 * HG_DK

    def project_half(dst, i, k):
        c = slice(k * half_n, (k + 1) * half_n)
        dst[:, c] = jnp.dot(lat_rows(i), wbuf[:, c], preferred_element_type=F32)

    ctx_p = [project(hc_ref[0, i * HG_BLOCK:(i + 1) * HG_BLOCK, :]) for i in range(n_ctx_chunks // HG_CPB)]
    pbuf0[...] = project(lat_rows(0))
    for i, p in enumerate(ctx_p):
        block(p, i * HG_CPB, None)
    pbuf1[...] = project(lat_rows(1))

    def lat_quad(i0, last):
        project_half(pbuf2, i0 + 2, 0)
        lat_terms(pbuf0, i0, lambda: project_half(pbuf2, i0 + 2, 1))
        project_half(pbuf3, i0 + 3, 0)
        lat_terms(pbuf1, i0 + 1, lambda: project_half(pbuf3, i0 + 3, 1))
        if not last:
            project_half(pbuf0, i0 + 4, 0)
        lat_terms(pbuf2, i0 + 2, None if last else (lambda: project_half(pbuf0, i0 + 4, 1)))
        if not last:
            project_half(pbuf1, i0 + 5, 0)
        lat_terms(pbuf3, i0 + 3, None if last else (lambda: project_half(pbuf1, i0 + 5, 1)))

    def lat_body(t, carry):
        lat_quad(4 * t, False)
        return carry

    lax.fori_loop(0, n_blocks // 4 - 1, lat_body, 0)
    lat_quad(n_blocks - 4, True)

    def advance(s, kv_ref, dec_ref, n):
        return dec_ref[n] * s + kv_ref[n]

    sf = jnp.zeros((HG_DK, HG_DK), F32)
    for n in range(n_ctx_chunks):
        sf = advance(sf, kvf, decf, n)
    sb = jnp.zeros((HG_DK, HG_DK), F32)
    for n in reversed(range(n_ctx_chunks)):
        sb = advance(sb, kvb, decb, n)

    def scan_step(t, carry):
        sf, sb = carry
        jb = n_lat_chunks - 1 - t
        spf[t] = sf.astype(BF16)
        spb[jb] = sb.astype(BF16)
        return (advance(sf, kvf, decf, n_ctx_chunks + t), advance(sb, kvb, decb, n_ctx_chunks + jb))

    lax.fori_loop(0, n_lat_chunks, scan_step, (sf, sb), unroll=2)

    gn = gn_ref[...]

    def block_rows(i):
        return pl.ds(pl.multiple_of(i * HG_BLOCK, HG_BLOCK), HG_BLOCK)

    def inter(i):
        rows = block_rows(i)
        chunks = pl.ds(i * HG_CPB, HG_CPB)
        return (jnp.einsum('nck,nvk->ncv', c3(qif[rows, :]), spf[chunks], preferred_element_type=F32)
                + jnp.einsum('nck,nvk->ncv', c3(qib[rows, :]), spb[chunks], preferred_element_type=F32)
                ).reshape(HG_BLOCK, HG_DK)

    def readout(o_ref, i):
        rows = block_rows(i)
        o = oacc[rows, :] + o_ref[...]
        ms = jnp.mean(o * o, axis=-1, keepdims=True)
        o = o * lax.rsqrt(ms + EPS) * gn
        y_ref[0, rows, :] = (o * _silu_tanh(g_ref[0, rows, :].astype(F32))).astype(BF16)

    obuf0, obuf1 = pbuf0.at[:, 0:HG_DK], pbuf1.at[:, 0:HG_DK]
    obuf0[...] = inter(0)

    def readout_pair(t, carry):
        obuf1[...] = inter(2 * t + 1)
        readout(obuf0, 2 * t)
        obuf0[...] = inter(2 * t + 2)
        readout(obuf1, 2 * t + 1)
        return carry

    lax.fori_loop(0, n_blocks // 2 - 1, readout_pair, 0)
    obuf1[...] = inter(n_blocks - 1)
    readout(obuf0, n_blocks - 2)
    readout(obuf1, n_blocks - 1)


def _hgrn_call(h_ctx, h_lat, g_lat, w_in, lb_fwd, lb_bwd, gnorm, slot, to_cast):
    b, lc, d = h_ctx.shape
    ll = h_lat.shape[1]
    nrow = lb_fwd.shape[0]
    ncc, nlc = lc // HG_CHUNK, ll // HG_CHUNK
    kern = functools.partial(_hgrn_kernel, slot=slot, n_ctx_chunks=ncc, n_lat_chunks=nlc, n_cast=len(to_cast))
    lb_spec = pl.BlockSpec((nrow, 1, 1, HG_DK), lambda i, h: (0, h, 0, 0))
    head_cols = pl.BlockSpec((1, ll, HG_DK), lambda i, h: (i, 0, h))
    n_rec = N_HG_PROJ - 1
    w_cols = [pl.BlockSpec((None, d, HG_DK), functools.partial(lambda i, h, p: (slot, 0, p * HG_HEADS + h), p=p))
              for p in range(n_rec)]
    steps = b * HG_HEADS
    slabs = [a.reshape(steps, -1, *a.shape[-2:]) for a in to_cast]
    slab_specs = [pl.BlockSpec((1,) + a.shape[1:], lambda i, h: (i * HG_HEADS + h, 0, 0, 0)) for a in slabs]
    outs = pl.pallas_call(
        kern,
        grid=(b, HG_HEADS),
        in_specs=[
            pl.BlockSpec((1, lc, d), lambda i, h: (i, 0, 0)),
            pl.BlockSpec((1, ll, d), lambda i, h: (i, 0, 0)),
            head_cols,
            *w_cols,
            lb_spec, lb_spec,
            pl.BlockSpec((1, HG_DK), lambda i, h: (0, 0)),
            *slab_specs,
        ],
        out_specs=[head_cols, *slab_specs],
        out_shape=[jax.ShapeDtypeStruct((b, ll, d), BF16)]
        + [jax.ShapeDtypeStruct(a.shape, BF16) for a in slabs],
        scratch_shapes=[
            pltpu.VMEM((d, n_rec * HG_DK), BF16),
            pltpu.VMEM((HG_BLOCK, n_rec * HG_DK), F32),
            pltpu.VMEM((HG_BLOCK, n_rec * HG_DK), F32),
            pltpu.VMEM((HG_BLOCK, n_rec * HG_DK), F32),
            pltpu.VMEM((HG_BLOCK, n_rec * HG_DK), F32),
            pltpu.VMEM((ll, HG_DK), F32),
            pltpu.VMEM((ll, HG_DK), BF16),
            pltpu.VMEM((ll, HG_DK), BF16),
            pltpu.VMEM((ncc + nlc, HG_DK, HG_DK), F32),
            pltpu.VMEM((ncc + nlc, HG_DK, HG_DK), F32),
            pltpu.VMEM((ncc + nlc, 1, HG_DK), F32),
            pltpu.VMEM((ncc + nlc, 1, HG_DK), F32),
            pltpu.VMEM((nlc, HG_DK, HG_DK), BF16),
            pltpu.VMEM((nlc, HG_DK, HG_DK), BF16),
        ],
        compiler_params=_cparams(("parallel", "arbitrary"), HGRN_VMEM_LIMIT),
        name="hgrn2",
    )(h_ctx, h_lat, g_lat, *([w_in] * n_rec),
      lb_fwd.reshape(nrow, HG_HEADS, 1, HG_DK), lb_bwd.reshape(nrow, HG_HEADS, 1, HG_DK),
      gnorm.reshape(1, HG_DK), *slabs)
    return outs[0], [o.reshape(a.shape) for o, a in zip(outs[1:], to_cast)]


POOL_STRIP = 8
POOL_PAD = 8
POOL_BAND = 256


def _pool_group(h_ref, win_ref, wg_ref, ps_ref, o_ref, upad, *, win, rows):
    gd = win_ref.shape[1]
    half = win // 2
    zeros = jnp.zeros((POOL_PAD, GRID_W, gd), F32)
    upad[0:POOL_PAD] = zeros
    upad[POOL_PAD + rows:POOL_PAD + rows + POOL_PAD] = zeros
    w_in = win_ref[...].astype(BF16)
    tok = POOL_STRIP * GRID_W
    n_strips = rows // POOL_STRIP

    def project(c):
        u = jnp.dot(h_ref[0, c * tok:(c + 1) * tok, :], w_in, preferred_element_type=F32)
        upad[POOL_PAD + c * POOL_STRIP:POOL_PAD + (c + 1) * POOL_STRIP] = u.reshape(POOL_STRIP, GRID_W, gd)

    bi = lax.broadcasted_iota(jnp.int32, (POOL_BAND, POOL_BAND), 0)
    bj = lax.broadcasted_iota(jnp.int32, (POOL_BAND, POOL_BAND), 1)
    lo_c = (bi & (GRID_W - 1)) - half
    cj = bj & (GRID_W - 1)
    band = jnp.where(bi >> GRID_SHIFT == bj >> GRID_SHIFT, 1.0, 0.0)
    band = jnp.where(cj >= lo_c, band, 0.0)
    band = jnp.where(cj < lo_c + win, band, 0.0).astype(BF16)
    t = lax.broadcasted_iota(jnp.int32, (tok, LANES), 0)
    col = t & (GRID_W - 1)
    cnt_c = jnp.minimum(col - half + win, GRID_W) - jnp.maximum(col - half, 0)
    wg = wg_ref[0].astype(BF16)
    ps = ps_ref[...]

    def strip(i):
        r0 = i * POOL_STRIP
        slab = upad[r0 + POOL_PAD - half:r0 + POOL_PAD - half + POOL_STRIP + win - 1]
        span = 1
        while span < win:
            n = slab.shape[0] - span
            slab = slab[0:n] + slab[span:span + n]
            span *= 2
        rs = slab.reshape(tok, gd)
        hi = rs.astype(BF16)
        lo = (rs - hi.astype(F32)).astype(BF16)
        parts = []
        for k in range(tok // POOL_BAND):
            sl = slice(k * POOL_BAND, (k + 1) * POOL_BAND)
            parts.append(jnp.dot(band, hi[sl], preferred_element_type=F32)
                         + jnp.dot(band, lo[sl], preferred_element_type=F32))
        box = jnp.concatenate(parts, axis=0)
        r = r0 + (t >> GRID_SHIFT)
        cnt_r = jnp.minimum(r - half + win, rows) - jnp.maximum(r - half, 0)
        inv = 1.0 / (cnt_r * cnt_c).astype(F32)
        mean = box * jnp.concatenate([inv] * (gd // LANES), axis=1)
        ug = upad[r0 + POOL_PAD:r0 + POOL_PAD + POOL_STRIP].reshape(tok, gd)
        z = jnp.dot((mean - ug).astype(BF16), wg, preferred_element_type=F32) * ps
        o_ref[0, i * tok:(i + 1) * tok, :] = z.astype(BF16)

    project(0)
    project(1)
    for i in range(n_strips):
        if i + 2 < n_strips:
            project(i + 2)
        strip(i)


def _pool_kernel(h_ref, win_ref, wg_ref, ps_ref, o_ref, upad, *, rows):
    g = pl.program_id(1)
    for gi, win in enumerate(POOL_WINDOWS):
        @pl.when(g == gi)
        def _(win=win):
            _pool_group(h_ref, win_ref, wg_ref, ps_ref, o_ref, upad, win=win, rows=rows)


def _pool_call(h, w_in, w_grp, p_scale, slot):
    b, l, d = h.shape
    ng = len(POOL_WINDOWS)
    gd = d // ng
    rows = l // GRID_W
    return pl.pallas_call(
        functools.partial(_pool_kernel, rows=rows),
        grid=(b, ng),
        in_specs=[
            pl.BlockSpec((1, l, d), lambda i, g: (i, 0, 0)),
            pl.BlockSpec((None, d, gd), lambda i, g: (slot, 0, g)),
            pl.BlockSpec((None, 1, gd, gd), lambda i, g: (slot, g, 0, 0)),
            pl.BlockSpec((None, 1, gd), lambda i, g: (slot, 0, g)),
        ],
        out_specs=pl.BlockSpec((1, l, gd), lambda i, g: (i, 0, g)),
        out_shape=jax.ShapeDtypeStruct((b, l, d), BF16),
        scratch_shapes=[pltpu.VMEM((rows + 2 * POOL_PAD, GRID_W, gd), F32)],
        compiler_params=_cparams(("parallel", "arbitrary")),
        name="pool_mix",
    )(h, w_in, w_grp, p_scale.reshape(p_scale.shape[0], 1, d))


MOE_TILE = 1024
MOE_WINDOWS = (128, 160, 192, 224, 256, 288, 320, 352, 384)
MOE_FULL = 256
ROW_ALIGN = 16
DEST_LANE = N_EXPERTS
PIECE_STRIDE = 32
N_PIECES = 3
AUX_ROWS = 8


def _route(sel, s):
    keep = []
    gsum = []
    for g in range(N_EXPERT_GROUPS):
        a = sel[g * EXPERTS_PER_GROUP:(g + 1) * EXPERTS_PER_GROUP]
        beaten = [jnp.zeros_like(a[0]) for _ in a]
        for i in range(EXPERTS_PER_GROUP):
            for j in range(i + 1, EXPERTS_PER_GROUP):
                ge = jnp.where(a[i] >= a[j], 1.0, 0.0)
                beaten[j] = beaten[j] + ge
                beaten[i] = beaten[i] + (1.0 - ge)
        kg = [jnp.where(bt < 1.5, 1.0, 0.0) for bt in beaten]
        keep.append(kg)
        gsum.append(sum(k * x for k, x in zip(kg, a)))
    picked = []
    bests = []
    for g in range(N_EXPERT_GROUPS):
        better = jnp.zeros_like(gsum[0])
        for o in range(N_EXPERT_GROUPS):
            if o < g:
                better = better + jnp.where(gsum[o] >= gsum[g], 1.0, 0.0)
            elif o > g:
                better = better + jnp.where(gsum[o] > gsum[g], 1.0, 0.0)
        best = jnp.where(better < 0.5, 1.0, 0.0)
        bests.append(best)
        for i in range(EXPERTS_PER_GROUP):
            picked.append(best * keep[g][i] * s[g * EXPERTS_PER_GROUP + i])
    den = sum(picked)
    return [p / den for p in picked], bests


def _post_kernel(x_ref, y_ref, w_ref, mod_ref, g_ref, rw_ref, rb_ref,
                 xo_ref, h2_ref, rt_ref, aux_ref, wbuf, prod0, prod1, *, n_tiles):
    s = pl.program_id(0)

    def tile_part(prod_ref):
        yw = jnp.dot(y_ref[...], wbuf[...], preferred_element_type=F32)
        xn = x_ref[...] + mod_ref[0, 2:3, :] * yw
        xo_ref[...] = xn
        h2 = _modulate(xn, g_ref[...], mod_ref[0, 3:4, :], mod_ref[0, 4:5, :])
        hi, lo = _split_bf16(h2)
        h2_ref[...] = hi
        rw = rw_ref[...]
        half = hi.shape[0] // 2
        for r in (0, half):
            prod_ref[r:r + half, :] = (jnp.dot(hi[r:r + half], rw, preferred_element_type=F32)
                                       + jnp.dot(lo[r:r + half], rw, preferred_element_type=F32))

    route = functools.partial(_post_route, rb_ref, rt_ref, aux_ref)

    @pl.when(s == 0)
    def _():
        wbuf[...] = w_ref[...].astype(BF16)
        tile_part(prod0)

    for parity, (cur, prev) in enumerate(((prod0, prod1), (prod1, prod0))):
        @pl.when((s > 0) & (s < n_tiles) & (s % 2 == parity))
        def _(cur=cur, prev=prev):
            tile_part(cur)
            route(prev)

    @pl.when(s == n_tiles)
    def _():
        route((prod0, prod1)[(n_tiles - 1) % 2])


def _post_route(rb_ref, rt_ref, aux_ref, prod_ref):
    prod = prod_ref[...].T
    logits = prod[0:N_EXPERTS, :] + prod[N_EXPERTS:2 * N_EXPERTS, :]
    s = _sigmoid_pair(logits)[0]
    sel = s + rb_ref[...]
    comb, bests = _route([sel[e:e + 1, :] for e in range(N_EXPERTS)],
                         [s[e:e + 1, :] for e in range(N_EXPERTS)])
    t = logits.shape[1]

    ind = jnp.concatenate(bests + [jnp.zeros((AUX_ROWS - N_EXPERT_GROUPS, t), F32)], axis=0)
    n_seg = t // LANES
    stacked = jnp.concatenate([ind[:, j * LANES:(j + 1) * LANES] for j in range(n_seg)], axis=0)
    upper = jnp.where(lax.broadcasted_iota(jnp.int32, (LANES, LANES), 0)
                      <= lax.broadcasted_iota(jnp.int32, (LANES, LANES), 1), 1.0, 0.0).astype(BF16)
    local = jnp.dot(stacked.astype(BF16), upper, preferred_element_type=F32)
    off = jnp.zeros((AUX_ROWS, 1), F32)
    pieces = []
    for j in range(n_seg):
        seg_cum = local[j * AUX_ROWS:(j + 1) * AUX_ROWS, :]
        pieces.append(seg_cum + off)
        off = off + seg_cum[:, LANES - 1:LANES]
    cum = jnp.concatenate(pieces, axis=1)
    counts = [off[g:g + 1, :] for g in range(N_EXPERT_GROUPS)]
    starts = [jnp.zeros((1, 1), F32)]
    for g in range(1, N_EXPERT_GROUPS):
        starts.append(starts[-1] + counts[g - 1])
    dest = sum(bests[g] * (starts[g] + cum[g:g + 1, :] - 1.0) for g in range(N_EXPERT_GROUPS))

    lane = lax.broadcasted_iota(jnp.int32, (1, t), 1)
    seg = sum(jnp.where(lane == k, v, 0.0) for k, v in enumerate(starts + counts))
    aux_ref[0] = jnp.concatenate([dest, seg, jnp.zeros((AUX_ROWS - 2, t), F32)], axis=0)
    p1 = [w.astype(BF16).astype(F32) for w in comb]
    r1 = [w - p for w, p in zip(comb, p1)]
    p2 = [r.astype(BF16).astype(F32) for r in r1]
    p3 = [r - p for r, p in zip(r1, p2)]
    pad = [jnp.zeros((PIECE_STRIDE - N_EXPERTS, t), F32)]
    table = jnp.concatenate(p1 + [dest] + [jnp.zeros((PIECE_STRIDE - N_EXPERTS - 1, t), F32)] + p2 + pad + p3 + pad
                            + [jnp.zeros((LANES - N_PIECES * PIECE_STRIDE, t), F32)], axis=0)
    rt_ref[...] = table.T


def _post_call(x, y, w, w_slot, mod, g, router_w, router_b, tokens_per_batch, tile=MOE_TILE):
    n, d = x.shape
    per_b = tokens_per_batch // tile
    rwh = router_w.astype(BF16)
    rwl = (router_w - rwh.astype(F32)).astype(BF16)
    rw = jnp.concatenate([rwh, rwl, jnp.zeros((d, LANES - 2 * N_EXPERTS), BF16)], axis=1)
    n_tiles = n // tile
    row = lambda s: (jnp.minimum(s, n_tiles - 1), 0)
    routed = lambda s: jnp.maximum(s - 1, 0)
    fixed = lambda s: (0, 0)
    return pl.pallas_call(
        functools.partial(_post_kernel, n_tiles=n_tiles),
        grid=(n_tiles + 1,),
        in_specs=[
            pl.BlockSpec((tile, d), row),
            pl.BlockSpec((tile, d), row),
            pl.BlockSpec((None, d, d), lambda s: (w_slot, 0, 0)),
            pl.BlockSpec((1, N_MOD, d), lambda s: (jnp.minimum(s, n_tiles - 1) // per_b, 0, 0)),
            pl.BlockSpec((1, d), fixed),
            pl.BlockSpec((d, LANES), fixed),
            pl.BlockSpec((N_EXPERTS, 1), fixed),
        ],
        out_specs=[
            pl.BlockSpec((tile, d), row),
            pl.BlockSpec((tile, d), row),
            pl.BlockSpec((tile, LANES), lambda s: (routed(s), 0)),
            pl.BlockSpec((1, AUX_ROWS, tile), lambda s: (routed(s), 0, 0)),
        ],
        out_shape=[
            jax.ShapeDtypeStruct((n, d), F32),
            jax.ShapeDtypeStruct((n, d), BF16),
            jax.ShapeDtypeStruct((n, LANES), F32),
            jax.ShapeDtypeStruct((n_tiles, AUX_ROWS, tile), F32),
        ],
        scratch_shapes=[
            pltpu.VMEM((d, d), BF16),
            pltpu.VMEM((tile, LANES), F32),
            pltpu.VMEM((tile, LANES), F32),
        ],
        compiler_params=_cparams(("arbitrary",)),
        name="mixer_out_router",
    )(x, y, w, mod, g.reshape(1, d), rw, router_b.reshape(N_EXPERTS, 1))


def _moe_kernel(seg_ref, h_ref, rt_ref, aux_ref, wg_ref, wu_ref, wd_ref, x_ref, mod_ref, ng_ref, nmod_ref,
                o_ref, *rest, final_norm):
    if final_norm:
        xs, cws, ys = rest
    else:
        hn_ref, xs, cws, ys = rest
    i = pl.program_id(0)
    g = pl.program_id(1)
    t = h_ref.shape[0]

    @pl.when(g == 0)
    def _():
        dest_row = aux_ref[0, 0:1, :]
        perm = jnp.where(lax.broadcasted_iota(jnp.int32, (t, t), 0).astype(F32) == dest_row,
                         1.0, 0.0).astype(BF16)
        xs[...] = jnp.dot(perm, h_ref[...], preferred_element_type=F32).astype(BF16)
        cws[...] = jnp.dot(perm, rt_ref[...].astype(BF16), preferred_element_type=F32)
        ys[...] = jnp.zeros_like(ys)

    start = seg_ref[i, g]
    end = start + seg_ref[i, N_EXPERT_GROUPS + g]
    wd = wd_ref[...].reshape(EXPERTS_PER_GROUP * D_EXPERT, wd_ref.shape[2])

    def window(w0, rows, lo, hi):
        sl = pl.ds(pl.multiple_of(w0, ROW_ALIGN), rows)
        xw = xs[sl, :]
        cw = cws[sl, :]
        r = w0 + lax.broadcasted_iota(jnp.int32, cw.shape, 0)
        lane = lax.broadcasted_iota(jnp.int32, cw.shape, 1)
        piece_lane = jnp.where(lane < N_PIECES * PIECE_STRIDE, lane & (PIECE_STRIDE - 1), -1)
        cw = jnp.where(r >= lo, cw, 0.0)
        cw = jnp.where(r < hi, cw, 0.0)
        acts = []
        for e in range(EXPERTS_PER_GROUP):
            ce = jnp.sum(jnp.where(piece_lane == g * EXPERTS_PER_GROUP + e, cw, 0.0), axis=1, keepdims=True)
            ge = g * EXPERTS_PER_GROUP + e
            a = _silu_tanh(jnp.dot(xw, wg_ref[ge], preferred_element_type=F32)) \
                * jnp.dot(xw, wu_ref[ge], preferred_element_type=F32)
            acts.append((a * ce).astype(BF16))
        ys[sl, :] += jnp.dot(jnp.concatenate(acts, axis=1), wd, preferred_element_type=F32)

    aligned = start & -ROW_ALIGN
    need = end - aligned
    biggest = MOE_WINDOWS[-1]
    n_full = jnp.where(need > biggest, (need - (MOE_WINDOWS[0] + 1)) >> (MOE_FULL.bit_length() - 1), 0)

    def full(k, carry):
        w0 = aligned + k * MOE_FULL
        window(w0, MOE_FULL, jnp.maximum(start, w0), w0 + MOE_FULL)
        return carry

    lax.fori_loop(0, n_full, full, 0)
    rest0 = aligned + n_full * MOE_FULL
    rest = end - rest0
    for smaller, m in zip((0,) + MOE_WINDOWS[:-1], MOE_WINDOWS):
        @pl.when((end > start) & (rest > smaller) & (rest <= m))
        def _(m=m):
            window(jnp.minimum(rest0, t - m), m, jnp.maximum(start, rest0), end)

    @pl.when(g == N_EXPERT_GROUPS - 1)
    def _():
        dest_col = rt_ref[:, DEST_LANE:DEST_LANE + 1]
        unperm = jnp.where(lax.broadcasted_iota(jnp.int32, (t, t), 1).astype(F32) == dest_col,
                           1.0, 0.0).astype(BF16)
        ff = jnp.dot(unperm, ys[...].astype(BF16), preferred_element_type=F32)
        xn = x_ref[...] + mod_ref[0, 5:6, :] * ff
        if final_norm:
            ms = jnp.mean(xn * xn, axis=-1, keepdims=True)
            xn = xn * lax.rsqrt(ms + EPS) * ng_ref[...]
        else:
            hn_ref[...] = _modulate(xn, ng_ref[...], nmod_ref[0, 0:1, :], nmod_ref[0, 1:2, :]).astype(BF16)
        o_ref[...] = xn


def _moe_call(h2, table, aux, wg, wu, wd, layer, x, mod, norm_g, next_mod, tokens_per_batch, final_norm):
    n, d = x.shape
    tile = MOE_TILE
    per_b = tokens_per_batch // tile
    seg = aux[:, 1, :2 * N_EXPERT_GROUPS].astype(jnp.int32)
    row = lambda i, g, seg: (i, 0)
    grp = lambda i, g, seg: (layer, g, 0, 0)
    tok = pl.BlockSpec((tile, d), row)
    x_out = jax.ShapeDtypeStruct((n, d), F32)
    grid_spec = pltpu.PrefetchScalarGridSpec(
        num_scalar_prefetch=1,
        grid=(n // tile, N_EXPERT_GROUPS),
        in_specs=[
            pl.BlockSpec((tile, d), row),
            pl.BlockSpec((tile, LANES), row),
            pl.BlockSpec((1, AUX_ROWS, tile), lambda i, g, seg: (i, 0, 0)),
            pl.BlockSpec((None, N_EXPERTS, d, D_EXPERT), lambda i, g, seg: (layer, 0, 0, 0)),
            pl.BlockSpec((None, N_EXPERTS, d, D_EXPERT), lambda i, g, seg: (layer, 0, 0, 0)),
            pl.BlockSpec((None, EXPERTS_PER_GROUP, D_EXPERT, d), grp),
            pl.BlockSpec((tile, d), row),
            pl.BlockSpec((1, N_MOD, d), lambda i, g, seg: (i // per_b, 0, 0)),
            pl.BlockSpec((1, d), lambda i, g, seg: (0, 0)),
            pl.BlockSpec((1, N_MOD, d), lambda i, g, seg: (i // per_b, 0, 0)),
        ],
        out_specs=tok if final_norm else [tok, tok],
        scratch_shapes=[
            pltpu.VMEM((tile, d), BF16),
            pltpu.VMEM((tile, LANES), F32),
            pltpu.VMEM((tile, d), F32),
        ],
    )
    return pl.pallas_call(
        functools.partial(_moe_kernel, final_norm=final_norm),
        grid_spec=grid_spec,
        out_shape=x_out if final_norm else [x_out, jax.ShapeDtypeStruct((n, d), BF16)],
        compiler_params=_cparams(("parallel", "arbitrary"), MOE_VMEM_LIMIT),
        name="moe_ffn",
    )(seg, h2, table, aux, wg, wu, wd, x, mod, norm_g.reshape(1, d), next_mod)


def kernel(x, c, ctx, c_ctx, w_mod, b_mod, norm1_g, norm2_g, hg_w_in, hg_lb_fwd, hg_lb_bwd, hg_gnorm,
           hg_w_out, pool_w_in, pool_w_grp, pool_scale, pool_w_out, router_w, router_b, moe_w_gate,
           moe_w_up, moe_w_down, final_g):
    b, l, d = x.shape
    depth = w_mod.shape[0]
    n_mixers = 2

    cc = jnp.concatenate([c, c_ctx[None, :], jnp.zeros((MOD_ROWS - b - 1, d), F32)], axis=0)
    mods = _mod_call(cc, w_mod, b_mod)

    experts = None
    x_lat = x.reshape(b * l, d)
    h_lat = None
    for i in range(depth):
        slot = i // n_mixers
        mod_lat = mods[i, :b].reshape(b, N_MOD, d)
        if i % n_mixers == 0:
            h_lat, g_lat = _norm_call(x_lat.reshape(b, l, d), norm1_g[i], mod_lat, 1024,
                                      hg_w_in, slot, N_HG_PROJ - 1)
            mod_ctx = jnp.broadcast_to(mods[i, b].reshape(1, N_MOD, d), (b, N_MOD, d))
            h_ctx = _norm_call(ctx, norm1_g[i], mod_ctx, ctx.shape[1])
            to_cast = (moe_w_gate, moe_w_up, moe_w_down) if experts is None else ()
            y, cast = _hgrn_call(h_ctx, h_lat, g_lat, hg_w_in, hg_lb_fwd, hg_lb_bwd, hg_gnorm[slot], slot, to_cast)
            experts = experts or tuple(cast)
            w_out = hg_w_out
        else:
            if h_lat is None:
                h_lat = _norm_call(x_lat.reshape(b, l, d), norm1_g[i], mod_lat, 512)
            y = _pool_call(h_lat, pool_w_in, pool_w_grp, pool_scale, slot)
            w_out = pool_w_out
        x_lat, h2, table, aux = _post_call(x_lat, y.reshape(b * l, d), w_out, slot, mod_lat,
                                           norm2_g[i], router_w, router_b, l)
        if experts is None:
            experts = (moe_w_gate.astype(BF16), moe_w_up.astype(BF16), moe_w_down.astype(BF16))
        if i == depth - 1:
            x_lat = _moe_call(h2, table, aux, *experts, i, x_lat, mod_lat, final_g, mod_lat, l, final_norm=True)
        else:
            mod_next = mods[i + 1, :b].reshape(b, N_MOD, d)
            x_lat, h_next = _moe_call(h2, table, aux, *experts, i, x_lat, mod_lat, norm1_g[i + 1], mod_next, l,
                                      final_norm=False)
            h_lat = h_next.reshape(b, l, d) if (i + 1) % n_mixers != 0 else None
    return x_lat.reshape(b, l, d)
```

```python
import functools

import jax
import jax.numpy as jnp
from jax import lax
from jax.experimental import pallas as pl
from jax.experimental.pallas import tpu as pltpu

F32 = jnp.float32
BF16 = jnp.bfloat16

EPS = 1e-6
N_MOD = 6
HG_HEADS = 8
HG_DK = 128
HG_CHUNK = 64
N_HG_PROJ = 5
POOL_WINDOWS = (2, 4, 8, 16)
GRID_W = 64
GRID_SHIFT = GRID_W.bit_length() - 1
assert 1 << GRID_SHIFT == GRID_W
N_EXPERTS = 16
N_EXPERT_GROUPS = 4
EXPERTS_PER_GROUP = N_EXPERTS // N_EXPERT_GROUPS
D_EXPERT = 256

LANES = 128
MOD_ROWS = 8
VMEM_LIMIT = 56 * 1024 * 1024
HGRN_VMEM_LIMIT = 60 * 1024 * 1024
MOE_VMEM_LIMIT = 60 * 1024 * 1024


def _cparams(sem, vmem=VMEM_LIMIT):
    return pltpu.CompilerParams(dimension_semantics=sem, vmem_limit_bytes=vmem)


def _sigmoid_pair(z):
    e = jnp.exp(-jnp.abs(z))
    r = 1.0 / (1.0 + e)
    er = e * r
    pos = z >= 0
    return jnp.where(pos, r, er), jnp.where(pos, er, r)


def _silu(z):
    return z * _sigmoid_pair(z)[0]


def _sigmoid_tanh(z):
    return 0.5 * jnp.tanh(0.5 * z) + 0.5


def _silu_tanh(z):
    return z * _sigmoid_tanh(z)


def _split_bf16(x):
    hi = x.astype(BF16)
    return hi, (x - hi.astype(F32)).astype(BF16)


def _mod_kernel(c_ref, w_ref, b_ref, o_ref):
    a = jnp.concatenate(_split_bf16(_silu(c_ref[...])), axis=0)
    w_hi, w_lo = _split_bf16(w_ref[0])
    acc = jnp.dot(a, w_hi, preferred_element_type=F32) + jnp.dot(a, w_lo, preferred_element_type=F32)
    o_ref[0] = acc[0:MOD_ROWS] + acc[MOD_ROWS:2 * MOD_ROWS] + b_ref[0]


def _mod_call(cc, w_mod, b_mod):
    depth, d, n = w_mod.shape
    tn = 1024
    return pl.pallas_call(
        _mod_kernel,
        grid=(depth, n // tn),
        in_specs=[
            pl.BlockSpec((MOD_ROWS, d), lambda i, j: (0, 0)),
            pl.BlockSpec((1, d, tn), lambda i, j: (i, 0, j)),
            pl.BlockSpec((1, 1, tn), lambda i, j: (i, 0, j)),
        ],
        out_specs=pl.BlockSpec((1, MOD_ROWS, tn), lambda i, j: (i, 0, j)),
        out_shape=jax.ShapeDtypeStruct((depth, MOD_ROWS, n), F32),
        compiler_params=_cparams(("parallel", "parallel")),
        name="mod_proj",
    )(cc, w_mod, b_mod.reshape(depth, 1, n))


def _modulate(x, g, shift, scale):
    ms = jnp.mean(x * x, axis=-1, keepdims=True)
    return (x * lax.rsqrt(ms + EPS) * g) * (1.0 + scale) + shift


def _norm_kernel(x_ref, g_ref, mod_ref, *rest):
    h = _modulate(x_ref[0], g_ref[...], mod_ref[0, 0:1, :], mod_ref[0, 1:2, :]).astype(BF16)
    if len(rest) == 1:
        (o_ref,) = rest
    else:
        w_ref, o_ref, p_ref, wbuf = rest

        @pl.when((pl.program_id(0) == 0) & (pl.program_id(1) == 0))
        def _():
            wbuf[...] = w_ref[...].astype(BF16)

        p_ref[0] = jnp.dot(h, wbuf[...], preferred_element_type=F32).astype(BF16)
    o_ref[0] = h


def _norm_call(x, g, mod, tile, w_proj=None, w_slot=0, w_col=0):
    b, t, d = x.shape
    tok = pl.BlockSpec((1, tile, d), lambda i, j: (i, j, 0))
    in_specs = [tok, pl.BlockSpec((1, d), lambda i, j: (0, 0)),
                pl.BlockSpec((1, N_MOD, d), lambda i, j: (i, 0, 0))]
    args = [x, g.reshape(1, d), mod]
    out_specs, out_shape = tok, jax.ShapeDtypeStruct((b, t, d), BF16)
    scratch = []
    if w_proj is not None:
        in_specs.append(pl.BlockSpec((None, d, d), lambda i, j: (w_slot, 0, w_col)))
        args.append(w_proj)
        out_specs, out_shape = [tok, tok], [out_shape, out_shape]
        scratch = [pltpu.VMEM((d, d), BF16)]
    return pl.pallas_call(
        _norm_kernel,
        grid=(b, t // tile),
        in_specs=in_specs,
        out_specs=out_specs,
        out_shape=out_shape,
        scratch_shapes=scratch,
        compiler_params=_cparams(("arbitrary", "arbitrary")),
        name="norm1",
    )(*args)


HG_BLOCK = 256
HG_CPB = HG_BLOCK // HG_CHUNK


def _chunk_prefix(x, row):
    for d in (1, 2, 4, 8, 16, 32):
        x = x + jnp.where(row >= d, pltpu.roll(x, d, axis=0), 0.0)
    return x


def _chunk_suffix(x, row):
    n = x.shape[0]
    for d in (1, 2, 4, 8, 16, 32):
        x = x + jnp.where(row < HG_CHUNK - d, pltpu.roll(x, n - d, axis=0), 0.0)
    return x


def _lower_bound(lb_ref, slot):
    rows = [lb_ref[j, 0] for j in range(lb_ref.shape[0])]
    m = functools.reduce(jnp.maximum, rows)
    es = [jnp.exp(r - m) for r in rows]
    return sum(es[:slot + 1]) / sum(es)


N_HGRN_IN = 10


def _hgrn_kernel(*refs, slot, n_ctx_chunks, n_lat_chunks, n_cast):
    hc_ref, hl_ref, g_ref, wq_ref, wv_ref, wf_ref, wb_ref, lbf_ref, lbb_ref, gn_ref = refs[:N_HGRN_IN]
    cast_in = refs[N_HGRN_IN:N_HGRN_IN + n_cast]
    y_ref = refs[N_HGRN_IN + n_cast]
    cast_out = refs[N_HGRN_IN + n_cast + 1:N_HGRN_IN + 2 * n_cast + 1]
    (wbuf, pbuf0, pbuf1, pbuf2, pbuf3, oacc, qif, qib,
     kvf, kvb, decf, decb, spf, spb) = refs[N_HGRN_IN + 2 * n_cast + 1:]
    for c_ref, o_ref in zip(cast_in, cast_out):
        o_ref[...] = c_ref[...].astype(BF16)

    lb_f = _lower_bound(lbf_ref, slot)
    lb_b = _lower_bound(lbb_ref, slot)
    row = lax.broadcasted_iota(jnp.int32, (HG_BLOCK, HG_DK), 0) & (HG_CHUNK - 1)
    ci = lax.broadcasted_iota(jnp.int32, (HG_CPB, HG_CHUNK, HG_CHUNK), 1)
    si = lax.broadcasted_iota(jnp.int32, (HG_CPB, HG_CHUNK, HG_CHUNK), 2)

    def c3(t):
        return t.reshape(HG_CPB, HG_CHUNK, HG_DK)

    def direction(qs3, v3b, z, lb, fwd):
        sig = _sigmoid_tanh(z)
        f = lb + (1.0 - lb) * sig
        k3 = c3((1.0 - lb) * (1.0 - sig))
        lf = jnp.log(f)
        if fwd:
            cum = c3(_chunk_prefix(lf, row))
            ref = cum[:, HG_CHUNK // 2 - 1:HG_CHUNK // 2, :]
            last = cum[:, HG_CHUNK - 1:HG_CHUNK, :]
        else:
            cum = c3(_chunk_suffix(lf, row))
            ref = cum[:, HG_CHUNK // 2:HG_CHUNK // 2 + 1, :]
            last = cum[:, 0:1, :]
        dec = jnp.exp(last)
        if qs3 is None:
            kl = k3 * jnp.exp(last - cum)
        else:
            e1 = jnp.exp(cum - ref)
            qd = qs3 * e1
            qi = qd * jnp.exp(ref)
            kd = k3 * (1.0 / e1)
            kl = kd * jnp.exp(last - ref)
        kvt = jnp.einsum('ncv,nck->nvk', v3b, kl.astype(BF16), preferred_element_type=F32)
        if qs3 is None:
            return None, None, kvt, dec
        sc = jnp.einsum('nck,nsk->ncs', qd.astype(BF16), kd.astype(BF16), preferred_element_type=F32)
        sc = jnp.where((ci >= si) if fwd else (ci <= si), sc, 0.0)
        intra = jnp.einsum('ncs,nsv->ncv', sc.astype(BF16), v3b, preferred_element_type=F32)
        return intra, qi, kvt, dec

    for p, wp_ref in enumerate((wq_ref, wv_ref, wf_ref, wb_ref)):
        wbuf[:, p * HG_DK:(p + 1) * HG_DK] = wp_ref[...].astype(BF16)

    def project(hrows):
        return jnp.dot(hrows, wbuf[...], preferred_element_type=F32)

    def block(p, chunk0, lat_row0, between=None):
        v3b = c3(p[:, HG_DK:2 * HG_DK]).astype(BF16)
        zf = p[:, 2 * HG_DK:3 * HG_DK]
        zb = p[:, 3 * HG_DK:4 * HG_DK]
        if lat_row0 is None:
            qs3 = None
        else:
            qs3 = c3(_silu_tanh(p[:, 0:HG_DK]))
        in_f, qi_f, kv_f, dec_f = direction(qs3, v3b, zf, lb_f, True)
        if between is not None:
            between()
        in_b, qi_b, kv_b, dec_b = direction(qs3, v3b, zb, lb_b, False)
        kvf[pl.ds(chunk0, HG_CPB)] = kv_f
        kvb[pl.ds(chunk0, HG_CPB)] = kv_b
        decf[pl.ds(chunk0, HG_CPB)] = dec_f
        decb[pl.ds(chunk0, HG_CPB)] = dec_b
        if lat_row0 is not None:
            rows = pl.ds(lat_row0, HG_BLOCK)
            oacc[rows, :] = (in_f + in_b).reshape(HG_BLOCK, HG_DK)
            qif[rows, :] = qi_f.reshape(HG_BLOCK, HG_DK).astype(BF16)
            qib[rows, :] = qi_b.reshape(HG_BLOCK, HG_DK).astype(BF16)

    n_blocks = n_lat_chunks // HG_CPB

    def lat_rows(i):
        return hl_ref[0, pl.ds(pl.multiple_of(i * HG_BLOCK, HG_BLOCK), HG_BLOCK), :]

    def lat_terms(p_ref, i, between=None):
        block(p_ref[...], n_ctx_chunks + i * HG_CPB, pl.multiple_of(i * HG_BLOCK, HG_BLOCK), between)

    half_n = 2 * HG_DK

    def project_half(dst, i, k):
        c = slice(k * half_n, (k + 1) * half_n)
        dst[:, c] = jnp.dot(lat_rows(i), wbuf[:, c], preferred_element_type=F32)

    ctx_p = [project(hc_ref[0, i * HG_BLOCK:(i + 1) * HG_BLOCK, :]) for i in range(n_ctx_chunks // HG_CPB)]
    pbuf0[...] = project(lat_rows(0))
    for i, p in enumerate(ctx_p):
        block(p, i * HG_CPB, None)
    pbuf1[...] = project(lat_rows(1))

    def lat_quad(i0, last):
        project_half(pbuf2, i0 + 2, 0)
        lat_terms(pbuf0, i0, lambda: project_half(pbuf2, i0 + 2, 1))
        project_half(pbuf3, i0 + 3, 0)
        lat_terms(pbuf1, i0 + 1, lambda: project_half(pbuf3, i0 + 3, 1))
        if not last:
            project_half(pbuf0, i0 + 4, 0)
        lat_terms(pbuf2, i0 + 2, None if last else (lambda: project_half(pbuf0, i0 + 4, 1)))
        if not last:
            project_half(pbuf1, i0 + 5, 0)
        lat_terms(pbuf3, i0 + 3, None if last else (lambda: project_half(pbuf1, i0 + 5, 1)))

    def lat_body(t, carry):
        lat_quad(4 * t, False)
        return carry

    lax.fori_loop(0, n_blocks // 4 - 1, lat_body, 0)
    lat_quad(n_blocks - 4, True)

    def advance(s, kv_ref, dec_ref, n):
        return dec_ref[n] * s + kv_ref[n]

    sf = jnp.zeros((HG_DK, HG_DK), F32)
    for n in range(n_ctx_chunks):
        sf = advance(sf, kvf, decf, n)
    sb = jnp.zeros((HG_DK, HG_DK), F32)
    for n in reversed(range(n_ctx_chunks)):
        sb = advance(sb, kvb, decb, n)

    def scan_step(t, carry):
        sf, sb = carry
        jb = n_lat_chunks - 1 - t
        spf[t] = sf.astype(BF16)
        spb[jb] = sb.astype(BF16)
        return (advance(sf, kvf, decf, n_ctx_chunks + t), advance(sb, kvb, decb, n_ctx_chunks + jb))

    lax.fori_loop(0, n_lat_chunks, scan_step, (sf, sb), unroll=2)

    gn = gn_ref[...]

    def block_rows(i):
        return pl.ds(pl.multiple_of(i * HG_BLOCK, HG_BLOCK), HG_BLOCK)

    def inter(i):
        rows = block_rows(i)
        chunks = pl.ds(i * HG_CPB, HG_CPB)
        return (jnp.einsum('nck,nvk->ncv', c3(qif[rows, :]), spf[chunks], preferred_element_type=F32)
                + jnp.einsum('nck,nvk->ncv', c3(qib[rows, :]), spb[chunks], preferred_element_type=F32)
                ).reshape(HG_BLOCK, HG_DK)

    def readout(o_ref, i):
        rows = block_rows(i)
        o = oacc[rows, :] + o_ref[...]
        ms = jnp.mean(o * o, axis=-1, keepdims=True)
        o = o * lax.rsqrt(ms + EPS) * gn
        y_ref[0, rows, :] = (o * _silu_tanh(g_ref[0, rows, :].astype(F32))).astype(BF16)

    obuf0, obuf1 = pbuf0.at[:, 0:HG_DK], pbuf1.at[:, 0:HG_DK]
    obuf0[...] = inter(0)

    def readout_pair(t, carry):
        obuf1[...] = inter(2 * t + 1)
        readout(obuf0, 2 * t)
        obuf0[...] = inter(2 * t + 2)
        readout(obuf1, 2 * t + 1)
        return carry

    lax.fori_loop(0, n_blocks // 2 - 1, readout_pair, 0)
    obuf1[...] = inter(n_blocks - 1)
    readout(obuf0, n_blocks - 2)
    readout(obuf1, n_blocks - 1)


def _hgrn_call(h_ctx, h_lat, g_lat, w_in, lb_fwd, lb_bwd, gnorm, slot, to_cast):
    b, lc, d = h_ctx.shape
    ll = h_lat.shape[1]
    nrow = lb_fwd.shape[0]
    ncc, nlc = lc // HG_CHUNK, ll // HG_CHUNK
    kern = functools.partial(_hgrn_kernel, slot=slot, n_ctx_chunks=ncc, n_lat_chunks=nlc, n_cast=len(to_cast))
    lb_spec = pl.BlockSpec((nrow, 1, 1, HG_DK), lambda i, h: (0, h, 0, 0))
    head_cols = pl.BlockSpec((1, ll, HG_DK), lambda i, h: (i, 0, h))
    n_rec = N_HG_PROJ - 1
    w_cols = [pl.BlockSpec((None, d, HG_DK), functools.partial(lambda i, h, p: (slot, 0, p * HG_HEADS + h), p=p))
              for p in range(n_rec)]
    steps = b * HG_HEADS
    slabs = [a.reshape(steps, -1, *a.shape[-2:]) for a in to_cast]
    slab_specs = [pl.BlockSpec((1,) + a.shape[1:], lambda i, h: (i * HG_HEADS + h, 0, 0, 0)) for a in slabs]
    outs = pl.pallas_call(
        kern,
        grid=(b, HG_HEADS),
        in_specs=[
            pl.BlockSpec((1, lc, d), lambda i, h: (i, 0, 0)),
            pl.BlockSpec((1, ll, d), lambda i, h: (i, 0, 0)),
            head_cols,
            *w_cols,
            lb_spec, lb_spec,
            pl.BlockSpec((1, HG_DK), lambda i, h: (0, 0)),
            *slab_specs,
        ],
        out_specs=[head_cols, *slab_specs],
        out_shape=[jax.ShapeDtypeStruct((b, ll, d), BF16)]
        + [jax.ShapeDtypeStruct(a.shape, BF16) for a in slabs],
        scratch_shapes=[
            pltpu.VMEM((d, n_rec * HG_DK), BF16),
            pltpu.VMEM((HG_BLOCK, n_rec * HG_DK), F32),
            pltpu.VMEM((HG_BLOCK, n_rec * HG_DK), F32),
            pltpu.VMEM((HG_BLOCK, n_rec * HG_DK), F32),
            pltpu.VMEM((HG_BLOCK, n_rec * HG_DK), F32),
            pltpu.VMEM((ll, HG_DK), F32),
            pltpu.VMEM((ll, HG_DK), BF16),
            pltpu.VMEM((ll, HG_DK), BF16),
            pltpu.VMEM((ncc + nlc, HG_DK, HG_DK), F32),
            pltpu.VMEM((ncc + nlc, HG_DK, HG_DK), F32),
            pltpu.VMEM((ncc + nlc, 1, HG_DK), F32),
            pltpu.VMEM((ncc + nlc, 1, HG_DK), F32),
            pltpu.VMEM((nlc, HG_DK, HG_DK), BF16),
            pltpu.VMEM((nlc, HG_DK, HG_DK), BF16),
        ],
        compiler_params=_cparams(("parallel", "arbitrary"), HGRN_VMEM_LIMIT),
        name="hgrn2",
    )(h_ctx, h_lat, g_lat, *([w_in] * n_rec),
      lb_fwd.reshape(nrow, HG_HEADS, 1, HG_DK), lb_bwd.reshape(nrow, HG_HEADS, 1, HG_DK),
      gnorm.reshape(1, HG_DK), *slabs)
    return outs[0], [o.reshape(a.shape) for o, a in zip(outs[1:], to_cast)]


POOL_STRIP = 8
POOL_PAD = 8
POOL_BAND = 256


def _pool_group(h_ref, win_ref, wg_ref, ps_ref, o_ref, upad, *, win, rows):
    gd = win_ref.shape[1]
    half = win // 2
    zeros = jnp.zeros((POOL_PAD, GRID_W, gd), F32)
    upad[0:POOL_PAD] = zeros
    upad[POOL_PAD + rows:POOL_PAD + rows + POOL_PAD] = zeros
    w_in = win_ref[...].astype(BF16)
    tok = POOL_STRIP * GRID_W
    n_strips = rows // POOL_STRIP

    def project(c):
        u = jnp.dot(h_ref[0, c * tok:(c + 1) * tok, :], w_in, preferred_element_type=F32)
        upad[POOL_PAD + c * POOL_STRIP:POOL_PAD + (c + 1) * POOL_STRIP] = u.reshape(POOL_STRIP, GRID_W, gd)

    bi = lax.broadcasted_iota(jnp.int32, (POOL_BAND, POOL_BAND), 0)
    bj = lax.broadcasted_iota(jnp.int32, (POOL_BAND, POOL_BAND), 1)
    lo_c = (bi & (GRID_W - 1)) - half
    cj = bj & (GRID_W - 1)
    band = jnp.where(bi >> GRID_SHIFT == bj >> GRID_SHIFT, 1.0, 0.0)
    band = jnp.where(cj >= lo_c, band, 0.0)
    band = jnp.where(cj < lo_c + win, band, 0.0).astype(BF16)
    t = lax.broadcasted_iota(jnp.int32, (tok, LANES), 0)
    col = t & (GRID_W - 1)
    cnt_c = jnp.minimum(col - half + win, GRID_W) - jnp.maximum(col - half, 0)
    wg = wg_ref[0].astype(BF16)
    ps = ps_ref[...]

    def strip(i):
        r0 = i * POOL_STRIP
        slab = upad[r0 + POOL_PAD - half:r0 + POOL_PAD - half + POOL_STRIP + win - 1]
        span = 1
        while span < win:
            n = slab.shape[0] - span
            slab = slab[0:n] + slab[span:span + n]
            span *= 2
        rs = slab.reshape(tok, gd)
        hi = rs.astype(BF16)
        lo = (rs - hi.astype(F32)).astype(BF16)
        parts = []
        for k in range(tok // POOL_BAND):
            sl = slice(k * POOL_BAND, (k + 1) * POOL_BAND)
            parts.append(jnp.dot(band, hi[sl], preferred_element_type=F32)
                         + jnp.dot(band, lo[sl], preferred_element_type=F32))
        box = jnp.concatenate(parts, axis=0)
        r = r0 + (t >> GRID_SHIFT)
        cnt_r = jnp.minimum(r - half + win, rows) - jnp.maximum(r - half, 0)
        inv = 1.0 / (cnt_r * cnt_c).astype(F32)
        mean = box * jnp.concatenate([inv] * (gd // LANES), axis=1)
        ug = upad[r0 + POOL_PAD:r0 + POOL_PAD + POOL_STRIP].reshape(tok, gd)
        z = jnp.dot((mean - ug).astype(BF16), wg, preferred_element_type=F32) * ps
        o_ref[0, i * tok:(i + 1) * tok, :] = z.astype(BF16)

    project(0)
    project(1)
    for i in range(n_strips):
        if i + 2 < n_strips:
            project(i + 2)
        strip(i)


def _pool_kernel(h_ref, win_ref, wg_ref, ps_ref, o_ref, upad, *, rows):
    g = pl.program_id(1)
    for gi, win in enumerate(POOL_WINDOWS):
        @pl.when(g == gi)
        def _(win=win):
            _pool_group(h_ref, win_ref, wg_ref, ps_ref, o_ref, upad, win=win, rows=rows)


def _pool_call(h, w_in, w_grp, p_scale, slot):
    b, l, d = h.shape
    ng = len(POOL_WINDOWS)
    gd = d // ng
    rows = l // GRID_W
    return pl.pallas_call(
        functools.partial(_pool_kernel, rows=rows),
        grid=(b, ng),
        in_specs=[
            pl.BlockSpec((1, l, d), lambda i, g: (i, 0, 0)),
            pl.BlockSpec((None, d, gd), lambda i, g: (slot, 0, g)),
            pl.BlockSpec((None, 1, gd, gd), lambda i, g: (slot, g, 0, 0)),
            pl.BlockSpec((None, 1, gd), lambda i, g: (slot, 0, g)),
        ],
        out_specs=pl.BlockSpec((1, l, gd), lambda i, g: (i, 0, g)),
        out_shape=jax.ShapeDtypeStruct((b, l, d), BF16),
        scratch_shapes=[pltpu.VMEM((rows + 2 * POOL_PAD, GRID_W, gd), F32)],
        compiler_params=_cparams(("parallel", "arbitrary")),
        name="pool_mix",
    )(h, w_in, w_grp, p_scale.reshape(p_scale.shape[0], 1, d))


MOE_TILE = 1024
MOE_WINDOWS = (128, 160, 192, 224, 256, 288, 320, 352, 384)
MOE_FULL = 256
assert MOE_WINDOWS[-1] == MOE_WINDOWS[0] + MOE_FULL and MOE_FULL & (MOE_FULL - 1) == 0
ROW_ALIGN = 16
DEST_LANE = N_EXPERTS
PIECE_STRIDE = 32
N_PIECES = 3
AUX_ROWS = 8


def _route(sel, s):
    keep = []
    gsum = []
    for g in range(N_EXPERT_GROUPS):
        a = sel[g * EXPERTS_PER_GROUP:(g + 1) * EXPERTS_PER_GROUP]
        beaten = [jnp.zeros_like(a[0]) for _ in a]
        for i in range(EXPERTS_PER_GROUP):
            for j in range(i + 1, EXPERTS_PER_GROUP):
                ge = jnp.where(a[i] >= a[j], 1.0, 0.0)
                beaten[j] = beaten[j] + ge
                beaten[i] = beaten[i] + (1.0 - ge)
        kg = [jnp.where(bt < 1.5, 1.0, 0.0) for bt in beaten]
        keep.append(kg)
        gsum.append(sum(k * x for k, x in zip(kg, a)))
    picked = []
    bests = []
    for g in range(N_EXPERT_GROUPS):
        better = jnp.zeros_like(gsum[0])
        for o in range(N_EXPERT_GROUPS):
            if o < g:
                better = better + jnp.where(gsum[o] >= gsum[g], 1.0, 0.0)
            elif o > g:
                better = better + jnp.where(gsum[o] > gsum[g], 1.0, 0.0)
        best = jnp.where(better < 0.5, 1.0, 0.0)
        bests.append(best)
        for i in range(EXPERTS_PER_GROUP):
            picked.append(best * keep[g][i] * s[g * EXPERTS_PER_GROUP + i])
    den = sum(picked)
    return [p / den for p in picked], bests


def _post_kernel(x_ref, y_ref, w_ref, mod_ref, g_ref, rw_ref, rb_ref,
                 xo_ref, h2_ref, rt_ref, aux_ref, wbuf):
    @pl.when(pl.program_id(0) == 0)
    def _():
        wbuf[...] = w_ref[...].astype(BF16)

    yw = jnp.dot(y_ref[...], wbuf[...], preferred_element_type=F32)
    xn = x_ref[...] + mod_ref[0, 2:3, :] * yw
    xo_ref[...] = xn
    h2 = _modulate(xn, g_ref[...], mod_ref[0, 3:4, :], mod_ref[0, 4:5, :])
    hi = h2.astype(BF16)
    h2_ref[...] = hi
    lo = (h2 - hi.astype(F32)).astype(BF16)
    rw = rw_ref[...]
    half = hi.shape[0] // 2
    prod = jnp.concatenate(
        [jnp.dot(hi[r:r + half], rw, preferred_element_type=F32)
         + jnp.dot(lo[r:r + half], rw, preferred_element_type=F32) for r in (0, half)], axis=0).T
    logits = prod[0:N_EXPERTS, :] + prod[N_EXPERTS:2 * N_EXPERTS, :]
    s = _sigmoid_pair(logits)[0]
    sel = s + rb_ref[...]
    comb, bests = _route([sel[e:e + 1, :] for e in range(N_EXPERTS)],
                         [s[e:e + 1, :] for e in range(N_EXPERTS)])
    t = logits.shape[1]

    ind = jnp.concatenate(bests + [jnp.zeros((AUX_ROWS - N_EXPERT_GROUPS, t), F32)], axis=0)
    n_seg = t // LANES
    stacked = jnp.concatenate([ind[:, j * LANES:(j + 1) * LANES] for j in range(n_seg)], axis=0)
    upper = jnp.where(lax.broadcasted_iota(jnp.int32, (LANES, LANES), 0)
                      <= lax.broadcasted_iota(jnp.int32, (LANES, LANES), 1), 1.0, 0.0).astype(BF16)
    local = jnp.dot(stacked.astype(BF16), upper, preferred_element_type=F32)
    off = jnp.zeros((AUX_ROWS, 1), F32)
    pieces = []
    for j in range(n_seg):
        seg_cum = local[j * AUX_ROWS:(j + 1) * AUX_ROWS, :]
        pieces.append(seg_cum + off)
        off = off + seg_cum[:, LANES - 1:LANES]
    cum = jnp.concatenate(pieces, axis=1)
    counts = [off[g:g + 1, :] for g in range(N_EXPERT_GROUPS)]
    starts = [jnp.zeros((1, 1), F32)]
    for g in range(1, N_EXPERT_GROUPS):
        starts.append(starts[-1] + counts[g - 1])
    dest = sum(bests[g] * (starts[g] + cum[g:g + 1, :] - 1.0) for g in range(N_EXPERT_GROUPS))

    lane = lax.broadcasted_iota(jnp.int32, (1, t), 1)
    seg = sum(jnp.where(lane == k, v, 0.0) for k, v in enumerate(starts + counts))
    aux_ref[0] = jnp.concatenate([dest, seg, jnp.zeros((AUX_ROWS - 2, t), F32)], axis=0)
    p1 = [w.astype(BF16).astype(F32) for w in comb]
    r1 = [w - p for w, p in zip(comb, p1)]
    p2 = [r.astype(BF16).astype(F32) for r in r1]
    p3 = [r - p for r, p in zip(r1, p2)]
    pad = [jnp.zeros((PIECE_STRIDE - N_EXPERTS, t), F32)]
    table = jnp.concatenate(p1 + [dest] + [jnp.zeros((PIECE_STRIDE - N_EXPERTS - 1, t), F32)] + p2 + pad + p3 + pad
                            + [jnp.zeros((LANES - N_PIECES * PIECE_STRIDE, t), F32)], axis=0)
    rt_ref[...] = table.T


def _post_call(x, y, w, w_slot, mod, g, router_w, router_b, tokens_per_batch, tile=MOE_TILE):
    n, d = x.shape
    per_b = tokens_per_batch // tile
    rwh = router_w.astype(BF16)
    rwl = (router_w - rwh.astype(F32)).astype(BF16)
    rw = jnp.concatenate([rwh, rwl, jnp.zeros((d, LANES - 2 * N_EXPERTS), BF16)], axis=1)
    row = lambda i: (i, 0)
    fixed = lambda i: (0, 0)
    return pl.pallas_call(
        _post_kernel,
        grid=(n // tile,),
        in_specs=[
            pl.BlockSpec((tile, d), row),
            pl.BlockSpec((tile, d), row),
            pl.BlockSpec((None, d, d), lambda i: (w_slot, 0, 0)),
            pl.BlockSpec((1, N_MOD, d), lambda i: (i // per_b, 0, 0)),
            pl.BlockSpec((1, d), fixed),
            pl.BlockSpec((d, LANES), fixed),
            pl.BlockSpec((N_EXPERTS, 1), fixed),
        ],
        out_specs=[
            pl.BlockSpec((tile, d), row),
            pl.BlockSpec((tile, d), row),
            pl.BlockSpec((tile, LANES), row),
            pl.BlockSpec((1, AUX_ROWS, tile), lambda i: (i, 0, 0)),
        ],
        out_shape=[
            jax.ShapeDtypeStruct((n, d), F32),
            jax.ShapeDtypeStruct((n, d), BF16),
            jax.ShapeDtypeStruct((n, LANES), F32),
            jax.ShapeDtypeStruct((n // tile, AUX_ROWS, tile), F32),
        ],
        scratch_shapes=[pltpu.VMEM((d, d), BF16)],
        compiler_params=_cparams(("arbitrary",)),
        name="mixer_out_router",
    )(x, y, w, mod, g.reshape(1, d), rw, router_b.reshape(N_EXPERTS, 1))


def _moe_kernel(seg_ref, h_ref, rt_ref, aux_ref, wg_ref, wu_ref, wd_ref, x_ref, mod_ref, ng_ref, nmod_ref,
                o_ref, *rest, final_norm):
    if final_norm:
        xs, cws, ys = rest
    else:
        hn_ref, xs, cws, ys = rest
    i = pl.program_id(0)
    g = pl.program_id(1)
    t = h_ref.shape[0]

    @pl.when(g == 0)
    def _():
        dest_row = aux_ref[0, 0:1, :]
        perm = jnp.where(lax.broadcasted_iota(jnp.int32, (t, t), 0).astype(F32) == dest_row,
                         1.0, 0.0).astype(BF16)
        xs[...] = jnp.dot(perm, h_ref[...], preferred_element_type=F32).astype(BF16)
        cws[...] = jnp.dot(perm, rt_ref[...].astype(BF16), preferred_element_type=F32)
        ys[...] = jnp.zeros_like(ys)

    start = seg_ref[i, g]
    end = start + seg_ref[i, N_EXPERT_GROUPS + g]
    wd = wd_ref[...].reshape(EXPERTS_PER_GROUP * D_EXPERT, wd_ref.shape[2])

    def window(w0, rows, lo, hi):
        sl = pl.ds(pl.multiple_of(w0, ROW_ALIGN), rows)
        xw = xs[sl, :]
        cw = cws[sl, :]
        r = w0 + lax.broadcasted_iota(jnp.int32, cw.shape, 0)
        lane = lax.broadcasted_iota(jnp.int32, cw.shape, 1)
        piece_lane = jnp.where(lane < N_PIECES * PIECE_STRIDE, lane & (PIECE_STRIDE - 1), -1)
        cw = jnp.where(r >= lo, cw, 0.0)
        cw = jnp.where(r < hi, cw, 0.0)
        acts = []
        for e in range(EXPERTS_PER_GROUP):
            ce = jnp.sum(jnp.where(piece_lane == g * EXPERTS_PER_GROUP + e, cw, 0.0), axis=1, keepdims=True)
            ge = g * EXPERTS_PER_GROUP + e
            a = _silu_tanh(jnp.dot(xw, wg_ref[ge], preferred_element_type=F32)) \
                * jnp.dot(xw, wu_ref[ge], preferred_element_type=F32)
            acts.append((a * ce).astype(BF16))
        ys[sl, :] += jnp.dot(jnp.concatenate(acts, axis=1), wd, preferred_element_type=F32)

    aligned = start & -ROW_ALIGN
    need = end - aligned
    biggest = MOE_WINDOWS[-1]
    n_full = jnp.where(need > biggest, (need - (MOE_WINDOWS[0] + 1)) >> (MOE_FULL.bit_length() - 1), 0)

    def full(k, carry):
        w0 = aligned + k * MOE_FULL
        window(w0, MOE_FULL, jnp.maximum(start, w0), w0 + MOE_FULL)
        return carry

    lax.fori_loop(0, n_full, full, 0)
    rest0 = aligned + n_full * MOE_FULL
    rest = end - rest0
    for smaller, m in zip((0,) + MOE_WINDOWS[:-1], MOE_WINDOWS):
        @pl.when((end > start) & (rest > smaller) & (rest <= m))
        def _(m=m):
            window(jnp.minimum(rest0, t - m), m, jnp.maximum(start, rest0), end)

    @pl.when(g == N_EXPERT_GROUPS - 1)
    def _():
        dest_col = rt_ref[:, DEST_LANE:DEST_LANE + 1]
        unperm = jnp.where(lax.broadcasted_iota(jnp.int32, (t, t), 1).astype(F32) == dest_col,
                           1.0, 0.0).astype(BF16)
        ff = jnp.dot(unperm, ys[...].astype(BF16), preferred_element_type=F32)
        xn = x_ref[...] + mod_ref[0, 5:6, :] * ff
        if final_norm:
            ms = jnp.mean(xn * xn, axis=-1, keepdims=True)
            xn = xn * lax.rsqrt(ms + EPS) * ng_ref[...]
        else:
            hn_ref[...] = _modulate(xn, ng_ref[...], nmod_ref[0, 0:1, :], nmod_ref[0, 1:2, :]).astype(BF16)
        o_ref[...] = xn


def _moe_call(h2, table, aux, wg, wu, wd, layer, x, mod, norm_g, next_mod, tokens_per_batch, final_norm):
    n, d = x.shape
    tile = MOE_TILE
    per_b = tokens_per_batch // tile
    seg = aux[:, 1, :2 * N_EXPERT_GROUPS].astype(jnp.int32)
    row = lambda i, g, seg: (i, 0)
    grp = lambda i, g, seg: (layer, g, 0, 0)
    tok = pl.BlockSpec((tile, d), row)
    x_out = jax.ShapeDtypeStruct((n, d), F32)
    grid_spec = pltpu.PrefetchScalarGridSpec(
        num_scalar_prefetch=1,
        grid=(n // tile, N_EXPERT_GROUPS),
        in_specs=[
            pl.BlockSpec((tile, d), row),
            pl.BlockSpec((tile, LANES), row),
            pl.BlockSpec((1, AUX_ROWS, tile), lambda i, g, seg: (i, 0, 0)),
            pl.BlockSpec((None, N_EXPERTS, d, D_EXPERT), lambda i, g, seg: (layer, 0, 0, 0)),
            pl.BlockSpec((None, N_EXPERTS, d, D_EXPERT), lambda i, g, seg: (layer, 0, 0, 0)),
            pl.BlockSpec((None, EXPERTS_PER_GROUP, D_EXPERT, d), grp),
            pl.BlockSpec((tile, d), row),
            pl.BlockSpec((1, N_MOD, d), lambda i, g, seg: (i // per_b, 0, 0)),
            pl.BlockSpec((1, d), lambda i, g, seg: (0, 0)),
            pl.BlockSpec((1, N_MOD, d), lambda i, g, seg: (i // per_b, 0, 0)),
        ],
        out_specs=tok if final_norm else [tok, tok],
        scratch_shapes=[
            pltpu.VMEM((tile, d), BF16),
            pltpu.VMEM((tile, LANES), F32),
            pltpu.VMEM((tile, d), F32),
        ],
    )
    return pl.pallas_call(
        functools.partial(_moe_kernel, final_norm=final_norm),
        grid_spec=grid_spec,
        out_shape=x_out if final_norm else [x_out, jax.ShapeDtypeStruct((n, d), BF16)],
        compiler_params=_cparams(("parallel", "arbitrary"), MOE_VMEM_LIMIT),
        name="moe_ffn",
    )(seg, h2, table, aux, wg, wu, wd, x, mod, norm_g.reshape(1, d), next_mod)


def kernel(x, c, ctx, c_ctx, w_mod, b_mod, norm1_g, norm2_g, hg_w_in, hg_lb_fwd, hg_lb_bwd, hg_gnorm,
           hg_w_out, pool_w_in, pool_w_grp, pool_scale, pool_w_out, router_w, router_b, moe_w_gate,
           moe_w_up, moe_w_down, final_g):
    b, l, d = x.shape
    depth = w_mod.shape[0]
    n_mixers = 2

    cc = jnp.concatenate([c, c_ctx[None, :], jnp.zeros((MOD_ROWS - b - 1, d), F32)], axis=0)
    mods = _mod_call(cc, w_mod, b_mod)

    experts = None
    x_lat = x.reshape(b * l, d)
    h_lat = None
    for i in range(depth):
        slot = i // n_mixers
        mod_lat = mods[i, :b].reshape(b, N_MOD, d)
        if i % n_mixers == 0:
            h_lat, g_lat = _norm_call(x_lat.reshape(b, l, d), norm1_g[i], mod_lat, 1024,
                                      hg_w_in, slot, N_HG_PROJ - 1)
            mod_ctx = jnp.broadcast_to(mods[i, b].reshape(1, N_MOD, d), (b, N_MOD, d))
            h_ctx = _norm_call(ctx, norm1_g[i], mod_ctx, ctx.shape[1])
            to_cast = (moe_w_gate, moe_w_up, moe_w_down) if experts is None else ()
            y, cast = _hgrn_call(h_ctx, h_lat, g_lat, hg_w_in, hg_lb_fwd, hg_lb_bwd, hg_gnorm[slot], slot, to_cast)
            experts = experts or tuple(cast)
            w_out = hg_w_out
        else:
            if h_lat is None:
                h_lat = _norm_call(x_lat.reshape(b, l, d), norm1_g[i], mod_lat, 512)
            y = _pool_call(h_lat, pool_w_in, pool_w_grp, pool_scale, slot)
            w_out = pool_w_out
        x_lat, h2, table, aux = _post_call(x_lat, y.reshape(b * l, d), w_out, slot, mod_lat,
                                           norm2_g[i], router_w, router_b, l)
        if experts is None:
            experts = (moe_w_gate.astype(BF16), moe_w_up.astype(BF16), moe_w_down.astype(BF16))
        if i == depth - 1:
            x_lat = _moe_call(h2, table, aux, *experts, i, x_lat, mod_lat, final_g, mod_lat, l, final_norm=True)
        else:
            mod_next = mods[i + 1, :b].reshape(b, N_MOD, d)
            x_lat, h_next = _moe_call(h2, table, aux, *experts, i, x_lat, mod_lat, norm1_g[i + 1], mod_next, l,
                                      final_norm=False)
            h_lat = h_next.reshape(b, l, d) if (i + 1) % n_mixers != 0 else None
    return x_lat.reshape(b, l, d)
```

```python
import functools

import jax
import jax.numpy as jnp
from jax import lax
from jax.experimental import pallas as pl
from jax.experimental.pallas import tpu as pltpu

F32 = jnp.float32
BF16 = jnp.bfloat16

EPS = 1e-6
N_MOD = 6
HG_HEADS = 8
HG_DK = 128
HG_CHUNK = 64
N_HG_PROJ = 5
POOL_WINDOWS = (2, 4, 8, 16)
GRID_W = 64
GRID_SHIFT = GRID_W.bit_length() - 1
assert 1 << GRID_SHIFT == GRID_W
N_EXPERTS = 16
N_EXPERT_GROUPS = 4
EXPERTS_PER_GROUP = N_EXPERTS // N_EXPERT_GROUPS
D_EXPERT = 256

LANES = 128
MOD_ROWS = 8
VMEM_LIMIT = 56 * 1024 * 1024
HGRN_VMEM_LIMIT = 60 * 1024 * 1024
MOE_VMEM_LIMIT = 60 * 1024 * 1024


def _cparams(sem, vmem=VMEM_LIMIT):
    return pltpu.CompilerParams(dimension_semantics=sem, vmem_limit_bytes=vmem)


def _sigmoid_pair(z):
    e = jnp.exp(-jnp.abs(z))
    r = 1.0 / (1.0 + e)
    er = e * r
    pos = z >= 0
    return jnp.where(pos, r, er), jnp.where(pos, er, r)


def _silu(z):
    return z * _sigmoid_pair(z)[0]


def _sigmoid_tanh(z):
    return 0.5 * jnp.tanh(0.5 * z) + 0.5


def _silu_tanh(z):
    return z * _sigmoid_tanh(z)


def _split_bf16(x):
    hi = x.astype(BF16)
    return hi, (x - hi.astype(F32)).astype(BF16)


def _mod_kernel(c_ref, w_ref, b_ref, o_ref):
    a = jnp.concatenate(_split_bf16(_silu(c_ref[...])), axis=0)
    w_hi, w_lo = _split_bf16(w_ref[0])
    acc = jnp.dot(a, w_hi, preferred_element_type=F32) + jnp.dot(a, w_lo, preferred_element_type=F32)
    o_ref[0] = acc[0:MOD_ROWS] + acc[MOD_ROWS:2 * MOD_ROWS] + b_ref[0]


def _mod_call(cc, w_mod, b_mod):
    depth, d, n = w_mod.shape
    tn = 1024
    return pl.pallas_call(
        _mod_kernel,
        grid=(depth, n // tn),
        in_specs=[
            pl.BlockSpec((MOD_ROWS, d), lambda i, j: (0, 0)),
            pl.BlockSpec((1, d, tn), lambda i, j: (i, 0, j)),
            pl.BlockSpec((1, 1, tn), lambda i, j: (i, 0, j)),
        ],
        out_specs=pl.BlockSpec((1, MOD_ROWS, tn), lambda i, j: (i, 0, j)),
        out_shape=jax.ShapeDtypeStruct((depth, MOD_ROWS, n), F32),
        compiler_params=_cparams(("parallel", "parallel")),
        name="mod_proj",
    )(cc, w_mod, b_mod.reshape(depth, 1, n))


def _modulate(x, g, shift, scale):
    ms = jnp.mean(x * x, axis=-1, keepdims=True)
    return (x * lax.rsqrt(ms + EPS) * g) * (1.0 + scale) + shift


def _norm_kernel(x_ref, g_ref, mod_ref, *rest):
    h = _modulate(x_ref[0], g_ref[...], mod_ref[0, 0:1, :], mod_ref[0, 1:2, :]).astype(BF16)
    if len(rest) == 1:
        (o_ref,) = rest
    else:
        w_ref, o_ref, p_ref, wbuf = rest

        @pl.when((pl.program_id(0) == 0) & (pl.program_id(1) == 0))
        def _():
            wbuf[...] = w_ref[...].astype(BF16)

        p_ref[0] = jnp.dot(h, wbuf[...], preferred_element_type=F32).astype(BF16)
    o_ref[0] = h


def _norm_call(x, g, mod, tile, w_proj=None, w_slot=0, w_col=0):
    b, t, d = x.shape
    tok = pl.BlockSpec((1, tile, d), lambda i, j: (i, j, 0))
    in_specs = [tok, pl.BlockSpec((1, d), lambda i, j: (0, 0)),
                pl.BlockSpec((1, N_MOD, d), lambda i, j: (i, 0, 0))]
    args = [x, g.reshape(1, d), mod]
    out_specs, out_shape = tok, jax.ShapeDtypeStruct((b, t, d), BF16)
    scratch = []
    if w_proj is not None:
        in_specs.append(pl.BlockSpec((None, d, d), lambda i, j: (w_slot, 0, w_col)))
        args.append(w_proj)
        out_specs, out_shape = [tok, tok], [out_shape, out_shape]
        scratch = [pltpu.VMEM((d, d), BF16)]
    return pl.pallas_call(
        _norm_kernel,
        grid=(b, t // tile),
        in_specs=in_specs,
        out_specs=out_specs,
        out_shape=out_shape,
        scratch_shapes=scratch,
        compiler_params=_cparams(("arbitrary", "arbitrary")),
        name="norm1",
    )(*args)


HG_BLOCK = 256
HG_CPB = HG_BLOCK // HG_CHUNK


def _chunk_prefix(x, row):
    for d in (1, 2, 4, 8, 16, 32):
        x = x + jnp.where(row >= d, pltpu.roll(x, d, axis=0), 0.0)
    return x


def _chunk_suffix(x, row):
    n = x.shape[0]
    for d in (1, 2, 4, 8, 16, 32):
        x = x + jnp.where(row < HG_CHUNK - d, pltpu.roll(x, n - d, axis=0), 0.0)
    return x


def _lower_bound(lb_ref, slot):
    rows = [lb_ref[j, 0] for j in range(lb_ref.shape[0])]
    m = functools.reduce(jnp.maximum, rows)
    es = [jnp.exp(r - m) for r in rows]
    return sum(es[:slot + 1]) / sum(es)


N_HGRN_IN = 10


def _hgrn_kernel(*refs, slot, n_ctx_chunks, n_lat_chunks, n_cast):
    hc_ref, hl_ref, g_ref, wq_ref, wv_ref, wf_ref, wb_ref, lbf_ref, lbb_ref, gn_ref = refs[:N_HGRN_IN]
    cast_in = refs[N_HGRN_IN:N_HGRN_IN + n_cast]
    y_ref = refs[N_HGRN_IN + n_cast]
    cast_out = refs[N_HGRN_IN + n_cast + 1:N_HGRN_IN + 2 * n_cast + 1]
    (wbuf, pbuf0, pbuf1, pbuf2, pbuf3, oacc, qif, qib,
     kvf, kvb, decf, decb, spf, spb) = refs[N_HGRN_IN + 2 * n_cast + 1:]
    for c_ref, o_ref in zip(cast_in, cast_out):
        o_ref[...] = c_ref[...].astype(BF16)

    lb_f = _lower_bound(lbf_ref, slot)
    lb_b = _lower_bound(lbb_ref, slot)
    row = lax.broadcasted_iota(jnp.int32, (HG_BLOCK, HG_DK), 0) & (HG_CHUNK - 1)
    ci = lax.broadcasted_iota(jnp.int32, (HG_CPB, HG_CHUNK, HG_CHUNK), 1)
    si = lax.broadcasted_iota(jnp.int32, (HG_CPB, HG_CHUNK, HG_CHUNK), 2)

    def c3(t):
        return t.reshape(HG_CPB, HG_CHUNK, HG_DK)

    def direction(qs3, v3b, z, lb, fwd):
        sig = _sigmoid_tanh(z)
        f = lb + (1.0 - lb) * sig
        k3 = c3((1.0 - lb) * (1.0 - sig))
        lf = jnp.log(f)
        if fwd:
            cum = c3(_chunk_prefix(lf, row))
            ref = cum[:, HG_CHUNK // 2 - 1:HG_CHUNK // 2, :]
            last = cum[:, HG_CHUNK - 1:HG_CHUNK, :]
        else:
            cum = c3(_chunk_suffix(lf, row))
            ref = cum[:, HG_CHUNK // 2:HG_CHUNK // 2 + 1, :]
            last = cum[:, 0:1, :]
        dec = jnp.exp(last)
        if qs3 is None:
            kl = k3 * jnp.exp(last - cum)
        else:
            e1 = jnp.exp(cum - ref)
            qd = qs3 * e1
            qi = qd * jnp.exp(ref)
            kd = k3 * (1.0 / e1)
            kl = kd * jnp.exp(last - ref)
        kvt = jnp.einsum('ncv,nck->nvk', v3b, kl.astype(BF16), preferred_element_type=F32)
        if qs3 is None:
            return None, None, kvt, dec
        sc = jnp.einsum('nck,nsk->ncs', qd.astype(BF16), kd.astype(BF16), preferred_element_type=F32)
        sc = jnp.where((ci >= si) if fwd else (ci <= si), sc, 0.0)
        intra = jnp.einsum('ncs,nsv->ncv', sc.astype(BF16), v3b, preferred_element_type=F32)
        return intra, qi, kvt, dec

    for p, wp_ref in enumerate((wq_ref, wv_ref, wf_ref, wb_ref)):
        wbuf[:, p * HG_DK:(p + 1) * HG_DK] = wp_ref[...].astype(BF16)

    def project(hrows):
        return jnp.dot(hrows, wbuf[...], preferred_element_type=F32)

    def block(p, chunk0, lat_row0, between=None):
        v3b = c3(p[:, HG_DK:2 * HG_DK]).astype(BF16)
        zf = p[:, 2 * HG_DK:3 * HG_DK]
        zb = p[:, 3 * HG_DK:4 * HG_DK]
        if lat_row0 is None:
            qs3 = None
        else:
            qs3 = c3(_silu_tanh(p[:, 0:HG_DK]))
        in_f, qi_f, kv_f, dec_f = direction(qs3, v3b, zf, lb_f, True)
        if between is not None:
            between()
        in_b, qi_b, kv_b, dec_b = direction(qs3, v3b, zb, lb_b, False)
        kvf[pl.ds(chunk0, HG_CPB)] = kv_f
        kvb[pl.ds(chunk0, HG_CPB)] = kv_b
        decf[pl.ds(chunk0, HG_CPB)] = dec_f
        decb[pl.ds(chunk0, HG_CPB)] = dec_b
        if lat_row0 is not None:
            rows = pl.ds(lat_row0, HG_BLOCK)
            oacc[rows, :] = (in_f + in_b).reshape(HG_BLOCK, HG_DK)
            qif[rows, :] = qi_f.reshape(HG_BLOCK, HG_DK).astype(BF16)
            qib[rows, :] = qi_b.reshape(HG_BLOCK, HG_DK).astype(BF16)

    n_blocks = n_lat_chunks // HG_CPB

    def lat_rows(i):
        return hl_ref[0, pl.ds(pl.multiple_of(i * HG_BLOCK, HG_BLOCK), HG_BLOCK), :]

    def lat_terms(p_ref, i, between=None):
        block(p_ref[...], n_ctx_chunks + i * HG_CPB, pl.multiple_of(i * HG_BLOCK, HG_BLOCK), between)

    half_n = 2 * HG_DK

    def project_half(dst, i, k):
        c = slice(k * half_n, (k + 1) * half_n)
        dst[:, c] = jnp.dot(lat_rows(i), wbuf[:, c], preferred_element_type=F32)

    ctx_p = [project(hc_ref[0, i * HG_BLOCK:(i + 1) * HG_BLOCK, :]) for i in range(n_ctx_chunks // HG_CPB)]
    pbuf0[...] = project(lat_rows(0))
    for i, p in enumerate(ctx_p):
        block(p, i * HG_CPB, None)
    pbuf1[...] = project(lat_rows(1))

    def lat_quad(i0, last):
        project_half(pbuf2, i0 + 2, 0)
        lat_terms(pbuf0, i0, lambda: project_half(pbuf2, i0 + 2, 1))
        project_half(pbuf3, i0 + 3, 0)
        lat_terms(pbuf1, i0 + 1, lambda: project_half(pbuf3, i0 + 3, 1))
        if not last:
            project_half(pbuf0, i0 + 4, 0)
        lat_terms(pbuf2, i0 + 2, None if last else (lambda: project_half(pbuf0, i0 + 4, 1)))
        if not last:
            project_half(pbuf1, i0 + 5, 0)
        lat_terms(pbuf3, i0 + 3, None if last else (lambda: project_half(pbuf1, i0 + 5, 1)))

    def lat_body(t, carry):
        lat_quad(4 * t, False)
        return carry

    lax.fori_loop(0, n_blocks // 4 - 1, lat_body, 0)
    lat_quad(n_blocks - 4, True)

    def advance(s, kv_ref, dec_ref, n):
        return dec_ref[n] * s + kv_ref[n]

    sf = jnp.zeros((HG_DK, HG_DK), F32)
    for n in range(n_ctx_chunks):
        sf = advance(sf, kvf, decf, n)
    sb = jnp.zeros((HG_DK, HG_DK), F32)
    for n in reversed(range(n_ctx_chunks)):
        sb = advance(sb, kvb, decb, n)

    def scan_step(t, carry):
        sf, sb = carry
        jb = n_lat_chunks - 1 - t
        spf[t] = sf.astype(BF16)
        spb[jb] = sb.astype(BF16)
        return (advance(sf, kvf, decf, n_ctx_chunks + t), advance(sb, kvb, decb, n_ctx_chunks + jb))

    lax.fori_loop(0, n_lat_chunks, scan_step, (sf, sb), unroll=2)

    gn = gn_ref[...]

    def block_rows(i):
        return pl.ds(pl.multiple_of(i * HG_BLOCK, HG_BLOCK), HG_BLOCK)

    def inter(i):
        rows = block_rows(i)
        chunks = pl.ds(i * HG_CPB, HG_CPB)
        return (jnp.einsum('nck,nvk->ncv', c3(qif[rows, :]), spf[chunks], preferred_element_type=F32)
                + jnp.einsum('nck,nvk->ncv', c3(qib[rows, :]), spb[chunks], preferred_element_type=F32)
                ).reshape(HG_BLOCK, HG_DK)

    def readout(o_ref, i):
        rows = block_rows(i)
        o = oacc[rows, :] + o_ref[...]
        ms = jnp.mean(o * o, axis=-1, keepdims=True)
        o = o * lax.rsqrt(ms + EPS) * gn
        y_ref[0, rows, :] = (o * _silu_tanh(g_ref[0, rows, :].astype(F32))).astype(BF16)

    obuf0, obuf1 = pbuf0.at[:, 0:HG_DK], pbuf1.at[:, 0:HG_DK]
    obuf0[...] = inter(0)

    def readout_pair(t, carry):
        obuf1[...] = inter(2 * t + 1)
        readout(obuf0, 2 * t)
        obuf0[...] = inter(2 * t + 2)
        readout(obuf1, 2 * t + 1)
        return carry

    for t in range(n_blocks // 2 - 1):
        readout_pair(t, 0)
    obuf1[...] = inter(n_blocks - 1)
    readout(obuf0, n_blocks - 2)
    readout(obuf1, n_blocks - 1)


def _hgrn_call(h_ctx, h_lat, g_lat, w_in, lb_fwd, lb_bwd, gnorm, slot, to_cast):
    b, lc, d = h_ctx.shape
    ll = h_lat.shape[1]
    nrow = lb_fwd.shape[0]
    ncc, nlc = lc // HG_CHUNK, ll // HG_CHUNK
    kern = functools.partial(_hgrn_kernel, slot=slot, n_ctx_chunks=ncc, n_lat_chunks=nlc, n_cast=len(to_cast))
    lb_spec = pl.BlockSpec((nrow, 1, 1, HG_DK), lambda i, h: (0, h, 0, 0))
    head_cols = pl.BlockSpec((1, ll, HG_DK), lambda i, h: (i, 0, h))
    n_rec = N_HG_PROJ - 1
    w_cols = [pl.BlockSpec((None, d, HG_DK), functools.partial(lambda i, h, p: (slot, 0, p * HG_HEADS + h), p=p))
              for p in range(n_rec)]
    steps = b * HG_HEADS
    slabs = [a.reshape(steps, -1, *a.shape[-2:]) for a in to_cast]
    slab_specs = [pl.BlockSpec((1,) + a.shape[1:], lambda i, h: (i * HG_HEADS + h, 0, 0, 0)) for a in slabs]
    outs = pl.pallas_call(
        kern,
        grid=(b, HG_HEADS),
        in_specs=[
            pl.BlockSpec((1, lc, d), lambda i, h: (i, 0, 0)),
            pl.BlockSpec((1, ll, d), lambda i, h: (i, 0, 0)),
            head_cols,
            *w_cols,
            lb_spec, lb_spec,
            pl.BlockSpec((1, HG_DK), lambda i, h: (0, 0)),
            *slab_specs,
        ],
        out_specs=[head_cols, *slab_specs],
        out_shape=[jax.ShapeDtypeStruct((b, ll, d), BF16)]
        + [jax.ShapeDtypeStruct(a.shape, BF16) for a in slabs],
        scratch_shapes=[
            pltpu.VMEM((d, n_rec * HG_DK), BF16),
            pltpu.VMEM((HG_BLOCK, n_rec * HG_DK), F32),
            pltpu.VMEM((HG_BLOCK, n_rec * HG_DK), F32),
            pltpu.VMEM((HG_BLOCK, n_rec * HG_DK), F32),
            pltpu.VMEM((HG_BLOCK, n_rec * HG_DK), F32),
            pltpu.VMEM((ll, HG_DK), F32),
            pltpu.VMEM((ll, HG_DK), BF16),
            pltpu.VMEM((ll, HG_DK), BF16),
            pltpu.VMEM((ncc + nlc, HG_DK, HG_DK), F32),
            pltpu.VMEM((ncc + nlc, HG_DK, HG_DK), F32),
            pltpu.VMEM((ncc + nlc, 1, HG_DK), F32),
            pltpu.VMEM((ncc + nlc, 1, HG_DK), F32),
            pltpu.VMEM((nlc, HG_DK, HG_DK), BF16),
            pltpu.VMEM((nlc, HG_DK, HG_DK), BF16),
        ],
        compiler_params=_cparams(("parallel", "arbitrary"), HGRN_VMEM_LIMIT),
        name="hgrn2",
    )(h_ctx, h_lat, g_lat, *([w_in] * n_rec),
      lb_fwd.reshape(nrow, HG_HEADS, 1, HG_DK), lb_bwd.reshape(nrow, HG_HEADS, 1, HG_DK),
      gnorm.reshape(1, HG_DK), *slabs)
    return outs[0], [o.reshape(a.shape) for o, a in zip(outs[1:], to_cast)]


POOL_STRIP = 8
POOL_PAD = 8
POOL_BAND = 256


def _pool_group(h_ref, win_ref, wg_ref, ps_ref, o_ref, upad, *, win, rows):
    gd = win_ref.shape[1]
    half = win // 2
    zeros = jnp.zeros((POOL_PAD, GRID_W, gd), F32)
    upad[0:POOL_PAD] = zeros
    upad[POOL_PAD + rows:POOL_PAD + rows + POOL_PAD] = zeros
    w_in = win_ref[...].astype(BF16)
    tok = POOL_STRIP * GRID_W
    n_strips = rows // POOL_STRIP

    def project(c):
        u = jnp.dot(h_ref[0, c * tok:(c + 1) * tok, :], w_in, preferred_element_type=F32)
        upad[POOL_PAD + c * POOL_STRIP:POOL_PAD + (c + 1) * POOL_STRIP] = u.reshape(POOL_STRIP, GRID_W, gd)

    bi = lax.broadcasted_iota(jnp.int32, (POOL_BAND, POOL_BAND), 0)
    bj = lax.broadcasted_iota(jnp.int32, (POOL_BAND, POOL_BAND), 1)
    lo_c = (bi & (GRID_W - 1)) - half
    cj = bj & (GRID_W - 1)
    band = jnp.where(bi >> GRID_SHIFT == bj >> GRID_SHIFT, 1.0, 0.0)
    band = jnp.where(cj >= lo_c, band, 0.0)
    band = jnp.where(cj < lo_c + win, band, 0.0).astype(BF16)
    t = lax.broadcasted_iota(jnp.int32, (tok, LANES), 0)
    col = t & (GRID_W - 1)
    cnt_c = jnp.minimum(col - half + win, GRID_W) - jnp.maximum(col - half, 0)
    wg = wg_ref[0].astype(BF16)
    ps = ps_ref[...]

    def strip(i):
        r0 = i * POOL_STRIP
        slab = upad[r0 + POOL_PAD - half:r0 + POOL_PAD - half + POOL_STRIP + win - 1]
        span = 1
        while span < win:
            n = slab.shape[0] - span
            slab = slab[0:n] + slab[span:span + n]
            span *= 2
        rs = slab.reshape(tok, gd)
        hi = rs.astype(BF16)
        lo = (rs - hi.astype(F32)).astype(BF16)
        parts = []
        for k in range(tok // POOL_BAND):
            sl = slice(k * POOL_BAND, (k + 1) * POOL_BAND)
            parts.append(jnp.dot(band, hi[sl], preferred_element_type=F32)
                         + jnp.dot(band, lo[sl], preferred_element_type=F32))
        box = jnp.concatenate(parts, axis=0)
        r = r0 + (t >> GRID_SHIFT)
        cnt_r = jnp.minimum(r - half + win, rows) - jnp.maximum(r - half, 0)
        inv = 1.0 / (cnt_r * cnt_c).astype(F32)
        mean = box * jnp.concatenate([inv] * (gd // LANES), axis=1)
        ug = upad[r0 + POOL_PAD:r0 + POOL_PAD + POOL_STRIP].reshape(tok, gd)
        z = jnp.dot((mean - ug).astype(BF16), wg, preferred_element_type=F32) * ps
        o_ref[0, i * tok:(i + 1) * tok, :] = z.astype(BF16)

    project(0)
    project(1)
    for i in range(n_strips):
        if i + 2 < n_strips:
            project(i + 2)
        strip(i)


def _pool_kernel(h_ref, win_ref, wg_ref, ps_ref, o_ref, upad, *, rows):
    g = pl.program_id(1)
    for gi, win in enumerate(POOL_WINDOWS):
        @pl.when(g == gi)
        def _(win=win):
            _pool_group(h_ref, win_ref, wg_ref, ps_ref, o_ref, upad, win=win, rows=rows)


def _pool_call(h, w_in, w_grp, p_scale, slot):
    b, l, d = h.shape
    ng = len(POOL_WINDOWS)
    gd = d // ng
    rows = l // GRID_W
    return pl.pallas_call(
        functools.partial(_pool_kernel, rows=rows),
        grid=(b, ng),
        in_specs=[
            pl.BlockSpec((1, l, d), lambda i, g: (i, 0, 0)),
            pl.BlockSpec((None, d, gd), lambda i, g: (slot, 0, g)),
            pl.BlockSpec((None, 1, gd, gd), lambda i, g: (slot, g, 0, 0)),
            pl.BlockSpec((None, 1, gd), lambda i, g: (slot, 0, g)),
        ],
        out_specs=pl.BlockSpec((1, l, gd), lambda i, g: (i, 0, g)),
        out_shape=jax.ShapeDtypeStruct((b, l, d), BF16),
        scratch_shapes=[pltpu.VMEM((rows + 2 * POOL_PAD, GRID_W, gd), F32)],
        compiler_params=_cparams(("parallel", "arbitrary")),
        name="pool_mix",
    )(h, w_in, w_grp, p_scale.reshape(p_scale.shape[0], 1, d))


MOE_TILE = 1024
MOE_WINDOWS = (128, 160, 192, 224, 256, 288, 320, 352, 384)
MOE_FULL = 256
ROW_ALIGN = 16
DEST_LANE = N_EXPERTS
PIECE_STRIDE = 32
N_PIECES = 3
AUX_ROWS = 8


def _route(sel, s):
    keep = []
    gsum = []
    for g in range(N_EXPERT_GROUPS):
        a = sel[g * EXPERTS_PER_GROUP:(g + 1) * EXPERTS_PER_GROUP]
        beaten = [jnp.zeros_like(a[0]) for _ in a]
        for i in range(EXPERTS_PER_GROUP):
            for j in range(i + 1, EXPERTS_PER_GROUP):
                ge = jnp.where(a[i] >= a[j], 1.0, 0.0)
                beaten[j] = beaten[j] + ge
                beaten[i] = beaten[i] + (1.0 - ge)
        kg = [jnp.where(bt < 1.5, 1.0, 0.0) for bt in beaten]
        keep.append(kg)
        gsum.append(sum(k * x for k, x in zip(kg, a)))
    picked = []
    bests = []
    for g in range(N_EXPERT_GROUPS):
        better = jnp.zeros_like(gsum[0])
        for o in range(N_EXPERT_GROUPS):
            if o < g:
                better = better + jnp.where(gsum[o] >= gsum[g], 1.0, 0.0)
            elif o > g:
                better = better + jnp.where(gsum[o] > gsum[g], 1.0, 0.0)
        best = jnp.where(better < 0.5, 1.0, 0.0)
        bests.append(best)
        for i in range(EXPERTS_PER_GROUP):
            picked.append(best * keep[g][i] * s[g * EXPERTS_PER_GROUP + i])
    den = sum(picked)
    return [p / den for p in picked], bests


def _post_kernel(x_ref, y_ref, w_ref, mod_ref, g_ref, rw_ref, rb_ref,
                 xo_ref, h2_ref, rt_ref, aux_ref, wbuf):
    @pl.when(pl.program_id(0) == 0)
    def _():
        wbuf[...] = w_ref[...].astype(BF16)

    yw = jnp.dot(y_ref[...], wbuf[...], preferred_element_type=F32)
    xn = x_ref[...] + mod_ref[0, 2:3, :] * yw
    xo_ref[...] = xn
    h2 = _modulate(xn, g_ref[...], mod_ref[0, 3:4, :], mod_ref[0, 4:5, :])
    hi = h2.astype(BF16)
    h2_ref[...] = hi
    lo = (h2 - hi.astype(F32)).astype(BF16)
    rw = rw_ref[...]
    half = hi.shape[0] // 2
    prod = jnp.concatenate(
        [jnp.dot(hi[r:r + half], rw, preferred_element_type=F32)
         + jnp.dot(lo[r:r + half], rw, preferred_element_type=F32) for r in (0, half)], axis=0).T
    logits = prod[0:N_EXPERTS, :] + prod[N_EXPERTS:2 * N_EXPERTS, :]
    s = _sigmoid_pair(logits)[0]
    sel = s + rb_ref[...]
    comb, bests = _route([sel[e:e + 1, :] for e in range(N_EXPERTS)],
                         [s[e:e + 1, :] for e in range(N_EXPERTS)])
    t = logits.shape[1]

    ind = jnp.concatenate(bests + [jnp.zeros((AUX_ROWS - N_EXPERT_GROUPS, t), F32)], axis=0)
    n_seg = t // LANES
    stacked = jnp.concatenate([ind[:, j * LANES:(j + 1) * LANES] for j in range(n_seg)], axis=0)
    upper = jnp.where(lax.broadcasted_iota(jnp.int32, (LANES, LANES), 0)
                      <= lax.broadcasted_iota(jnp.int32, (LANES, LANES), 1), 1.0, 0.0).astype(BF16)
    local = jnp.dot(stacked.astype(BF16), upper, preferred_element_type=F32)
    off = jnp.zeros((AUX_ROWS, 1), F32)
    pieces = []
    for j in range(n_seg):
        seg_cum = local[j * AUX_ROWS:(j + 1) * AUX_ROWS, :]
        pieces.append(seg_cum + off)
        off = off + seg_cum[:, LANES - 1:LANES]
    cum = jnp.concatenate(pieces, axis=1)
    counts = [off[g:g + 1, :] for g in range(N_EXPERT_GROUPS)]
    starts = [jnp.zeros((1, 1), F32)]
    for g in range(1, N_EXPERT_GROUPS):
        starts.append(starts[-1] + counts[g - 1])
    dest = sum(bests[g] * (starts[g] + cum[g:g + 1, :] - 1.0) for g in range(N_EXPERT_GROUPS))

    lane = lax.broadcasted_iota(jnp.int32, (1, t), 1)
    seg = sum(jnp.where(lane == k, v, 0.0) for k, v in enumerate(starts + counts))
    aux_ref[0] = jnp.concatenate([dest, seg, jnp.zeros((AUX_ROWS - 2, t), F32)], axis=0)
    p1 = [w.astype(BF16).astype(F32) for w in comb]
    r1 = [w - p for w, p in zip(comb, p1)]
    p2 = [r.astype(BF16).astype(F32) for r in r1]
    p3 = [r - p for r, p in zip(r1, p2)]
    pad = [jnp.zeros((PIECE_STRIDE - N_EXPERTS, t), F32)]
    table = jnp.concatenate(p1 + [dest] + [jnp.zeros((PIECE_STRIDE - N_EXPERTS - 1, t), F32)] + p2 + pad + p3 + pad
                            + [jnp.zeros((LANES - N_PIECES * PIECE_STRIDE, t), F32)], axis=0)
    rt_ref[...] = table.T


def _post_call(x, y, w, w_slot, mod, g, router_w, router_b, tokens_per_batch, tile=MOE_TILE):
    n, d = x.shape
    per_b = tokens_per_batch // tile
    rwh = router_w.astype(BF16)
    rwl = (router_w - rwh.astype(F32)).astype(BF16)
    rw = jnp.concatenate([rwh, rwl, jnp.zeros((d, LANES - 2 * N_EXPERTS), BF16)], axis=1)
    row = lambda i: (i, 0)
    fixed = lambda i: (0, 0)
    return pl.pallas_call(
        _post_kernel,
        grid=(n // tile,),
        in_specs=[
            pl.BlockSpec((tile, d), row),
            pl.BlockSpec((tile, d), row),
            pl.BlockSpec((None, d, d), lambda i: (w_slot, 0, 0)),
            pl.BlockSpec((1, N_MOD, d), lambda i: (i // per_b, 0, 0)),
            pl.BlockSpec((1, d), fixed),
            pl.BlockSpec((d, LANES), fixed),
            pl.BlockSpec((N_EXPERTS, 1), fixed),
        ],
        out_specs=[
            pl.BlockSpec((tile, d), row),
            pl.BlockSpec((tile, d), row),
            pl.BlockSpec((tile, LANES), row),
            pl.BlockSpec((1, AUX_ROWS, tile), lambda i: (i, 0, 0)),
        ],
        out_shape=[
            jax.ShapeDtypeStruct((n, d), F32),
            jax.ShapeDtypeStruct((n, d), BF16),
            jax.ShapeDtypeStruct((n, LANES), F32),
            jax.ShapeDtypeStruct((n // tile, AUX_ROWS, tile), F32),
        ],
        scratch_shapes=[pltpu.VMEM((d, d), BF16)],
        compiler_params=_cparams(("arbitrary",)),
        name="mixer_out_router",
    )(x, y, w, mod, g.reshape(1, d), rw, router_b.reshape(N_EXPERTS, 1))


def _moe_kernel(seg_ref, h_ref, rt_ref, aux_ref, wg_ref, wu_ref, wd_ref, x_ref, mod_ref, ng_ref, nmod_ref,
                o_ref, *rest, final_norm):
    if final_norm:
        xs, cws, ys = rest
    else:
        hn_ref, xs, cws, ys = rest
    i = pl.program_id(0)
    g = pl.program_id(1)
    t = h_ref.shape[0]

    @pl.when(g == 0)
    def _():
        dest_row = aux_ref[0, 0:1, :]
        perm = jnp.where(lax.broadcasted_iota(jnp.int32, (t, t), 0).astype(F32) == dest_row,
                         1.0, 0.0).astype(BF16)
        xs[...] = jnp.dot(perm, h_ref[...], preferred_element_type=F32).astype(BF16)
        cws[...] = jnp.dot(perm, rt_ref[...].astype(BF16), preferred_element_type=F32)
        ys[...] = jnp.zeros_like(ys)

    start = seg_ref[i, g]
    end = start + seg_ref[i, N_EXPERT_GROUPS + g]
    wd = wd_ref[...].reshape(EXPERTS_PER_GROUP * D_EXPERT, wd_ref.shape[2])

    def window(w0, rows, lo, hi):
        sl = pl.ds(pl.multiple_of(w0, ROW_ALIGN), rows)
        xw = xs[sl, :]
        cw = cws[sl, :]
        r = w0 + lax.broadcasted_iota(jnp.int32, cw.shape, 0)
        lane = lax.broadcasted_iota(jnp.int32, cw.shape, 1)
        piece_lane = jnp.where(lane < N_PIECES * PIECE_STRIDE, lane & (PIECE_STRIDE - 1), -1)
        cw = jnp.where(r >= lo, cw, 0.0)
        cw = jnp.where(r < hi, cw, 0.0)
        acts = []
        for e in range(EXPERTS_PER_GROUP):
            ce = jnp.sum(jnp.where(piece_lane == g * EXPERTS_PER_GROUP + e, cw, 0.0), axis=1, keepdims=True)
            ge = g * EXPERTS_PER_GROUP + e
            a = _silu_tanh(jnp.dot(xw, wg_ref[ge], preferred_element_type=F32)) \
                * jnp.dot(xw, wu_ref[ge], preferred_element_type=F32)
            acts.append((a * ce).astype(BF16))
        ys[sl, :] += jnp.dot(jnp.concatenate(acts, axis=1), wd, preferred_element_type=F32)

    aligned = start & -ROW_ALIGN
    need = end - aligned
    biggest = MOE_WINDOWS[-1]
    n_full = jnp.where(need > biggest, (need - (MOE_WINDOWS[0] + 1)) >> (MOE_FULL.bit_length() - 1), 0)

    def full(k, carry):
        w0 = aligned + k * MOE_FULL
        window(w0, MOE_FULL, jnp.maximum(start, w0), w0 + MOE_FULL)
        return carry

    lax.fori_loop(0, n_full, full, 0)
    rest0 = aligned + n_full * MOE_FULL
    rest = end - rest0
    for smaller, m in zip((0,) + MOE_WINDOWS[:-1], MOE_WINDOWS):
        @pl.when((end > start) & (rest > smaller) & (rest <= m))
        def _(m=m):
            window(jnp.minimum(rest0, t - m), m, jnp.maximum(start, rest0), end)

    @pl.when(g == N_EXPERT_GROUPS - 1)
    def _():
        dest_col = rt_ref[:, DEST_LANE:DEST_LANE + 1]
        unperm = jnp.where(lax.broadcasted_iota(jnp.int32, (t, t), 1).astype(F32) == dest_col,
                           1.0, 0.0).astype(BF16)
        ff = jnp.dot(unperm, ys[...].astype(BF16), preferred_element_type=F32)
        xn = x_ref[...] + mod_ref[0, 5:6, :] * ff
        if final_norm:
            ms = jnp.mean(xn * xn, axis=-1, keepdims=True)
            xn = xn * lax.rsqrt(ms + EPS) * ng_ref[...]
        else:
            hn_ref[...] = _modulate(xn, ng_ref[...], nmod_ref[0, 0:1, :], nmod_ref[0, 1:2, :]).astype(BF16)
        o_ref[...] = xn


def _moe_call(h2, table, aux, wg, wu, wd, layer, x, mod, norm_g, next_mod, tokens_per_batch, final_norm):
    n, d = x.shape
    tile = MOE_TILE
    per_b = tokens_per_batch // tile
    seg = aux[:, 1, :2 * N_EXPERT_GROUPS].astype(jnp.int32)
    row = lambda i, g, seg: (i, 0)
    grp = lambda i, g, seg: (layer, g, 0, 0)
    tok = pl.BlockSpec((tile, d), row)
    x_out = jax.ShapeDtypeStruct((n, d), F32)
    grid_spec = pltpu.PrefetchScalarGridSpec(
        num_scalar_prefetch=1,
        grid=(n // tile, N_EXPERT_GROUPS),
        in_specs=[
            pl.BlockSpec((tile, d), row),
            pl.BlockSpec((tile, LANES), row),
            pl.BlockSpec((1, AUX_ROWS, tile), lambda i, g, seg: (i, 0, 0)),
            pl.BlockSpec((None, N_EXPERTS, d, D_EXPERT), lambda i, g, seg: (layer, 0, 0, 0)),
            pl.BlockSpec((None, N_EXPERTS, d, D_EXPERT), lambda i, g, seg: (layer, 0, 0, 0)),
            pl.BlockSpec((None, EXPERTS_PER_GROUP, D_EXPERT, d), grp),
            pl.BlockSpec((tile, d), row),
            pl.BlockSpec((1, N_MOD, d), lambda i, g, seg: (i // per_b, 0, 0)),
            pl.BlockSpec((1, d), lambda i, g, seg: (0, 0)),
            pl.BlockSpec((1, N_MOD, d), lambda i, g, seg: (i // per_b, 0, 0)),
        ],
        out_specs=tok if final_norm else [tok, tok],
        scratch_shapes=[
            pltpu.VMEM((tile, d), BF16),
            pltpu.VMEM((tile, LANES), F32),
            pltpu.VMEM((tile, d), F32),
        ],
    )
    return pl.pallas_call(
        functools.partial(_moe_kernel, final_norm=final_norm),
        grid_spec=grid_spec,
        out_shape=x_out if final_norm else [x_out, jax.ShapeDtypeStruct((n, d), BF16)],
        compiler_params=_cparams(("parallel", "arbitrary"), MOE_VMEM_LIMIT),
        name="moe_ffn",
    )(seg, h2, table, aux, wg, wu, wd, x, mod, norm_g.reshape(1, d), next_mod)


def kernel(x, c, ctx, c_ctx, w_mod, b_mod, norm1_g, norm2_g, hg_w_in, hg_lb_fwd, hg_lb_bwd, hg_gnorm,
           hg_w_out, pool_w_in, pool_w_grp, pool_scale, pool_w_out, router_w, router_b, moe_w_gate,
           moe_w_up, moe_w_down, final_g):
    b, l, d = x.shape
    depth = w_mod.shape[0]
    n_mixers = 2

    cc = jnp.concatenate([c, c_ctx[None, :], jnp.zeros((MOD_ROWS - b - 1, d), F32)], axis=0)
    mods = _mod_call(cc, w_mod, b_mod)

    experts = None
    x_lat = x.reshape(b * l, d)
    h_lat = None
    for i in range(depth):
        slot = i // n_mixers
        mod_lat = mods[i, :b].reshape(b, N_MOD, d)
        if i % n_mixers == 0:
            h_lat, g_lat = _norm_call(x_lat.reshape(b, l, d), norm1_g[i], mod_lat, 1024,
                                      hg_w_in, slot, N_HG_PROJ - 1)
            mod_ctx = jnp.broadcast_to(mods[i, b].reshape(1, N_MOD, d), (b, N_MOD, d))
            h_ctx = _norm_call(ctx, norm1_g[i], mod_ctx, ctx.shape[1])
            to_cast = (moe_w_gate, moe_w_up, moe_w_down) if experts is None else ()
            y, cast = _hgrn_call(h_ctx, h_lat, g_lat, hg_w_in, hg_lb_fwd, hg_lb_bwd, hg_gnorm[slot], slot, to_cast)
            experts = experts or tuple(cast)
            w_out = hg_w_out
        else:
            if h_lat is None:
                h_lat = _norm_call(x_lat.reshape(b, l, d), norm1_g[i], mod_lat, 512)
            y = _pool_call(h_lat, pool_w_in, pool_w_grp, pool_scale, slot)
            w_out = pool_w_out
        x_lat, h2, table, aux = _post_call(x_lat, y.reshape(b * l, d), w_out, slot, mod_lat,
                                           norm2_g[i], router_w, router_b, l)
        if experts is None:
            experts = (moe_w_gate.astype(BF16), moe_w_up.astype(BF16), moe_w_down.astype(BF16))
        if i == depth - 1:
            x_lat = _moe_call(h2, table, aux, *experts, i, x_lat, mod_lat, final_g, mod_lat, l, final_norm=True)
        else:
            mod_next = mods[i + 1, :b].reshape(b, N_MOD, d)
            x_lat, h_next = _moe_call(h2, table, aux, *experts, i, x_lat, mod_lat, norm1_g[i + 1], mod_next, l,
                                      final_norm=False)
            h_lat = h_next.reshape(b, l, d) if (i + 1) % n_mixers != 0 else None
    return x_lat.reshape(b, l, d)
```
